```python
import jax, jax.numpy as jnp
from jax import lax
import numpy as np

D_MODEL = 1024
BATCH = 32
SEQ = 256
DEPTH = 2
DEC_BATCH = 8
DEC_SEQ = 1024
PAST_LEN = 512

GRID_W = 64
MIX_WIDTH = D_MODEL
A_HEADS = 4
A_DV = MIX_WIDTH // 16
A_DK = A_DV // 2
A_KEY_W = A_HEADS * A_DK
A_VAL_W = A_HEADS * A_DV
GATE_RANK = 16
GATE_TEMP = 16.0
GLA_CHUNK = 32
F_GROUPS = 4
F_GROUP_W = MIX_WIDTH // 16
F_WIDTH = F_GROUPS * F_GROUP_W
HEAD_DIM = 64
C_HEADS = MIX_WIDTH // 2 // HEAD_DIM
C_KV_HEADS = 2
C_GROUP = C_HEADS // C_KV_HEADS
C_Q_W = C_HEADS * HEAD_DIM
C_KV_W = C_KV_HEADS * HEAD_DIM
ROPE_THETA = 10000.0
Q_BLOCK = 128
N_EXPERTS = 32
TOP_K = 4
D_FF = D_MODEL
SWIGLU_LIMIT = 7.0
SWIGLU_ALPHA = 1.702
MOE_BLOCK = 128
EPS = 1e-6
PROJ_SIZES = (A_KEY_W, A_KEY_W, A_VAL_W, A_VAL_W, 2 * GATE_RANK, F_WIDTH, C_Q_W, C_KV_W, C_KV_W)
PROJ_COLS = sum(PROJ_SIZES)
PROJ_SPLITS = tuple(sum(PROJ_SIZES[: i + 1]) for i in range(len(PROJ_SIZES) - 1))

kernel_name = "hymba_gla_fnet_gqa_moe_prefix_step"


def rmsnorm(x, g):
    xf = x.astype(jnp.float32)
    xf = xf * lax.rsqrt(jnp.mean(xf * xf, axis=-1, keepdims=True) + EPS)
    return xf.astype(x.dtype) * g


def axial_rope(x):
    T = x.shape[1]
    n_rows = T // GRID_W
    rows = jnp.repeat(jnp.arange(n_rows), GRID_W).astype(jnp.float32)
    cols = jnp.tile(jnp.arange(GRID_W), n_rows).astype(jnp.float32)
    half = HEAD_DIM // 2
    nf = half // 2
    inv_freq = ROPE_THETA ** (-jnp.arange(nf, dtype=jnp.float32) / nf)

    def rot(xh, pos):
        ang = pos[:, None] * inv_freq[None, :]
        cos = jnp.concatenate([jnp.cos(ang)] * 2, -1)[None, :, None, :].astype(x.dtype)
        sin = jnp.concatenate([jnp.sin(ang)] * 2, -1)[None, :, None, :].astype(x.dtype)
        x1, x2 = jnp.split(xh, 2, axis=-1)
        return xh * cos + jnp.concatenate([-x2, x1], -1) * sin

    return jnp.concatenate([rot(x[..., :half], rows), rot(x[..., half:], cols)], -1)


def gla_chunked(q, k, v, log_a, s0):
    B, T, H, DK = q.shape
    DV = v.shape[-1]
    n = T // GLA_CHUNK
    f = lambda z: z.astype(jnp.float32).reshape(B, n, GLA_CHUNK, H, z.shape[-1])
    q, k, v, la = f(q) * (DK ** -0.5), f(k), f(v), f(log_a)
    b = jnp.cumsum(la, axis=2)
    b_last = b[:, :, -1]
    causal = jnp.tril(jnp.ones((GLA_CHUNK, GLA_CHUNK), bool))[:, :, None, None]
    decay = jnp.exp(jnp.where(causal, b[:, :, :, None] - b[:, :, None], -jnp.inf))
    scores = jnp.einsum("bnthd,bnshd,bntshd->bnhts", q, k, decay)
    o_intra = jnp.einsum("bnhts,bnshv->bnthv", scores, v)
    ds = jnp.einsum("bnshd,bnshv->bnhdv", k * jnp.exp(b_last[:, :, None] - b), v)

    def step(s, inp):
        ds_n, bl_n = inp
        return jnp.exp(bl_n)[..., None] * s + ds_n, s

    s_fin, s_in = lax.scan(step, s0.astype(jnp.float32), (jnp.moveaxis(ds, 1, 0), jnp.moveaxis(b_last, 1, 0)))
    o_inter = jnp.einsum("bnthd,nbhdv->bnthv", q * jnp.exp(b), s_in)
    return (o_intra + o_inter).reshape(B, T, H, DV), s_fin


def attend_blocks(q, k, v):
    B, T = q.shape[:2]
    nb = T // Q_BLOCK
    qb = jnp.moveaxis(q.reshape(B, nb, Q_BLOCK, C_KV_HEADS, C_GROUP, HEAD_DIM), 1, 0)
    scale = HEAD_DIM ** -0.5

    def one_block(qblk):
        s = jnp.einsum("bqkgd,bskd->bkgqs", qblk, k).astype(jnp.float32) * scale
        p = jax.nn.softmax(s, axis=-1).astype(v.dtype)
        return jnp.einsum("bkgqs,bskd->bqkgd", p, v)

    o = lax.map(one_block, qb)
    return jnp.moveaxis(o, 0, 1).reshape(B, T, C_Q_W)


def token_mixers(h, ctx, w_in, w_out, gla_gate_w, gla_gate_b, gla_norm, q_norm, k_norm):
    B, T, _ = h.shape
    pa_q, pa_k, pa_v, pa_g, pa_lr, pf, pc_q, pc_k, pc_v = jnp.split(h @ w_in, PROJ_SPLITS, axis=-1)

    q = pa_q.reshape(B, T, A_HEADS, A_DK)
    k = pa_k.reshape(B, T, A_HEADS, A_DK)
    v = pa_v.reshape(B, T, A_HEADS, A_DV)
    lr = pa_lr.reshape(B, T, 2, GATE_RANK)
    gate_logit = jnp.einsum("btjr,jrk->btjk", lr, gla_gate_w) + gla_gate_b
    log_a = (jax.nn.log_sigmoid(gate_logit.astype(jnp.float32)) / GATE_TEMP).reshape(B, T, 2, A_HEADS, A_DK)
    if ctx is None:
        s0 = jnp.zeros((B, 2, A_HEADS, A_DK, A_DV), jnp.float32)
    else:
        s0 = ctx[0]
    o_f, s_f = gla_chunked(q, k, v, log_a[:, :, 0], s0[:, 0])
    flip = lambda z: jnp.flip(z, axis=1)
    o_b, s_b = gla_chunked(flip(q), flip(k), flip(v), flip(log_a[:, :, 1]), s0[:, 1])
    o_a = (o_f + flip(o_b)).astype(h.dtype)
    o_a = rmsnorm(o_a, gla_norm).reshape(B, T, A_VAL_W) * jax.nn.silu(pa_g)

    fz = jnp.fft.fft2(pf.reshape(B, T, F_GROUPS, F_GROUP_W).astype(jnp.float32), axes=(1, 3), norm="ortho")
    o_f_mix = fz.real.astype(h.dtype).reshape(B, T, F_WIDTH)

    qa = rmsnorm(pc_q.reshape(B, T, C_HEADS, HEAD_DIM), q_norm)
    ka = rmsnorm(pc_k.reshape(B, T, C_KV_HEADS, HEAD_DIM), k_norm)
    va = pc_v.reshape(B, T, C_KV_HEADS, HEAD_DIM)
    if ctx is None:
        q_use, k_all, v_all = qa, ka, va
    else:
        q_use = axial_rope(qa)
        k_all = jnp.concatenate([axial_rope(ka), ctx[1]], axis=1)
        v_all = jnp.concatenate([va, ctx[2]], axis=1)
    o_c = attend_blocks(q_use.reshape(B, T, C_KV_HEADS, C_GROUP, HEAD_DIM), k_all, v_all)

    out = jnp.concatenate([o_a, o_f_mix, o_c], axis=-1) @ w_out
    if ctx is None:
        return out, (jnp.stack([s_f, s_b], axis=1).astype(h.dtype), ka, va)
    return out, None


def moe_ffn(h, router_w, router_b, w_gu, b_gu, w_dn, b_dn):
    T = h.shape[0]
    logits = (h @ router_w + router_b).astype(jnp.float32)
    top_v, top_i = lax.top_k(logits, TOP_K)
    top_w = jax.nn.softmax(top_v, axis=-1)
    A = T * TOP_K
    exp_a = top_i.reshape(A)
    tok_a = jnp.arange(A, dtype=jnp.int32) // TOP_K
    w_a = top_w.reshape(A)
    order = jnp.argsort(exp_a)
    e_s, tok_s, w_s = exp_a[order], tok_a[order], w_a[order]
    counts = jnp.bincount(exp_a, length=N_EXPERTS)
    start = jnp.cumsum(counts) - counts
    pcounts = (counts + MOE_BLOCK - 1) // MOE_BLOCK * MOE_BLOCK
    pend = jnp.cumsum(pcounts)
    pstart = pend - pcounts
    dest = pstart[e_s] + (jnp.arange(A, dtype=jnp.int32) - start[e_s])
    nb = -(-A // MOE_BLOCK) + N_EXPERTS
    tok_buf = jnp.zeros((nb * MOE_BLOCK,), jnp.int32).at[dest].set(tok_s)
    w_buf = jnp.zeros((nb * MOE_BLOCK,), jnp.float32).at[dest].set(w_s)
    blk_exp = jnp.minimum(jnp.searchsorted(pend, jnp.arange(nb) * MOE_BLOCK, side="right"), N_EXPERTS - 1)

    def expert_block(args):
        tok, e = args
        gu = h[tok] @ w_gu[e] + b_gu[e]
        gate, up = jnp.split(gu, 2, axis=-1)
        gate = jnp.minimum(gate, SWIGLU_LIMIT)
        up = jnp.clip(up, -SWIGLU_LIMIT, SWIGLU_LIMIT)
        glu = gate * jax.nn.sigmoid(gate * SWIGLU_ALPHA)
        return ((up + 1) * glu) @ w_dn[e] + b_dn[e]

    y = lax.map(expert_block, (tok_buf.reshape(nb, MOE_BLOCK), blk_exp))
    y = y.reshape(nb * MOE_BLOCK, D_MODEL) * w_buf[:, None].astype(y.dtype)
    return jnp.zeros_like(h).at[tok_buf].add(y)


def layer(x, cond, ctx, w_ada, b_ada, norm_mix, norm_ffn, w_in, w_out, gla_gate_w, gla_gate_b, gla_norm,
          q_norm, k_norm, router_w, router_b, expert_w_gu, expert_b_gu, expert_w_dn, expert_b_dn):
    mod = (jax.nn.silu(cond) @ w_ada + b_ada)[:, None, :]
    shift1, scale1, gate1, shift2, scale2, gate2 = jnp.split(mod, 6, axis=-1)
    h = rmsnorm(x, norm_mix) * (1 + scale1) + shift1
    mixed, st = token_mixers(h, ctx, w_in, w_out, gla_gate_w, gla_gate_b, gla_norm, q_norm, k_norm)
    x = x + gate1 * mixed
    h = rmsnorm(x, norm_ffn) * (1 + scale2) + shift2
    ff = moe_ffn(h.reshape(-1, D_MODEL), router_w, router_b, expert_w_gu, expert_b_gu, expert_w_dn, expert_b_dn)
    x = x + gate2 * ff.reshape(x.shape)
    return x, st


def setup_inputs(seed: int = 0) -> dict:
    key = jax.random.key(seed)
    ks = jax.random.split(key, 32)
    nrm = lambda k, shape, s: jax.random.normal(k, shape, jnp.float32) * s
    return {
        "x_prompt": nrm(ks[0], (BATCH, SEQ, D_MODEL), 1.0),
        "x_sample": nrm(ks[1], (DEC_BATCH, DEC_SEQ, D_MODEL), 1.0),
        "state_gla": nrm(ks[2], (DEC_BATCH, DEPTH, 2, A_HEADS, A_DK, A_DV), 0.5),
        "cache_k": nrm(ks[3], (DEC_BATCH, DEPTH, PAST_LEN, C_KV_HEADS, HEAD_DIM), 1.0),
        "cache_v": nrm(ks[4], (DEC_BATCH, DEPTH, PAST_LEN, C_KV_HEADS, HEAD_DIM), 1.0),
        "c": nrm(ks[5], (DEC_BATCH, D_MODEL), 1.0),
        "c_ctx": nrm(ks[6], (D_MODEL,), 1.0),
        "w_ada": nrm(ks[7], (DEPTH, D_MODEL, 6 * D_MODEL), 0.02),
        "b_ada": nrm(ks[8], (DEPTH, 6 * D_MODEL), 0.01),
        "norm_mix": 1.0 + nrm(ks[9], (DEPTH, D_MODEL), 0.01),
        "norm_ffn": 1.0 + nrm(ks[10], (DEPTH, D_MODEL), 0.01),
        "norm_final": 1.0 + nrm(ks[11], (D_MODEL,), 0.01),
        "w_in": nrm(ks[12], (DEPTH, D_MODEL, PROJ_COLS), D_MODEL ** -0.5),
        "w_out": nrm(ks[13], (DEPTH, MIX_WIDTH, D_MODEL), MIX_WIDTH ** -0.5),
        "gla_gate_w": nrm(ks[14], (DEPTH, 2, GATE_RANK, A_KEY_W), GATE_RANK ** -0.5),
        "gla_gate_b": nrm(ks[15], (DEPTH, 2, A_KEY_W), 0.1),
        "gla_norm": 1.0 + nrm(ks[16], (DEPTH, A_DV), 0.01),
        "q_norm": 1.0 + nrm(ks[17], (DEPTH, HEAD_DIM), 0.01),
        "k_norm": 1.0 + nrm(ks[18], (DEPTH, HEAD_DIM), 0.01),
        "router_w": nrm(ks[19], (DEPTH, D_MODEL, N_EXPERTS), D_MODEL ** -0.5),
        "router_b": nrm(ks[20], (DEPTH, N_EXPERTS), 0.01),
        "expert_w_gu": nrm(ks[21], (DEPTH, N_EXPERTS, D_MODEL, 2 * D_FF), D_MODEL ** -0.5),
        "expert_b_gu": nrm(ks[22], (DEPTH, N_EXPERTS, 2 * D_FF), 0.01),
        "expert_w_dn": nrm(ks[23], (DEPTH, N_EXPERTS, D_FF, D_MODEL), D_FF ** -0.5),
        "expert_b_dn": nrm(ks[24], (DEPTH, N_EXPERTS, D_MODEL), 0.01),
    }


def reference(x_prompt, x_sample, state_gla, cache_k, cache_v, c, c_ctx, w_ada, b_ada, norm_mix, norm_ffn,
              norm_final, w_in, w_out, gla_gate_w, gla_gate_b, gla_norm, q_norm, k_norm, router_w, router_b,
              expert_w_gu, expert_b_gu, expert_w_dn, expert_b_dn):
    xp, xs = x_prompt, x_sample
    new_s, new_k, new_v = [], [], []
    for l in range(DEPTH):
        lw = (w_ada[l], b_ada[l], norm_mix[l], norm_ffn[l], w_in[l], w_out[l], gla_gate_w[l], gla_gate_b[l],
              gla_norm[l], q_norm[l], k_norm[l], router_w[l], router_b[l], expert_w_gu[l], expert_b_gu[l],
              expert_w_dn[l], expert_b_dn[l])
        xp, (s_l, k_l, v_l) = layer(xp, c_ctx[None, :], None, *lw)
        new_s.append(s_l)
        new_k.append(k_l)
        new_v.append(v_l)
        xs, _ = layer(xs, c, (state_gla[:, l], cache_k[:, l], cache_v[:, l]), *lw)
    y_prompt = rmsnorm(xp, norm_final)
    y_sample = rmsnorm(xs, norm_final)
    new_state_gla = jnp.stack(new_s, axis=1)
    new_cache_k = jnp.stack(new_k, axis=1)
    new_cache_v = jnp.stack(new_v, axis=1)
    return (y_prompt, y_sample, new_state_gla, new_cache_k, new_cache_v)
```

```python
import functools

import jax
import jax.numpy as jnp
from jax import lax
from jax.experimental import pallas as pl
from jax.experimental.pallas import tpu as pltpu

D_MODEL = 1024
BATCH = 32
SEQ = 256
DEPTH = 2
DEC_BATCH = 8
DEC_SEQ = 1024
PAST_LEN = 512
GRID_W = 64
A_HEADS = 4
A_DV = 64
A_DK = 32
A_KEY_W = 128
A_VAL_W = 256
GATE_RANK = 16
GATE_TEMP = 16.0
GLA_CHUNK = 32
F_GROUPS = 4
F_GROUP_W = 64
F_WIDTH = 256
HEAD_DIM = 64
C_HEADS = 8
C_KV_HEADS = 2
C_Q_W = 512
C_KV_W = 128
ROPE_THETA = 10000.0
N_EXPERTS = 32
TOP_K = 4
D_FF = 1024
SWIGLU_LIMIT = 7.0
SWIGLU_ALPHA = 1.702
EPS = 1e-6

N_CTX = BATCH * SEQ
N_DEC = DEC_BATCH * DEC_SEQ
N_TOK = N_CTX + N_DEC
MOD_ROWS = 16

COL_GLA = 0
COL_PF = 768
COL_CQ = 1024
COL_CK = 1536
COL_CV = 1664
COL_LR = 1792
PROJ_W = 1920

LANES = 128
TM = 512
GLA_TILE = 256
Q_TILE = 256
MOE_BM = 256
MOE_NB = N_TOK * TOP_K // MOE_BM + N_EXPERTS
MOE_ROWS = MOE_NB * MOE_BM
ROW_TILE = 256
VMEM_LIMIT = 56 * 1024 * 1024

F32 = jnp.float32
BF16 = jnp.bfloat16


def _params(sem, vmem=VMEM_LIMIT):
    return pltpu.CompilerParams(dimension_semantics=sem, vmem_limit_bytes=vmem)


def _bdot(a, b):
    return jnp.dot(a.astype(BF16), b.astype(BF16), preferred_element_type=F32)


def _split(a):
    hi = a.astype(BF16)
    lo = (a - hi.astype(F32)).astype(BF16)
    return hi, lo


def _dot3(a, b):
    ah, al = _split(a)
    bh, bl = _split(b)
    d = functools.partial(jnp.dot, preferred_element_type=F32)
    return d(ah, bh) + (d(ah, bl) + d(al, bh))


def _dot2(a, b16):
    ah, al = _split(a)
    d = functools.partial(jnp.dot, preferred_element_type=F32)
    return d(ah, b16) + d(al, b16)


def _sigmoid(x):
    return 1.0 / (1.0 + jnp.exp(-x))


def _mod_row(i, tm):
    tok = i * tm
    return jnp.where(tok < N_CTX, 0, 1 + (tok - N_CTX) // DEC_SEQ)


def _ada_body(c_ref, w_ref, b_ref, o_ref):
    c = c_ref[...]
    o_ref[0] = _dot3(c * _sigmoid(c), w_ref[0]) + b_ref[0]


def _ada(cond, w_ada, b_ada):
    return pl.pallas_call(
        _ada_body,
        grid=(DEPTH, 6),
        in_specs=[
            pl.BlockSpec((MOD_ROWS, D_MODEL), lambda l, j: (0, 0)),
            pl.BlockSpec((1, D_MODEL, D_MODEL), lambda l, j: (l, 0, j)),
            pl.BlockSpec((1, 1, D_MODEL), lambda l, j: (l, 0, j)),
        ],
        out_specs=pl.BlockSpec((1, MOD_ROWS, D_MODEL), lambda l, j: (l, 0, j)),
        out_shape=jax.ShapeDtypeStruct((DEPTH, MOD_ROWS, 6 * D_MODEL), F32),
        compiler_params=_params(("parallel", "parallel")),
        name="ada",
    )(cond, w_ada, b_ada.reshape(DEPTH, 1, 6 * D_MODEL))


def _rms(x):
    return x * lax.rsqrt(jnp.mean(x * x, axis=-1, keepdims=True) + EPS)


def _inproj_body(x_ref, mod_ref, g_ref, w_ref, o_ref):
    mod = mod_ref[0]
    h = _rms(x_ref[...]) * g_ref[...] * (1.0 + mod[1:2]) + mod[0:1]
    o_ref[...] = jnp.dot(h.astype(BF16), w_ref[...], preferred_element_type=F32)


def _inproj(x, mod3, gain, w):
    return pl.pallas_call(
        _inproj_body,
        grid=(N_TOK // TM,),
        in_specs=[
            pl.BlockSpec((TM, D_MODEL), lambda i: (i, 0)),
            pl.BlockSpec((1, 6, D_MODEL), lambda i: (_mod_row(i, TM), 0, 0)),
            pl.BlockSpec((1, D_MODEL), lambda i: (0, 0)),
            pl.BlockSpec((D_MODEL, PROJ_W), lambda i: (0, 0)),
        ],
        out_specs=pl.BlockSpec((TM, PROJ_W), lambda i: (i, 0)),
        out_shape=jax.ShapeDtypeStruct((N_TOK, PROJ_W), F32),
        compiler_params=_params(("parallel",)),
        name="inproj",
    )(x, mod3, gain, w)


def _log_sigmoid(x):
    return jnp.minimum(x, 0.0) - jnp.log(1.0 + jnp.exp(-jnp.abs(x)))


def _chunk_cumsum(x, pos, forward):
    rows = x.shape[0]
    for s in (1, 2, 4, 8, 16):
        if forward:
            x = x + jnp.where(pos >= s, pltpu.roll(x, s, 0), 0.0)
        else:
            x = x + jnp.where(pos < GLA_CHUNK - s, pltpu.roll(x, rows - s, 0), 0.0)
    return x


def _gla_tile(q, k, v16, la, st, forward):
    R = GLA_TILE
    nch = R // GLA_CHUNK
    pos = lax.broadcasted_iota(jnp.int32, (R, 1), 0) & (GLA_CHUNK - 1)
    b = _chunk_cumsum(la, pos, forward)
    b3 = b.reshape(nch, GLA_CHUNK, A_KEY_W)
    tot3 = b3[:, GLA_CHUNK - 1:GLA_CHUNK, :] if forward else b3[:, 0:1, :]
    totb = jnp.broadcast_to(tot3, (nch, GLA_CHUNK, A_KEY_W)).reshape(R, A_KEY_W)
    qt = q * (A_DK ** -0.5) * jnp.exp(b)
    kt = (k * jnp.exp(-b)).astype(BF16)
    kend = (k * jnp.exp(totb - b)).astype(BF16)
    dec3 = jnp.exp(tot3)

    ri = lax.broadcasted_iota(jnp.int32, (R, R), 0)
    ci = lax.broadcasted_iota(jnp.int32, (R, R), 1)
    tri = (ci <= ri) if forward else (ci >= ri)
    mask = ((ri >> 5) == (ci >> 5)) & tri
    lane_k = lax.broadcasted_iota(jnp.int32, (R, A_KEY_W), 1) >> 5
    lane_v = lax.broadcasted_iota(jnp.int32, (R, A_VAL_W), 1) >> 6

    o = jnp.zeros((R, A_VAL_W), F32)
    for h in range(A_HEADS):
        qh = jnp.where(lane_k == h, qt, 0.0).astype(BF16)
        sc = lax.dot_general(qh, kt, (((1,), (1,)), ((), ())), preferred_element_type=F32)
        p = jnp.where(mask, sc, 0.0).astype(BF16)
        oh = jnp.dot(p, v16, preferred_element_type=F32)
        o = jnp.where(lane_v == h, oh, o)

    sr = lax.broadcasted_iota(jnp.int32, (A_VAL_W, A_KEY_W), 0) >> 6
    sl = lax.broadcasted_iota(jnp.int32, (A_VAL_W, A_KEY_W), 1) >> 5
    smask = sr == sl
    qt16 = qt.astype(BF16)
    inter = [None] * nch
    order = range(nch) if forward else range(nch - 1, -1, -1)
    for c in order:
        lo = c * GLA_CHUNK
        inter[c] = lax.dot_general(qt16[lo:lo + GLA_CHUNK], st.astype(BF16),
                                   (((1,), (1,)), ((), ())), preferred_element_type=F32)
        ds = lax.dot_general(v16[lo:lo + GLA_CHUNK], kend[lo:lo + GLA_CHUNK],
                             (((0,), (0,)), ((), ())), preferred_element_type=F32)
        st = st * dec3[c] + jnp.where(smask, ds, 0.0)
    return o + jnp.concatenate(inter, axis=0), st


def _gla_body(gla_ref, lr_ref, wg_ref, gb_ref, s0_ref, gn_ref, gm_ref, o_ref, sfin_ref, la_scr, *, seq):
    ntile = seq // GLA_TILE
    logit = _bdot(lr_ref[...], wg_ref[...]) + gb_ref[...]
    la_scr[...] = _log_sigmoid(logit) * (1.0 / GATE_TEMP)

    def run(forward):
        d = 0 if forward else 1

        def step(i, st):
            t = i if forward else ntile - 1 - i
            r0 = pl.multiple_of(t * GLA_TILE, GLA_TILE)
            rows = pl.ds(r0, GLA_TILE)
            q = gla_ref[rows, 0:128]
            k = gla_ref[rows, 128:256]
            v16 = gla_ref[rows, 256:512].astype(BF16)
            la = la_scr[rows, d * A_KEY_W:(d + 1) * A_KEY_W]
            o, st = _gla_tile(q, k, v16, la, st, forward)
            if forward:
                o_ref[rows, :] = o
            else:
                o_ref[rows, :] += o
            return st

        sfin_ref[0, d] = lax.fori_loop(0, ntile, step, s0_ref[0, d])

    run(True)
    run(False)
    o = o_ref[...]
    ms = _dot2(o * o, gm_ref[...])
    g = gla_ref[:, 512:768]
    o_ref[...] = o * lax.rsqrt(ms + EPS) * gn_ref[...] * (g * _sigmoid(g))


def _gla(proj, wg, gb, s0t, gnorm, gmat, *, nseq, seq, row_block0):
    return pl.pallas_call(
        functools.partial(_gla_body, seq=seq),
        grid=(nseq,),
        in_specs=[
            pl.BlockSpec((seq, 768), lambda b: (row_block0 + b, 0)),
            pl.BlockSpec((seq, LANES), lambda b: (row_block0 + b, COL_LR // LANES)),
            pl.BlockSpec((LANES, 2 * A_KEY_W), lambda b: (0, 0)),
            pl.BlockSpec((1, 2 * A_KEY_W), lambda b: (0, 0)),
            pl.BlockSpec((1, 2, A_VAL_W, A_KEY_W), lambda b: (b, 0, 0, 0)),
            pl.BlockSpec((1, A_VAL_W), lambda b: (0, 0)),
            pl.BlockSpec((A_VAL_W, A_VAL_W), lambda b: (0, 0)),
        ],
        out_specs=[
            pl.BlockSpec((seq, A_VAL_W), lambda b: (b, 0)),
            pl.BlockSpec((1, 2, A_VAL_W, A_KEY_W), lambda b: (b, 0, 0, 0)),
        ],
        out_shape=[
            jax.ShapeDtypeStruct((nseq * seq, A_VAL_W), F32),
            jax.ShapeDtypeStruct((nseq, 2, A_VAL_W, A_KEY_W), F32),
        ],
        scratch_shapes=[pltpu.VMEM((seq, 2 * A_KEY_W), F32)],
        compiler_params=_params(("parallel",)),
        name=f"gla{seq}",
    )(proj, proj, wg, gb, s0t, gnorm, gmat)


def _fnet_body(x_ref, cb_ref, sb_ref, ct_ref, st_ref, o_ref):
    x = x_ref[...].astype(BF16)
    xc = jnp.dot(x, cb_ref[...], preferred_element_type=F32).astype(BF16)
    xs = jnp.dot(x, sb_ref[...], preferred_element_type=F32).astype(BF16)
    o_ref[...] = (jnp.dot(ct_ref[...], xc, preferred_element_type=F32)
                  - jnp.dot(st_ref[...], xs, preferred_element_type=F32))


def _fnet(proj, cb, sb, ct, st, *, nseq, seq, row_block0):
    return pl.pallas_call(
        _fnet_body,
        grid=(nseq,),
        in_specs=[
            pl.BlockSpec((seq, F_WIDTH), lambda b: (row_block0 + b, COL_PF // F_WIDTH)),
            pl.BlockSpec((F_WIDTH, F_WIDTH), lambda b: (0, 0)),
            pl.BlockSpec((F_WIDTH, F_WIDTH), lambda b: (0, 0)),
            pl.BlockSpec((seq, seq), lambda b: (0, 0)),
            pl.BlockSpec((seq, seq), lambda b: (0, 0)),
        ],
        out_specs=pl.BlockSpec((seq, F_WIDTH), lambda b: (b, 0)),
        out_shape=jax.ShapeDtypeStruct((nseq * seq, F_WIDTH), F32),
        compiler_params=_params(("parallel",)),
        name=f"fnet{seq}",
    )(proj, cb, sb, ct, st)


def _dft_mats(n, scale):
    j = lax.broadcasted_iota(jnp.int32, (n, n), 0)
    k = lax.broadcasted_iota(jnp.int32, (n, n), 1)
    ang = ((j * k) % n).astype(F32) * (2.0 * jnp.pi / n)
    return jnp.cos(ang) * scale, jnp.sin(ang) * scale


def _head_rms(x, gm16, gain):
    ms = _dot2(x * x, gm16)
    return x * lax.rsqrt(ms + EPS) * gain


def _rope(x, cos, sin_signed):
    w = x.shape[1]
    lane = lax.broadcasted_iota(jnp.int32, x.shape, 1)
    partner = jnp.where((lane & 31) < 16, pltpu.roll(x, w - 16, 1), pltpu.roll(x, 16, 1))
    return x * cos + partner * sin_signed


def _attn_body(*refs, rope, n_new, n_past):
    if n_past:
        (q_ref, k_ref, v_ref, ck_ref, cv_ref, cq_ref, sq_ref, cosk_ref, sink_ref,
         qn_ref, kn_ref, gm_ref, o_ref, ka_ref, k_scr, ks_scr, v_scr, vs_scr) = refs
    else:
        (q_ref, k_ref, v_ref, qn_ref, kn_ref, gm_ref, o_ref, ka_ref,
         k_scr, ks_scr, v_scr, vs_scr) = refs
    gm = gm_ref[...]

    @pl.when(pl.program_id(1) == 0)
    def _():
        kn = _head_rms(k_ref[...], gm, kn_ref[...])
        ka_ref[...] = kn
        if rope:
            kn = _rope(kn, cosk_ref[...], sink_ref[...])
        v = v_ref[...]
        new = pl.ds(0, n_new)
        k_scr[new, :] = kn.astype(BF16)
        ks_scr[new, :] = pltpu.roll(kn, HEAD_DIM, 1).astype(BF16)
        v_scr[new, :] = v.astype(BF16)
        vs_scr[new, :] = pltpu.roll(v, HEAD_DIM, 1).astype(BF16)
        if n_past:
            past = pl.ds(n_new, n_past)
            ck = ck_ref[0]
            cv = cv_ref[0]
            k_scr[past, :] = ck.astype(BF16)
            ks_scr[past, :] = pltpu.roll(ck, HEAD_DIM, 1).astype(BF16)
            v_scr[past, :] = cv.astype(BF16)
            vs_scr[past, :] = pltpu.roll(cv, HEAD_DIM, 1).astype(BF16)

    lane = lax.broadcasted_iota(jnp.int32, (Q_TILE, LANES), 1)
    low = lane < HEAD_DIM
    for jp in range(C_Q_W // LANES):
        kv = jp // 2
        q2 = _head_rms(q_ref[:, jp * LANES:(jp + 1) * LANES], gm, qn_ref[...])
        if rope:
            q2 = _rope(q2, cq_ref[...], sq_ref[...])
        q2 = q2 * (HEAD_DIM ** -0.5)
        halves = []
        for half in range(2):
            swapped = (half == 1) != (kv == 1)
            kk = ks_scr[...] if swapped else k_scr[...]
            vv = vs_scr[...] if swapped else v_scr[...]
            qm = jnp.where(low if half == 0 else ~low, q2, 0.0).astype(BF16)
            s = lax.dot_general(qm, kk, (((1,), (1,)), ((), ())), preferred_element_type=F32)
            p = jnp.exp(s - jnp.max(s, axis=-1, keepdims=True))
            den = jnp.sum(p, axis=-1, keepdims=True)
            halves.append(jnp.dot(p.astype(BF16), vv, preferred_element_type=F32) / den)
        o_ref[:, jp * LANES:(jp + 1) * LANES] = jnp.where(low, halves[0], halves[1])


def _attn(proj, qn, kn, gm, *, nseq, seq, row_block0, past=None, tables=None):
    nq = seq // Q_TILE
    n_past = 0 if past is None else past[0].shape[1]
    qrow = lambda b, j: ((row_block0 + b) * nq + j, COL_CQ // C_Q_W)
    in_specs = [
        pl.BlockSpec((Q_TILE, C_Q_W), qrow),
        pl.BlockSpec((seq, C_KV_W), lambda b, j: (row_block0 + b, COL_CK // C_KV_W)),
        pl.BlockSpec((seq, C_KV_W), lambda b, j: (row_block0 + b, COL_CV // C_KV_W)),
    ]
    args = [proj, proj, proj]
    if past is not None:
        in_specs += [pl.BlockSpec((1, n_past, C_KV_W), lambda b, j: (b, 0, 0))] * 2
        in_specs += [pl.BlockSpec((Q_TILE, LANES), lambda b, j: (j, 0))] * 2
        in_specs += [pl.BlockSpec((seq, LANES), lambda b, j: (0, 0))] * 2
        args += [past[0], past[1], tables[0], tables[1], tables[0], tables[1]]
    in_specs += [
        pl.BlockSpec((1, LANES), lambda b, j: (0, 0)),
        pl.BlockSpec((1, LANES), lambda b, j: (0, 0)),
        pl.BlockSpec((LANES, LANES), lambda b, j: (0, 0)),
    ]
    args += [qn, kn, gm]
    s_all = seq + n_past
    return pl.pallas_call(
        functools.partial(_attn_body, rope=past is not None, n_new=seq, n_past=n_past),
        grid=(nseq, nq),
        in_specs=in_specs,
        out_specs=[
            pl.BlockSpec((Q_TILE, C_Q_W), lambda b, j: (b * nq + j, 0)),
            pl.BlockSpec((seq, C_KV_W), lambda b, j: (b, 0)),
        ],
        out_shape=[
            jax.ShapeDtypeStruct((nseq * seq, C_Q_W), F32),
            jax.ShapeDtypeStruct((nseq * seq, C_KV_W), F32),
        ],
        scratch_shapes=[pltpu.VMEM((s_all, C_KV_W), BF16)] * 4,
        compiler_params=_params(("parallel", "arbitrary")),
        name=f"attn{seq}",
    )(*args)


def _rope_tables():
    t = jnp.arange(DEC_SEQ)
    rows = (t // GRID_W).astype(F32)
    cols = (t % GRID_W).astype(F32)
    nf = HEAD_DIM // 4
    inv_freq = ROPE_THETA ** (-jnp.arange(nf, dtype=F32) / nf)
    d = jnp.arange(LANES)
    pos = jnp.where(((d % HEAD_DIM) < HEAD_DIM // 2)[None, :], rows[:, None], cols[:, None])
    ang = pos * inv_freq[d % nf][None, :]
    sign = jnp.where((d % 32) < 16, -1.0, 1.0)[None, :]
    return jnp.cos(ang), jnp.sin(ang) * sign


def _outproj_body(oa_ref, of_ref, oc_ref, w_ref, x_ref, mod_ref, g_ref, rw_ref, rb_ref,
                  x1_ref, h2_ref, meta_ref, sel_ref):
    mod = mod_ref[0]
    mixed = (jnp.dot(oa_ref[...].astype(BF16), w_ref[0:256, :], preferred_element_type=F32)
             + jnp.dot(of_ref[...].astype(BF16), w_ref[256:512, :], preferred_element_type=F32)
             + jnp.dot(oc_ref[...].astype(BF16), w_ref[512:1024, :], preferred_element_type=F32))
    x1 = x_ref[...] + mod[2:3] * mixed
    x1_ref[...] = x1
    h2 = _rms(x1) * g_ref[...] * (1.0 + mod[4:5]) + mod[3:4]
    h2_ref[...] = h2
    lane = lax.broadcasted_iota(jnp.int32, (TM, LANES), 1)
    logits = jnp.where(lane < N_EXPERTS, _dot3(h2, rw_ref[...]) + rb_ref[...], -jnp.inf)
    sel = jnp.zeros((TM, LANES), F32)
    meta = jnp.zeros((TM, LANES), F32)
    tops = []
    for k in range(TOP_K):
        m = jnp.max(logits, axis=-1, keepdims=True)
        idx = jnp.min(jnp.where(logits == m, lane, LANES), axis=-1, keepdims=True)
        hit = lane == idx
        sel = jnp.where(hit, 1.0, sel)
        logits = jnp.where(hit, -jnp.inf, logits)
        meta = jnp.where(lane == k, idx.astype(F32), meta)
        tops.append(m)
    ex = [jnp.exp(t - tops[0]) for t in tops]
    den = ex[0] + ex[1] + ex[2] + ex[3]
    for k in range(TOP_K):
        meta = jnp.where(lane == TOP_K + k, ex[k] / den, meta)
    meta_ref[...] = meta
    sel_ref[...] = sel


def _outproj(oa, of, oc, w, x, mod3, gain, rw, rb):
    tok = lambda w_: pl.BlockSpec((TM, w_), lambda i: (i, 0))
    full = lambda a, b: pl.BlockSpec((a, b), lambda i: (0, 0))
    return pl.pallas_call(
        _outproj_body,
        grid=(N_TOK // TM,),
        in_specs=[tok(A_VAL_W), tok(F_WIDTH), tok(C_Q_W), full(D_MODEL, D_MODEL), tok(D_MODEL),
                  pl.BlockSpec((1, 6, D_MODEL), lambda i: (_mod_row(i, TM), 0, 0)),
                  full(1, D_MODEL), full(D_MODEL, LANES), full(1, LANES)],
        out_specs=[tok(D_MODEL), tok(D_MODEL), tok(LANES), tok(LANES)],
        out_shape=[jax.ShapeDtypeStruct((N_TOK, D_MODEL), F32),
                   jax.ShapeDtypeStruct((N_TOK, D_MODEL), F32),
                   jax.ShapeDtypeStruct((N_TOK, LANES), F32),
                   jax.ShapeDtypeStruct((N_TOK, LANES), F32)],
        compiler_params=_params(("parallel",)),
        name="outproj",
    )(oa, of, oc, w, x, mod3, gain, rw, rb)


def _pos_body(sel_ref, meta_ref, pos_ref, cnt_ref, carry):
    i = pl.program_id(0)

    @pl.when(i == 0)
    def _():
        carry[...] = jnp.zeros_like(carry)

    sel = sel_ref[...]
    ri = lax.broadcasted_iota(jnp.int32, (TM, TM), 0)
    ci = lax.broadcasted_iota(jnp.int32, (TM, TM), 1)
    lower = jnp.where(ci < ri, 1.0, 0.0).astype(BF16)
    before = jnp.dot(lower, sel.astype(BF16), preferred_element_type=F32) + carry[...]
    lane = lax.broadcasted_iota(jnp.int32, (TM, LANES), 1)
    meta = meta_ref[...]
    out = jnp.zeros((TM, LANES), F32)
    for k in range(TOP_K):
        idx = meta[:, k:k + 1].astype(jnp.int32)
        pk = jnp.sum(jnp.where(lane == idx, before, 0.0), axis=-1, keepdims=True)
        out = jnp.where(lane == k, pk, out)
    pos_ref[...] = out
    carry[...] += jnp.sum(sel, axis=0, keepdims=True)
    cnt_ref[...] = carry[...]


def _positions(sel, meta):
    return pl.pallas_call(
        _pos_body,
        grid=(N_TOK // TM,),
        in_specs=[pl.BlockSpec((TM, LANES), lambda i: (i, 0))] * 2,
        out_specs=[pl.BlockSpec((TM, LANES), lambda i: (i, 0)),
                   pl.BlockSpec((1, LANES), lambda i: (0, 0))],
        out_shape=[jax.ShapeDtypeStruct((N_TOK, LANES), F32),
                   jax.ShapeDtypeStruct((1, LANES), F32)],
        scratch_shapes=[pltpu.VMEM((1, LANES), F32)],
        compiler_params=_params(("arbitrary",)),
        name="positions",
    )(sel, meta)


def _dispatch_body(dest_ref, h_hbm, init_hbm, xs_hbm, sem):
    del init_hbm
    base = pl.program_id(0) * ROW_TILE

    def copy(r, k):
        d = dest_ref[r * TOP_K + k]
        return pltpu.make_async_copy(h_hbm.at[pl.ds(base + r, 1)], xs_hbm.at[pl.ds(d, 1)], sem)

    def start(r, c):
        for k in range(TOP_K):
            copy(r, k).start()
        return c

    def wait(r, c):
        for k in range(TOP_K):
            copy(r, k).wait()
        return c

    lax.fori_loop(0, ROW_TILE, start, 0)
    lax.fori_loop(0, ROW_TILE, wait, 0)


def _dispatch(dest_flat, h2, init):
    return pl.pallas_call(
        _dispatch_body,
        grid=(N_TOK // ROW_TILE,),
        in_specs=[pl.BlockSpec((ROW_TILE * TOP_K,), lambda i: (i,), memory_space=pltpu.SMEM),
                  pl.BlockSpec(memory_space=pl.ANY),
                  pl.BlockSpec(memory_space=pl.ANY)],
        out_specs=pl.BlockSpec(memory_space=pl.ANY),
        out_shape=jax.ShapeDtypeStruct((MOE_ROWS, D_MODEL), F32),
        scratch_shapes=[pltpu.SemaphoreType.DMA],
        input_output_aliases={2: 0},
        compiler_params=_params(("arbitrary",)),
        name="dispatch",
    )(dest_flat, h2, init)


def _experts_body(bexp_ref, nused_ref, x_ref, wgu_ref, bgu_ref, wdn_ref, bdn_ref, y_ref):
    del bexp_ref
    i = pl.program_id(0)

    @pl.when(i < nused_ref[0])
    def _():
        gu = jnp.dot(x_ref[...].astype(BF16), wgu_ref[0], preferred_element_type=F32) + bgu_ref[0]
        gate = jnp.minimum(gu[:, :D_FF], SWIGLU_LIMIT)
        up = jnp.clip(gu[:, D_FF:], -SWIGLU_LIMIT, SWIGLU_LIMIT)
        glu = gate * _sigmoid(gate * SWIGLU_ALPHA)
        act = ((up + 1.0) * glu).astype(BF16)
        y_ref[...] = jnp.dot(act, wdn_ref[0], preferred_element_type=F32) + bdn_ref[0]

    @pl.when(i >= nused_ref[0])
    def _():
        y_ref[...] = jnp.zeros_like(y_ref)


def _experts(blk_exp, nused, xs, wgu, bgu, wdn, bdn):
    e = lambda i, be, nu: (be[i], 0, 0)
    grid_spec = pltpu.PrefetchScalarGridSpec(
        num_scalar_prefetch=2,
        grid=(MOE_NB,),
        in_specs=[
            pl.BlockSpec((MOE_BM, D_MODEL), lambda i, be, nu: (i, 0)),
            pl.BlockSpec((1, D_MODEL, 2 * D_FF), e),
            pl.BlockSpec((1, 1, 2 * D_FF), e),
            pl.BlockSpec((1, D_FF, D_MODEL), e),
            pl.BlockSpec((1, 1, D_MODEL), e),
        ],
        out_specs=pl.BlockSpec((MOE_BM, D_MODEL), lambda i, be, nu: (i, 0)),
    )
    return pl.pallas_call(
        _experts_body,
        grid_spec=grid_spec,
        out_shape=jax.ShapeDtypeStruct((MOE_ROWS, D_MODEL), F32),
        compiler_params=_params(("arbitrary",)),
        name="experts",
    )(blk_exp, nused, xs, wgu, bgu, wdn, bdn)


def _combine_body(dest_ref, y_hbm, meta_ref, x1_ref, mod_ref, nf_ref, o_ref, buf, sem, *, final):
    def copy(r, k):
        d = dest_ref[r * TOP_K + k]
        return pltpu.make_async_copy(y_hbm.at[pl.ds(d, 1)], buf.at[k, pl.ds(r, 1)], sem)

    def start(r, c):
        for k in range(TOP_K):
            copy(r, k).start()
        return c

    def wait(r, c):
        for k in range(TOP_K):
            copy(r, k).wait()
        return c

    lax.fori_loop(0, ROW_TILE, start, 0)
    lax.fori_loop(0, ROW_TILE, wait, 0)
    meta = meta_ref[...]
    ff = buf[0] * meta[:, TOP_K:TOP_K + 1]
    for k in range(1, TOP_K):
        ff = ff + buf[k] * meta[:, TOP_K + k:TOP_K + k + 1]
    x2 = x1_ref[...] + mod_ref[0][5:6] * ff
    if final:
        x2 = _rms(x2) * nf_ref[...]
    o_ref[...] = x2


def _combine(dest_flat, y, meta, x1, mod3, nfinal, *, final):
    return pl.pallas_call(
        functools.partial(_combine_body, final=final),
        grid=(N_TOK // ROW_TILE,),
        in_specs=[pl.BlockSpec((ROW_TILE * TOP_K,), lambda i: (i,), memory_space=pltpu.SMEM),
                  pl.BlockSpec(memory_space=pl.ANY),
                  pl.BlockSpec((ROW_TILE, LANES), lambda i: (i, 0)),
                  pl.BlockSpec((ROW_TILE, D_MODEL), lambda i: (i, 0)),
                  pl.BlockSpec((1, 6, D_MODEL), lambda i: (_mod_row(i, ROW_TILE), 0, 0)),
                  pl.BlockSpec((1, D_MODEL), lambda i: (0, 0))],
        out_specs=pl.BlockSpec((ROW_TILE, D_MODEL), lambda i: (i, 0)),
        out_shape=jax.ShapeDtypeStruct((N_TOK, D_MODEL), F32),
        scratch_shapes=[pltpu.VMEM((TOP_K, ROW_TILE, D_MODEL), F32), pltpu.SemaphoreType.DMA],
        compiler_params=_params(("arbitrary",)),
        name="combine",
    )(dest_flat, y, meta, x1, mod3, nfinal)


def _state_to_blockdiag(s):
    eye = jnp.eye(A_HEADS, dtype=s.dtype)
    t = jnp.einsum("bdhkv,hg->bdhvgk", s, eye)
    return t.reshape(s.shape[0], 2, A_VAL_W, A_KEY_W)


def _blockdiag_to_state(t):
    b = t.shape[0]
    t = t.reshape(b, 2, A_HEADS, A_DV, A_HEADS, A_DK)
    s = jnp.stack([t[:, :, h, :, h, :] for h in range(A_HEADS)], axis=2)
    return jnp.swapaxes(s, -1, -2)


def kernel(x_prompt, x_sample, state_gla, cache_k, cache_v, c, c_ctx, w_ada, b_ada, norm_mix, norm_ffn,
           norm_final, w_in, w_out, gla_gate_w, gla_gate_b, gla_norm, q_norm, k_norm, router_w, router_b,
           expert_w_gu, expert_b_gu, expert_w_dn, expert_b_dn):
    x = jnp.concatenate([x_prompt.reshape(N_CTX, D_MODEL), x_sample.reshape(N_DEC, D_MODEL)], axis=0)
    cond = jnp.zeros((MOD_ROWS, D_MODEL), F32).at[0].set(c_ctx).at[1:1 + DEC_BATCH].set(c)
    mod = _ada(cond, w_ada, b_ada)

    gm256 = (jnp.kron(jnp.eye(A_HEADS), jnp.ones((A_DV, A_DV))) / A_DV).astype(BF16)
    gm128 = (jnp.kron(jnp.eye(2), jnp.ones((HEAD_DIM, HEAD_DIM))) / HEAD_DIM).astype(BF16)
    c64, s64 = _dft_mats(F_GROUP_W, F_GROUP_W ** -0.5)
    cb = jnp.kron(jnp.eye(F_GROUPS), c64).astype(BF16)
    sb = jnp.kron(jnp.eye(F_GROUPS), s64).astype(BF16)
    dft_ctx = [m.astype(BF16) for m in _dft_mats(SEQ, SEQ ** -0.5)]
    dft_dec = [m.astype(BF16) for m in _dft_mats(DEC_SEQ, DEC_SEQ ** -0.5)]
    rope_tabs = _rope_tables()
    zero_state = jnp.zeros((BATCH, 2, A_VAL_W, A_KEY_W), F32)

    new_s, new_k, new_v = [], [], []
    for l in range(DEPTH):
        mod3 = mod[l].reshape(MOD_ROWS, 6, D_MODEL)
        w_l = w_in[l]
        w_r = jnp.concatenate([w_l[:, :768], w_l[:, 800:], w_l[:, 768:800],
                               jnp.zeros((D_MODEL, PROJ_W - 1824), F32)], axis=1).astype(BF16)
        proj = _inproj(x, mod3, norm_mix[l][None, :], w_r)

        wg = jnp.zeros((LANES, 2 * A_KEY_W), F32)
        wg = wg.at[0:GATE_RANK, 0:A_KEY_W].set(gla_gate_w[l, 0])
        wg = wg.at[GATE_RANK:2 * GATE_RANK, A_KEY_W:].set(gla_gate_w[l, 1])
        gb = gla_gate_b[l].reshape(1, 2 * A_KEY_W)
        gn = jnp.tile(gla_norm[l], A_HEADS)[None, :]
        s0_dec = _state_to_blockdiag(state_gla[:, l])
        oa_ctx, sfin = _gla(proj, wg, gb, zero_state, gn, gm256, nseq=BATCH, seq=SEQ, row_block0=0)
        oa_dec, _ = _gla(proj, wg, gb, s0_dec, gn, gm256, nseq=DEC_BATCH, seq=DEC_SEQ,
                         row_block0=N_CTX // DEC_SEQ)
        of_ctx = _fnet(proj, cb, sb, *dft_ctx, nseq=BATCH, seq=SEQ, row_block0=0)
        of_dec = _fnet(proj, cb, sb, *dft_dec, nseq=DEC_BATCH, seq=DEC_SEQ, row_block0=N_CTX // DEC_SEQ)
        qn = jnp.tile(q_norm[l], 2)[None, :]
        kn = jnp.tile(k_norm[l], 2)[None, :]
        oc_ctx, ka = _attn(proj, qn, kn, gm128, nseq=BATCH, seq=SEQ, row_block0=0)
        past = (cache_k[:, l].reshape(DEC_BATCH, PAST_LEN, C_KV_W),
                cache_v[:, l].reshape(DEC_BATCH, PAST_LEN, C_KV_W))
        oc_dec, _ = _attn(proj, qn, kn, gm128, nseq=DEC_BATCH, seq=DEC_SEQ,
                          row_block0=N_CTX // DEC_SEQ, past=past, tables=rope_tabs)
        new_s.append(_blockdiag_to_state(sfin))
        new_k.append(ka.reshape(BATCH, SEQ, C_KV_HEADS, HEAD_DIM))
        new_v.append(proj[:N_CTX, COL_CV:COL_CV + C_KV_W].reshape(BATCH, SEQ, C_KV_HEADS, HEAD_DIM))

        oa = jnp.concatenate([oa_ctx, oa_dec], axis=0)
        of = jnp.concatenate([of_ctx, of_dec], axis=0)
        oc = jnp.concatenate([oc_ctx, oc_dec], axis=0)
        rw = jnp.zeros((D_MODEL, LANES), F32).at[:, :N_EXPERTS].set(router_w[l])
        rb = jnp.zeros((1, LANES), F32).at[0, :N_EXPERTS].set(router_b[l])
        x1, h2, meta, sel = _outproj(oa, of, oc, w_out[l].astype(BF16), x, mod3, norm_ffn[l][None, :], rw, rb)

        posk, counts = _positions(sel, meta)
        cnt = counts[0, :N_EXPERTS].astype(jnp.int32)
        pcnt = (cnt + MOE_BM - 1) // MOE_BM * MOE_BM
        pend = jnp.cumsum(pcnt)
        pstart = pend - pcnt
        top_i = meta[:, :TOP_K].astype(jnp.int32)
        dest = (pstart[top_i] + posk[:, :TOP_K].astype(jnp.int32)).reshape(N_TOK * TOP_K)
        blk_exp = jnp.minimum(jnp.searchsorted(pend, jnp.arange(MOE_NB, dtype=jnp.int32) * MOE_BM, side="right"),
                              N_EXPERTS - 1).astype(jnp.int32)
        nused = (pend[-1:] // MOE_BM).astype(jnp.int32)

        xs = _dispatch(dest, h2, jnp.zeros((MOE_ROWS, D_MODEL), F32))
        y = _experts(blk_exp, nused, xs, expert_w_gu[l].astype(BF16),
                     expert_b_gu[l].reshape(N_EXPERTS, 1, 2 * D_FF), expert_w_dn[l].astype(BF16),
                     expert_b_dn[l].reshape(N_EXPERTS, 1, D_MODEL))
        x = _combine(dest, y, meta, x1, mod3, norm_final[None, :], final=(l == DEPTH - 1))

    y_prompt = x[:N_CTX].reshape(BATCH, SEQ, D_MODEL)
    y_sample = x[N_CTX:].reshape(DEC_BATCH, DEC_SEQ, D_MODEL)
    return (y_prompt, y_sample, jnp.stack(new_s, axis=1), jnp.stack(new_k, axis=1), jnp.stack(new_v, axis=1))
```

```python
import functools

import jax
import jax.numpy as jnp
from jax import lax
from jax.experimental import pallas as pl
from jax.experimental.pallas import tpu as pltpu

D_MODEL = 1024
BATCH = 32
SEQ = 256
DEPTH = 2
DEC_BATCH = 8
DEC_SEQ = 1024
PAST_LEN = 512
GRID_W = 64
A_HEADS = 4
A_DV = 64
A_DK = 32
A_KEY_W = 128
A_VAL_W = 256
GATE_RANK = 16
GATE_TEMP = 16.0
GLA_CHUNK = 32
F_GROUPS = 4
F_GROUP_W = 64
F_WIDTH = 256
HEAD_DIM = 64
C_HEADS = 8
C_KV_HEADS = 2
C_Q_W = 512
C_KV_W = 128
ROPE_THETA = 10000.0
N_EXPERTS = 32
TOP_K = 4
D_FF = 1024
SWIGLU_LIMIT = 7.0
SWIGLU_ALPHA = 1.702
EPS = 1e-6

N_CTX = BATCH * SEQ
N_DEC = DEC_BATCH * DEC_SEQ
N_TOK = N_CTX + N_DEC
MOD_ROWS = 16

COL_GLA = 0
COL_PF = 768
COL_CQ = 1024
COL_CK = 1536
COL_CV = 1664
COL_LR = 1792
PROJ_W = 1920

LANES = 128
TM = 512
GLA_TILE = 256
Q_TILE = 256
MOE_BM = 256
MOE_NB = N_TOK * TOP_K // MOE_BM + N_EXPERTS
MOE_ROWS = MOE_NB * MOE_BM
ROW_TILE = 256
VMEM_LIMIT = 56 * 1024 * 1024

F32 = jnp.float32
BF16 = jnp.bfloat16


def _params(sem, vmem=VMEM_LIMIT):
    return pltpu.CompilerParams(dimension_semantics=sem, vmem_limit_bytes=vmem)


def _bdot(a, b):
    return jnp.dot(a.astype(BF16), b.astype(BF16), preferred_element_type=F32)


def _split(a):
    hi = a.astype(BF16)
    lo = (a - hi.astype(F32)).astype(BF16)
    return hi, lo


def _dot3(a, b):
    ah, al = _split(a)
    bh, bl = _split(b)
    d = functools.partial(jnp.dot, preferred_element_type=F32)
    return d(ah, bh) + (d(ah, bl) + d(al, bh))


def _dot2(a, b16):
    ah, al = _split(a)
    d = functools.partial(jnp.dot, preferred_element_type=F32)
    return d(ah, b16) + d(al, b16)


def _sigmoid(x):
    return 1.0 / (1.0 + jnp.exp(-x))


def _mod_row(i, tm):
    tok = i * tm
    return jnp.where(tok < N_CTX, 0, 1 + (tok - N_CTX) // DEC_SEQ)


def _ada_body(c_ref, w_ref, b_ref, o_ref):
    c = c_ref[...]
    o_ref[0] = _dot3(c * _sigmoid(c), w_ref[0]) + b_ref[0]


def _ada(cond, w_ada, b_ada):
    return pl.pallas_call(
        _ada_body,
        grid=(DEPTH, 6),
        in_specs=[
            pl.BlockSpec((MOD_ROWS, D_MODEL), lambda l, j: (0, 0)),
            pl.BlockSpec((1, D_MODEL, D_MODEL), lambda l, j: (l, 0, j)),
            pl.BlockSpec((1, 1, D_MODEL), lambda l, j: (l, 0, j)),
        ],
        out_specs=pl.BlockSpec((1, MOD_ROWS, D_MODEL), lambda l, j: (l, 0, j)),
        out_shape=jax.ShapeDtypeStruct((DEPTH, MOD_ROWS, 6 * D_MODEL), F32),
        compiler_params=_params(("parallel", "parallel")),
        name="ada",
    )(cond, w_ada, b_ada.reshape(DEPTH, 1, 6 * D_MODEL))


def _rms(x):
    return x * lax.rsqrt(jnp.mean(x * x, axis=-1, keepdims=True) + EPS)


def _inproj_body(x_ref, mod_ref, g_ref, w_ref, o_ref):
    mod = mod_ref[0]
    h = _rms(x_ref[...]) * g_ref[...] * (1.0 + mod[1:2]) + mod[0:1]
    o_ref[...] = jnp.dot(h.astype(BF16), w_ref[...], preferred_element_type=F32)


def _inproj(x, mod3, gain, w):
    return pl.pallas_call(
        _inproj_body,
        grid=(N_TOK // TM,),
        in_specs=[
            pl.BlockSpec((TM, D_MODEL), lambda i: (i, 0)),
            pl.BlockSpec((1, 6, D_MODEL), lambda i: (_mod_row(i, TM), 0, 0)),
            pl.BlockSpec((1, D_MODEL), lambda i: (0, 0)),
            pl.BlockSpec((D_MODEL, PROJ_W), lambda i: (0, 0)),
        ],
        out_specs=pl.BlockSpec((TM, PROJ_W), lambda i: (i, 0)),
        out_shape=jax.ShapeDtypeStruct((N_TOK, PROJ_W), F32),
        compiler_params=_params(("parallel",)),
        name="inproj",
    )(x, mod3, gain, w)


def _log_sigmoid(x):
    return jnp.minimum(x, 0.0) - jnp.log(1.0 + jnp.exp(-jnp.abs(x)))


def _chunk_cumsum(x, pos, forward):
    rows = x.shape[0]
    for s in (1, 2, 4, 8, 16):
        if forward:
            x = x + jnp.where(pos >= s, pltpu.roll(x, s, 0), 0.0)
        else:
            x = x + jnp.where(pos < GLA_CHUNK - s, pltpu.roll(x, rows - s, 0), 0.0)
    return x


def _gla_tile(q, k, v16, la, st, forward):
    R = GLA_TILE
    nch = R // GLA_CHUNK
    pos = lax.broadcasted_iota(jnp.int32, (R, 1), 0) & (GLA_CHUNK - 1)
    b = _chunk_cumsum(la, pos, forward)
    b3 = b.reshape(nch, GLA_CHUNK, A_KEY_W)
    tot3 = b3[:, GLA_CHUNK - 1:GLA_CHUNK, :] if forward else b3[:, 0:1, :]
    totb = jnp.broadcast_to(tot3, (nch, GLA_CHUNK, A_KEY_W)).reshape(R, A_KEY_W)
    qt = q * (A_DK ** -0.5) * jnp.exp(b)
    kt = (k * jnp.exp(-b)).astype(BF16)
    kend = (k * jnp.exp(totb - b)).astype(BF16)
    dec3 = jnp.exp(tot3)

    ri = lax.broadcasted_iota(jnp.int32, (R, R), 0)
    ci = lax.broadcasted_iota(jnp.int32, (R, R), 1)
    tri = (ci <= ri) if forward else (ci >= ri)
    mask = ((ri >> 5) == (ci >> 5)) & tri
    lane_k = lax.broadcasted_iota(jnp.int32, (R, A_KEY_W), 1) >> 5
    lane_v = lax.broadcasted_iota(jnp.int32, (R, A_VAL_W), 1) >> 6

    o = jnp.zeros((R, A_VAL_W), F32)
    for h in range(A_HEADS):
        qh = jnp.where(lane_k == h, qt, 0.0).astype(BF16)
        sc = lax.dot_general(qh, kt, (((1,), (1,)), ((), ())), preferred_element_type=F32)
        p = jnp.where(mask, sc, 0.0).astype(BF16)
        oh = jnp.dot(p, v16, preferred_element_type=F32)
        o = jnp.where(lane_v == h, oh, o)

    sr = lax.broadcasted_iota(jnp.int32, (A_VAL_W, A_KEY_W), 0) >> 6
    sl = lax.broadcasted_iota(jnp.int32, (A_VAL_W, A_KEY_W), 1) >> 5
    smask = sr == sl
    qt16 = qt.astype(BF16)
    inter = [None] * nch
    order = range(nch) if forward else range(nch - 1, -1, -1)
    for c in order:
        lo = c * GLA_CHUNK
        inter[c] = lax.dot_general(qt16[lo:lo + GLA_CHUNK], st.astype(BF16),
                                   (((1,), (1,)), ((), ())), preferred_element_type=F32)
        ds = lax.dot_general(v16[lo:lo + GLA_CHUNK], kend[lo:lo + GLA_CHUNK],
                             (((0,), (0,)), ((), ())), preferred_element_type=F32)
        st = st * dec3[c] + jnp.where(smask, ds, 0.0)
    return o + jnp.concatenate(inter, axis=0), st


def _gla_body(gla_ref, lr_ref, wg_ref, gb_ref, s0_ref, gn_ref, gm_ref, o_ref, sfin_ref, la_scr, *, seq):
    ntile = seq // GLA_TILE
    logit = _bdot(lr_ref[...], wg_ref[...]) + gb_ref[...]
    la_scr[...] = _log_sigmoid(logit) * (1.0 / GATE_TEMP)

    def run(forward):
        d = 0 if forward else 1

        def step(i, st):
            t = i if forward else ntile - 1 - i
            r0 = pl.multiple_of(t * GLA_TILE, GLA_TILE)
            rows = pl.ds(r0, GLA_TILE)
            q = gla_ref[rows, 0:128]
            k = gla_ref[rows, 128:256]
            v16 = gla_ref[rows, 256:512].astype(BF16)
            la = la_scr[rows, d * A_KEY_W:(d + 1) * A_KEY_W]
            o, st = _gla_tile(q, k, v16, la, st, forward)
            if forward:
                o_ref[rows, :] = o
            else:
                o_ref[rows, :] += o
            return st

        sfin_ref[0, d] = lax.fori_loop(0, ntile, step, s0_ref[0, d])

    run(True)
    run(False)
    o = o_ref[...]
    ms = _dot2(o * o, gm_ref[...])
    g = gla_ref[:, 512:768]
    o_ref[...] = o * lax.rsqrt(ms + EPS) * gn_ref[...] * (g * _sigmoid(g))


def _gla(proj, wg, gb, s0t, gnorm, gmat, *, nseq, seq, row_block0):
    return pl.pallas_call(
        functools.partial(_gla_body, seq=seq),
        grid=(nseq,),
        in_specs=[
            pl.BlockSpec((seq, 768), lambda b: (row_block0 + b, 0)),
            pl.BlockSpec((seq, LANES), lambda b: (row_block0 + b, COL_LR // LANES)),
            pl.BlockSpec((LANES, 2 * A_KEY_W), lambda b: (0, 0)),
            pl.BlockSpec((1, 2 * A_KEY_W), lambda b: (0, 0)),
            pl.BlockSpec((1, 2, A_VAL_W, A_KEY_W), lambda b: (b, 0, 0, 0)),
            pl.BlockSpec((1, A_VAL_W), lambda b: (0, 0)),
            pl.BlockSpec((A_VAL_W, A_VAL_W), lambda b: (0, 0)),
        ],
        out_specs=[
            pl.BlockSpec((seq, A_VAL_W), lambda b: (b, 0)),
            pl.BlockSpec((1, 2, A_VAL_W, A_KEY_W), lambda b: (b, 0, 0, 0)),
        ],
        out_shape=[
            jax.ShapeDtypeStruct((nseq * seq, A_VAL_W), F32),
            jax.ShapeDtypeStruct((nseq, 2, A_VAL_W, A_KEY_W), F32),
        ],
        scratch_shapes=[pltpu.VMEM((seq, 2 * A_KEY_W), F32)],
        compiler_params=_params(("parallel",)),
        name=f"gla{seq}",
    )(proj, proj, wg, gb, s0t, gnorm, gmat)


def _fnet_body(x_ref, cb_ref, sb_ref, ct_ref, st_ref, o_ref):
    x = x_ref[...].astype(BF16)
    xc = jnp.dot(x, cb_ref[...], preferred_element_type=F32).astype(BF16)
    xs = jnp.dot(x, sb_ref[...], preferred_element_type=F32).astype(BF16)
    o_ref[...] = (jnp.dot(ct_ref[...], xc, preferred_element_type=F32)
                  - jnp.dot(st_ref[...], xs, preferred_element_type=F32))


def _fnet(proj, cb, sb, ct, st, *, nseq, seq, row_block0):
    return pl.pallas_call(
        _fnet_body,
        grid=(nseq,),
        in_specs=[
            pl.BlockSpec((seq, F_WIDTH), lambda b: (row_block0 + b, COL_PF // F_WIDTH)),
            pl.BlockSpec((F_WIDTH, F_WIDTH), lambda b: (0, 0)),
            pl.BlockSpec((F_WIDTH, F_WIDTH), lambda b: (0, 0)),
            pl.BlockSpec((seq, seq), lambda b: (0, 0)),
            pl.BlockSpec((seq, seq), lambda b: (0, 0)),
        ],
        out_specs=pl.BlockSpec((seq, F_WIDTH), lambda b: (b, 0)),
        out_shape=jax.ShapeDtypeStruct((nseq * seq, F_WIDTH), F32),
        compiler_params=_params(("parallel",)),
        name=f"fnet{seq}",
    )(proj, cb, sb, ct, st)


def _dft_mats(n, scale):
    j = lax.broadcasted_iota(jnp.int32, (n, n), 0)
    k = lax.broadcasted_iota(jnp.int32, (n, n), 1)
    ang = ((j * k) % n).astype(F32) * (2.0 * jnp.pi / n)
    return jnp.cos(ang) * scale, jnp.sin(ang) * scale


def _head_rms(x, gm16, gain):
    ms = _dot2(x * x, gm16)
    return x * lax.rsqrt(ms + EPS) * gain


def _rope(x, cos, sin_signed):
    w = x.shape[1]
    lane = lax.broadcasted_iota(jnp.int32, x.shape, 1)
    partner = jnp.where((lane & 31) < 16, pltpu.roll(x, w - 16, 1), pltpu.roll(x, 16, 1))
    return x * cos + partner * sin_signed


def _attn_body(*refs, rope, n_new, n_past):
    if n_past:
        (q_ref, k_ref, v_ref, ck_ref, cv_ref, cq_ref, sq_ref, cosk_ref, sink_ref,
         qn_ref, kn_ref, gm_ref, o_ref, ka_ref, k_scr, ks_scr, v_scr, vs_scr) = refs
    else:
        (q_ref, k_ref, v_ref, qn_ref, kn_ref, gm_ref, o_ref, ka_ref,
         k_scr, ks_scr, v_scr, vs_scr) = refs
    gm = gm_ref[...]

    @pl.when(pl.program_id(1) == 0)
    def _():
        kn = _head_rms(k_ref[...], gm, kn_ref[...])
        ka_ref[...] = kn
        if rope:
            kn = _rope(kn, cosk_ref[...], sink_ref[...])
        v = v_ref[...]
        new = pl.ds(0, n_new)
        k_scr[new, :] = kn.astype(BF16)
        ks_scr[new, :] = pltpu.roll(kn, HEAD_DIM, 1).astype(BF16)
        v_scr[new, :] = v.astype(BF16)
        vs_scr[new, :] = pltpu.roll(v, HEAD_DIM, 1).astype(BF16)
        if n_past:
            past = pl.ds(n_new, n_past)
            ck = ck_ref[0]
            cv = cv_ref[0]
            k_scr[past, :] = ck.astype(BF16)
            ks_scr[past, :] = pltpu.roll(ck, HEAD_DIM, 1).astype(BF16)
            v_scr[past, :] = cv.astype(BF16)
            vs_scr[past, :] = pltpu.roll(cv, HEAD_DIM, 1).astype(BF16)

    lane = lax.broadcasted_iota(jnp.int32, (Q_TILE, LANES), 1)
    low = lane < HEAD_DIM
    for jp in range(C_Q_W // LANES):
        kv = jp // 2
        q2 = _head_rms(q_ref[:, jp * LANES:(jp + 1) * LANES], gm, qn_ref[...])
        if rope:
            q2 = _rope(q2, cq_ref[...], sq_ref[...])
        q2 = q2 * (HEAD_DIM ** -0.5)
        halves = []
        for half in range(2):
            swapped = (half == 1) != (kv == 1)
            kk = ks_scr[...] if swapped else k_scr[...]
            vv = vs_scr[...] if swapped else v_scr[...]
            qm = jnp.where(low if half == 0 else ~low, q2, 0.0).astype(BF16)
            s = lax.dot_general(qm, kk, (((1,), (1,)), ((), ())), preferred_element_type=F32)
            p = jnp.exp(s - jnp.max(s, axis=-1, keepdims=True))
            den = jnp.sum(p, axis=-1, keepdims=True)
            halves.append(jnp.dot(p.astype(BF16), vv, preferred_element_type=F32) / den)
        o_ref[:, jp * LANES:(jp + 1) * LANES] = jnp.where(low, halves[0], halves[1])


def _attn(proj, qn, kn, gm, *, nseq, seq, row_block0, past=None, tables=None):
    nq = seq // Q_TILE
    n_past = 0 if past is None else past[0].shape[1]
    qrow = lambda b, j: ((row_block0 + b) * nq + j, COL_CQ // C_Q_W)
    in_specs = [
        pl.BlockSpec((Q_TILE, C_Q_W), qrow),
        pl.BlockSpec((seq, C_KV_W), lambda b, j: (row_block0 + b, COL_CK // C_KV_W)),
        pl.BlockSpec((seq, C_KV_W), lambda b, j: (row_block0 + b, COL_CV // C_KV_W)),
    ]
    args = [proj, proj, proj]
    if past is not None:
        in_specs += [pl.BlockSpec((1, n_past, C_KV_W), lambda b, j: (b, 0, 0))] * 2
        in_specs += [pl.BlockSpec((Q_TILE, LANES), lambda b, j: (j, 0))] * 2
        in_specs += [pl.BlockSpec((seq, LANES), lambda b, j: (0, 0))] * 2
        args += [past[0], past[1], tables[0], tables[1], tables[0], tables[1]]
    in_specs += [
        pl.BlockSpec((1, LANES), lambda b, j: (0, 0)),
        pl.BlockSpec((1, LANES), lambda b, j: (0, 0)),
        pl.BlockSpec((LANES, LANES), lambda b, j: (0, 0)),
    ]
    args += [qn, kn, gm]
    s_all = seq + n_past
    return pl.pallas_call(
        functools.partial(_attn_body, rope=past is not None, n_new=seq, n_past=n_past),
        grid=(nseq, nq),
        in_specs=in_specs,
        out_specs=[
            pl.BlockSpec((Q_TILE, C_Q_W), lambda b, j: (b * nq + j, 0)),
            pl.BlockSpec((seq, C_KV_W), lambda b, j: (b, 0)),
        ],
        out_shape=[
            jax.ShapeDtypeStruct((nseq * seq, C_Q_W), F32),
            jax.ShapeDtypeStruct((nseq * seq, C_KV_W), F32),
        ],
        scratch_shapes=[pltpu.VMEM((s_all, C_KV_W), BF16)] * 4,
        compiler_params=_params(("parallel", "arbitrary")),
        name=f"attn{seq}",
    )(*args)


def _rope_tables():
    t = jnp.arange(DEC_SEQ)
    rows = (t // GRID_W).astype(F32)
    cols = (t % GRID_W).astype(F32)
    nf = HEAD_DIM // 4
    inv_freq = ROPE_THETA ** (-jnp.arange(nf, dtype=F32) / nf)
    d = jnp.arange(LANES)
    pos = jnp.where(((d % HEAD_DIM) < HEAD_DIM // 2)[None, :], rows[:, None], cols[:, None])
    ang = pos * inv_freq[d % nf][None, :]
    sign = jnp.where((d % 32) < 16, -1.0, 1.0)[None, :]
    return jnp.cos(ang), jnp.sin(ang) * sign


def _outproj_body(oa_ref, of_ref, oc_ref, w_ref, x_ref, mod_ref, g_ref, rw_ref, rb_ref,
                  x1_ref, h2_ref, meta_ref, sel_ref):
    mod = mod_ref[0]
    mixed = (jnp.dot(oa_ref[...].astype(BF16), w_ref[0:256, :], preferred_element_type=F32)
             + jnp.dot(of_ref[...].astype(BF16), w_ref[256:512, :], preferred_element_type=F32)
             + jnp.dot(oc_ref[...].astype(BF16), w_ref[512:1024, :], preferred_element_type=F32))
    x1 = x_ref[...] + mod[2:3] * mixed
    x1_ref[...] = x1
    h2 = _rms(x1) * g_ref[...] * (1.0 + mod[4:5]) + mod[3:4]
    h2_ref[...] = h2
    lane = lax.broadcasted_iota(jnp.int32, (TM, LANES), 1)
    logits = jnp.where(lane < N_EXPERTS, _dot3(h2, rw_ref[...]) + rb_ref[...], -jnp.inf)
    sel = jnp.zeros((TM, LANES), F32)
    meta = jnp.zeros((TM, LANES), F32)
    tops = []
    for k in range(TOP_K):
        m = jnp.max(logits, axis=-1, keepdims=True)
        idx = jnp.min(jnp.where(logits == m, lane, LANES), axis=-1, keepdims=True)
        hit = lane == idx
        sel = jnp.where(hit, 1.0, sel)
        logits = jnp.where(hit, -jnp.inf, logits)
        meta = jnp.where(lane == k, idx.astype(F32), meta)
        tops.append(m)
    ex = [jnp.exp(t - tops[0]) for t in tops]
    den = ex[0] + ex[1] + ex[2] + ex[3]
    for k in range(TOP_K):
        meta = jnp.where(lane == TOP_K + k, ex[k] / den, meta)
    meta_ref[...] = meta
    sel_ref[...] = sel


def _outproj(oa, of, oc, w, x, mod3, gain, rw, rb):
    tok = lambda w_: pl.BlockSpec((TM, w_), lambda i: (i, 0))
    full = lambda a, b: pl.BlockSpec((a, b), lambda i: (0, 0))
    return pl.pallas_call(
        _outproj_body,
        grid=(N_TOK // TM,),
        in_specs=[tok(A_VAL_W), tok(F_WIDTH), tok(C_Q_W), full(D_MODEL, D_MODEL), tok(D_MODEL),
                  pl.BlockSpec((1, 6, D_MODEL), lambda i: (_mod_row(i, TM), 0, 0)),
                  full(1, D_MODEL), full(D_MODEL, LANES), full(1, LANES)],
        out_specs=[tok(D_MODEL), tok(D_MODEL), tok(LANES), tok(LANES)],
        out_shape=[jax.ShapeDtypeStruct((N_TOK, D_MODEL), F32),
                   jax.ShapeDtypeStruct((N_TOK, D_MODEL), F32),
                   jax.ShapeDtypeStruct((N_TOK, LANES), F32),
                   jax.ShapeDtypeStruct((N_TOK, LANES), F32)],
        compiler_params=_params(("parallel",)),
        name="outproj",
    )(oa, of, oc, w, x, mod3, gain, rw, rb)


def _pos_body(sel_ref, meta_ref, pos_ref, cnt_ref, carry):
    i = pl.program_id(0)

    @pl.when(i == 0)
    def _():
        carry[...] = jnp.zeros_like(carry)

    sel = sel_ref[...]
    ri = lax.broadcasted_iota(jnp.int32, (TM, TM), 0)
    ci = lax.broadcasted_iota(jnp.int32, (TM, TM), 1)
    lower = jnp.where(ci < ri, 1.0, 0.0).astype(BF16)
    before = jnp.dot(lower, sel.astype(BF16), preferred_element_type=F32) + carry[...]
    lane = lax.broadcasted_iota(jnp.int32, (TM, LANES), 1)
    meta = meta_ref[...]
    out = jnp.zeros((TM, LANES), F32)
    for k in range(TOP_K):
        idx = meta[:, k:k + 1].astype(jnp.int32)
        pk = jnp.sum(jnp.where(lane == idx, before, 0.0), axis=-1, keepdims=True)
        out = jnp.where(lane == k, pk, out)
    pos_ref[...] = out
    carry[...] += jnp.sum(sel, axis=0, keepdims=True)
    cnt_ref[...] = carry[...]


def _positions(sel, meta):
    return pl.pallas_call(
        _pos_body,
        grid=(N_TOK // TM,),
        in_specs=[pl.BlockSpec((TM, LANES), lambda i: (i, 0))] * 2,
        out_specs=[pl.BlockSpec((TM, LANES), lambda i: (i, 0)),
                   pl.BlockSpec((1, LANES), lambda i: (0, 0))],
        out_shape=[jax.ShapeDtypeStruct((N_TOK, LANES), F32),
                   jax.ShapeDtypeStruct((1, LANES), F32)],
        scratch_shapes=[pltpu.VMEM((1, LANES), F32)],
        compiler_params=_params(("arbitrary",)),
        name="positions",
    )(sel, meta)


def _dispatch_body(dest_ref, h_ref, init_hbm, xs_hbm, sem):
    del init_hbm

    def copy(r, k):
        d = dest_ref[r * TOP_K + k]
        return pltpu.make_async_copy(h_ref.at[pl.ds(r, 1)], xs_hbm.at[pl.ds(d, 1)], sem)

    def start(r, c):
        for k in range(TOP_K):
            copy(r, k).start(priority=k % 2)
        return c

    def wait(r, c):
        for k in range(TOP_K):
            copy(r, k).wait()
        return c

    lax.fori_loop(0, ROW_TILE, start, 0)
    lax.fori_loop(0, ROW_TILE, wait, 0)


def _dispatch(dest_flat, h2, init):
    return pl.pallas_call(
        _dispatch_body,
        grid=(N_TOK // ROW_TILE,),
        in_specs=[pl.BlockSpec((ROW_TILE * TOP_K,), lambda i: (i,), memory_space=pltpu.SMEM),
                  pl.BlockSpec((ROW_TILE, D_MODEL), lambda i: (i, 0)),
                  pl.BlockSpec(memory_space=pl.ANY)],
        out_specs=pl.BlockSpec(memory_space=pl.ANY),
        out_shape=jax.ShapeDtypeStruct((MOE_ROWS, D_MODEL), F32),
        scratch_shapes=[pltpu.SemaphoreType.DMA],
        input_output_aliases={2: 0},
        compiler_params=_params(("arbitrary",)),
        name="dispatch",
    )(dest_flat, h2, init)


def _experts_body(bexp_ref, nused_ref, x_ref, wgu_ref, bgu_ref, wdn_ref, bdn_ref, y_ref, wgu16, wdn16):
    i = pl.program_id(0)

    @pl.when((i == 0) | (bexp_ref[i] != bexp_ref[jnp.maximum(i - 1, 0)]))
    def _():
        wgu16[...] = wgu_ref[0, 0].astype(BF16)
        wdn16[...] = wdn_ref[0, 0].astype(BF16)

    @pl.when(i < nused_ref[0])
    def _():
        gu = jnp.dot(x_ref[...].astype(BF16), wgu16[...], preferred_element_type=F32) + bgu_ref[0, 0]
        gate = jnp.minimum(gu[:, :D_FF], SWIGLU_LIMIT)
        up = jnp.clip(gu[:, D_FF:], -SWIGLU_LIMIT, SWIGLU_LIMIT)
        glu = gate * _sigmoid(gate * SWIGLU_ALPHA)
        act = ((up + 1.0) * glu).astype(BF16)
        y_ref[...] = jnp.dot(act, wdn16[...], preferred_element_type=F32) + bdn_ref[0, 0]

    @pl.when(i >= nused_ref[0])
    def _():
        y_ref[...] = jnp.zeros_like(y_ref)


def _experts(blk_exp, nused, xs, wgu, bgu, wdn, bdn, layer):
    e = lambda i, be, nu: (layer, be[i], 0, 0)
    grid_spec = pltpu.PrefetchScalarGridSpec(
        num_scalar_prefetch=2,
        grid=(MOE_NB,),
        in_specs=[
            pl.BlockSpec((MOE_BM, D_MODEL), lambda i, be, nu: (i, 0)),
            pl.BlockSpec((1, 1, D_MODEL, 2 * D_FF), e),
            pl.BlockSpec((1, 1, 1, 2 * D_FF), e),
            pl.BlockSpec((1, 1, D_FF, D_MODEL), e),
            pl.BlockSpec((1, 1, 1, D_MODEL), e),
        ],
        out_specs=pl.BlockSpec((MOE_BM, D_MODEL), lambda i, be, nu: (i, 0)),
        scratch_shapes=[pltpu.VMEM((D_MODEL, 2 * D_FF), BF16), pltpu.VMEM((D_FF, D_MODEL), BF16)],
    )
    return pl.pallas_call(
        _experts_body,
        grid_spec=grid_spec,
        out_shape=jax.ShapeDtypeStruct((MOE_ROWS, D_MODEL), F32),
        compiler_params=_params(("arbitrary",)),
        name="experts",
    )(blk_exp, nused, xs, wgu, bgu, wdn, bdn)


def _combine_body(dest_ref, y_hbm, meta_ref, x1_ref, mod_ref, nf_ref, o_ref, buf, sem, *, final):
    def copy(r, k):
        d = dest_ref[r * TOP_K + k]
        return pltpu.make_async_copy(y_hbm.at[pl.ds(d, 1)], buf.at[k, pl.ds(r, 1)], sem)

    def start(r, c):
        for k in range(TOP_K):
            copy(r, k).start(priority=k % 2)
        return c

    def wait(r, c):
        for k in range(TOP_K):
            copy(r, k).wait()
        return c

    lax.fori_loop(0, ROW_TILE, start, 0)
    lax.fori_loop(0, ROW_TILE, wait, 0)
    meta = meta_ref[...]
    ff = buf[0] * meta[:, TOP_K:TOP_K + 1]
    for k in range(1, TOP_K):
        ff = ff + buf[k] * meta[:, TOP_K + k:TOP_K + k + 1]
    x2 = x1_ref[...] + mod_ref[0][5:6] * ff
    if final:
        x2 = _rms(x2) * nf_ref[...]
    o_ref[...] = x2


def _combine(dest_flat, y, meta, x1, mod3, nfinal, *, final):
    return pl.pallas_call(
        functools.partial(_combine_body, final=final),
        grid=(N_TOK // ROW_TILE,),
        in_specs=[pl.BlockSpec((ROW_TILE * TOP_K,), lambda i: (i,), memory_space=pltpu.SMEM),
                  pl.BlockSpec(memory_space=pl.ANY),
                  pl.BlockSpec((ROW_TILE, LANES), lambda i: (i, 0)),
                  pl.BlockSpec((ROW_TILE, D_MODEL), lambda i: (i, 0)),
                  pl.BlockSpec((1, 6, D_MODEL), lambda i: (_mod_row(i, ROW_TILE), 0, 0)),
                  pl.BlockSpec((1, D_MODEL), lambda i: (0, 0))],
        out_specs=pl.BlockSpec((ROW_TILE, D_MODEL), lambda i: (i, 0)),
        out_shape=jax.ShapeDtypeStruct((N_TOK, D_MODEL), F32),
        scratch_shapes=[pltpu.VMEM((TOP_K, ROW_TILE, D_MODEL), F32), pltpu.SemaphoreType.DMA],
        compiler_params=_params(("arbitrary",)),
        name="combine",
    )(dest_flat, y, meta, x1, mod3, nfinal)


def _state_to_blockdiag(s):
    eye = jnp.eye(A_HEADS, dtype=s.dtype)
    t = jnp.einsum("bdhkv,hg->bdhvgk", s, eye)
    return t.reshape(s.shape[0], 2, A_VAL_W, A_KEY_W)


def _blockdiag_to_state(t):
    b = t.shape[0]
    t = t.reshape(b, 2, A_HEADS, A_DV, A_HEADS, A_DK)
    s = jnp.stack([t[:, :, h, :, h, :] for h in range(A_HEADS)], axis=2)
    return jnp.swapaxes(s, -1, -2)


def kernel(x_prompt, x_sample, state_gla, cache_k, cache_v, c, c_ctx, w_ada, b_ada, norm_mix, norm_ffn,
           norm_final, w_in, w_out, gla_gate_w, gla_gate_b, gla_norm, q_norm, k_norm, router_w, router_b,
           expert_w_gu, expert_b_gu, expert_w_dn, expert_b_dn):
    x = jnp.concatenate([x_prompt.reshape(N_CTX, D_MODEL), x_sample.reshape(N_DEC, D_MODEL)], axis=0)
    cond = jnp.zeros((MOD_ROWS, D_MODEL), F32).at[0].set(c_ctx).at[1:1 + DEC_BATCH].set(c)
    mod = _ada(cond, w_ada, b_ada)

    gm256 = (jnp.kron(jnp.eye(A_HEADS), jnp.ones((A_DV, A_DV))) / A_DV).astype(BF16)
    gm128 = (jnp.kron(jnp.eye(2), jnp.ones((HEAD_DIM, HEAD_DIM))) / HEAD_DIM).astype(BF16)
    c64, s64 = _dft_mats(F_GROUP_W, F_GROUP_W ** -0.5)
    cb = jnp.kron(jnp.eye(F_GROUPS), c64).astype(BF16)
    sb = jnp.kron(jnp.eye(F_GROUPS), s64).astype(BF16)
    dft_ctx = [m.astype(BF16) for m in _dft_mats(SEQ, SEQ ** -0.5)]
    dft_dec = [m.astype(BF16) for m in _dft_mats(DEC_SEQ, DEC_SEQ ** -0.5)]
    rope_tabs = _rope_tables()
    zero_state = jnp.zeros((BATCH, 2, A_VAL_W, A_KEY_W), F32)

    new_s, new_k, new_v = [], [], []
    for l in range(DEPTH):
        mod3 = mod[l].reshape(MOD_ROWS, 6, D_MODEL)
        w_l = w_in[l]
        w_r = jnp.concatenate([w_l[:, :768], w_l[:, 800:], w_l[:, 768:800],
                               jnp.zeros((D_MODEL, PROJ_W - 1824), F32)], axis=1).astype(BF16)
        proj = _inproj(x, mod3, norm_mix[l][None, :], w_r)

        wg = jnp.zeros((LANES, 2 * A_KEY_W), F32)
        wg = wg.at[0:GATE_RANK, 0:A_KEY_W].set(gla_gate_w[l, 0])
        wg = wg.at[GATE_RANK:2 * GATE_RANK, A_KEY_W:].set(gla_gate_w[l, 1])
        gb = gla_gate_b[l].reshape(1, 2 * A_KEY_W)
        gn = jnp.tile(gla_norm[l], A_HEADS)[None, :]
        s0_dec = _state_to_blockdiag(state_gla[:, l])
        oa_ctx, sfin = _gla(proj, wg, gb, zero_state, gn, gm256, nseq=BATCH, seq=SEQ, row_block0=0)
        oa_dec, _ = _gla(proj, wg, gb, s0_dec, gn, gm256, nseq=DEC_BATCH, seq=DEC_SEQ,
                         row_block0=N_CTX // DEC_SEQ)
        of_ctx = _fnet(proj, cb, sb, *dft_ctx, nseq=BATCH, seq=SEQ, row_block0=0)
        of_dec = _fnet(proj, cb, sb, *dft_dec, nseq=DEC_BATCH, seq=DEC_SEQ, row_block0=N_CTX // DEC_SEQ)
        qn = jnp.tile(q_norm[l], 2)[None, :]
        kn = jnp.tile(k_norm[l], 2)[None, :]
        oc_ctx, ka = _attn(proj, qn, kn, gm128, nseq=BATCH, seq=SEQ, row_block0=0)
        past = (cache_k[:, l].reshape(DEC_BATCH, PAST_LEN, C_KV_W),
                cache_v[:, l].reshape(DEC_BATCH, PAST_LEN, C_KV_W))
        oc_dec, _ = _attn(proj, qn, kn, gm128, nseq=DEC_BATCH, seq=DEC_SEQ,
                          row_block0=N_CTX // DEC_SEQ, past=past, tables=rope_tabs)
        new_s.append(_blockdiag_to_state(sfin))
        new_k.append(ka.reshape(BATCH, SEQ, C_KV_HEADS, HEAD_DIM))
        new_v.append(proj[:N_CTX, COL_CV:COL_CV + C_KV_W].reshape(BATCH, SEQ, C_KV_HEADS, HEAD_DIM))

        oa = jnp.concatenate([oa_ctx, oa_dec], axis=0)
        of = jnp.concatenate([of_ctx, of_dec], axis=0)
        oc = jnp.concatenate([oc_ctx, oc_dec], axis=0)
        rw = jnp.zeros((D_MODEL, LANES), F32).at[:, :N_EXPERTS].set(router_w[l])
        rb = jnp.zeros((1, LANES), F32).at[0, :N_EXPERTS].set(router_b[l])
        x1, h2, meta, sel = _outproj(oa, of, oc, w_out[l].astype(BF16), x, mod3, norm_ffn[l][None, :], rw, rb)

        posk, counts = _positions(sel, meta)
        cnt = counts[0, :N_EXPERTS].astype(jnp.int32)
        pcnt = (cnt + MOE_BM - 1) // MOE_BM * MOE_BM
        pend = jnp.cumsum(pcnt)
        pstart = pend - pcnt
        top_i = meta[:, :TOP_K].astype(jnp.int32)
        dest = (pstart[top_i] + posk[:, :TOP_K].astype(jnp.int32)).reshape(N_TOK * TOP_K)
        blk_start = jnp.arange(MOE_NB, dtype=jnp.int32) * MOE_BM
        blk_exp = jnp.minimum(jnp.sum((pend[None, :] <= blk_start[:, None]).astype(jnp.int32), axis=1),
                              N_EXPERTS - 1)
        nused = (pend[-1:] // MOE_BM).astype(jnp.int32)

        xs = _dispatch(dest, h2, jnp.zeros((MOE_ROWS, D_MODEL), F32))
        y = _experts(blk_exp, nused, xs, expert_w_gu,
                     expert_b_gu.reshape(DEPTH, N_EXPERTS, 1, 2 * D_FF), expert_w_dn,
                     expert_b_dn.reshape(DEPTH, N_EXPERTS, 1, D_MODEL), l)
        x = _combine(dest, y, meta, x1, mod3, norm_final[None, :], final=(l == DEPTH - 1))

    y_prompt = x[:N_CTX].reshape(BATCH, SEQ, D_MODEL)
    y_sample = x[N_CTX:].reshape(DEC_BATCH, DEC_SEQ, D_MODEL)
    return (y_prompt, y_sample, jnp.stack(new_s, axis=1), jnp.stack(new_k, axis=1), jnp.stack(new_v, axis=1))
```

```python
import functools

import jax
import jax.numpy as jnp
from jax import lax
from jax.experimental import pallas as pl
from jax.experimental.pallas import tpu as pltpu

D_MODEL = 1024
BATCH = 32
SEQ = 256
DEPTH = 2
DEC_BATCH = 8
DEC_SEQ = 1024
PAST_LEN = 512
GRID_W = 64
A_HEADS = 4
A_DV = 64
A_DK = 32
A_KEY_W = 128
A_VAL_W = 256
GATE_RANK = 16
GATE_TEMP = 16.0
GLA_CHUNK = 32
F_GROUPS = 4
F_GROUP_W = 64
F_WIDTH = 256
HEAD_DIM = 64
C_HEADS = 8
C_KV_HEADS = 2
C_Q_W = 512
C_KV_W = 128
ROPE_THETA = 10000.0
N_EXPERTS = 32
TOP_K = 4
D_FF = 1024
SWIGLU_LIMIT = 7.0
SWIGLU_ALPHA = 1.702
EPS = 1e-6

N_CTX = BATCH * SEQ
N_DEC = DEC_BATCH * DEC_SEQ
N_TOK = N_CTX + N_DEC
MOD_ROWS = 16

COL_GLA = 0
COL_PF = 768
COL_CQ = 1024
COL_CK = 1536
COL_CV = 1664
COL_LR = 1792
PROJ_W = 1920

LANES = 128
TM = 512
GLA_TILE = 256
Q_TILE = 256
MOE_BM = 256
ROW_TILE = 256
N_TILES = N_TOK // ROW_TILE
RUN_ALIGN = 8
RUN_SHIFT = 3
RUN_BITS = 6
SORT_ROWS = ROW_TILE * TOP_K + N_EXPERTS * RUN_ALIGN
HALF_D = D_MODEL // 2
MOE_NB = (N_TOK * TOP_K + N_TILES * N_EXPERTS * (RUN_ALIGN - 1)) // MOE_BM + N_EXPERTS
MOE_ROWS = MOE_NB * MOE_BM
VMEM_LIMIT = 56 * 1024 * 1024

F32 = jnp.float32
BF16 = jnp.bfloat16


def _params(sem, vmem=VMEM_LIMIT):
    return pltpu.CompilerParams(dimension_semantics=sem, vmem_limit_bytes=vmem)


def _bdot(a, b):
    return jnp.dot(a.astype(BF16), b.astype(BF16), preferred_element_type=F32)


def _split(a):
    hi = a.astype(BF16)
    lo = (a - hi.astype(F32)).astype(BF16)
    return hi, lo


def _dot3(a, b):
    ah, al = _split(a)
    bh, bl = _split(b)
    d = functools.partial(jnp.dot, preferred_element_type=F32)
    return d(ah, bh) + (d(ah, bl) + d(al, bh))


def _dot2(a, b16):
    ah, al = _split(a)
    d = functools.partial(jnp.dot, preferred_element_type=F32)
    return d(ah, b16) + d(al, b16)


def _sigmoid(x):
    return 1.0 / (1.0 + jnp.exp(-x))


def _mod_row(i, tm):
    tok = i * tm
    return jnp.where(tok < N_CTX, 0, 1 + (tok - N_CTX) // DEC_SEQ)


def _ada_body(c_ref, w_ref, b_ref, o_ref):
    c = c_ref[...]
    o_ref[0] = _dot3(c * _sigmoid(c), w_ref[0]) + b_ref[0]


def _ada(cond, w_ada, b_ada):
    return pl.pallas_call(
        _ada_body,
        grid=(DEPTH, 6),
        in_specs=[
            pl.BlockSpec((MOD_ROWS, D_MODEL), lambda l, j: (0, 0)),
            pl.BlockSpec((1, D_MODEL, D_MODEL), lambda l, j: (l, 0, j)),
            pl.BlockSpec((1, 1, D_MODEL), lambda l, j: (l, 0, j)),
        ],
        out_specs=pl.BlockSpec((1, MOD_ROWS, D_MODEL), lambda l, j: (l, 0, j)),
        out_shape=jax.ShapeDtypeStruct((DEPTH, MOD_ROWS, 6 * D_MODEL), F32),
        compiler_params=_params(("parallel", "parallel")),
        name="ada",
    )(cond, w_ada, b_ada.reshape(DEPTH, 1, 6 * D_MODEL))


def _rms(x):
    return x * lax.rsqrt(jnp.mean(x * x, axis=-1, keepdims=True) + EPS)


def _inproj_body(x_ref, mod_ref, g_ref, w_ref, o_ref):
    mod = mod_ref[0]
    h = _rms(x_ref[...]) * g_ref[...] * (1.0 + mod[1:2]) + mod[0:1]
    o_ref[...] = jnp.dot(h.astype(BF16), w_ref[...], preferred_element_type=F32)


def _inproj(x, mod3, gain, w):
    return pl.pallas_call(
        _inproj_body,
        grid=(N_TOK // TM,),
        in_specs=[
            pl.BlockSpec((TM, D_MODEL), lambda i: (i, 0)),
            pl.BlockSpec((1, 6, D_MODEL), lambda i: (_mod_row(i, TM), 0, 0)),
            pl.BlockSpec((1, D_MODEL), lambda i: (0, 0)),
            pl.BlockSpec((D_MODEL, PROJ_W), lambda i: (0, 0)),
        ],
        out_specs=pl.BlockSpec((TM, PROJ_W), lambda i: (i, 0)),
        out_shape=jax.ShapeDtypeStruct((N_TOK, PROJ_W), F32),
        compiler_params=_params(("parallel",)),
        name="inproj",
    )(x, mod3, gain, w)


def _log_sigmoid(x):
    return jnp.minimum(x, 0.0) - jnp.log(1.0 + jnp.exp(-jnp.abs(x)))


def _chunk_cumsum(x, pos, forward):
    rows = x.shape[0]
    for s in (1, 2, 4, 8, 16):
        if forward:
            x = x + jnp.where(pos >= s, pltpu.roll(x, s, 0), 0.0)
        else:
            x = x + jnp.where(pos < GLA_CHUNK - s, pltpu.roll(x, rows - s, 0), 0.0)
    return x


def _gla_tile(q, k, v16, la, st, forward):
    R = GLA_TILE
    nch = R // GLA_CHUNK
    pos = lax.broadcasted_iota(jnp.int32, (R, 1), 0) & (GLA_CHUNK - 1)
    b = _chunk_cumsum(la, pos, forward)
    b3 = b.reshape(nch, GLA_CHUNK, A_KEY_W)
    tot3 = b3[:, GLA_CHUNK - 1:GLA_CHUNK, :] if forward else b3[:, 0:1, :]
    totb = jnp.broadcast_to(tot3, (nch, GLA_CHUNK, A_KEY_W)).reshape(R, A_KEY_W)
    qt = q * (A_DK ** -0.5) * jnp.exp(b)
    kt = (k * jnp.exp(-b)).astype(BF16)
    kend = (k * jnp.exp(totb - b)).astype(BF16)
    dec3 = jnp.exp(tot3)

    ri = lax.broadcasted_iota(jnp.int32, (R, R), 0)
    ci = lax.broadcasted_iota(jnp.int32, (R, R), 1)
    tri = (ci <= ri) if forward else (ci >= ri)
    mask = ((ri >> 5) == (ci >> 5)) & tri
    lane_k = lax.broadcasted_iota(jnp.int32, (R, A_KEY_W), 1) >> 5
    lane_v = lax.broadcasted_iota(jnp.int32, (R, A_VAL_W), 1) >> 6

    o = jnp.zeros((R, A_VAL_W), F32)
    for h in range(A_HEADS):
        qh = jnp.where(lane_k == h, qt, 0.0).astype(BF16)
        sc = lax.dot_general(qh, kt, (((1,), (1,)), ((), ())), preferred_element_type=F32)
        p = jnp.where(mask, sc, 0.0).astype(BF16)
        oh = jnp.dot(p, v16, preferred_element_type=F32)
        o = jnp.where(lane_v == h, oh, o)

    sr = lax.broadcasted_iota(jnp.int32, (A_VAL_W, A_KEY_W), 0) >> 6
    sl = lax.broadcasted_iota(jnp.int32, (A_VAL_W, A_KEY_W), 1) >> 5
    smask = sr == sl
    qt16 = qt.astype(BF16)
    inter = [None] * nch
    order = range(nch) if forward else range(nch - 1, -1, -1)
    for c in order:
        lo = c * GLA_CHUNK
        inter[c] = lax.dot_general(qt16[lo:lo + GLA_CHUNK], st.astype(BF16),
                                   (((1,), (1,)), ((), ())), preferred_element_type=F32)
        ds = lax.dot_general(v16[lo:lo + GLA_CHUNK], kend[lo:lo + GLA_CHUNK],
                             (((0,), (0,)), ((), ())), preferred_element_type=F32)
        st = st * dec3[c] + jnp.where(smask, ds, 0.0)
    return o + jnp.concatenate(inter, axis=0), st


def _gla_body(gla_ref, lr_ref, wg_ref, gb_ref, s0_ref, gn_ref, gm_ref, o_ref, sfin_ref, la_scr, *, seq):
    ntile = seq // GLA_TILE
    logit = _bdot(lr_ref[...], wg_ref[...]) + gb_ref[...]
    la_scr[...] = _log_sigmoid(logit) * (1.0 / GATE_TEMP)

    def run(forward):
        d = 0 if forward else 1

        def step(i, st):
            t = i if forward else ntile - 1 - i
            r0 = pl.multiple_of(t * GLA_TILE, GLA_TILE)
            rows = pl.ds(r0, GLA_TILE)
            q = gla_ref[rows, 0:128]
            k = gla_ref[rows, 128:256]
            v16 = gla_ref[rows, 256:512].astype(BF16)
            la = la_scr[rows, d * A_KEY_W:(d + 1) * A_KEY_W]
            o, st = _gla_tile(q, k, v16, la, st, forward)
            if forward:
                o_ref[rows, :] = o
            else:
                o_ref[rows, :] += o
            return st

        sfin_ref[0, d] = lax.fori_loop(0, ntile, step, s0_ref[0, d])

    run(True)
    run(False)
    o = o_ref[...]
    ms = _dot2(o * o, gm_ref[...])
    g = gla_ref[:, 512:768]
    o_ref[...] = o * lax.rsqrt(ms + EPS) * gn_ref[...] * (g * _sigmoid(g))


def _gla(proj, wg, gb, s0t, gnorm, gmat, *, nseq, seq, row_block0):
    return pl.pallas_call(
        functools.partial(_gla_body, seq=seq),
        grid=(nseq,),
        in_specs=[
            pl.BlockSpec((seq, 768), lambda b: (row_block0 + b, 0)),
            pl.BlockSpec((seq, LANES), lambda b: (row_block0 + b, COL_LR // LANES)),
            pl.BlockSpec((LANES, 2 * A_KEY_W), lambda b: (0, 0)),
            pl.BlockSpec((1, 2 * A_KEY_W), lambda b: (0, 0)),
            pl.BlockSpec((1, 2, A_VAL_W, A_KEY_W), lambda b: (b, 0, 0, 0)),
            pl.BlockSpec((1, A_VAL_W), lambda b: (0, 0)),
            pl.BlockSpec((A_VAL_W, A_VAL_W), lambda b: (0, 0)),
        ],
        out_specs=[
            pl.BlockSpec((seq, A_VAL_W), lambda b: (b, 0)),
            pl.BlockSpec((1, 2, A_VAL_W, A_KEY_W), lambda b: (b, 0, 0, 0)),
        ],
        out_shape=[
            jax.ShapeDtypeStruct((nseq * seq, A_VAL_W), F32),
            jax.ShapeDtypeStruct((nseq, 2, A_VAL_W, A_KEY_W), F32),
        ],
        scratch_shapes=[pltpu.VMEM((seq, 2 * A_KEY_W), F32)],
        compiler_params=_params(("parallel",)),
        name=f"gla{seq}",
    )(proj, proj, wg, gb, s0t, gnorm, gmat)


def _fnet_body(x_ref, cb_ref, sb_ref, ct_ref, st_ref, o_ref):
    x = x_ref[...].astype(BF16)
    xc = jnp.dot(x, cb_ref[...], preferred_element_type=F32).astype(BF16)
    xs = jnp.dot(x, sb_ref[...], preferred_element_type=F32).astype(BF16)
    o_ref[...] = (jnp.dot(ct_ref[...], xc, preferred_element_type=F32)
                  - jnp.dot(st_ref[...], xs, preferred_element_type=F32))


def _fnet(proj, cb, sb, ct, st, *, nseq, seq, row_block0):
    return pl.pallas_call(
        _fnet_body,
        grid=(nseq,),
        in_specs=[
            pl.BlockSpec((seq, F_WIDTH), lambda b: (row_block0 + b, COL_PF // F_WIDTH)),
            pl.BlockSpec((F_WIDTH, F_WIDTH), lambda b: (0, 0)),
            pl.BlockSpec((F_WIDTH, F_WIDTH), lambda b: (0, 0)),
            pl.BlockSpec((seq, seq), lambda b: (0, 0)),
            pl.BlockSpec((seq, seq), lambda b: (0, 0)),
        ],
        out_specs=pl.BlockSpec((seq, F_WIDTH), lambda b: (b, 0)),
        out_shape=jax.ShapeDtypeStruct((nseq * seq, F_WIDTH), F32),
        compiler_params=_params(("parallel",)),
        name=f"fnet{seq}",
    )(proj, cb, sb, ct, st)


def _dft_mats(n, scale):
    j = lax.broadcasted_iota(jnp.int32, (n, n), 0)
    k = lax.broadcasted_iota(jnp.int32, (n, n), 1)
    ang = ((j * k) % n).astype(F32) * (2.0 * jnp.pi / n)
    return jnp.cos(ang) * scale, jnp.sin(ang) * scale


def _head_rms(x, gm16, gain):
    ms = _dot2(x * x, gm16)
    return x * lax.rsqrt(ms + EPS) * gain


def _rope(x, cos, sin_signed):
    w = x.shape[1]
    lane = lax.broadcasted_iota(jnp.int32, x.shape, 1)
    partner = jnp.where((lane & 31) < 16, pltpu.roll(x, w - 16, 1), pltpu.roll(x, 16, 1))
    return x * cos + partner * sin_signed


def _attn_body(*refs, rope, n_new, n_past):
    if n_past:
        (q_ref, k_ref, v_ref, ck_ref, cv_ref, cq_ref, sq_ref, cosk_ref, sink_ref,
         qn_ref, kn_ref, gm_ref, o_ref, ka_ref, k_scr, ks_scr, v_scr, vs_scr) = refs
    else:
        (q_ref, k_ref, v_ref, qn_ref, kn_ref, gm_ref, o_ref, ka_ref,
         k_scr, ks_scr, v_scr, vs_scr) = refs
    gm = gm_ref[...]

    @pl.when(pl.program_id(1) == 0)
    def _():
        kn = _head_rms(k_ref[...], gm, kn_ref[...])
        ka_ref[...] = kn
        if rope:
            kn = _rope(kn, cosk_ref[...], sink_ref[...])
        v = v_ref[...]
        new = pl.ds(0, n_new)
        k_scr[new, :] = kn.astype(BF16)
        ks_scr[new, :] = pltpu.roll(kn, HEAD_DIM, 1).astype(BF16)
        v_scr[new, :] = v.astype(BF16)
        vs_scr[new, :] = pltpu.roll(v, HEAD_DIM, 1).astype(BF16)
        if n_past:
            past = pl.ds(n_new, n_past)
            ck = ck_ref[0]
            cv = cv_ref[0]
            k_scr[past, :] = ck.astype(BF16)
            ks_scr[past, :] = pltpu.roll(ck, HEAD_DIM, 1).astype(BF16)
            v_scr[past, :] = cv.astype(BF16)
            vs_scr[past, :] = pltpu.roll(cv, HEAD_DIM, 1).astype(BF16)

    lane = lax.broadcasted_iota(jnp.int32, (Q_TILE, LANES), 1)
    low = lane < HEAD_DIM
    for jp in range(C_Q_W // LANES):
        kv = jp // 2
        q2 = _head_rms(q_ref[:, jp * LANES:(jp + 1) * LANES], gm, qn_ref[...])
        if rope:
            q2 = _rope(q2, cq_ref[...], sq_ref[...])
        q2 = q2 * (HEAD_DIM ** -0.5)
        halves = []
        for half in range(2):
            swapped = (half == 1) != (kv == 1)
            kk = ks_scr[...] if swapped else k_scr[...]
            vv = vs_scr[...] if swapped else v_scr[...]
            qm = jnp.where(low if half == 0 else ~low, q2, 0.0).astype(BF16)
            s = lax.dot_general(qm, kk, (((1,), (1,)), ((), ())), preferred_element_type=F32)
            p = jnp.exp(s - jnp.max(s, axis=-1, keepdims=True))
            den = jnp.sum(p, axis=-1, keepdims=True)
            halves.append(jnp.dot(p.astype(BF16), vv, preferred_element_type=F32) / den)
        o_ref[:, jp * LANES:(jp + 1) * LANES] = jnp.where(low, halves[0], halves[1])


def _attn(proj, qn, kn, gm, *, nseq, seq, row_block0, past=None, tables=None):
    nq = seq // Q_TILE
    n_past = 0 if past is None else past[0].shape[1]
    qrow = lambda b, j: ((row_block0 + b) * nq + j, COL_CQ // C_Q_W)
    in_specs = [
        pl.BlockSpec((Q_TILE, C_Q_W), qrow),
        pl.BlockSpec((seq, C_KV_W), lambda b, j: (row_block0 + b, COL_CK // C_KV_W)),
        pl.BlockSpec((seq, C_KV_W), lambda b, j: (row_block0 + b, COL_CV // C_KV_W)),
    ]
    args = [proj, proj, proj]
    if past is not None:
        in_specs += [pl.BlockSpec((1, n_past, C_KV_W), lambda b, j: (b, 0, 0))] * 2
        in_specs += [pl.BlockSpec((Q_TILE, LANES), lambda b, j: (j, 0))] * 2
        in_specs += [pl.BlockSpec((seq, LANES), lambda b, j: (0, 0))] * 2
        args += [past[0], past[1], tables[0], tables[1], tables[0], tables[1]]
    in_specs += [
        pl.BlockSpec((1, LANES), lambda b, j: (0, 0)),
        pl.BlockSpec((1, LANES), lambda b, j: (0, 0)),
        pl.BlockSpec((LANES, LANES), lambda b, j: (0, 0)),
    ]
    args += [qn, kn, gm]
    s_all = seq + n_past
    return pl.pallas_call(
        functools.partial(_attn_body, rope=past is not None, n_new=seq, n_past=n_past),
        grid=(nseq, nq),
        in_specs=in_specs,
        out_specs=[
            pl.BlockSpec((Q_TILE, C_Q_W), lambda b, j: (b * nq + j, 0)),
            pl.BlockSpec((seq, C_KV_W), lambda b, j: (b, 0)),
        ],
        out_shape=[
            jax.ShapeDtypeStruct((nseq * seq, C_Q_W), F32),
            jax.ShapeDtypeStruct((nseq * seq, C_KV_W), F32),
        ],
        scratch_shapes=[pltpu.VMEM((s_all, C_KV_W), BF16)] * 4,
        compiler_params=_params(("parallel", "arbitrary")),
        name=f"attn{seq}",
    )(*args)


def _rope_tables():
    t = jnp.arange(DEC_SEQ)
    rows = (t // GRID_W).astype(F32)
    cols = (t % GRID_W).astype(F32)
    nf = HEAD_DIM // 4
    inv_freq = ROPE_THETA ** (-jnp.arange(nf, dtype=F32) / nf)
    d = jnp.arange(LANES)
    pos = jnp.where(((d % HEAD_DIM) < HEAD_DIM // 2)[None, :], rows[:, None], cols[:, None])
    ang = pos * inv_freq[d % nf][None, :]
    sign = jnp.where((d % 32) < 16, -1.0, 1.0)[None, :]
    return jnp.cos(ang), jnp.sin(ang) * sign


def _outproj_body(oa_ref, of_ref, oc_ref, w_ref, x_ref, mod_ref, g_ref, rw_ref, rb_ref,
                  x1_ref, h2_ref, meta_ref):
    mod = mod_ref[0]
    mixed = (jnp.dot(oa_ref[...].astype(BF16), w_ref[0:256, :], preferred_element_type=F32)
             + jnp.dot(of_ref[...].astype(BF16), w_ref[256:512, :], preferred_element_type=F32)
             + jnp.dot(oc_ref[...].astype(BF16), w_ref[512:1024, :], preferred_element_type=F32))
    x1 = x_ref[...] + mod[2:3] * mixed
    x1_ref[...] = x1
    h2 = _rms(x1) * g_ref[...] * (1.0 + mod[4:5]) + mod[3:4]
    h2_ref[...] = h2.astype(BF16)
    lane = lax.broadcasted_iota(jnp.int32, (TM, LANES), 1)
    logits = jnp.where(lane < N_EXPERTS, _dot3(h2, rw_ref[...]) + rb_ref[...], -jnp.inf)
    meta = jnp.zeros((TM, LANES), F32)
    tops = []
    for k in range(TOP_K):
        m = jnp.max(logits, axis=-1, keepdims=True)
        idx = jnp.min(jnp.where(logits == m, lane, LANES), axis=-1, keepdims=True)
        hit = lane == idx
        logits = jnp.where(hit, -jnp.inf, logits)
        meta = jnp.where(lane == k, idx.astype(F32), meta)
        tops.append(m)
    ex = [jnp.exp(t - tops[0]) for t in tops]
    den = ex[0] + ex[1] + ex[2] + ex[3]
    for k in range(TOP_K):
        meta = jnp.where(lane == TOP_K + k, ex[k] / den, meta)
    meta_ref[...] = meta


def _outproj(oa, of, oc, w, x, mod3, gain, rw, rb):
    tok = lambda w_: pl.BlockSpec((TM, w_), lambda i: (i, 0))
    full = lambda a, b: pl.BlockSpec((a, b), lambda i: (0, 0))
    return pl.pallas_call(
        _outproj_body,
        grid=(N_TOK // TM,),
        in_specs=[tok(A_VAL_W), tok(F_WIDTH), tok(C_Q_W), full(D_MODEL, D_MODEL), tok(D_MODEL),
                  pl.BlockSpec((1, 6, D_MODEL), lambda i: (_mod_row(i, TM), 0, 0)),
                  full(1, D_MODEL), full(D_MODEL, LANES), full(1, LANES)],
        out_specs=[tok(D_MODEL), tok(D_MODEL), tok(LANES)],
        out_shape=[jax.ShapeDtypeStruct((N_TOK, D_MODEL), F32),
                   jax.ShapeDtypeStruct((N_TOK, D_MODEL), BF16),
                   jax.ShapeDtypeStruct((N_TOK, LANES), F32)],
        compiler_params=_params(("parallel",)),
        name="outproj",
    )(oa, of, oc, w, x, mod3, gain, rw, rb)


def _pos_body(meta_ref, slot_ref, info_ref):
    meta = meta_ref[...]
    lane = lax.broadcasted_iota(jnp.int32, (ROW_TILE, LANES), 1)
    idx = [meta[:, k:k + 1].astype(jnp.int32) for k in range(TOP_K)]
    sel = jnp.zeros((ROW_TILE, LANES), F32)
    for k in range(TOP_K):
        sel = jnp.where(lane == idx[k], 1.0, sel)
    ri = lax.broadcasted_iota(jnp.int32, (ROW_TILE, ROW_TILE), 0)
    ci = lax.broadcasted_iota(jnp.int32, (ROW_TILE, ROW_TILE), 1)
    lower = jnp.where(ci < ri, 1.0, 0.0).astype(BF16)
    rank = jnp.dot(lower, sel.astype(BF16), preferred_element_type=F32)
    cnt = jnp.sum(sel, axis=0, keepdims=True)
    n8 = jnp.floor((cnt + (RUN_ALIGN - 1)) * (1.0 / RUN_ALIGN)) * RUN_ALIGN
    er = lax.broadcasted_iota(jnp.int32, (LANES, LANES), 0)
    ec = lax.broadcasted_iota(jnp.int32, (LANES, LANES), 1)
    before = jnp.where(er < ec, 1.0, 0.0).astype(BF16)
    off = jnp.dot(jnp.broadcast_to(n8, (8, LANES)).astype(BF16), before, preferred_element_type=F32)
    tot = rank + off[0:1]
    out = jnp.zeros((ROW_TILE, LANES), F32)
    for k in range(TOP_K):
        pk = jnp.sum(jnp.where(lane == idx[k], tot, 0.0), axis=-1, keepdims=True)
        out = jnp.where(lane == k, pk, out)
    slot_ref[...] = out
    row = lax.broadcasted_iota(jnp.int32, (8, LANES), 0)
    info_ref[0] = jnp.where(row == 0, n8, jnp.where(row == 1, off, 0.0))


def _positions(meta):
    return pl.pallas_call(
        _pos_body,
        grid=(N_TILES,),
        in_specs=[pl.BlockSpec((ROW_TILE, LANES), lambda i: (i, 0))],
        out_specs=[pl.BlockSpec((ROW_TILE, LANES), lambda i: (i, 0)),
                   pl.BlockSpec((1, 8, LANES), lambda i: (i, 0, 0))],
        out_shape=[jax.ShapeDtypeStruct((N_TOK, LANES), F32),
                   jax.ShapeDtypeStruct((N_TILES, 8, LANES), F32)],
        compiler_params=_params(("parallel",)),
        name="positions",
    )(meta)


def _run_copies(tile, n8_tbl, off_tbl, base_tbl, make_copy, action):
    def per_expert(e, c):
        j = tile * N_EXPERTS + e
        m = lax.shift_right_logical(n8_tbl[j], RUN_SHIFT)
        for bit in range(RUN_BITS):
            rows = RUN_ALIGN << bit
            lo = (m & ((1 << bit) - 1)) * RUN_ALIGN

            @pl.when(((m >> bit) & 1) == 1)
            def _():
                cp = make_copy(pl.multiple_of(off_tbl[j] + lo, RUN_ALIGN),
                               pl.multiple_of(base_tbl[j] + lo, RUN_ALIGN), rows)
                if action == "start":
                    cp.start()
                else:
                    cp.wait()
        return c

    lax.fori_loop(0, N_EXPERTS, per_expert, 0)


def _unpack_pairs(u):
    hi = lax.bitcast_convert_type(u & jnp.uint32(0xFFFF0000), F32)
    lo = lax.bitcast_convert_type(u << 16, F32)
    return hi, lo


def _pack_pairs(a, b):
    return lax.bitcast_convert_type(a, jnp.uint32) | (lax.bitcast_convert_type(b, jnp.uint32) >> 16)


def _slot_onehot(slot, weights=None):
    lane = lax.broadcasted_iota(jnp.int32, (ROW_TILE, SORT_ROWS), 1)
    p = jnp.zeros((ROW_TILE, SORT_ROWS), F32)
    for k in range(TOP_K):
        hit = lane == slot[:, k:k + 1].astype(jnp.int32)
        p = jnp.where(hit, 1.0 if weights is None else weights[:, k:k + 1], p)
    return p.astype(BF16)


def _dispatch_body(n8_tbl, off_tbl, base_tbl, h_ref, slot_ref, init_hbm, xs_hbm, sorted_scr, sem):
    del init_hbm
    p = _slot_onehot(slot_ref[...])
    srt = lax.dot_general(p, h_ref[...], (((0,), (0,)), ((), ())), preferred_element_type=F32)
    sorted_scr[...] = _pack_pairs(srt[:, :HALF_D], srt[:, HALF_D:])

    def make_copy(src_row, dst_row, rows):
        return pltpu.make_async_copy(sorted_scr.at[pl.ds(src_row, rows)], xs_hbm.at[pl.ds(dst_row, rows)], sem)

    tile = pl.program_id(0)
    _run_copies(tile, n8_tbl, off_tbl, base_tbl, make_copy, "start")
    _run_copies(tile, n8_tbl, off_tbl, base_tbl, make_copy, "wait")


def _dispatch(tables, h2, slots, init):
    grid_spec = pltpu.PrefetchScalarGridSpec(
        num_scalar_prefetch=3,
        grid=(N_TILES,),
        in_specs=[pl.BlockSpec((ROW_TILE, D_MODEL), lambda i, *_: (i, 0)),
                  pl.BlockSpec((ROW_TILE, LANES), lambda i, *_: (i, 0)),
                  pl.BlockSpec(memory_space=pl.ANY)],
        out_specs=pl.BlockSpec(memory_space=pl.ANY),
        scratch_shapes=[pltpu.VMEM((SORT_ROWS, HALF_D), jnp.uint32), pltpu.SemaphoreType.DMA],
    )
    return pl.pallas_call(
        _dispatch_body,
        grid_spec=grid_spec,
        out_shape=jax.ShapeDtypeStruct((MOE_ROWS, HALF_D), jnp.uint32),
        input_output_aliases={5: 0},
        compiler_params=_params(("arbitrary",)),
        name="dispatch",
    )(*tables, h2, slots, init)


def _experts_body(bexp_ref, nused_ref, x_ref, wgu_ref, bgu_ref, wdn_ref, bdn_ref, y_ref, wgu16, wdn16):
    i = pl.program_id(0)

    @pl.when((i == 0) | (bexp_ref[i] != bexp_ref[jnp.maximum(i - 1, 0)]))
    def _():
        wgu16[...] = wgu_ref[0, 0].astype(BF16)
        wdn16[...] = wdn_ref[0, 0].astype(BF16)

    @pl.when(i < nused_ref[0])
    def _():
        x = jnp.concatenate(_unpack_pairs(x_ref[...]), axis=1).astype(BF16)
        gu = jnp.dot(x, wgu16[...], preferred_element_type=F32) + bgu_ref[0, 0]
        gate = jnp.minimum(gu[:, :D_FF], SWIGLU_LIMIT)
        up = jnp.clip(gu[:, D_FF:], -SWIGLU_LIMIT, SWIGLU_LIMIT)
        glu = gate * _sigmoid(gate * SWIGLU_ALPHA)
        act = ((up + 1.0) * glu).astype(BF16)
        y = jnp.dot(act, wdn16[...], preferred_element_type=F32) + bdn_ref[0, 0]
        y = y.astype(BF16).astype(F32)
        y_ref[...] = _pack_pairs(y[:, :HALF_D], y[:, HALF_D:])

    @pl.when(i >= nused_ref[0])
    def _():
        y_ref[...] = jnp.zeros_like(y_ref)


def _experts(blk_exp, nused, xs, wgu, bgu, wdn, bdn, layer):
    e = lambda i, be, nu: (layer, be[i], 0, 0)
    grid_spec = pltpu.PrefetchScalarGridSpec(
        num_scalar_prefetch=2,
        grid=(MOE_NB,),
        in_specs=[
            pl.BlockSpec((MOE_BM, HALF_D), lambda i, be, nu: (i, 0)),
            pl.BlockSpec((1, 1, D_MODEL, 2 * D_FF), e),
            pl.BlockSpec((1, 1, 1, 2 * D_FF), e),
            pl.BlockSpec((1, 1, D_FF, D_MODEL), e),
            pl.BlockSpec((1, 1, 1, D_MODEL), e),
        ],
        out_specs=pl.BlockSpec((MOE_BM, HALF_D), lambda i, be, nu: (i, 0)),
        scratch_shapes=[pltpu.VMEM((D_MODEL, 2 * D_FF), BF16), pltpu.VMEM((D_FF, D_MODEL), BF16)],
    )
    return pl.pallas_call(
        _experts_body,
        grid_spec=grid_spec,
        out_shape=jax.ShapeDtypeStruct((MOE_ROWS, HALF_D), jnp.uint32),
        compiler_params=_params(("arbitrary",)),
        name="experts",
    )(blk_exp, nused, xs, wgu, bgu, wdn, bdn)


def _combine_body(n8_tbl, off_tbl, base_tbl, y_hbm, slot_ref, meta_ref, x1_ref, mod_ref, nf_ref, o_ref,
                  ybuf, sem, *, final):
    ybuf[...] = jnp.zeros_like(ybuf)

    def make_copy(buf_row, hbm_row, rows):
        return pltpu.make_async_copy(y_hbm.at[pl.ds(hbm_row, rows)], ybuf.at[pl.ds(buf_row, rows)], sem)

    tile = pl.program_id(0)
    _run_copies(tile, n8_tbl, off_tbl, base_tbl, make_copy, "start")
    _run_copies(tile, n8_tbl, off_tbl, base_tbl, make_copy, "wait")
    pw = _slot_onehot(slot_ref[...], meta_ref[...][:, TOP_K:2 * TOP_K])
    hi, lo = _unpack_pairs(ybuf[...])
    ff = jnp.concatenate([jnp.dot(pw, hi.astype(BF16), preferred_element_type=F32),
                          jnp.dot(pw, lo.astype(BF16), preferred_element_type=F32)], axis=1)
    x2 = x1_ref[...] + mod_ref[0][5:6] * ff
    if final:
        x2 = _rms(x2) * nf_ref[...]
    o_ref[...] = x2


def _combine(tables, y, slots, meta, x1, mod3, nfinal, *, final):
    grid_spec = pltpu.PrefetchScalarGridSpec(
        num_scalar_prefetch=3,
        grid=(N_TILES,),
        in_specs=[pl.BlockSpec(memory_space=pl.ANY),
                  pl.BlockSpec((ROW_TILE, LANES), lambda i, *_: (i, 0)),
                  pl.BlockSpec((ROW_TILE, LANES), lambda i, *_: (i, 0)),
                  pl.BlockSpec((ROW_TILE, D_MODEL), lambda i, *_: (i, 0)),
                  pl.BlockSpec((1, 6, D_MODEL), lambda i, *_: (_mod_row(i, ROW_TILE), 0, 0)),
                  pl.BlockSpec((1, D_MODEL), lambda i, *_: (0, 0))],
        out_specs=pl.BlockSpec((ROW_TILE, D_MODEL), lambda i, *_: (i, 0)),
        scratch_shapes=[pltpu.VMEM((SORT_ROWS, HALF_D), jnp.uint32), pltpu.SemaphoreType.DMA],
    )
    return pl.pallas_call(
        functools.partial(_combine_body, final=final),
        grid_spec=grid_spec,
        out_shape=jax.ShapeDtypeStruct((N_TOK, D_MODEL), F32),
        compiler_params=_params(("arbitrary",)),
        name="combine",
    )(*tables, y, slots, meta, x1, mod3, nfinal)


def _state_to_blockdiag(s):
    eye = jnp.eye(A_HEADS, dtype=s.dtype)
    t = jnp.einsum("bdhkv,hg->bdhvgk", s, eye)
    return t.reshape(s.shape[0], 2, A_VAL_W, A_KEY_W)


def _blockdiag_to_state(t):
    b = t.shape[0]
    t = t.reshape(b, 2, A_HEADS, A_DV, A_HEADS, A_DK)
    s = jnp.stack([t[:, :, h, :, h, :] for h in range(A_HEADS)], axis=2)
    return jnp.swapaxes(s, -1, -2)


def kernel(x_prompt, x_sample, state_gla, cache_k, cache_v, c, c_ctx, w_ada, b_ada, norm_mix, norm_ffn,
           norm_final, w_in, w_out, gla_gate_w, gla_gate_b, gla_norm, q_norm, k_norm, router_w, router_b,
           expert_w_gu, expert_b_gu, expert_w_dn, expert_b_dn):
    x = jnp.concatenate([x_prompt.reshape(N_CTX, D_MODEL), x_sample.reshape(N_DEC, D_MODEL)], axis=0)
    cond = jnp.zeros((MOD_ROWS, D_MODEL), F32).at[0].set(c_ctx).at[1:1 + DEC_BATCH].set(c)
    mod = _ada(cond, w_ada, b_ada)

    gm256 = (jnp.kron(jnp.eye(A_HEADS), jnp.ones((A_DV, A_DV))) / A_DV).astype(BF16)
    gm128 = (jnp.kron(jnp.eye(2), jnp.ones((HEAD_DIM, HEAD_DIM))) / HEAD_DIM).astype(BF16)
    c64, s64 = _dft_mats(F_GROUP_W, F_GROUP_W ** -0.5)
    cb = jnp.kron(jnp.eye(F_GROUPS), c64).astype(BF16)
    sb = jnp.kron(jnp.eye(F_GROUPS), s64).astype(BF16)
    dft_ctx = [m.astype(BF16) for m in _dft_mats(SEQ, SEQ ** -0.5)]
    dft_dec = [m.astype(BF16) for m in _dft_mats(DEC_SEQ, DEC_SEQ ** -0.5)]
    rope_tabs = _rope_tables()
    zero_state = jnp.zeros((BATCH, 2, A_VAL_W, A_KEY_W), F32)

    new_s, new_k, new_v = [], [], []
    for l in range(DEPTH):
        mod3 = mod[l].reshape(MOD_ROWS, 6, D_MODEL)
        w_l = w_in[l]
        w_r = jnp.concatenate([w_l[:, :768], w_l[:, 800:], w_l[:, 768:800],
                               jnp.zeros((D_MODEL, PROJ_W - 1824), F32)], axis=1).astype(BF16)
        proj = _inproj(x, mod3, norm_mix[l][None, :], w_r)

        wg = jnp.zeros((LANES, 2 * A_KEY_W), F32)
        wg = wg.at[0:GATE_RANK, 0:A_KEY_W].set(gla_gate_w[l, 0])
        wg = wg.at[GATE_RANK:2 * GATE_RANK, A_KEY_W:].set(gla_gate_w[l, 1])
        gb = gla_gate_b[l].reshape(1, 2 * A_KEY_W)
        gn = jnp.tile(gla_norm[l], A_HEADS)[None, :]
        s0_dec = _state_to_blockdiag(state_gla[:, l])
        oa_ctx, sfin = _gla(proj, wg, gb, zero_state, gn, gm256, nseq=BATCH, seq=SEQ, row_block0=0)
        oa_dec, _ = _gla(proj, wg, gb, s0_dec, gn, gm256, nseq=DEC_BATCH, seq=DEC_SEQ,
                         row_block0=N_CTX // DEC_SEQ)
        of_ctx = _fnet(proj, cb, sb, *dft_ctx, nseq=BATCH, seq=SEQ, row_block0=0)
        of_dec = _fnet(proj, cb, sb, *dft_dec, nseq=DEC_BATCH, seq=DEC_SEQ, row_block0=N_CTX // DEC_SEQ)
        qn = jnp.tile(q_norm[l], 2)[None, :]
        kn = jnp.tile(k_norm[l], 2)[None, :]
        oc_ctx, ka = _attn(proj, qn, kn, gm128, nseq=BATCH, seq=SEQ, row_block0=0)
        past = (cache_k[:, l].reshape(DEC_BATCH, PAST_LEN, C_KV_W),
                cache_v[:, l].reshape(DEC_BATCH, PAST_LEN, C_KV_W))
        oc_dec, _ = _attn(proj, qn, kn, gm128, nseq=DEC_BATCH, seq=DEC_SEQ,
                          row_block0=N_CTX // DEC_SEQ, past=past, tables=rope_tabs)
        new_s.append(_blockdiag_to_state(sfin))
        new_k.append(ka.reshape(BATCH, SEQ, C_KV_HEADS, HEAD_DIM))
        new_v.append(proj[:N_CTX, COL_CV:COL_CV + C_KV_W].reshape(BATCH, SEQ, C_KV_HEADS, HEAD_DIM))

        oa = jnp.concatenate([oa_ctx, oa_dec], axis=0)
        of = jnp.concatenate([of_ctx, of_dec], axis=0)
        oc = jnp.concatenate([oc_ctx, oc_dec], axis=0)
        rw = jnp.zeros((D_MODEL, LANES), F32).at[:, :N_EXPERTS].set(router_w[l])
        rb = jnp.zeros((1, LANES), F32).at[0, :N_EXPERTS].set(router_b[l])
        x1, h2, meta = _outproj(oa, of, oc, w_out[l].astype(BF16), x, mod3, norm_ffn[l][None, :], rw, rb)

        slots, info = _positions(meta)
        n8 = info[:, 0, :N_EXPERTS].astype(jnp.int32)
        off = info[:, 1, :N_EXPERTS].astype(jnp.int32)
        cnt = jnp.sum(n8, axis=0)
        pcnt = (cnt + MOE_BM - 1) // MOE_BM * MOE_BM
        pend = jnp.cumsum(pcnt)
        base = (pend - pcnt)[None, :] + jnp.cumsum(n8, axis=0) - n8
        tables = (n8.reshape(-1), off.reshape(-1), base.reshape(-1))
        blk_start = jnp.arange(MOE_NB, dtype=jnp.int32) * MOE_BM
        blk_exp = jnp.minimum(jnp.sum((pend[None, :] <= blk_start[:, None]).astype(jnp.int32), axis=1),
                              N_EXPERTS - 1)
        nused = (pend[-1:] // MOE_BM).astype(jnp.int32)

        xs = _dispatch(tables, h2, slots, jnp.zeros((MOE_ROWS, HALF_D), jnp.uint32))
        y = _experts(blk_exp, nused, xs, expert_w_gu,
                     expert_b_gu.reshape(DEPTH, N_EXPERTS, 1, 2 * D_FF), expert_w_dn,
                     expert_b_dn.reshape(DEPTH, N_EXPERTS, 1, D_MODEL), l)
        x = _combine(tables, y, slots, meta, x1, mod3, norm_final[None, :], final=(l == DEPTH - 1))

    y_prompt = x[:N_CTX].reshape(BATCH, SEQ, D_MODEL)
    y_sample = x[N_CTX:].reshape(DEC_BATCH, DEC_SEQ, D_MODEL)
    return (y_prompt, y_sample, jnp.stack(new_s, axis=1), jnp.stack(new_k, axis=1), jnp.stack(new_v, axis=1))
```

```python
import functools

import jax
import jax.numpy as jnp
from jax import lax
from jax.experimental import pallas as pl
from jax.experimental.pallas import tpu as pltpu

D_MODEL = 1024
BATCH = 32
SEQ = 256
DEPTH = 2
DEC_BATCH = 8
DEC_SEQ = 1024
PAST_LEN = 512
GRID_W = 64
A_HEADS = 4
A_DV = 64
A_DK = 32
A_KEY_W = 128
A_VAL_W = 256
GATE_RANK = 16
GATE_TEMP = 16.0
GLA_CHUNK = 32
F_GROUPS = 4
F_GROUP_W = 64
F_WIDTH = 256
HEAD_DIM = 64
C_HEADS = 8
C_KV_HEADS = 2
C_Q_W = 512
C_KV_W = 128
ROPE_THETA = 10000.0
N_EXPERTS = 32
TOP_K = 4
D_FF = 1024
SWIGLU_LIMIT = 7.0
SWIGLU_ALPHA = 1.702
EPS = 1e-6

N_CTX = BATCH * SEQ
N_DEC = DEC_BATCH * DEC_SEQ
N_TOK = N_CTX + N_DEC
MOD_ROWS = 16

COL_GLA = 0
COL_PF = 768
COL_CQ = 1024
COL_CK = 1536
COL_CV = 1664
COL_LR = 1792
PROJ_W = 1920

LANES = 128
TM = 512
GLA_TILE = 256
Q_TILE = 256
MOE_BM = 256
ROW_TILE = 256
N_TILES = N_TOK // ROW_TILE
RUN_ALIGN = 8
RUN_SHIFT = 3
RUN_BITS = 6
TAIL_BITS = 5
SORT_ROWS = ROW_TILE * TOP_K + N_EXPERTS * RUN_ALIGN
HALF_D = D_MODEL // 2
MOE_NB = (N_TOK * TOP_K + N_TILES * N_EXPERTS * (RUN_ALIGN - 1)) // MOE_BM + N_EXPERTS
MOE_ROWS = MOE_NB * MOE_BM
VMEM_LIMIT = 56 * 1024 * 1024

F32 = jnp.float32
BF16 = jnp.bfloat16


def _params(sem, vmem=VMEM_LIMIT):
    return pltpu.CompilerParams(dimension_semantics=sem, vmem_limit_bytes=vmem)


def _bdot(a, b):
    return jnp.dot(a.astype(BF16), b.astype(BF16), preferred_element_type=F32)


def _split(a):
    hi = a.astype(BF16)
    lo = (a - hi.astype(F32)).astype(BF16)
    return hi, lo


def _dot3(a, b):
    ah, al = _split(a)
    bh, bl = _split(b)
    d = functools.partial(jnp.dot, preferred_element_type=F32)
    return d(ah, bh) + (d(ah, bl) + d(al, bh))


def _dot2(a, b16):
    ah, al = _split(a)
    d = functools.partial(jnp.dot, preferred_element_type=F32)
    return d(ah, b16) + d(al, b16)


def _sigmoid(x):
    return 1.0 / (1.0 + jnp.exp(-x))


def _mod_row(i, tm):
    tok = i * tm
    return jnp.where(tok < N_CTX, 0, 1 + (tok - N_CTX) // DEC_SEQ)


def _ada_body(c_ref, w_ref, b_ref, o_ref):
    c = c_ref[...]
    o_ref[0] = _dot3(c * _sigmoid(c), w_ref[0]) + b_ref[0]


def _ada(cond, w_ada, b_ada):
    return pl.pallas_call(
        _ada_body,
        grid=(DEPTH, 6),
        in_specs=[
            pl.BlockSpec((MOD_ROWS, D_MODEL), lambda l, j: (0, 0)),
            pl.BlockSpec((1, D_MODEL, D_MODEL), lambda l, j: (l, 0, j)),
            pl.BlockSpec((1, 1, D_MODEL), lambda l, j: (l, 0, j)),
        ],
        out_specs=pl.BlockSpec((1, MOD_ROWS, D_MODEL), lambda l, j: (l, 0, j)),
        out_shape=jax.ShapeDtypeStruct((DEPTH, MOD_ROWS, 6 * D_MODEL), F32),
        compiler_params=_params(("parallel", "parallel")),
        name="ada",
    )(cond, w_ada, b_ada.reshape(DEPTH, 1, 6 * D_MODEL))


def _rms(x):
    return x * lax.rsqrt(jnp.mean(x * x, axis=-1, keepdims=True) + EPS)


def _inproj_body(x_ref, mod_ref, g_ref, w_ref, o_ref):
    mod = mod_ref[0]
    h = _rms(x_ref[...]) * g_ref[...] * (1.0 + mod[1:2]) + mod[0:1]
    o_ref[...] = jnp.dot(h.astype(BF16), w_ref[...], preferred_element_type=F32)


def _inproj(x, mod3, gain, w):
    return pl.pallas_call(
        _inproj_body,
        grid=(N_TOK // TM,),
        in_specs=[
            pl.BlockSpec((TM, D_MODEL), lambda i: (i, 0)),
            pl.BlockSpec((1, 6, D_MODEL), lambda i: (_mod_row(i, TM), 0, 0)),
            pl.BlockSpec((1, D_MODEL), lambda i: (0, 0)),
            pl.BlockSpec((D_MODEL, PROJ_W), lambda i: (0, 0)),
        ],
        out_specs=pl.BlockSpec((TM, PROJ_W), lambda i: (i, 0)),
        out_shape=jax.ShapeDtypeStruct((N_TOK, PROJ_W), F32),
        compiler_params=_params(("parallel",)),
        name="inproj",
    )(x, mod3, gain, w)


def _log_sigmoid(x):
    return jnp.minimum(x, 0.0) - jnp.log(1.0 + jnp.exp(-jnp.abs(x)))


def _chunk_cumsum(x, pos, forward):
    rows = x.shape[0]
    for s in (1, 2, 4, 8, 16):
        if forward:
            x = x + jnp.where(pos >= s, pltpu.roll(x, s, 0), 0.0)
        else:
            x = x + jnp.where(pos < GLA_CHUNK - s, pltpu.roll(x, rows - s, 0), 0.0)
    return x


def _gla_tile(q, k, v16, la, st, forward):
    R = GLA_TILE
    nch = R // GLA_CHUNK
    pos = lax.broadcasted_iota(jnp.int32, (R, 1), 0) & (GLA_CHUNK - 1)
    b = _chunk_cumsum(la, pos, forward)
    b3 = b.reshape(nch, GLA_CHUNK, A_KEY_W)
    tot3 = b3[:, GLA_CHUNK - 1:GLA_CHUNK, :] if forward else b3[:, 0:1, :]
    totb = jnp.broadcast_to(tot3, (nch, GLA_CHUNK, A_KEY_W)).reshape(R, A_KEY_W)
    qt = q * (A_DK ** -0.5) * jnp.exp(b)
    kt = (k * jnp.exp(-b)).astype(BF16)
    kend = (k * jnp.exp(totb - b)).astype(BF16)
    dec3 = jnp.exp(tot3)

    ri = lax.broadcasted_iota(jnp.int32, (R, R), 0)
    ci = lax.broadcasted_iota(jnp.int32, (R, R), 1)
    tri = (ci <= ri) if forward else (ci >= ri)
    mask = ((ri >> 5) == (ci >> 5)) & tri
    lane_k = lax.broadcasted_iota(jnp.int32, (R, A_KEY_W), 1) >> 5
    lane_v = lax.broadcasted_iota(jnp.int32, (R, A_VAL_W), 1) >> 6

    o = jnp.zeros((R, A_VAL_W), F32)
    for h in range(A_HEADS):
        qh = jnp.where(lane_k == h, qt, 0.0).astype(BF16)
        sc = lax.dot_general(qh, kt, (((1,), (1,)), ((), ())), preferred_element_type=F32)
        p = jnp.where(mask, sc, 0.0).astype(BF16)
        oh = jnp.dot(p, v16, preferred_element_type=F32)
        o = jnp.where(lane_v == h, oh, o)

    sr = lax.broadcasted_iota(jnp.int32, (A_VAL_W, A_KEY_W), 0) >> 6
    sl = lax.broadcasted_iota(jnp.int32, (A_VAL_W, A_KEY_W), 1) >> 5
    smask = sr == sl
    qt16 = qt.astype(BF16)
    inter = [None] * nch
    order = range(nch) if forward else range(nch - 1, -1, -1)
    for c in order:
        lo = c * GLA_CHUNK
        inter[c] = lax.dot_general(qt16[lo:lo + GLA_CHUNK], st.astype(BF16),
                                   (((1,), (1,)), ((), ())), preferred_element_type=F32)
        ds = lax.dot_general(v16[lo:lo + GLA_CHUNK], kend[lo:lo + GLA_CHUNK],
                             (((0,), (0,)), ((), ())), preferred_element_type=F32)
        st = st * dec3[c] + jnp.where(smask, ds, 0.0)
    return o + jnp.concatenate(inter, axis=0), st


def _gla_body(gla_ref, lr_ref, wg_ref, gb_ref, s0_ref, gn_ref, gm_ref, o_ref, sfin_ref, la_scr, *, seq):
    ntile = seq // GLA_TILE
    logit = _bdot(lr_ref[...], wg_ref[...]) + gb_ref[...]
    la_scr[...] = _log_sigmoid(logit) * (1.0 / GATE_TEMP)

    def run(forward):
        d = 0 if forward else 1

        def step(i, st):
            t = i if forward else ntile - 1 - i
            r0 = pl.multiple_of(t * GLA_TILE, GLA_TILE)
            rows = pl.ds(r0, GLA_TILE)
            q = gla_ref[rows, 0:128]
            k = gla_ref[rows, 128:256]
            v16 = gla_ref[rows, 256:512].astype(BF16)
            la = la_scr[rows, d * A_KEY_W:(d + 1) * A_KEY_W]
            o, st = _gla_tile(q, k, v16, la, st, forward)
            if forward:
                o_ref[rows, :] = o
            else:
                o_ref[rows, :] += o
            return st

        sfin_ref[0, d] = lax.fori_loop(0, ntile, step, s0_ref[0, d])

    run(True)
    run(False)
    o = o_ref[...]
    ms = _dot2(o * o, gm_ref[...])
    g = gla_ref[:, 512:768]
    o_ref[...] = o * lax.rsqrt(ms + EPS) * gn_ref[...] * (g * _sigmoid(g))


def _gla(proj, wg, gb, s0t, gnorm, gmat, *, nseq, seq, row_block0):
    in_specs = [
        pl.BlockSpec((seq, 768), lambda b: (row_block0 + b, 0)),
        pl.BlockSpec((seq, LANES), lambda b: (row_block0 + b, COL_LR // LANES)),
        pl.BlockSpec((LANES, 2 * A_KEY_W), lambda b: (0, 0)),
        pl.BlockSpec((1, 2 * A_KEY_W), lambda b: (0, 0)),
        pl.BlockSpec((1, 2, A_VAL_W, A_KEY_W), lambda b: (b, 0, 0, 0)),
        pl.BlockSpec((1, A_VAL_W), lambda b: (0, 0)),
        pl.BlockSpec((A_VAL_W, A_VAL_W), lambda b: (0, 0)),
    ]
    return pl.pallas_call(
        functools.partial(_gla_body, seq=seq),
        grid=(nseq,),
        in_specs=in_specs,
        out_specs=[
            pl.BlockSpec((seq, A_VAL_W), lambda b: (b, 0)),
            pl.BlockSpec((1, 2, A_VAL_W, A_KEY_W), lambda b: (b, 0, 0, 0)),
        ],
        out_shape=[
            jax.ShapeDtypeStruct((nseq * seq, A_VAL_W), F32),
            jax.ShapeDtypeStruct((nseq, 2, A_VAL_W, A_KEY_W), F32),
        ],
        scratch_shapes=[pltpu.VMEM((seq, 2 * A_KEY_W), F32)],
        compiler_params=_params(("parallel",)),
        name=f"gla{seq}",
    )(proj, proj, wg, gb, s0t, gnorm, gmat)


def _fnet_body(x_ref, cb_ref, sb_ref, ct_ref, st_ref, o_ref):
    x = x_ref[...].astype(BF16)
    xc = jnp.dot(x, cb_ref[...], preferred_element_type=F32).astype(BF16)
    xs = jnp.dot(x, sb_ref[...], preferred_element_type=F32).astype(BF16)
    o_ref[...] = (jnp.dot(ct_ref[...], xc, preferred_element_type=F32)
                  - jnp.dot(st_ref[...], xs, preferred_element_type=F32))


def _fnet(proj, cb, sb, ct, st, *, nseq, seq, row_block0):
    in_specs = [
        pl.BlockSpec((seq, F_WIDTH), lambda b: (row_block0 + b, COL_PF // F_WIDTH)),
        pl.BlockSpec((F_WIDTH, F_WIDTH), lambda b: (0, 0)),
        pl.BlockSpec((F_WIDTH, F_WIDTH), lambda b: (0, 0)),
        pl.BlockSpec((seq, seq), lambda b: (0, 0)),
        pl.BlockSpec((seq, seq), lambda b: (0, 0)),
    ]
    return pl.pallas_call(
        _fnet_body,
        grid=(nseq,),
        in_specs=in_specs,
        out_specs=pl.BlockSpec((seq, F_WIDTH), lambda b: (b, 0)),
        out_shape=jax.ShapeDtypeStruct((nseq * seq, F_WIDTH), F32),
        compiler_params=_params(("parallel",)),
        name=f"fnet{seq}",
    )(proj, cb, sb, ct, st)


def _dft_mats(n, scale):
    j = lax.broadcasted_iota(jnp.int32, (n, n), 0)
    k = lax.broadcasted_iota(jnp.int32, (n, n), 1)
    ang = ((j * k) % n).astype(F32) * (2.0 * jnp.pi / n)
    return jnp.cos(ang) * scale, jnp.sin(ang) * scale


def _head_rms(x, gm16, gain):
    ms = _dot2(x * x, gm16)
    return x * lax.rsqrt(ms + EPS) * gain


def _rope(x, cos, sin_signed):
    w = x.shape[1]
    lane = lax.broadcasted_iota(jnp.int32, x.shape, 1)
    partner = jnp.where((lane & 31) < 16, pltpu.roll(x, w - 16, 1), pltpu.roll(x, 16, 1))
    return x * cos + partner * sin_signed


def _attn_body(*refs, rope, n_new, n_past):
    if n_past:
        (q_ref, k_ref, v_ref, ck_ref, cv_ref, cq_ref, sq_ref, cosk_ref, sink_ref,
         qn_ref, kn_ref, gm_ref, o_ref, ka_ref, k_scr, ks_scr, v_scr, vs_scr) = refs
    else:
        (q_ref, k_ref, v_ref, qn_ref, kn_ref, gm_ref, o_ref, ka_ref,
         k_scr, ks_scr, v_scr, vs_scr) = refs
    gm = gm_ref[...]

    @pl.when(pl.program_id(1) == 0)
    def _():
        kn = _head_rms(k_ref[...], gm, kn_ref[...])
        ka_ref[...] = kn
        if rope:
            kn = _rope(kn, cosk_ref[...], sink_ref[...])
        v = v_ref[...]
        new = pl.ds(0, n_new)
        k_scr[new, :] = kn.astype(BF16)
        ks_scr[new, :] = pltpu.roll(kn, HEAD_DIM, 1).astype(BF16)
        v_scr[new, :] = v.astype(BF16)
        vs_scr[new, :] = pltpu.roll(v, HEAD_DIM, 1).astype(BF16)
        if n_past:
            past = pl.ds(n_new, n_past)
            ck = ck_ref[0]
            cv = cv_ref[0]
            k_scr[past, :] = ck.astype(BF16)
            ks_scr[past, :] = pltpu.roll(ck, HEAD_DIM, 1).astype(BF16)
            v_scr[past, :] = cv.astype(BF16)
            vs_scr[past, :] = pltpu.roll(cv, HEAD_DIM, 1).astype(BF16)

    lane = lax.broadcasted_iota(jnp.int32, (Q_TILE, LANES), 1)
    low = lane < HEAD_DIM
    for jp in range(C_Q_W // LANES):
        kv = jp // 2
        q2 = _head_rms(q_ref[:, jp * LANES:(jp + 1) * LANES], gm, qn_ref[...])
        if rope:
            q2 = _rope(q2, cq_ref[...], sq_ref[...])
        q2 = q2 * (HEAD_DIM ** -0.5)
        halves = []
        for half in range(2):
            swapped = (half == 1) != (kv == 1)
            kk = ks_scr[...] if swapped else k_scr[...]
            vv = vs_scr[...] if swapped else v_scr[...]
            qm = jnp.where(low if half == 0 else ~low, q2, 0.0).astype(BF16)
            s = lax.dot_general(qm, kk, (((1,), (1,)), ((), ())), preferred_element_type=F32)
            p = jnp.exp(s - jnp.max(s, axis=-1, keepdims=True))
            den = jnp.sum(p, axis=-1, keepdims=True)
            halves.append(jnp.dot(p.astype(BF16), vv, preferred_element_type=F32) / den)
        o_ref[:, jp * LANES:(jp + 1) * LANES] = jnp.where(low, halves[0], halves[1])


def _attn(proj, qn, kn, gm, *, nseq, seq, row_block0, past=None, tables=None):
    nq = seq // Q_TILE
    n_past = 0 if past is None else past[0].shape[1]
    qrow = lambda b, j: ((row_block0 + b) * nq + j, COL_CQ // C_Q_W)
    in_specs = [
        pl.BlockSpec((Q_TILE, C_Q_W), qrow),
        pl.BlockSpec((seq, C_KV_W), lambda b, j: (row_block0 + b, COL_CK // C_KV_W)),
        pl.BlockSpec((seq, C_KV_W), lambda b, j: (row_block0 + b, COL_CV // C_KV_W)),
    ]
    args = [proj, proj, proj]
    if past is not None:
        in_specs += [pl.BlockSpec((1, n_past, C_KV_W), lambda b, j: (b, 0, 0))] * 2
        in_specs += [pl.BlockSpec((Q_TILE, LANES), lambda b, j: (j, 0))] * 2
        in_specs += [pl.BlockSpec((seq, LANES), lambda b, j: (0, 0))] * 2
        args += [past[0], past[1], tables[0], tables[1], tables[0], tables[1]]
    in_specs += [
        pl.BlockSpec((1, LANES), lambda b, j: (0, 0)),
        pl.BlockSpec((1, LANES), lambda b, j: (0, 0)),
        pl.BlockSpec((LANES, LANES), lambda b, j: (0, 0)),
    ]
    args += [qn, kn, gm]
    s_all = seq + n_past
    return pl.pallas_call(
        functools.partial(_attn_body, rope=past is not None, n_new=seq, n_past=n_past),
        grid=(nseq, nq),
        in_specs=in_specs,
        out_specs=[
            pl.BlockSpec((Q_TILE, C_Q_W), lambda b, j: (b * nq + j, 0)),
            pl.BlockSpec((seq, C_KV_W), lambda b, j: (b, 0)),
        ],
        out_shape=[
            jax.ShapeDtypeStruct((nseq * seq, C_Q_W), F32),
            jax.ShapeDtypeStruct((nseq * seq, C_KV_W), F32),
        ],
        scratch_shapes=[pltpu.VMEM((s_all, C_KV_W), BF16)] * 4,
        compiler_params=_params(("parallel", "arbitrary")),
        name=f"attn{seq}",
    )(*args)


def _rope_tables():
    t = jnp.arange(DEC_SEQ)
    rows = (t // GRID_W).astype(F32)
    cols = (t % GRID_W).astype(F32)
    nf = HEAD_DIM // 4
    inv_freq = ROPE_THETA ** (-jnp.arange(nf, dtype=F32) / nf)
    d = jnp.arange(LANES)
    pos = jnp.where(((d % HEAD_DIM) < HEAD_DIM // 2)[None, :], rows[:, None], cols[:, None])
    ang = pos * inv_freq[d % nf][None, :]
    sign = jnp.where((d % 32) < 16, -1.0, 1.0)[None, :]
    return jnp.cos(ang), jnp.sin(ang) * sign


def _outproj_body(oa_c, of_c, oc_c, oa_d, of_d, oc_d, w_ref, x_ref, mod_ref, g_ref, rw_ref, rb_ref,
                  x1_ref, h2_ref, meta_ref):
    mod = mod_ref[0]
    is_ctx = pl.program_id(0) < N_CTX // TM
    pick = lambda c_ref, d_ref: jnp.where(is_ctx, c_ref[...], d_ref[...]).astype(BF16)
    mixed = (jnp.dot(pick(oa_c, oa_d), w_ref[0:256, :], preferred_element_type=F32)
             + jnp.dot(pick(of_c, of_d), w_ref[256:512, :], preferred_element_type=F32)
             + jnp.dot(pick(oc_c, oc_d), w_ref[512:1024, :], preferred_element_type=F32))
    x1 = x_ref[...] + mod[2:3] * mixed
    x1_ref[...] = x1
    h2 = _rms(x1) * g_ref[...] * (1.0 + mod[4:5]) + mod[3:4]
    h2_ref[...] = h2.astype(BF16)
    lane = lax.broadcasted_iota(jnp.int32, (TM, LANES), 1)
    logits = jnp.where(lane < N_EXPERTS, _dot3(h2, rw_ref[...]) + rb_ref[...], -jnp.inf)
    meta = jnp.zeros((TM, LANES), F32)
    tops = []
    for k in range(TOP_K):
        m = jnp.max(logits, axis=-1, keepdims=True)
        idx = jnp.min(jnp.where(logits == m, lane, LANES), axis=-1, keepdims=True)
        hit = lane == idx
        logits = jnp.where(hit, -jnp.inf, logits)
        meta = jnp.where(lane == k, idx.astype(F32), meta)
        tops.append(m)
    ex = [jnp.exp(t - tops[0]) for t in tops]
    den = ex[0] + ex[1] + ex[2] + ex[3]
    for k in range(TOP_K):
        meta = jnp.where(lane == TOP_K + k, ex[k] / den, meta)
    meta_ref[...] = meta


def _outproj(mix_ctx, mix_dec, w, x, mod3, gain, rw, rb):
    n_ctx_tiles = N_CTX // TM
    tok = lambda w_: pl.BlockSpec((TM, w_), lambda i: (i, 0))
    ctx = lambda w_: pl.BlockSpec((TM, w_), lambda i: (jnp.minimum(i, n_ctx_tiles - 1), 0))
    dec = lambda w_: pl.BlockSpec((TM, w_), lambda i: (jnp.maximum(i - n_ctx_tiles, 0), 0))
    full = lambda a, b: pl.BlockSpec((a, b), lambda i: (0, 0))
    return pl.pallas_call(
        _outproj_body,
        grid=(N_TOK // TM,),
        in_specs=[ctx(A_VAL_W), ctx(F_WIDTH), ctx(C_Q_W), dec(A_VAL_W), dec(F_WIDTH), dec(C_Q_W),
                  full(D_MODEL, D_MODEL), tok(D_MODEL),
                  pl.BlockSpec((1, 6, D_MODEL), lambda i: (_mod_row(i, TM), 0, 0)),
                  full(1, D_MODEL), full(D_MODEL, LANES), full(1, LANES)],
        out_specs=[tok(D_MODEL), tok(D_MODEL), tok(LANES)],
        out_shape=[jax.ShapeDtypeStruct((N_TOK, D_MODEL), F32),
                   jax.ShapeDtypeStruct((N_TOK, D_MODEL), BF16),
                   jax.ShapeDtypeStruct((N_TOK, LANES), F32)],
        compiler_params=_params(("parallel",)),
        name="outproj",
    )(*mix_ctx, *mix_dec, w, x, mod3, gain, rw, rb)


def _pos_body(meta_ref, slot_ref, info_ref):
    meta = meta_ref[...]
    lane = lax.broadcasted_iota(jnp.int32, (ROW_TILE, LANES), 1)
    idx = [meta[:, k:k + 1].astype(jnp.int32) for k in range(TOP_K)]
    sel = jnp.zeros((ROW_TILE, LANES), F32)
    for k in range(TOP_K):
        sel = jnp.where(lane == idx[k], 1.0, sel)
    ri = lax.broadcasted_iota(jnp.int32, (ROW_TILE, ROW_TILE), 0)
    ci = lax.broadcasted_iota(jnp.int32, (ROW_TILE, ROW_TILE), 1)
    lower = jnp.where(ci < ri, 1.0, 0.0).astype(BF16)
    rank = jnp.dot(lower, sel.astype(BF16), preferred_element_type=F32)
    cnt = jnp.sum(sel, axis=0, keepdims=True)
    n8 = jnp.floor((cnt + (RUN_ALIGN - 1)) * (1.0 / RUN_ALIGN)) * RUN_ALIGN
    er = lax.broadcasted_iota(jnp.int32, (LANES, LANES), 0)
    ec = lax.broadcasted_iota(jnp.int32, (LANES, LANES), 1)
    before = jnp.where(er < ec, 1.0, 0.0).astype(BF16)
    off = jnp.dot(jnp.broadcast_to(n8, (8, LANES)).astype(BF16), before, preferred_element_type=F32)
    tot = rank + off[0:1]
    out = jnp.zeros((ROW_TILE, LANES), F32)
    for k in range(TOP_K):
        pk = jnp.sum(jnp.where(lane == idx[k], tot, 0.0), axis=-1, keepdims=True)
        out = jnp.where(lane == k, pk, out)
    slot_ref[...] = out
    row = lax.broadcasted_iota(jnp.int32, (8, LANES), 0)
    info_ref[0] = jnp.where(row == 0, n8, jnp.where(row == 1, off, 0.0))


def _positions(meta):
    return pl.pallas_call(
        _pos_body,
        grid=(N_TILES,),
        in_specs=[pl.BlockSpec((ROW_TILE, LANES), lambda i: (i, 0))],
        out_specs=[pl.BlockSpec((ROW_TILE, LANES), lambda i: (i, 0)),
                   pl.BlockSpec((1, 8, LANES), lambda i: (i, 0, 0))],
        out_shape=[jax.ShapeDtypeStruct((N_TOK, LANES), F32),
                   jax.ShapeDtypeStruct((N_TILES, 8, LANES), F32)],
        compiler_params=_params(("parallel",)),
        name="positions",
    )(meta)


def _piece_copies(n_rows, row_a, row_b, make_copy, action, bits):
    m = lax.shift_right_logical(n_rows, RUN_SHIFT)
    for bit in range(bits):
        rows = RUN_ALIGN << bit
        lo = (m & ((1 << bit) - 1)) * RUN_ALIGN

        @pl.when(((m >> bit) & 1) == 1)
        def _():
            cp = make_copy(pl.multiple_of(row_a + lo, RUN_ALIGN), pl.multiple_of(row_b + lo, RUN_ALIGN), rows)
            if action == "start":
                cp.start()
            else:
                cp.wait()


def _run_copies(tile, n8_tbl, off_tbl, base_tbl, make_copy, action):
    def per_expert(e, c):
        j = tile * N_EXPERTS + e
        _piece_copies(n8_tbl[j], off_tbl[j], base_tbl[j], make_copy, action, RUN_BITS)
        return c

    lax.fori_loop(0, N_EXPERTS, per_expert, 0)


def _unpack_pairs(u):
    hi = lax.bitcast_convert_type(u & jnp.uint32(0xFFFF0000), F32)
    lo = lax.bitcast_convert_type(u << 16, F32)
    return hi, lo


def _pack_pairs(a, b):
    return lax.bitcast_convert_type(a, jnp.uint32) | (lax.bitcast_convert_type(b, jnp.uint32) >> 16)


def _slot_onehot(slot, weights=None):
    lane = lax.broadcasted_iota(jnp.int32, (ROW_TILE, SORT_ROWS), 1)
    p = jnp.zeros((ROW_TILE, SORT_ROWS), F32)
    for k in range(TOP_K):
        hit = lane == slot[:, k:k + 1].astype(jnp.int32)
        p = jnp.where(hit, 1.0 if weights is None else weights[:, k:k + 1], p)
    return p.astype(BF16)


def _dispatch_body(n8_tbl, off_tbl, base_tbl, tail_n, tail_base, nused, h_ref, slot_ref, xs_hbm,
                   sorted_scr, zero_scr, sem):
    tile = pl.program_id(0)

    @pl.when(tile == 0)
    def _():
        zero_scr[...] = jnp.zeros_like(zero_scr)

        def make_zero_copy(_, dst_row, rows):
            return pltpu.make_async_copy(zero_scr.at[pl.ds(0, rows)], xs_hbm.at[pl.ds(dst_row, rows)], sem)

        for action in ("start", "wait"):
            def per_expert(e, c):
                _piece_copies(tail_n[e], 0, tail_base[e], make_zero_copy, action, TAIL_BITS)
                return c

            def per_block(b, c):
                cp = make_zero_copy(0, pl.multiple_of(b * MOE_BM, MOE_BM), MOE_BM)
                if action == "start":
                    cp.start()
                else:
                    cp.wait()
                return c

            lax.fori_loop(0, N_EXPERTS, per_expert, 0)
            lax.fori_loop(nused[0], MOE_NB, per_block, 0)

    p = _slot_onehot(slot_ref[...])
    srt = lax.dot_general(p, h_ref[...], (((0,), (0,)), ((), ())), preferred_element_type=F32)
    sorted_scr[...] = _pack_pairs(srt[:, :HALF_D], srt[:, HALF_D:])

    def make_copy(src_row, dst_row, rows):
        return pltpu.make_async_copy(sorted_scr.at[pl.ds(src_row, rows)], xs_hbm.at[pl.ds(dst_row, rows)], sem)

    _run_copies(tile, n8_tbl, off_tbl, base_tbl, make_copy, "start")
    _run_copies(tile, n8_tbl, off_tbl, base_tbl, make_copy, "wait")


def _dispatch(tables, tails, h2, slots):
    grid_spec = pltpu.PrefetchScalarGridSpec(
        num_scalar_prefetch=6,
        grid=(N_TILES,),
        in_specs=[pl.BlockSpec((ROW_TILE, D_MODEL), lambda i, *_: (i, 0)),
                  pl.BlockSpec((ROW_TILE, LANES), lambda i, *_: (i, 0))],
        out_specs=pl.BlockSpec(memory_space=pl.ANY),
        scratch_shapes=[pltpu.VMEM((SORT_ROWS, HALF_D), jnp.uint32),
                        pltpu.VMEM((MOE_BM, HALF_D), jnp.uint32), pltpu.SemaphoreType.DMA],
    )
    return pl.pallas_call(
        _dispatch_body,
        grid_spec=grid_spec,
        out_shape=jax.ShapeDtypeStruct((MOE_ROWS, HALF_D), jnp.uint32),
        compiler_params=_params(("arbitrary",)),
        name="dispatch",
    )(*tables, *tails, h2, slots)


def _experts_body(bexp_ref, nused_ref, first_ref, next_ref, slot_ref, x_ref, wgu_hbm, bgu_ref, wdn_hbm, bdn_ref,
                  y_ref, wgu32, wdn32, wgu16, wdn16, sem, *, layer):
    i = pl.program_id(0)

    def weight_copies(expert, s):
        return (pltpu.make_async_copy(wgu_hbm.at[layer, expert], wgu32.at[s], sem.at[0, s]),
                pltpu.make_async_copy(wdn_hbm.at[layer, expert], wdn32.at[s], sem.at[1, s]))

    @pl.when(first_ref[i] == 1)
    def _():
        s = slot_ref[i]

        @pl.when(i == 0)
        def _():
            for cp in weight_copies(bexp_ref[i], s):
                cp.start()

        for cp in weight_copies(bexp_ref[i], s):
            cp.wait()

        @pl.when(next_ref[i] >= 0)
        def _():
            for cp in weight_copies(next_ref[i], 1 - s):
                cp.start()

        wgu16[...] = wgu32[s].astype(BF16)
        wdn16[...] = wdn32[s].astype(BF16)

    @pl.when(i < nused_ref[0])
    def _():
        x = jnp.concatenate(_unpack_pairs(x_ref[...]), axis=1).astype(BF16)
        gu = jnp.dot(x, wgu16[...], preferred_element_type=F32) + bgu_ref[0, 0]
        gate = jnp.minimum(gu[:, :D_FF], SWIGLU_LIMIT)
        up = jnp.clip(gu[:, D_FF:], -SWIGLU_LIMIT, SWIGLU_LIMIT)
        glu = gate * _sigmoid(gate * SWIGLU_ALPHA)
        act = ((up + 1.0) * glu).astype(BF16)
        y = jnp.dot(act, wdn16[...], preferred_element_type=F32) + bdn_ref[0, 0]
        y = y.astype(BF16).astype(F32)
        y_ref[...] = _pack_pairs(y[:, :HALF_D], y[:, HALF_D:])

    @pl.when(i >= nused_ref[0])
    def _():
        y_ref[...] = jnp.zeros_like(y_ref)


def _experts(sched, xs, wgu, bgu, wdn, bdn, layer):
    e = lambda i, be, *_: (layer, be[i], 0, 0)
    grid_spec = pltpu.PrefetchScalarGridSpec(
        num_scalar_prefetch=5,
        grid=(MOE_NB,),
        in_specs=[
            pl.BlockSpec((MOE_BM, HALF_D), lambda i, be, nu, *_: (jnp.minimum(i, nu[0] - 1), 0)),
            pl.BlockSpec(memory_space=pl.ANY),
            pl.BlockSpec((1, 1, 1, 2 * D_FF), e),
            pl.BlockSpec(memory_space=pl.ANY),
            pl.BlockSpec((1, 1, 1, D_MODEL), e),
        ],
        out_specs=pl.BlockSpec((MOE_BM, HALF_D), lambda i, *_: (i, 0)),
        scratch_shapes=[pltpu.VMEM((2, D_MODEL, 2 * D_FF), F32), pltpu.VMEM((2, D_FF, D_MODEL), F32),
                        pltpu.VMEM((D_MODEL, 2 * D_FF), BF16), pltpu.VMEM((D_FF, D_MODEL), BF16),
                        pltpu.SemaphoreType.DMA((2, 2))],
    )
    return pl.pallas_call(
        functools.partial(_experts_body, layer=layer),
        grid_spec=grid_spec,
        out_shape=jax.ShapeDtypeStruct((MOE_ROWS, HALF_D), jnp.uint32),
        compiler_params=_params(("arbitrary",)),
        name="experts",
    )(*sched, xs, wgu, bgu, wdn, bdn)


def _combine_body(n8_tbl, off_tbl, base_tbl, y_hbm, slot_ref, meta_ref, x1_ref, mod_ref, nf_ref, o_ref,
                  ybuf, sem, *, final):
    ybuf[...] = jnp.zeros_like(ybuf)

    def make_copy(buf_row, hbm_row, rows):
        return pltpu.make_async_copy(y_hbm.at[pl.ds(hbm_row, rows)], ybuf.at[pl.ds(buf_row, rows)], sem)

    tile = pl.program_id(0)
    _run_copies(tile, n8_tbl, off_tbl, base_tbl, make_copy, "start")
    _run_copies(tile, n8_tbl, off_tbl, base_tbl, make_copy, "wait")
    pw = _slot_onehot(slot_ref[...], meta_ref[...][:, TOP_K:2 * TOP_K])
    hi, lo = _unpack_pairs(ybuf[...])
    ff = jnp.concatenate([jnp.dot(pw, hi.astype(BF16), preferred_element_type=F32),
                          jnp.dot(pw, lo.astype(BF16), preferred_element_type=F32)], axis=1)
    x2 = x1_ref[...] + mod_ref[0][5:6] * ff
    if final:
        x2 = _rms(x2) * nf_ref[...]
    o_ref[...] = x2


def _combine(tables, y, slots, meta, x1, mod3, nfinal, *, final):
    grid_spec = pltpu.PrefetchScalarGridSpec(
        num_scalar_prefetch=3,
        grid=(N_TILES,),
        in_specs=[pl.BlockSpec(memory_space=pl.ANY),
                  pl.BlockSpec((ROW_TILE, LANES), lambda i, *_: (i, 0)),
                  pl.BlockSpec((ROW_TILE, LANES), lambda i, *_: (i, 0)),
                  pl.BlockSpec((ROW_TILE, D_MODEL), lambda i, *_: (i, 0)),
                  pl.BlockSpec((1, 6, D_MODEL), lambda i, *_: (_mod_row(i, ROW_TILE), 0, 0)),
                  pl.BlockSpec((1, D_MODEL), lambda i, *_: (0, 0))],
        out_specs=pl.BlockSpec((ROW_TILE, D_MODEL), lambda i, *_: (i, 0)),
        scratch_shapes=[pltpu.VMEM((SORT_ROWS, HALF_D), jnp.uint32), pltpu.SemaphoreType.DMA],
    )
    return pl.pallas_call(
        functools.partial(_combine_body, final=final),
        grid_spec=grid_spec,
        out_shape=jax.ShapeDtypeStruct((N_TOK, D_MODEL), F32),
        compiler_params=_params(("arbitrary",)),
        name="combine",
    )(*tables, y, slots, meta, x1, mod3, nfinal)


def _state_to_blockdiag(s):
    eye = jnp.eye(A_HEADS, dtype=s.dtype)
    t = jnp.einsum("bdhkv,hg->bdhvgk", s, eye)
    return t.reshape(s.shape[0], 2, A_VAL_W, A_KEY_W)


def _blockdiag_to_state(t):
    b = t.shape[0]
    t = t.reshape(b, 2, A_HEADS, A_DV, A_HEADS, A_DK)
    s = jnp.stack([t[:, :, h, :, h, :] for h in range(A_HEADS)], axis=2)
    return jnp.swapaxes(s, -1, -2)


def kernel(x_prompt, x_sample, state_gla, cache_k, cache_v, c, c_ctx, w_ada, b_ada, norm_mix, norm_ffn,
           norm_final, w_in, w_out, gla_gate_w, gla_gate_b, gla_norm, q_norm, k_norm, router_w, router_b,
           expert_w_gu, expert_b_gu, expert_w_dn, expert_b_dn):
    x = jnp.concatenate([x_prompt.reshape(N_CTX, D_MODEL), x_sample.reshape(N_DEC, D_MODEL)], axis=0)
    cond = jnp.zeros((MOD_ROWS, D_MODEL), F32).at[0].set(c_ctx).at[1:1 + DEC_BATCH].set(c)
    mod = _ada(cond, w_ada, b_ada)

    gm256 = (jnp.kron(jnp.eye(A_HEADS), jnp.ones((A_DV, A_DV))) / A_DV).astype(BF16)
    gm128 = (jnp.kron(jnp.eye(2), jnp.ones((HEAD_DIM, HEAD_DIM))) / HEAD_DIM).astype(BF16)
    c64, s64 = _dft_mats(F_GROUP_W, F_GROUP_W ** -0.5)
    cb = jnp.kron(jnp.eye(F_GROUPS), c64).astype(BF16)
    sb = jnp.kron(jnp.eye(F_GROUPS), s64).astype(BF16)
    dft_ctx = [m.astype(BF16) for m in _dft_mats(SEQ, SEQ ** -0.5)]
    dft_dec = [m.astype(BF16) for m in _dft_mats(DEC_SEQ, DEC_SEQ ** -0.5)]
    rope_tabs = _rope_tables()
    zero_state = jnp.zeros((BATCH, 2, A_VAL_W, A_KEY_W), F32)

    new_s, new_k, new_v = [], [], []
    for l in range(DEPTH):
        mod3 = mod[l].reshape(MOD_ROWS, 6, D_MODEL)
        w_l = w_in[l]
        w_r = jnp.concatenate([w_l[:, :768], w_l[:, 800:], w_l[:, 768:800],
                               jnp.zeros((D_MODEL, PROJ_W - 1824), F32)], axis=1).astype(BF16)
        proj = _inproj(x, mod3, norm_mix[l][None, :], w_r)

        wg = jnp.zeros((LANES, 2 * A_KEY_W), F32)
        wg = wg.at[0:GATE_RANK, 0:A_KEY_W].set(gla_gate_w[l, 0])
        wg = wg.at[GATE_RANK:2 * GATE_RANK, A_KEY_W:].set(gla_gate_w[l, 1])
        gb = gla_gate_b[l].reshape(1, 2 * A_KEY_W)
        gn = jnp.tile(gla_norm[l], A_HEADS)[None, :]
        s0_dec = _state_to_blockdiag(state_gla[:, l])
        oa_ctx, sfin = _gla(proj, wg, gb, zero_state, gn, gm256, nseq=BATCH, seq=SEQ, row_block0=0)
        oa_dec, _ = _gla(proj, wg, gb, s0_dec, gn, gm256, nseq=DEC_BATCH, seq=DEC_SEQ,
                         row_block0=N_CTX // DEC_SEQ)
        of_ctx = _fnet(proj, cb, sb, *dft_ctx, nseq=BATCH, seq=SEQ, row_block0=0)
        of_dec = _fnet(proj, cb, sb, *dft_dec, nseq=DEC_BATCH, seq=DEC_SEQ, row_block0=N_CTX // DEC_SEQ)
        qn = jnp.tile(q_norm[l], 2)[None, :]
        kn = jnp.tile(k_norm[l], 2)[None, :]
        oc_ctx, ka = _attn(proj, qn, kn, gm128, nseq=BATCH, seq=SEQ, row_block0=0)
        past = (cache_k[:, l].reshape(DEC_BATCH, PAST_LEN, C_KV_W),
                cache_v[:, l].reshape(DEC_BATCH, PAST_LEN, C_KV_W))
        oc_dec, _ = _attn(proj, qn, kn, gm128, nseq=DEC_BATCH, seq=DEC_SEQ,
                          row_block0=N_CTX // DEC_SEQ, past=past, tables=rope_tabs)
        new_s.append(_blockdiag_to_state(sfin))
        new_k.append(ka.reshape(BATCH, SEQ, C_KV_HEADS, HEAD_DIM))
        new_v.append(proj[:N_CTX, COL_CV:COL_CV + C_KV_W].reshape(BATCH, SEQ, C_KV_HEADS, HEAD_DIM))

        rw = jnp.zeros((D_MODEL, LANES), F32).at[:, :N_EXPERTS].set(router_w[l])
        rb = jnp.zeros((1, LANES), F32).at[0, :N_EXPERTS].set(router_b[l])
        x1, h2, meta = _outproj((oa_ctx, of_ctx, oc_ctx), (oa_dec, of_dec, oc_dec), w_out[l].astype(BF16), x,
                                mod3, norm_ffn[l][None, :], rw, rb)

        slots, info = _positions(meta)
        n8 = info[:, 0, :N_EXPERTS].astype(jnp.int32)
        off = info[:, 1, :N_EXPERTS].astype(jnp.int32)
        cnt = jnp.sum(n8, axis=0)
        pcnt = (cnt + MOE_BM - 1) // MOE_BM * MOE_BM
        pend = jnp.cumsum(pcnt)
        base = (pend - pcnt)[None, :] + jnp.cumsum(n8, axis=0) - n8
        tables = (n8.reshape(-1), off.reshape(-1), base.reshape(-1))
        blk = jnp.arange(MOE_NB, dtype=jnp.int32)
        blk_exp = jnp.minimum(jnp.sum((pend[None, :] <= (blk * MOE_BM)[:, None]).astype(jnp.int32), axis=1),
                              N_EXPERTS - 1)
        nused = (pend[-1:] // MOE_BM).astype(jnp.int32)
        tails = (pcnt - cnt, pend - (pcnt - cnt), nused)
        prev_exp = jnp.concatenate([jnp.full((1,), -1, jnp.int32), blk_exp[:-1]])
        first = ((blk < nused[0]) & (blk_exp != prev_exp)).astype(jnp.int32)
        end_blk = (pend // MOE_BM)[blk_exp]
        nxt = jnp.where(end_blk < nused[0], blk_exp[jnp.minimum(end_blk, MOE_NB - 1)], -1)
        wslot = (jnp.cumsum(first) - 1) & 1
        sched = (blk_exp, nused, first, nxt, wslot)

        xs = _dispatch(tables, tails, h2, slots)
        y = _experts(sched, xs, expert_w_gu,
                     expert_b_gu.reshape(DEPTH, N_EXPERTS, 1, 2 * D_FF), expert_w_dn,
                     expert_b_dn.reshape(DEPTH, N_EXPERTS, 1, D_MODEL), l)
        x = _combine(tables, y, slots, meta, x1, mod3, norm_final[None, :], final=(l == DEPTH - 1))

    y_prompt = x[:N_CTX].reshape(BATCH, SEQ, D_MODEL)
    y_sample = x[N_CTX:].reshape(DEC_BATCH, DEC_SEQ, D_MODEL)
    return (y_prompt, y_sample, jnp.stack(new_s, axis=1), jnp.stack(new_k, axis=1), jnp.stack(new_v, axis=1))
```

```python
import functools

import jax
import jax.numpy as jnp
import numpy as np
from jax import lax
from jax.experimental import pallas as pl
from jax.experimental.pallas import tpu as pltpu

D_MODEL = 1024
BATCH = 32
SEQ = 256
DEPTH = 2
DEC_BATCH = 8
DEC_SEQ = 1024
PAST_LEN = 512
GRID_W = 64
A_HEADS = 4
A_DV = 64
A_DK = 32
A_KEY_W = 128
A_VAL_W = 256
GATE_RANK = 16
GATE_TEMP = 16.0
GLA_CHUNK = 32
F_GROUPS = 4
F_GROUP_W = 64
F_WIDTH = 256
HEAD_DIM = 64
C_HEADS = 8
C_KV_HEADS = 2
C_Q_W = 512
C_KV_W = 128
ROPE_THETA = 10000.0
N_EXPERTS = 32
TOP_K = 4
D_FF = 1024
SWIGLU_LIMIT = 7.0
SWIGLU_ALPHA = 1.702
EPS = 1e-6

N_CTX = BATCH * SEQ
N_DEC = DEC_BATCH * DEC_SEQ
N_TOK = N_CTX + N_DEC
MOD_ROWS = 16

COL_GLA = 0
COL_PF = 768
COL_CQ = 1024
COL_CK = 1536
COL_CV = 1664
COL_LR = 1792
PROJ_W = 1920

LANES = 128
TM = 512
GLA_TILE = 256
Q_TILE = 256
MOE_BM = 256
ROW_TILE = 256
N_TILES = N_TOK // ROW_TILE
RUN_ALIGN = 8
RUN_SHIFT = 3
RUN_BITS = 6
TAIL_BITS = 5
SORT_ROWS = ROW_TILE * TOP_K + N_EXPERTS * RUN_ALIGN
HALF_D = D_MODEL // 2
MOE_NB = (N_TOK * TOP_K + N_TILES * N_EXPERTS * (RUN_ALIGN - 1)) // MOE_BM + N_EXPERTS
MOE_ROWS = MOE_NB * MOE_BM
VMEM_LIMIT = 56 * 1024 * 1024

F32 = jnp.float32
BF16 = jnp.bfloat16


def _params(sem, vmem=VMEM_LIMIT):
    return pltpu.CompilerParams(dimension_semantics=sem, vmem_limit_bytes=vmem)


def _bdot(a, b):
    return jnp.dot(a.astype(BF16), b.astype(BF16), preferred_element_type=F32)


def _split(a):
    hi = a.astype(BF16)
    lo = (a - hi.astype(F32)).astype(BF16)
    return hi, lo


def _dot3(a, b):
    ah, al = _split(a)
    bh, bl = _split(b)
    d = functools.partial(jnp.dot, preferred_element_type=F32)
    return d(ah, bh) + (d(ah, bl) + d(al, bh))


def _dot2(a, b16):
    ah, al = _split(a)
    d = functools.partial(jnp.dot, preferred_element_type=F32)
    return d(ah, b16) + d(al, b16)


def _sigmoid(x):
    return 1.0 / (1.0 + jnp.exp(-x))


def _mod_row(i, tm):
    tok = i * tm
    return jnp.where(tok < N_CTX, 0, 1 + (tok - N_CTX) // DEC_SEQ)


def _ada_body(c_ref, w_ref, b_ref, o_ref):
    c = c_ref[...]
    o_ref[0] = _dot3(c * _sigmoid(c), w_ref[0]) + b_ref[0]


def _ada(cond, w_ada, b_ada):
    return pl.pallas_call(
        _ada_body,
        grid=(DEPTH, 6),
        in_specs=[
            pl.BlockSpec((MOD_ROWS, D_MODEL), lambda l, j: (0, 0)),
            pl.BlockSpec((1, D_MODEL, D_MODEL), lambda l, j: (l, 0, j)),
            pl.BlockSpec((1, 1, D_MODEL), lambda l, j: (l, 0, j)),
        ],
        out_specs=pl.BlockSpec((1, MOD_ROWS, D_MODEL), lambda l, j: (l, 0, j)),
        out_shape=jax.ShapeDtypeStruct((DEPTH, MOD_ROWS, 6 * D_MODEL), F32),
        compiler_params=_params(("parallel", "parallel")),
        name="ada",
    )(cond, w_ada, b_ada.reshape(DEPTH, 1, 6 * D_MODEL))


def _rms(x):
    return x * lax.rsqrt(jnp.mean(x * x, axis=-1, keepdims=True) + EPS)


def _inproj_body(*refs):
    *x_refs, mod_ref, g_ref, w_ref, o_ref = refs
    if len(x_refs) == 2:
        x = jnp.where(pl.program_id(0) < N_CTX // TM, x_refs[0][...], x_refs[1][...])
    else:
        x = x_refs[0][...]
    mod = mod_ref[0]
    h = _rms(x) * g_ref[...] * (1.0 + mod[1:2]) + mod[0:1]
    o_ref[...] = jnp.dot(h.astype(BF16), w_ref[...], preferred_element_type=F32)


def _inproj(xs, mod3, gain, w):
    n_ctx_tiles = N_CTX // TM
    if len(xs) == 2:
        x_specs = [pl.BlockSpec((TM, D_MODEL), lambda i: (jnp.minimum(i, n_ctx_tiles - 1), 0)),
                   pl.BlockSpec((TM, D_MODEL), lambda i: (jnp.maximum(i - n_ctx_tiles, 0), 0))]
    else:
        x_specs = [pl.BlockSpec((TM, D_MODEL), lambda i: (i, 0))]
    return pl.pallas_call(
        _inproj_body,
        grid=(N_TOK // TM,),
        in_specs=x_specs + [
            pl.BlockSpec((1, 6, D_MODEL), lambda i: (_mod_row(i, TM), 0, 0)),
            pl.BlockSpec((1, D_MODEL), lambda i: (0, 0)),
            pl.BlockSpec((D_MODEL, PROJ_W), lambda i: (0, 0)),
        ],
        out_specs=pl.BlockSpec((TM, PROJ_W), lambda i: (i, 0)),
        out_shape=jax.ShapeDtypeStruct((N_TOK, PROJ_W), F32),
        compiler_params=_params(("parallel",)),
        name="inproj",
    )(*xs, mod3, gain, w)


def _log_sigmoid(x):
    return jnp.minimum(x, 0.0) - jnp.log(1.0 + jnp.exp(-jnp.abs(x)))


def _chunk_cumsum(x, pos, forward):
    rows = x.shape[0]
    for s in (1, 2, 4, 8, 16):
        if forward:
            x = x + jnp.where(pos >= s, pltpu.roll(x, s, 0), 0.0)
        else:
            x = x + jnp.where(pos < GLA_CHUNK - s, pltpu.roll(x, rows - s, 0), 0.0)
    return x


def _gla_tile(q, k, v16, la, st, forward):
    R = GLA_TILE
    nch = R // GLA_CHUNK
    pos = lax.broadcasted_iota(jnp.int32, (R, 1), 0) & (GLA_CHUNK - 1)
    b = _chunk_cumsum(la, pos, forward)
    b3 = b.reshape(nch, GLA_CHUNK, A_KEY_W)
    tot3 = b3[:, GLA_CHUNK - 1:GLA_CHUNK, :] if forward else b3[:, 0:1, :]
    totb = jnp.broadcast_to(tot3, (nch, GLA_CHUNK, A_KEY_W)).reshape(R, A_KEY_W)
    qt = q * (A_DK ** -0.5) * jnp.exp(b)
    kt = (k * jnp.exp(-b)).astype(BF16)
    kend = (k * jnp.exp(totb - b)).astype(BF16)
    dec3 = jnp.exp(tot3)

    ri = lax.broadcasted_iota(jnp.int32, (R, R), 0)
    ci = lax.broadcasted_iota(jnp.int32, (R, R), 1)
    tri = (ci <= ri) if forward else (ci >= ri)
    mask = ((ri >> 5) == (ci >> 5)) & tri
    lane_k = lax.broadcasted_iota(jnp.int32, (R, A_KEY_W), 1) >> 5
    lane_v = lax.broadcasted_iota(jnp.int32, (R, A_VAL_W), 1) >> 6

    o = jnp.zeros((R, A_VAL_W), F32)
    for h in range(A_HEADS):
        qh = jnp.where(lane_k == h, qt, 0.0).astype(BF16)
        sc = lax.dot_general(qh, kt, (((1,), (1,)), ((), ())), preferred_element_type=F32)
        p = jnp.where(mask, sc, 0.0).astype(BF16)
        oh = jnp.dot(p, v16, preferred_element_type=F32)
        o = jnp.where(lane_v == h, oh, o)

    sr = lax.broadcasted_iota(jnp.int32, (A_VAL_W, A_KEY_W), 0) >> 6
    sl = lax.broadcasted_iota(jnp.int32, (A_VAL_W, A_KEY_W), 1) >> 5
    smask = sr == sl
    qt16 = qt.astype(BF16)
    inter = [None] * nch
    order = range(nch) if forward else range(nch - 1, -1, -1)
    for c in order:
        lo = c * GLA_CHUNK
        inter[c] = lax.dot_general(qt16[lo:lo + GLA_CHUNK], st.astype(BF16),
                                   (((1,), (1,)), ((), ())), preferred_element_type=F32)
        ds = lax.dot_general(v16[lo:lo + GLA_CHUNK], kend[lo:lo + GLA_CHUNK],
                             (((0,), (0,)), ((), ())), preferred_element_type=F32)
        st = st * dec3[c] + jnp.where(smask, ds, 0.0)
    return o + jnp.concatenate(inter, axis=0), st


def _gla_body(gla_ref, lr_ref, wg_ref, gb_ref, s0_ref, gn_ref, gm_ref, o_ref, sfin_ref, la_scr, *, seq):
    ntile = seq // GLA_TILE
    logit = _bdot(lr_ref[...], wg_ref[...]) + gb_ref[...]
    la_scr[...] = _log_sigmoid(logit) * (1.0 / GATE_TEMP)

    def run(forward):
        d = 0 if forward else 1

        def step(i, st):
            t = i if forward else ntile - 1 - i
            r0 = pl.multiple_of(t * GLA_TILE, GLA_TILE)
            rows = pl.ds(r0, GLA_TILE)
            q = gla_ref[rows, 0:128]
            k = gla_ref[rows, 128:256]
            v16 = gla_ref[rows, 256:512].astype(BF16)
            la = la_scr[rows, d * A_KEY_W:(d + 1) * A_KEY_W]
            o, st = _gla_tile(q, k, v16, la, st, forward)
            if forward:
                o_ref[rows, :] = o
            else:
                o_ref[rows, :] += o
            return st

        sfin_ref[0, d] = lax.fori_loop(0, ntile, step, s0_ref[0, d])

    run(True)
    run(False)
    o = o_ref[...]
    ms = _dot2(o * o, gm_ref[...])
    g = gla_ref[:, 512:768]
    o_ref[...] = o * lax.rsqrt(ms + EPS) * gn_ref[...] * (g * _sigmoid(g))


def _gla(proj, wg, gb, s0t, gnorm, gmat, *, nseq, seq, row_block0):
    in_specs = [
        pl.BlockSpec((seq, 768), lambda b: (row_block0 + b, 0)),
        pl.BlockSpec((seq, LANES), lambda b: (row_block0 + b, COL_LR // LANES)),
        pl.BlockSpec((LANES, 2 * A_KEY_W), lambda b: (0, 0)),
        pl.BlockSpec((1, 2 * A_KEY_W), lambda b: (0, 0)),
        pl.BlockSpec((1, 2, A_VAL_W, A_KEY_W), lambda b: (b, 0, 0, 0)),
        pl.BlockSpec((1, A_VAL_W), lambda b: (0, 0)),
        pl.BlockSpec((A_VAL_W, A_VAL_W), lambda b: (0, 0)),
    ]
    return pl.pallas_call(
        functools.partial(_gla_body, seq=seq),
        grid=(nseq,),
        in_specs=in_specs,
        out_specs=[
            pl.BlockSpec((seq, A_VAL_W), lambda b: (b, 0)),
            pl.BlockSpec((1, 2, A_VAL_W, A_KEY_W), lambda b: (b, 0, 0, 0)),
        ],
        out_shape=[
            jax.ShapeDtypeStruct((nseq * seq, A_VAL_W), F32),
            jax.ShapeDtypeStruct((nseq, 2, A_VAL_W, A_KEY_W), F32),
        ],
        scratch_shapes=[pltpu.VMEM((seq, 2 * A_KEY_W), F32)],
        compiler_params=_params(("parallel",)),
        name=f"gla{seq}",
    )(proj, proj, wg, gb, s0t, gnorm, gmat)


def _fnet_body(x_ref, cb_ref, sb_ref, ct_ref, st_ref, o_ref):
    x = x_ref[...].astype(BF16)
    xc = jnp.dot(x, cb_ref[...], preferred_element_type=F32).astype(BF16)
    xs = jnp.dot(x, sb_ref[...], preferred_element_type=F32).astype(BF16)
    o_ref[...] = (jnp.dot(ct_ref[...], xc, preferred_element_type=F32)
                  - jnp.dot(st_ref[...], xs, preferred_element_type=F32))


def _fnet(proj, cb, sb, ct, st, *, nseq, seq, row_block0):
    in_specs = [
        pl.BlockSpec((seq, F_WIDTH), lambda b: (row_block0 + b, COL_PF // F_WIDTH)),
        pl.BlockSpec((F_WIDTH, F_WIDTH), lambda b: (0, 0)),
        pl.BlockSpec((F_WIDTH, F_WIDTH), lambda b: (0, 0)),
        pl.BlockSpec((seq, seq), lambda b: (0, 0)),
        pl.BlockSpec((seq, seq), lambda b: (0, 0)),
    ]
    return pl.pallas_call(
        _fnet_body,
        grid=(nseq,),
        in_specs=in_specs,
        out_specs=pl.BlockSpec((seq, F_WIDTH), lambda b: (b, 0)),
        out_shape=jax.ShapeDtypeStruct((nseq * seq, F_WIDTH), F32),
        compiler_params=_params(("parallel",)),
        name=f"fnet{seq}",
    )(proj, cb, sb, ct, st)


def _dft_mats(n, scale):
    jk = np.outer(np.arange(n), np.arange(n)) % n
    ang = jk.astype(np.float64) * (2.0 * np.pi / n)
    return np.cos(ang) * scale, np.sin(ang) * scale


def _head_rms(x, gm16, gain):
    ms = _dot2(x * x, gm16)
    return x * lax.rsqrt(ms + EPS) * gain


def _rope(x, cos, sin_signed):
    w = x.shape[1]
    lane = lax.broadcasted_iota(jnp.int32, x.shape, 1)
    partner = jnp.where((lane & 31) < 16, pltpu.roll(x, w - 16, 1), pltpu.roll(x, 16, 1))
    return x * cos + partner * sin_signed


def _attn_body(*refs, rope, n_new, n_past):
    if n_past:
        (q_ref, k_ref, v_ref, ck_ref, cv_ref, cq_ref, sq_ref, cosk_ref, sink_ref,
         qn_ref, kn_ref, gm_ref, o_ref, ka_ref, k_scr, ks_scr, v_scr, vs_scr) = refs
    else:
        (q_ref, k_ref, v_ref, qn_ref, kn_ref, gm_ref, o_ref, ka_ref,
         k_scr, ks_scr, v_scr, vs_scr) = refs
    gm = gm_ref[...]

    @pl.when(pl.program_id(1) == 0)
    def _():
        kn = _head_rms(k_ref[...], gm, kn_ref[...])
        ka_ref[...] = kn
        if rope:
            kn = _rope(kn, cosk_ref[...], sink_ref[...])
        v = v_ref[...]
        new = pl.ds(0, n_new)
        k_scr[new, :] = kn.astype(BF16)
        ks_scr[new, :] = pltpu.roll(kn, HEAD_DIM, 1).astype(BF16)
        v_scr[new, :] = v.astype(BF16)
        vs_scr[new, :] = pltpu.roll(v, HEAD_DIM, 1).astype(BF16)
        if n_past:
            past = pl.ds(n_new, n_past)
            ck = ck_ref[0]
            cv = cv_ref[0]
            k_scr[past, :] = ck.astype(BF16)
            ks_scr[past, :] = pltpu.roll(ck, HEAD_DIM, 1).astype(BF16)
            v_scr[past, :] = cv.astype(BF16)
            vs_scr[past, :] = pltpu.roll(cv, HEAD_DIM, 1).astype(BF16)

    lane = lax.broadcasted_iota(jnp.int32, (Q_TILE, LANES), 1)
    low = lane < HEAD_DIM
    for jp in range(C_Q_W // LANES):
        kv = jp // 2
        q2 = _head_rms(q_ref[:, jp * LANES:(jp + 1) * LANES], gm, qn_ref[...])
        if rope:
            q2 = _rope(q2, cq_ref[...], sq_ref[...])
        q2 = q2 * (HEAD_DIM ** -0.5)
        halves = []
        for half in range(2):
            swapped = (half == 1) != (kv == 1)
            kk = ks_scr[...] if swapped else k_scr[...]
            vv = vs_scr[...] if swapped else v_scr[...]
            qm = jnp.where(low if half == 0 else ~low, q2, 0.0).astype(BF16)
            s = lax.dot_general(qm, kk, (((1,), (1,)), ((), ())), preferred_element_type=F32)
            p = jnp.exp(s - jnp.max(s, axis=-1, keepdims=True))
            den = jnp.sum(p, axis=-1, keepdims=True)
            halves.append(jnp.dot(p.astype(BF16), vv, preferred_element_type=F32) / den)
        o_ref[:, jp * LANES:(jp + 1) * LANES] = jnp.where(low, halves[0], halves[1])


def _attn(proj, qn, kn, gm, *, nseq, seq, row_block0, past=None, tables=None):
    nq = seq // Q_TILE
    n_past = 0 if past is None else past[0].shape[1]
    qrow = lambda b, j: ((row_block0 + b) * nq + j, COL_CQ // C_Q_W)
    in_specs = [
        pl.BlockSpec((Q_TILE, C_Q_W), qrow),
        pl.BlockSpec((seq, C_KV_W), lambda b, j: (row_block0 + b, COL_CK // C_KV_W)),
        pl.BlockSpec((seq, C_KV_W), lambda b, j: (row_block0 + b, COL_CV // C_KV_W)),
    ]
    args = [proj, proj, proj]
    if past is not None:
        in_specs += [pl.BlockSpec((1, n_past, C_KV_W), lambda b, j: (b, 0, 0))] * 2
        in_specs += [pl.BlockSpec((Q_TILE, LANES), lambda b, j: (j, 0))] * 2
        in_specs += [pl.BlockSpec((seq, LANES), lambda b, j: (0, 0))] * 2
        args += [past[0], past[1], tables[0], tables[1], tables[0], tables[1]]
    in_specs += [
        pl.BlockSpec((1, LANES), lambda b, j: (0, 0)),
        pl.BlockSpec((1, LANES), lambda b, j: (0, 0)),
        pl.BlockSpec((LANES, LANES), lambda b, j: (0, 0)),
    ]
    args += [qn, kn, gm]
    s_all = seq + n_past
    return pl.pallas_call(
        functools.partial(_attn_body, rope=past is not None, n_new=seq, n_past=n_past),
        grid=(nseq, nq),
        in_specs=in_specs,
        out_specs=[
            pl.BlockSpec((Q_TILE, C_Q_W), lambda b, j: (b * nq + j, 0)),
            pl.BlockSpec((seq, C_KV_W), lambda b, j: (b, 0)),
        ],
        out_shape=[
            jax.ShapeDtypeStruct((nseq * seq, C_Q_W), F32),
            jax.ShapeDtypeStruct((nseq * seq, C_KV_W), F32),
        ],
        scratch_shapes=[pltpu.VMEM((s_all, C_KV_W), BF16)] * 4,
        compiler_params=_params(("parallel", "arbitrary")),
        name=f"attn{seq}",
    )(*args)


def _rope_tables():
    t = np.arange(DEC_SEQ)
    rows = (t // GRID_W).astype(np.float32)
    cols = (t % GRID_W).astype(np.float32)
    nf = HEAD_DIM // 4
    inv_freq = (ROPE_THETA ** (-np.arange(nf, dtype=np.float32) / nf)).astype(np.float32)
    d = np.arange(LANES)
    pos = np.where(((d % HEAD_DIM) < HEAD_DIM // 2)[None, :], rows[:, None], cols[:, None])
    ang = (pos * inv_freq[d % nf][None, :]).astype(np.float32)
    sign = np.where((d % 32) < 16, -1.0, 1.0)[None, :]
    return jnp.asarray(np.cos(ang), F32), jnp.asarray(np.sin(ang) * sign, F32)


def _outproj_body(oa_c, of_c, oc_c, oa_d, of_d, oc_d, w_ref, *refs):
    *x_refs, mod_ref, g_ref, rw_ref, rb_ref, x1_ref, h2_ref, meta_ref, slot_ref, info_ref = refs
    mod = mod_ref[0]
    is_ctx = pl.program_id(0) < N_CTX // TM
    pick = lambda c_ref, d_ref: jnp.where(is_ctx, c_ref[...], d_ref[...]).astype(BF16)
    x_in = x_refs[0][...] if len(x_refs) == 1 else jnp.where(is_ctx, x_refs[0][...], x_refs[1][...])
    mixed = (jnp.dot(pick(oa_c, oa_d), w_ref[0:256, :], preferred_element_type=F32)
             + jnp.dot(pick(of_c, of_d), w_ref[256:512, :], preferred_element_type=F32)
             + jnp.dot(pick(oc_c, oc_d), w_ref[512:1024, :], preferred_element_type=F32))
    x1 = x_in + mod[2:3] * mixed
    x1_ref[...] = x1
    h2 = _rms(x1) * g_ref[...] * (1.0 + mod[4:5]) + mod[3:4]
    h2_ref[...] = h2.astype(BF16)
    lane = lax.broadcasted_iota(jnp.int32, (TM, LANES), 1)
    logits = jnp.where(lane < N_EXPERTS, _dot3(h2, rw_ref[...]) + rb_ref[...], -jnp.inf)
    meta = jnp.zeros((TM, LANES), F32)
    tops = []
    for k in range(TOP_K):
        m = jnp.max(logits, axis=-1, keepdims=True)
        idx = jnp.min(jnp.where(logits == m, lane, LANES), axis=-1, keepdims=True)
        hit = lane == idx
        logits = jnp.where(hit, -jnp.inf, logits)
        meta = jnp.where(lane == k, idx.astype(F32), meta)
        tops.append(m)
    ex = [jnp.exp(t - tops[0]) for t in tops]
    den = ex[0] + ex[1] + ex[2] + ex[3]
    for k in range(TOP_K):
        meta = jnp.where(lane == TOP_K + k, ex[k] / den, meta)
    meta_ref[...] = meta
    for t in range(TM // ROW_TILE):
        rows = slice(t * ROW_TILE, (t + 1) * ROW_TILE)
        slot_ref[rows, :], info_ref[t] = _tile_slots(meta[rows])


def _outproj(mix_ctx, mix_dec, w, xs, mod3, gain, rw, rb):
    n_ctx_tiles = N_CTX // TM
    tok = lambda w_: pl.BlockSpec((TM, w_), lambda i: (i, 0))
    ctx = lambda w_: pl.BlockSpec((TM, w_), lambda i: (jnp.minimum(i, n_ctx_tiles - 1), 0))
    dec = lambda w_: pl.BlockSpec((TM, w_), lambda i: (jnp.maximum(i - n_ctx_tiles, 0), 0))
    full = lambda a, b: pl.BlockSpec((a, b), lambda i: (0, 0))
    x_specs = [ctx(D_MODEL), dec(D_MODEL)] if len(xs) == 2 else [tok(D_MODEL)]
    tiles_per_step = TM // ROW_TILE
    return pl.pallas_call(
        _outproj_body,
        grid=(N_TOK // TM,),
        in_specs=[ctx(A_VAL_W), ctx(F_WIDTH), ctx(C_Q_W), dec(A_VAL_W), dec(F_WIDTH), dec(C_Q_W),
                  full(D_MODEL, D_MODEL)] + x_specs + [
                  pl.BlockSpec((1, 6, D_MODEL), lambda i: (_mod_row(i, TM), 0, 0)),
                  full(1, D_MODEL), full(D_MODEL, LANES), full(1, LANES)],
        out_specs=[tok(D_MODEL), tok(D_MODEL), tok(LANES), tok(LANES),
                   pl.BlockSpec((tiles_per_step, 8, LANES), lambda i: (i, 0, 0))],
        out_shape=[jax.ShapeDtypeStruct((N_TOK, D_MODEL), F32),
                   jax.ShapeDtypeStruct((N_TOK, D_MODEL), BF16),
                   jax.ShapeDtypeStruct((N_TOK, LANES), F32),
                   jax.ShapeDtypeStruct((N_TOK, LANES), F32),
                   jax.ShapeDtypeStruct((N_TILES, 8, LANES), F32)],
        compiler_params=_params(("parallel",)),
        name="outproj",
    )(*mix_ctx, *mix_dec, w, *xs, mod3, gain, rw, rb)


def _tile_slots(meta):
    lane = lax.broadcasted_iota(jnp.int32, (ROW_TILE, LANES), 1)
    idx = [meta[:, k:k + 1].astype(jnp.int32) for k in range(TOP_K)]
    sel = jnp.zeros((ROW_TILE, LANES), F32)
    for k in range(TOP_K):
        sel = jnp.where(lane == idx[k], 1.0, sel)
    ri = lax.broadcasted_iota(jnp.int32, (ROW_TILE, ROW_TILE), 0)
    ci = lax.broadcasted_iota(jnp.int32, (ROW_TILE, ROW_TILE), 1)
    lower = jnp.where(ci < ri, 1.0, 0.0).astype(BF16)
    rank = jnp.dot(lower, sel.astype(BF16), preferred_element_type=F32)
    cnt = jnp.sum(sel, axis=0, keepdims=True)
    n8 = jnp.floor((cnt + (RUN_ALIGN - 1)) * (1.0 / RUN_ALIGN)) * RUN_ALIGN
    er = lax.broadcasted_iota(jnp.int32, (LANES, LANES), 0)
    ec = lax.broadcasted_iota(jnp.int32, (LANES, LANES), 1)
    before = jnp.where(er < ec, 1.0, 0.0).astype(BF16)
    off = jnp.dot(jnp.broadcast_to(n8, (8, LANES)).astype(BF16), before, preferred_element_type=F32)
    tot = rank + off[0:1]
    out = jnp.zeros((ROW_TILE, LANES), F32)
    for k in range(TOP_K):
        pk = jnp.sum(jnp.where(lane == idx[k], tot, 0.0), axis=-1, keepdims=True)
        out = jnp.where(lane == k, pk, out)
    row = lax.broadcasted_iota(jnp.int32, (8, LANES), 0)
    return out, jnp.where(row == 0, n8, jnp.where(row == 1, off, 0.0))


def _piece_copies(n_rows, row_a, row_b, make_copy, action, bits):
    m = lax.shift_right_logical(n_rows, RUN_SHIFT)
    for bit in range(bits):
        rows = RUN_ALIGN << bit
        lo = (m & ((1 << bit) - 1)) * RUN_ALIGN

        @pl.when(((m >> bit) & 1) == 1)
        def _():
            cp = make_copy(pl.multiple_of(row_a + lo, RUN_ALIGN), pl.multiple_of(row_b + lo, RUN_ALIGN), rows)
            if action == "start":
                cp.start()
            else:
                cp.wait()


def _run_copies(tile, n8_tbl, off_tbl, base_tbl, make_copy, action):
    def per_expert(e, c):
        j = tile * N_EXPERTS + e
        _piece_copies(n8_tbl[j], off_tbl[j], base_tbl[j], make_copy, action, RUN_BITS)
        return c

    lax.fori_loop(0, N_EXPERTS, per_expert, 0)


def _unpack_pairs(u):
    hi = lax.bitcast_convert_type(u & jnp.uint32(0xFFFF0000), F32)
    lo = lax.bitcast_convert_type(u << 16, F32)
    return hi, lo


def _pack_pairs(a, b):
    return lax.bitcast_convert_type(a, jnp.uint32) | (lax.bitcast_convert_type(b, jnp.uint32) >> 16)


def _slot_onehot(slot, weights=None):
    lane = lax.broadcasted_iota(jnp.int32, (ROW_TILE, SORT_ROWS), 1)
    p = jnp.zeros((ROW_TILE, SORT_ROWS), F32)
    for k in range(TOP_K):
        hit = lane == slot[:, k:k + 1].astype(jnp.int32)
        p = jnp.where(hit, 1.0 if weights is None else weights[:, k:k + 1], p)
    return p.astype(BF16)


def _dispatch_body(n8_tbl, off_tbl, base_tbl, tail_n, tail_base, nused, h_ref, slot_ref, xs_hbm,
                   sorted_scr, zero_scr, sem):
    tile = pl.program_id(0)

    @pl.when(tile == 0)
    def _():
        zero_scr[...] = jnp.zeros_like(zero_scr)

        def make_zero_copy(_, dst_row, rows):
            return pltpu.make_async_copy(zero_scr.at[pl.ds(0, rows)], xs_hbm.at[pl.ds(dst_row, rows)], sem.at[2])

        for action in ("start", "wait"):
            def per_expert(e, c):
                _piece_copies(tail_n[e], 0, tail_base[e], make_zero_copy, action, TAIL_BITS)
                return c

            def per_block(b, c):
                cp = make_zero_copy(0, pl.multiple_of(b * MOE_BM, MOE_BM), MOE_BM)
                if action == "start":
                    cp.start()
                else:
                    cp.wait()
                return c

            lax.fori_loop(0, N_EXPERTS, per_expert, 0)
            lax.fori_loop(nused[0], MOE_NB, per_block, 0)

    p = _slot_onehot(slot_ref[...])
    srt = lax.dot_general(p, h_ref[...], (((0,), (0,)), ((), ())), preferred_element_type=F32)
    cur = tile & 1
    sorted_scr[cur] = _pack_pairs(srt[:, :HALF_D], srt[:, HALF_D:])

    def copies_from(buf):
        def make_copy(src_row, dst_row, rows):
            return pltpu.make_async_copy(sorted_scr.at[buf, pl.ds(src_row, rows)],
                                         xs_hbm.at[pl.ds(dst_row, rows)], sem.at[buf])
        return make_copy

    _run_copies(tile, n8_tbl, off_tbl, base_tbl, copies_from(cur), "start")

    @pl.when(tile > 0)
    def _():
        _run_copies(tile - 1, n8_tbl, off_tbl, base_tbl, copies_from(1 - cur), "wait")

    @pl.when(tile == N_TILES - 1)
    def _():
        _run_copies(tile, n8_tbl, off_tbl, base_tbl, copies_from(cur), "wait")


def _dispatch(tables, tails, h2, slots):
    grid_spec = pltpu.PrefetchScalarGridSpec(
        num_scalar_prefetch=6,
        grid=(N_TILES,),
        in_specs=[pl.BlockSpec((ROW_TILE, D_MODEL), lambda i, *_: (i, 0)),
                  pl.BlockSpec((ROW_TILE, LANES), lambda i, *_: (i, 0))],
        out_specs=pl.BlockSpec(memory_space=pl.ANY),
        scratch_shapes=[pltpu.VMEM((2, SORT_ROWS, HALF_D), jnp.uint32),
                        pltpu.VMEM((MOE_BM, HALF_D), jnp.uint32), pltpu.SemaphoreType.DMA((3,))],
    )
    return pl.pallas_call(
        _dispatch_body,
        grid_spec=grid_spec,
        out_shape=jax.ShapeDtypeStruct((MOE_ROWS, HALF_D), jnp.uint32),
        compiler_params=_params(("arbitrary",)),
        name="dispatch",
    )(*tables, *tails, h2, slots)


def _experts_body(bexp_ref, nused_ref, first_ref, next_ref, slot_ref, x_ref, wgu_hbm, bgu_ref, wdn_hbm, bdn_ref,
                  y_ref, wgu32, wdn32, wgu16, wdn16, sem, *, layer):
    i = pl.program_id(0)

    def weight_copies(expert, s):
        return (pltpu.make_async_copy(wgu_hbm.at[layer, expert], wgu32.at[s], sem.at[0, s]),
                pltpu.make_async_copy(wdn_hbm.at[layer, expert], wdn32.at[s], sem.at[1, s]))

    @pl.when(first_ref[i] == 1)
    def _():
        s = slot_ref[i]

        @pl.when(i == 0)
        def _():
            for cp in weight_copies(bexp_ref[i], s):
                cp.start()

        for cp in weight_copies(bexp_ref[i], s):
            cp.wait()

        @pl.when(next_ref[i] >= 0)
        def _():
            for cp in weight_copies(next_ref[i], 1 - s):
                cp.start()

        wgu16[...] = wgu32[s].astype(BF16)
        wdn16[...] = wdn32[s].astype(BF16)

    @pl.when(i < nused_ref[0])
    def _():
        x = jnp.concatenate(_unpack_pairs(x_ref[...]), axis=1).astype(BF16)
        gu = jnp.dot(x, wgu16[...], preferred_element_type=F32) + bgu_ref[0, 0]
        gate = jnp.minimum(gu[:, :D_FF], SWIGLU_LIMIT)
        up = jnp.clip(gu[:, D_FF:], -SWIGLU_LIMIT, SWIGLU_LIMIT)
        glu = gate * _sigmoid(gate * SWIGLU_ALPHA)
        act = ((up + 1.0) * glu).astype(BF16)
        y = jnp.dot(act, wdn16[...], preferred_element_type=F32) + bdn_ref[0, 0]
        y = y.astype(BF16).astype(F32)
        y_ref[...] = _pack_pairs(y[:, :HALF_D], y[:, HALF_D:])

    @pl.when(i >= nused_ref[0])
    def _():
        y_ref[...] = jnp.zeros_like(y_ref)


def _experts(sched, xs, wgu, bgu, wdn, bdn, layer):
    e = lambda i, be, *_: (layer, be[i], 0, 0)
    grid_spec = pltpu.PrefetchScalarGridSpec(
        num_scalar_prefetch=5,
        grid=(MOE_NB,),
        in_specs=[
            pl.BlockSpec((MOE_BM, HALF_D), lambda i, be, nu, *_: (jnp.minimum(i, nu[0] - 1), 0)),
            pl.BlockSpec(memory_space=pl.ANY),
            pl.BlockSpec((1, 1, 1, 2 * D_FF), e),
            pl.BlockSpec(memory_space=pl.ANY),
            pl.BlockSpec((1, 1, 1, D_MODEL), e),
        ],
        out_specs=pl.BlockSpec((MOE_BM, HALF_D), lambda i, *_: (i, 0)),
        scratch_shapes=[pltpu.VMEM((2, D_MODEL, 2 * D_FF), F32), pltpu.VMEM((2, D_FF, D_MODEL), F32),
                        pltpu.VMEM((D_MODEL, 2 * D_FF), BF16), pltpu.VMEM((D_FF, D_MODEL), BF16),
                        pltpu.SemaphoreType.DMA((2, 2))],
    )
    return pl.pallas_call(
        functools.partial(_experts_body, layer=layer),
        grid_spec=grid_spec,
        out_shape=jax.ShapeDtypeStruct((MOE_ROWS, HALF_D), jnp.uint32),
        compiler_params=_params(("arbitrary",)),
        name="experts",
    )(*sched, xs, wgu, bgu, wdn, bdn)


def _combine_body(n8_tbl, off_tbl, base_tbl, y_hbm, slot_ref, meta_ref, x1_ref, mod_ref, nf_ref, o_ref,
                  ybuf, sem, *, final):
    tile = pl.program_id(0)
    cur = tile & 1

    def fetch(t, buf):
        ybuf[buf] = jnp.zeros((SORT_ROWS, HALF_D), jnp.uint32)
        _run_copies(t, n8_tbl, off_tbl, base_tbl, copies_into(buf), "start")

    def copies_into(buf):
        def make_copy(buf_row, hbm_row, rows):
            return pltpu.make_async_copy(y_hbm.at[pl.ds(hbm_row, rows)],
                                         ybuf.at[buf, pl.ds(buf_row, rows)], sem.at[buf])
        return make_copy

    @pl.when(tile == 0)
    def _():
        fetch(tile, cur)

    @pl.when(tile + 1 < N_TILES)
    def _():
        fetch(tile + 1, 1 - cur)

    _run_copies(tile, n8_tbl, off_tbl, base_tbl, copies_into(cur), "wait")
    pw = _slot_onehot(slot_ref[...], meta_ref[...][:, TOP_K:2 * TOP_K])
    hi, lo = _unpack_pairs(ybuf[cur])
    ff = jnp.concatenate([jnp.dot(pw, hi.astype(BF16), preferred_element_type=F32),
                          jnp.dot(pw, lo.astype(BF16), preferred_element_type=F32)], axis=1)
    x2 = x1_ref[...] + mod_ref[0][5:6] * ff
    if final:
        x2 = _rms(x2) * nf_ref[...]
    o_ref[...] = x2


def _combine(tables, y, slots, meta, x1, mod3, nfinal, *, final):
    grid_spec = pltpu.PrefetchScalarGridSpec(
        num_scalar_prefetch=3,
        grid=(N_TILES,),
        in_specs=[pl.BlockSpec(memory_space=pl.ANY),
                  pl.BlockSpec((ROW_TILE, LANES), lambda i, *_: (i, 0)),
                  pl.BlockSpec((ROW_TILE, LANES), lambda i, *_: (i, 0)),
                  pl.BlockSpec((ROW_TILE, D_MODEL), lambda i, *_: (i, 0)),
                  pl.BlockSpec((1, 6, D_MODEL), lambda i, *_: (_mod_row(i, ROW_TILE), 0, 0)),
                  pl.BlockSpec((1, D_MODEL), lambda i, *_: (0, 0))],
        out_specs=pl.BlockSpec((ROW_TILE, D_MODEL), lambda i, *_: (i, 0)),
        scratch_shapes=[pltpu.VMEM((2, SORT_ROWS, HALF_D), jnp.uint32), pltpu.SemaphoreType.DMA((2,))],
    )
    return pl.pallas_call(
        functools.partial(_combine_body, final=final),
        grid_spec=grid_spec,
        out_shape=jax.ShapeDtypeStruct((N_TOK, D_MODEL), F32),
        compiler_params=_params(("arbitrary",)),
        name="combine",
    )(*tables, y, slots, meta, x1, mod3, nfinal)


def _state_to_blockdiag(s):
    eye = jnp.eye(A_HEADS, dtype=s.dtype)
    t = jnp.einsum("bdhkv,hg->bdhvgk", s, eye)
    return t.reshape(s.shape[0], 2, A_VAL_W, A_KEY_W)


def _blockdiag_to_state(t):
    b = t.shape[0]
    t = t.reshape(b, 2, A_HEADS, A_DV, A_HEADS, A_DK)
    s = jnp.stack([t[:, :, h, :, h, :] for h in range(A_HEADS)], axis=2)
    return jnp.swapaxes(s, -1, -2)


def kernel(x_prompt, x_sample, state_gla, cache_k, cache_v, c, c_ctx, w_ada, b_ada, norm_mix, norm_ffn,
           norm_final, w_in, w_out, gla_gate_w, gla_gate_b, gla_norm, q_norm, k_norm, router_w, router_b,
           expert_w_gu, expert_b_gu, expert_w_dn, expert_b_dn):
    xs = (x_prompt.reshape(N_CTX, D_MODEL), x_sample.reshape(N_DEC, D_MODEL))
    cond = jnp.zeros((MOD_ROWS, D_MODEL), F32).at[0].set(c_ctx).at[1:1 + DEC_BATCH].set(c)
    mod = _ada(cond, w_ada, b_ada)

    gm256 = jnp.asarray(np.kron(np.eye(A_HEADS), np.ones((A_DV, A_DV))) / A_DV, BF16)
    gm128 = jnp.asarray(np.kron(np.eye(2), np.ones((HEAD_DIM, HEAD_DIM))) / HEAD_DIM, BF16)
    c64, s64 = _dft_mats(F_GROUP_W, F_GROUP_W ** -0.5)
    cb = jnp.asarray(np.kron(np.eye(F_GROUPS), c64), BF16)
    sb = jnp.asarray(np.kron(np.eye(F_GROUPS), s64), BF16)
    dft_ctx = [jnp.asarray(m, BF16) for m in _dft_mats(SEQ, SEQ ** -0.5)]
    dft_dec = [jnp.asarray(m, BF16) for m in _dft_mats(DEC_SEQ, DEC_SEQ ** -0.5)]
    rope_tabs = _rope_tables()
    zero_state = jnp.zeros((BATCH, 2, A_VAL_W, A_KEY_W), F32)

    new_s, new_k, new_v = [], [], []
    for l in range(DEPTH):
        mod3 = mod[l].reshape(MOD_ROWS, 6, D_MODEL)
        w_l = w_in[l]
        w_r = jnp.concatenate([w_l[:, :768], w_l[:, 800:], w_l[:, 768:800],
                               jnp.zeros((D_MODEL, PROJ_W - 1824), F32)], axis=1).astype(BF16)
        proj = _inproj(xs, mod3, norm_mix[l][None, :], w_r)

        wg = jnp.zeros((LANES, 2 * A_KEY_W), F32)
        wg = wg.at[0:GATE_RANK, 0:A_KEY_W].set(gla_gate_w[l, 0])
        wg = wg.at[GATE_RANK:2 * GATE_RANK, A_KEY_W:].set(gla_gate_w[l, 1])
        gb = gla_gate_b[l].reshape(1, 2 * A_KEY_W)
        gn = jnp.tile(gla_norm[l], A_HEADS)[None, :]
        s0_dec = _state_to_blockdiag(state_gla[:, l])
        oa_ctx, sfin = _gla(proj, wg, gb, zero_state, gn, gm256, nseq=BATCH, seq=SEQ, row_block0=0)
        oa_dec, _ = _gla(proj, wg, gb, s0_dec, gn, gm256, nseq=DEC_BATCH, seq=DEC_SEQ,
                         row_block0=N_CTX // DEC_SEQ)
        of_ctx = _fnet(proj, cb, sb, *dft_ctx, nseq=BATCH, seq=SEQ, row_block0=0)
        of_dec = _fnet(proj, cb, sb, *dft_dec, nseq=DEC_BATCH, seq=DEC_SEQ, row_block0=N_CTX // DEC_SEQ)
        qn = jnp.tile(q_norm[l], 2)[None, :]
        kn = jnp.tile(k_norm[l], 2)[None, :]
        oc_ctx, ka = _attn(proj, qn, kn, gm128, nseq=BATCH, seq=SEQ, row_block0=0)
        past = (cache_k[:, l].reshape(DEC_BATCH, PAST_LEN, C_KV_W),
                cache_v[:, l].reshape(DEC_BATCH, PAST_LEN, C_KV_W))
        oc_dec, _ = _attn(proj, qn, kn, gm128, nseq=DEC_BATCH, seq=DEC_SEQ,
                          row_block0=N_CTX // DEC_SEQ, past=past, tables=rope_tabs)
        new_s.append(_blockdiag_to_state(sfin))
        new_k.append(ka.reshape(BATCH, SEQ, C_KV_HEADS, HEAD_DIM))
        new_v.append(proj[:N_CTX, COL_CV:COL_CV + C_KV_W].reshape(BATCH, SEQ, C_KV_HEADS, HEAD_DIM))

        rw = jnp.zeros((D_MODEL, LANES), F32).at[:, :N_EXPERTS].set(router_w[l])
        rb = jnp.zeros((1, LANES), F32).at[0, :N_EXPERTS].set(router_b[l])
        x1, h2, meta, slots, info = _outproj((oa_ctx, of_ctx, oc_ctx), (oa_dec, of_dec, oc_dec),
                                             w_out[l].astype(BF16), xs, mod3, norm_ffn[l][None, :], rw, rb)

        n8 = info[:, 0, :N_EXPERTS].astype(jnp.int32)
        off = info[:, 1, :N_EXPERTS].astype(jnp.int32)
        cnt = jnp.sum(n8, axis=0)
        pcnt = (cnt + MOE_BM - 1) // MOE_BM * MOE_BM
        pend = jnp.cumsum(pcnt)
        base = (pend - pcnt)[None, :] + jnp.cumsum(n8, axis=0) - n8
        tables = (n8.reshape(-1), off.reshape(-1), base.reshape(-1))
        blk = jnp.arange(MOE_NB, dtype=jnp.int32)
        blk_exp = jnp.minimum(jnp.sum((pend[None, :] <= (blk * MOE_BM)[:, None]).astype(jnp.int32), axis=1),
                              N_EXPERTS - 1)
        nused = (pend[-1:] // MOE_BM).astype(jnp.int32)
        tails = (pcnt - cnt, pend - (pcnt - cnt), nused)
        prev_exp = jnp.concatenate([jnp.full((1,), -1, jnp.int32), blk_exp[:-1]])
        first = ((blk < nused[0]) & (blk_exp != prev_exp)).astype(jnp.int32)
        end_blk = (pend // MOE_BM)[blk_exp]
        nxt = jnp.where(end_blk < nused[0], blk_exp[jnp.minimum(end_blk, MOE_NB - 1)], -1)
        wslot = (jnp.cumsum(first) - 1) & 1
        sched = (blk_exp, nused, first, nxt, wslot)

        xs = _dispatch(tables, tails, h2, slots)
        y = _experts(sched, xs, expert_w_gu,
                     expert_b_gu.reshape(DEPTH, N_EXPERTS, 1, 2 * D_FF), expert_w_dn,
                     expert_b_dn.reshape(DEPTH, N_EXPERTS, 1, D_MODEL), l)
        x = _combine(tables, y, slots, meta, x1, mod3, norm_final[None, :], final=(l == DEPTH - 1))
        xs = (x,)

    y_prompt = x[:N_CTX].reshape(BATCH, SEQ, D_MODEL)
    y_sample = x[N_CTX:].reshape(DEC_BATCH, DEC_SEQ, D_MODEL)
    return (y_prompt, y_sample, jnp.stack(new_s, axis=1), jnp.stack(new_k, axis=1), jnp.stack(new_v, axis=1))
```

```python
import functools

import jax
import jax.numpy as jnp
import numpy as np
from jax import lax
from jax.experimental import pallas as pl
from jax.experimental.pallas import tpu as pltpu

D_MODEL = 1024
BATCH = 32
SEQ = 256
DEPTH = 2
DEC_BATCH = 8
DEC_SEQ = 1024
PAST_LEN = 512
GRID_W = 64
A_HEADS = 4
A_DV = 64
A_DK = 32
A_KEY_W = 128
A_VAL_W = 256
GATE_RANK = 16
GATE_TEMP = 16.0
GLA_CHUNK = 32
F_GROUPS = 4
F_GROUP_W = 64
F_WIDTH = 256
HEAD_DIM = 64
C_HEADS = 8
C_KV_HEADS = 2
C_Q_W = 512
C_KV_W = 128
ROPE_THETA = 10000.0
N_EXPERTS = 32
TOP_K = 4
D_FF = 1024
SWIGLU_LIMIT = 7.0
SWIGLU_ALPHA = 1.702
EPS = 1e-6

N_CTX = BATCH * SEQ
N_DEC = DEC_BATCH * DEC_SEQ
N_TOK = N_CTX + N_DEC
MOD_ROWS = 16

COL_GLA = 0
COL_PF = 768
COL_CQ = 1024
COL_CK = 1536
COL_CV = 1664
COL_LR = 1792
PROJ_W = 1920

LANES = 128
TM = 512
GLA_TILE = 256
Q_TILE = 256
MOE_BM = 512
ROW_TILE = 256
N_TILES = N_TOK // ROW_TILE
RUN_ALIGN = 8
RUN_SHIFT = 3
RUN_BITS = 6
TAIL_BITS = 6
SORT_ROWS = ROW_TILE * TOP_K + N_EXPERTS * RUN_ALIGN
HALF_D = D_MODEL // 2
MOE_NB = (N_TOK * TOP_K + N_TILES * N_EXPERTS * (RUN_ALIGN - 1)) // MOE_BM + N_EXPERTS
MOE_ROWS = MOE_NB * MOE_BM
VMEM_LIMIT = 56 * 1024 * 1024

F32 = jnp.float32
BF16 = jnp.bfloat16


def _params(sem, vmem=VMEM_LIMIT):
    return pltpu.CompilerParams(dimension_semantics=sem, vmem_limit_bytes=vmem)


def _bdot(a, b):
    return jnp.dot(a.astype(BF16), b.astype(BF16), preferred_element_type=F32)


def _split(a):
    hi = a.astype(BF16)
    lo = (a - hi.astype(F32)).astype(BF16)
    return hi, lo


def _dot3(a, b):
    ah, al = _split(a)
    bh, bl = _split(b)
    d = functools.partial(jnp.dot, preferred_element_type=F32)
    return d(ah, bh) + (d(ah, bl) + d(al, bh))


def _dot2(a, b16):
    ah, al = _split(a)
    d = functools.partial(jnp.dot, preferred_element_type=F32)
    return d(ah, b16) + d(al, b16)


def _sigmoid(x):
    return 1.0 / (1.0 + jnp.exp(-x))


def _mod_row(i, tm):
    tok = i * tm
    return jnp.where(tok < N_CTX, 0, 1 + (tok - N_CTX) // DEC_SEQ)


def _ada_body(c_ref, w_ref, b_ref, o_ref):
    c = c_ref[...]
    o_ref[0] = _dot3(c * _sigmoid(c), w_ref[0]) + b_ref[0]


def _ada(cond, w_ada, b_ada):
    return pl.pallas_call(
        _ada_body,
        grid=(DEPTH, 6),
        in_specs=[
            pl.BlockSpec((MOD_ROWS, D_MODEL), lambda l, j: (0, 0)),
            pl.BlockSpec((1, D_MODEL, D_MODEL), lambda l, j: (l, 0, j)),
            pl.BlockSpec((1, 1, D_MODEL), lambda l, j: (l, 0, j)),
        ],
        out_specs=pl.BlockSpec((1, MOD_ROWS, D_MODEL), lambda l, j: (l, 0, j)),
        out_shape=jax.ShapeDtypeStruct((DEPTH, MOD_ROWS, 6 * D_MODEL), F32),
        compiler_params=_params(("parallel", "parallel")),
        name="ada",
    )(cond, w_ada, b_ada.reshape(DEPTH, 1, 6 * D_MODEL))


def _rms(x):
    return x * lax.rsqrt(jnp.mean(x * x, axis=-1, keepdims=True) + EPS)


def _inproj_body(*refs):
    *x_refs, mod_ref, g_ref, w_ref, o_ref = refs
    if len(x_refs) == 2:
        x = jnp.where(pl.program_id(0) < N_CTX // TM, x_refs[0][...], x_refs[1][...])
    else:
        x = x_refs[0][...]
    mod = mod_ref[0]
    h = _rms(x) * g_ref[...] * (1.0 + mod[1:2]) + mod[0:1]
    o_ref[...] = jnp.dot(h.astype(BF16), w_ref[...], preferred_element_type=F32)


def _inproj(xs, mod3, gain, w):
    n_ctx_tiles = N_CTX // TM
    if len(xs) == 2:
        x_specs = [pl.BlockSpec((TM, D_MODEL), lambda i: (jnp.minimum(i, n_ctx_tiles - 1), 0)),
                   pl.BlockSpec((TM, D_MODEL), lambda i: (jnp.maximum(i - n_ctx_tiles, 0), 0))]
    else:
        x_specs = [pl.BlockSpec((TM, D_MODEL), lambda i: (i, 0))]
    return pl.pallas_call(
        _inproj_body,
        grid=(N_TOK // TM,),
        in_specs=x_specs + [
            pl.BlockSpec((1, 6, D_MODEL), lambda i: (_mod_row(i, TM), 0, 0)),
            pl.BlockSpec((1, D_MODEL), lambda i: (0, 0)),
            pl.BlockSpec((D_MODEL, PROJ_W), lambda i: (0, 0)),
        ],
        out_specs=pl.BlockSpec((TM, PROJ_W), lambda i: (i, 0)),
        out_shape=jax.ShapeDtypeStruct((N_TOK, PROJ_W), F32),
        compiler_params=_params(("parallel",)),
        name="inproj",
    )(*xs, mod3, gain, w)


def _log_sigmoid(x):
    return jnp.minimum(x, 0.0) - jnp.log(1.0 + jnp.exp(-jnp.abs(x)))


def _chunk_cumsum(x, pos, forward):
    rows = x.shape[0]
    for s in (1, 2, 4, 8, 16):
        if forward:
            x = x + jnp.where(pos >= s, pltpu.roll(x, s, 0), 0.0)
        else:
            x = x + jnp.where(pos < GLA_CHUNK - s, pltpu.roll(x, rows - s, 0), 0.0)
    return x


def _gla_tile(q, k, v16, la, st, forward):
    R = GLA_TILE
    nch = R // GLA_CHUNK
    pos = lax.broadcasted_iota(jnp.int32, (R, 1), 0) & (GLA_CHUNK - 1)
    b = _chunk_cumsum(la, pos, forward)
    b3 = b.reshape(nch, GLA_CHUNK, A_KEY_W)
    tot3 = b3[:, GLA_CHUNK - 1:GLA_CHUNK, :] if forward else b3[:, 0:1, :]
    totb = jnp.broadcast_to(tot3, (nch, GLA_CHUNK, A_KEY_W)).reshape(R, A_KEY_W)
    qt = q * (A_DK ** -0.5) * jnp.exp(b)
    kt = (k * jnp.exp(-b)).astype(BF16)
    kend = (k * jnp.exp(totb - b)).astype(BF16)
    dec3 = jnp.exp(tot3)

    ri = lax.broadcasted_iota(jnp.int32, (R, R), 0)
    ci = lax.broadcasted_iota(jnp.int32, (R, R), 1)
    tri = (ci <= ri) if forward else (ci >= ri)
    mask = ((ri >> 5) == (ci >> 5)) & tri
    lane_k = lax.broadcasted_iota(jnp.int32, (R, A_KEY_W), 1) >> 5
    lane_v = lax.broadcasted_iota(jnp.int32, (R, A_VAL_W), 1) >> 6

    o = jnp.zeros((R, A_VAL_W), F32)
    for h in range(A_HEADS):
        qh = jnp.where(lane_k == h, qt, 0.0).astype(BF16)
        sc = lax.dot_general(qh, kt, (((1,), (1,)), ((), ())), preferred_element_type=F32)
        p = jnp.where(mask, sc, 0.0).astype(BF16)
        oh = jnp.dot(p, v16, preferred_element_type=F32)
        o = jnp.where(lane_v == h, oh, o)

    sr = lax.broadcasted_iota(jnp.int32, (A_VAL_W, A_KEY_W), 0) >> 6
    sl = lax.broadcasted_iota(jnp.int32, (A_VAL_W, A_KEY_W), 1) >> 5
    smask = sr == sl
    qt16 = qt.astype(BF16)
    inter = [None] * nch
    order = range(nch) if forward else range(nch - 1, -1, -1)
    for c in order:
        lo = c * GLA_CHUNK
        inter[c] = lax.dot_general(qt16[lo:lo + GLA_CHUNK], st.astype(BF16),
                                   (((1,), (1,)), ((), ())), preferred_element_type=F32)
        ds = lax.dot_general(v16[lo:lo + GLA_CHUNK], kend[lo:lo + GLA_CHUNK],
                             (((0,), (0,)), ((), ())), preferred_element_type=F32)
        st = st * dec3[c] + jnp.where(smask, ds, 0.0)
    return o + jnp.concatenate(inter, axis=0), st


def _gla_body(gla_ref, lr_ref, wg_ref, gb_ref, s0_ref, gn_ref, gm_ref, o_ref, sfin_ref, la_scr, *, seq):
    ntile = seq // GLA_TILE
    logit = _bdot(lr_ref[...], wg_ref[...]) + gb_ref[...]
    la_scr[...] = _log_sigmoid(logit) * (1.0 / GATE_TEMP)

    def run(forward):
        d = 0 if forward else 1

        def step(i, st):
            t = i if forward else ntile - 1 - i
            r0 = pl.multiple_of(t * GLA_TILE, GLA_TILE)
            rows = pl.ds(r0, GLA_TILE)
            q = gla_ref[rows, 0:128]
            k = gla_ref[rows, 128:256]
            v16 = gla_ref[rows, 256:512].astype(BF16)
            la = la_scr[rows, d * A_KEY_W:(d + 1) * A_KEY_W]
            o, st = _gla_tile(q, k, v16, la, st, forward)
            if forward:
                o_ref[rows, :] = o
            else:
                o_ref[rows, :] += o
            return st

        sfin_ref[0, d] = lax.fori_loop(0, ntile, step, s0_ref[0, d])

    run(True)
    run(False)
    o = o_ref[...]
    ms = _dot2(o * o, gm_ref[...])
    g = gla_ref[:, 512:768]
    o_ref[...] = o * lax.rsqrt(ms + EPS) * gn_ref[...] * (g * _sigmoid(g))


def _gla(proj, wg, gb, s0t, gnorm, gmat, *, nseq, seq, row_block0):
    in_specs = [
        pl.BlockSpec((seq, 768), lambda b: (row_block0 + b, 0)),
        pl.BlockSpec((seq, LANES), lambda b: (row_block0 + b, COL_LR // LANES)),
        pl.BlockSpec((LANES, 2 * A_KEY_W), lambda b: (0, 0)),
        pl.BlockSpec((1, 2 * A_KEY_W), lambda b: (0, 0)),
        pl.BlockSpec((1, 2, A_VAL_W, A_KEY_W), lambda b: (b, 0, 0, 0)),
        pl.BlockSpec((1, A_VAL_W), lambda b: (0, 0)),
        pl.BlockSpec((A_VAL_W, A_VAL_W), lambda b: (0, 0)),
    ]
    return pl.pallas_call(
        functools.partial(_gla_body, seq=seq),
        grid=(nseq,),
        in_specs=in_specs,
        out_specs=[
            pl.BlockSpec((seq, A_VAL_W), lambda b: (b, 0)),
            pl.BlockSpec((1, 2, A_VAL_W, A_KEY_W), lambda b: (b, 0, 0, 0)),
        ],
        out_shape=[
            jax.ShapeDtypeStruct((nseq * seq, A_VAL_W), F32),
            jax.ShapeDtypeStruct((nseq, 2, A_VAL_W, A_KEY_W), F32),
        ],
        scratch_shapes=[pltpu.VMEM((seq, 2 * A_KEY_W), F32)],
        compiler_params=_params(("parallel",)),
        name=f"gla{seq}",
    )(proj, proj, wg, gb, s0t, gnorm, gmat)


def _fnet_body(x_ref, cb_ref, sb_ref, ct_ref, st_ref, o_ref):
    x = x_ref[...].astype(BF16)
    xc = jnp.dot(x, cb_ref[...], preferred_element_type=F32).astype(BF16)
    xs = jnp.dot(x, sb_ref[...], preferred_element_type=F32).astype(BF16)
    o_ref[...] = (jnp.dot(ct_ref[...], xc, preferred_element_type=F32)
                  - jnp.dot(st_ref[...], xs, preferred_element_type=F32))


def _fnet(proj, cb, sb, ct, st, *, nseq, seq, row_block0):
    in_specs = [
        pl.BlockSpec((seq, F_WIDTH), lambda b: (row_block0 + b, COL_PF // F_WIDTH)),
        pl.BlockSpec((F_WIDTH, F_WIDTH), lambda b: (0, 0)),
        pl.BlockSpec((F_WIDTH, F_WIDTH), lambda b: (0, 0)),
        pl.BlockSpec((seq, seq), lambda b: (0, 0)),
        pl.BlockSpec((seq, seq), lambda b: (0, 0)),
    ]
    return pl.pallas_call(
        _fnet_body,
        grid=(nseq,),
        in_specs=in_specs,
        out_specs=pl.BlockSpec((seq, F_WIDTH), lambda b: (b, 0)),
        out_shape=jax.ShapeDtypeStruct((nseq * seq, F_WIDTH), F32),
        compiler_params=_params(("parallel",)),
        name=f"fnet{seq}",
    )(proj, cb, sb, ct, st)


def _dft_mats(n, scale):
    jk = np.outer(np.arange(n), np.arange(n)) % n
    ang = jk.astype(np.float64) * (2.0 * np.pi / n)
    return np.cos(ang) * scale, np.sin(ang) * scale


def _head_rms(x, gm16, gain):
    ms = _dot2(x * x, gm16)
    return x * lax.rsqrt(ms + EPS) * gain


def _rope(x, cos, sin_signed):
    w = x.shape[1]
    lane = lax.broadcasted_iota(jnp.int32, x.shape, 1)
    partner = jnp.where((lane & 31) < 16, pltpu.roll(x, w - 16, 1), pltpu.roll(x, 16, 1))
    return x * cos + partner * sin_signed


def _attn_body(*refs, rope, n_new, n_past):
    if n_past:
        (q_ref, k_ref, v_ref, ck_ref, cv_ref, cq_ref, sq_ref, cosk_ref, sink_ref,
         qn_ref, kn_ref, gm_ref, o_ref, ka_ref, k_scr, ks_scr, v_scr, vs_scr) = refs
    else:
        (q_ref, k_ref, v_ref, qn_ref, kn_ref, gm_ref, o_ref, ka_ref,
         k_scr, ks_scr, v_scr, vs_scr) = refs
    gm = gm_ref[...]

    @pl.when(pl.program_id(1) == 0)
    def _():
        kn = _head_rms(k_ref[...], gm, kn_ref[...])
        ka_ref[...] = kn
        if rope:
            kn = _rope(kn, cosk_ref[...], sink_ref[...])
        v = v_ref[...]
        new = pl.ds(0, n_new)
        k_scr[new, :] = kn.astype(BF16)
        ks_scr[new, :] = pltpu.roll(kn, HEAD_DIM, 1).astype(BF16)
        v_scr[new, :] = v.astype(BF16)
        vs_scr[new, :] = pltpu.roll(v, HEAD_DIM, 1).astype(BF16)
        if n_past:
            past = pl.ds(n_new, n_past)
            ck = ck_ref[0]
            cv = cv_ref[0]
            k_scr[past, :] = ck.astype(BF16)
            ks_scr[past, :] = pltpu.roll(ck, HEAD_DIM, 1).astype(BF16)
            v_scr[past, :] = cv.astype(BF16)
            vs_scr[past, :] = pltpu.roll(cv, HEAD_DIM, 1).astype(BF16)

    lane = lax.broadcasted_iota(jnp.int32, (Q_TILE, LANES), 1)
    low = lane < HEAD_DIM
    for jp in range(C_Q_W // LANES):
        kv = jp // 2
        q2 = _head_rms(q_ref[:, jp * LANES:(jp + 1) * LANES], gm, qn_ref[...])
        if rope:
            q2 = _rope(q2, cq_ref[...], sq_ref[...])
        q2 = q2 * (HEAD_DIM ** -0.5)
        halves = []
        for half in range(2):
            swapped = (half == 1) != (kv == 1)
            kk = ks_scr[...] if swapped else k_scr[...]
            vv = vs_scr[...] if swapped else v_scr[...]
            qm = jnp.where(low if half == 0 else ~low, q2, 0.0).astype(BF16)
            s = lax.dot_general(qm, kk, (((1,), (1,)), ((), ())), preferred_element_type=F32)
            p = jnp.exp(s - jnp.max(s, axis=-1, keepdims=True))
            den = jnp.sum(p, axis=-1, keepdims=True)
            halves.append(jnp.dot(p.astype(BF16), vv, preferred_element_type=F32) / den)
        o_ref[:, jp * LANES:(jp + 1) * LANES] = jnp.where(low, halves[0], halves[1])


def _attn(proj, qn, kn, gm, *, nseq, seq, row_block0, past=None, tables=None):
    nq = seq // Q_TILE
    n_past = 0 if past is None else past[0].shape[1]
    qrow = lambda b, j: ((row_block0 + b) * nq + j, COL_CQ // C_Q_W)
    in_specs = [
        pl.BlockSpec((Q_TILE, C_Q_W), qrow),
        pl.BlockSpec((seq, C_KV_W), lambda b, j: (row_block0 + b, COL_CK // C_KV_W)),
        pl.BlockSpec((seq, C_KV_W), lambda b, j: (row_block0 + b, COL_CV // C_KV_W)),
    ]
    args = [proj, proj, proj]
    if past is not None:
        in_specs += [pl.BlockSpec((1, n_past, C_KV_W), lambda b, j: (b, 0, 0))] * 2
        in_specs += [pl.BlockSpec((Q_TILE, LANES), lambda b, j: (j, 0))] * 2
        in_specs += [pl.BlockSpec((seq, LANES), lambda b, j: (0, 0))] * 2
        args += [past[0], past[1], tables[0], tables[1], tables[0], tables[1]]
    in_specs += [
        pl.BlockSpec((1, LANES), lambda b, j: (0, 0)),
        pl.BlockSpec((1, LANES), lambda b, j: (0, 0)),
        pl.BlockSpec((LANES, LANES), lambda b, j: (0, 0)),
    ]
    args += [qn, kn, gm]
    s_all = seq + n_past
    return pl.pallas_call(
        functools.partial(_attn_body, rope=past is not None, n_new=seq, n_past=n_past),
        grid=(nseq, nq),
        in_specs=in_specs,
        out_specs=[
            pl.BlockSpec((Q_TILE, C_Q_W), lambda b, j: (b * nq + j, 0)),
            pl.BlockSpec((seq, C_KV_W), lambda b, j: (b, 0)),
        ],
        out_shape=[
            jax.ShapeDtypeStruct((nseq * seq, C_Q_W), F32),
            jax.ShapeDtypeStruct((nseq * seq, C_KV_W), F32),
        ],
        scratch_shapes=[pltpu.VMEM((s_all, C_KV_W), BF16)] * 4,
        compiler_params=_params(("parallel", "arbitrary")),
        name=f"attn{seq}",
    )(*args)


def _rope_tables():
    t = np.arange(DEC_SEQ)
    rows = (t // GRID_W).astype(np.float32)
    cols = (t % GRID_W).astype(np.float32)
    nf = HEAD_DIM // 4
    inv_freq = (ROPE_THETA ** (-np.arange(nf, dtype=np.float32) / nf)).astype(np.float32)
    d = np.arange(LANES)
    pos = np.where(((d % HEAD_DIM) < HEAD_DIM // 2)[None, :], rows[:, None], cols[:, None])
    ang = (pos * inv_freq[d % nf][None, :]).astype(np.float32)
    sign = np.where((d % 32) < 16, -1.0, 1.0)[None, :]
    return jnp.asarray(np.cos(ang), F32), jnp.asarray(np.sin(ang) * sign, F32)


def _outproj_body(oa_c, of_c, oc_c, oa_d, of_d, oc_d, w_ref, *refs):
    *x_refs, mod_ref, g_ref, rw_ref, rb_ref, x1_ref, h2_ref, meta_ref, slot_ref, info_ref = refs
    mod = mod_ref[0]
    is_ctx = pl.program_id(0) < N_CTX // TM
    pick = lambda c_ref, d_ref: jnp.where(is_ctx, c_ref[...], d_ref[...]).astype(BF16)
    x_in = x_refs[0][...] if len(x_refs) == 1 else jnp.where(is_ctx, x_refs[0][...], x_refs[1][...])
    mixed = (jnp.dot(pick(oa_c, oa_d), w_ref[0:256, :], preferred_element_type=F32)
             + jnp.dot(pick(of_c, of_d), w_ref[256:512, :], preferred_element_type=F32)
             + jnp.dot(pick(oc_c, oc_d), w_ref[512:1024, :], preferred_element_type=F32))
    x1 = x_in + mod[2:3] * mixed
    x1_ref[...] = x1
    h2 = _rms(x1) * g_ref[...] * (1.0 + mod[4:5]) + mod[3:4]
    h2_ref[...] = h2.astype(BF16)
    lane = lax.broadcasted_iota(jnp.int32, (TM, LANES), 1)
    logits = jnp.where(lane < N_EXPERTS, _dot3(h2, rw_ref[...]) + rb_ref[...], -jnp.inf)
    meta = jnp.zeros((TM, LANES), F32)
    tops = []
    for k in range(TOP_K):
        m = jnp.max(logits, axis=-1, keepdims=True)
        idx = jnp.min(jnp.where(logits == m, lane, LANES), axis=-1, keepdims=True)
        hit = lane == idx
        logits = jnp.where(hit, -jnp.inf, logits)
        meta = jnp.where(lane == k, idx.astype(F32), meta)
        tops.append(m)
    ex = [jnp.exp(t - tops[0]) for t in tops]
    den = ex[0] + ex[1] + ex[2] + ex[3]
    for k in range(TOP_K):
        meta = jnp.where(lane == TOP_K + k, ex[k] / den, meta)
    meta_ref[...] = meta
    for t in range(TM // ROW_TILE):
        rows = slice(t * ROW_TILE, (t + 1) * ROW_TILE)
        slot_ref[rows, :], info_ref[t] = _tile_slots(meta[rows])


def _outproj(mix_ctx, mix_dec, w, xs, mod3, gain, rw, rb):
    n_ctx_tiles = N_CTX // TM
    tok = lambda w_: pl.BlockSpec((TM, w_), lambda i: (i, 0))
    ctx = lambda w_: pl.BlockSpec((TM, w_), lambda i: (jnp.minimum(i, n_ctx_tiles - 1), 0))
    dec = lambda w_: pl.BlockSpec((TM, w_), lambda i: (jnp.maximum(i - n_ctx_tiles, 0), 0))
    full = lambda a, b: pl.BlockSpec((a, b), lambda i: (0, 0))
    x_specs = [ctx(D_MODEL), dec(D_MODEL)] if len(xs) == 2 else [tok(D_MODEL)]
    tiles_per_step = TM // ROW_TILE
    return pl.pallas_call(
        _outproj_body,
        grid=(N_TOK // TM,),
        in_specs=[ctx(A_VAL_W), ctx(F_WIDTH), ctx(C_Q_W), dec(A_VAL_W), dec(F_WIDTH), dec(C_Q_W),
                  full(D_MODEL, D_MODEL)] + x_specs + [
                  pl.BlockSpec((1, 6, D_MODEL), lambda i: (_mod_row(i, TM), 0, 0)),
                  full(1, D_MODEL), full(D_MODEL, LANES), full(1, LANES)],
        out_specs=[tok(D_MODEL), tok(D_MODEL), tok(LANES), tok(LANES),
                   pl.BlockSpec((tiles_per_step, 8, LANES), lambda i: (i, 0, 0))],
        out_shape=[jax.ShapeDtypeStruct((N_TOK, D_MODEL), F32),
                   jax.ShapeDtypeStruct((N_TOK, D_MODEL), BF16),
                   jax.ShapeDtypeStruct((N_TOK, LANES), F32),
                   jax.ShapeDtypeStruct((N_TOK, LANES), F32),
                   jax.ShapeDtypeStruct((N_TILES, 8, LANES), F32)],
        compiler_params=_params(("parallel",)),
        name="outproj",
    )(*mix_ctx, *mix_dec, w, *xs, mod3, gain, rw, rb)


def _tile_slots(meta):
    lane = lax.broadcasted_iota(jnp.int32, (ROW_TILE, LANES), 1)
    idx = [meta[:, k:k + 1].astype(jnp.int32) for k in range(TOP_K)]
    sel = jnp.zeros((ROW_TILE, LANES), F32)
    for k in range(TOP_K):
        sel = jnp.where(lane == idx[k], 1.0, sel)
    ri = lax.broadcasted_iota(jnp.int32, (ROW_TILE, ROW_TILE), 0)
    ci = lax.broadcasted_iota(jnp.int32, (ROW_TILE, ROW_TILE), 1)
    lower = jnp.where(ci < ri, 1.0, 0.0).astype(BF16)
    rank = jnp.dot(lower, sel.astype(BF16), preferred_element_type=F32)
    cnt = jnp.sum(sel, axis=0, keepdims=True)
    n8 = jnp.floor((cnt + (RUN_ALIGN - 1)) * (1.0 / RUN_ALIGN)) * RUN_ALIGN
    er = lax.broadcasted_iota(jnp.int32, (LANES, LANES), 0)
    ec = lax.broadcasted_iota(jnp.int32, (LANES, LANES), 1)
    before = jnp.where(er < ec, 1.0, 0.0).astype(BF16)
    off = jnp.dot(jnp.broadcast_to(n8, (8, LANES)).astype(BF16), before, preferred_element_type=F32)
    tot = rank + off[0:1]
    out = jnp.zeros((ROW_TILE, LANES), F32)
    for k in range(TOP_K):
        pk = jnp.sum(jnp.where(lane == idx[k], tot, 0.0), axis=-1, keepdims=True)
        out = jnp.where(lane == k, pk, out)
    row = lax.broadcasted_iota(jnp.int32, (8, LANES), 0)
    return out, jnp.where(row == 0, n8, jnp.where(row == 1, off, 0.0))


def _piece_copies(n_rows, row_a, row_b, make_copy, action, bits):
    m = lax.shift_right_logical(n_rows, RUN_SHIFT)

    def piece(bit):
        rows = RUN_ALIGN << bit
        lo = (m & ((1 << bit) - 1)) * RUN_ALIGN

        @pl.when(((m >> bit) & 1) == 1)
        def _():
            cp = make_copy(pl.multiple_of(row_a + lo, RUN_ALIGN), pl.multiple_of(row_b + lo, RUN_ALIGN), rows)
            if action == "start":
                cp.start()
            else:
                cp.wait()

    small = min(bits, 3)
    for bit in range(small):
        piece(bit)
    if bits > small:
        @pl.when(lax.shift_right_logical(m, small) != 0)
        def _():
            for bit in range(small, bits):
                piece(bit)


def _run_copies(tile, n8_tbl, off_tbl, base_tbl, make_copy, action):
    def per_expert(e, c):
        j = tile * N_EXPERTS + e
        _piece_copies(n8_tbl[j], off_tbl[j], base_tbl[j], make_copy, action, RUN_BITS)
        return c

    lax.fori_loop(0, N_EXPERTS, per_expert, 0)


def _unpack_pairs(u):
    hi = lax.bitcast_convert_type(u & jnp.uint32(0xFFFF0000), F32)
    lo = lax.bitcast_convert_type(u << 16, F32)
    return hi, lo


def _pack_pairs(a, b):
    return lax.bitcast_convert_type(a, jnp.uint32) | (lax.bitcast_convert_type(b, jnp.uint32) >> 16)


def _slot_onehot(slot, weights=None):
    lane = lax.broadcasted_iota(jnp.int32, (ROW_TILE, SORT_ROWS), 1)
    p = jnp.zeros((ROW_TILE, SORT_ROWS), F32)
    for k in range(TOP_K):
        hit = lane == slot[:, k:k + 1].astype(jnp.int32)
        p = jnp.where(hit, 1.0 if weights is None else weights[:, k:k + 1], p)
    return p.astype(BF16)


def _dispatch_body(n8_tbl, off_tbl, base_tbl, tail_n, tail_base, nused, h_ref, slot_ref, xs_hbm,
                   sorted_scr, zero_scr, sem):
    tile = pl.program_id(0)

    @pl.when(tile == 0)
    def _():
        zero_scr[...] = jnp.zeros_like(zero_scr)

        def make_zero_copy(_, dst_row, rows):
            return pltpu.make_async_copy(zero_scr.at[pl.ds(0, rows)], xs_hbm.at[pl.ds(dst_row, rows)], sem.at[2])

        for action in ("start", "wait"):
            def per_expert(e, c):
                _piece_copies(tail_n[e], 0, tail_base[e], make_zero_copy, action, TAIL_BITS)
                return c

            def per_block(b, c):
                cp = make_zero_copy(0, pl.multiple_of(b * MOE_BM, MOE_BM), MOE_BM)
                if action == "start":
                    cp.start()
                else:
                    cp.wait()
                return c

            lax.fori_loop(0, N_EXPERTS, per_expert, 0)
            lax.fori_loop(nused[0], MOE_NB, per_block, 0)

    p = _slot_onehot(slot_ref[...])
    srt = lax.dot_general(p, h_ref[...], (((0,), (0,)), ((), ())), preferred_element_type=F32)
    cur = tile & 1
    sorted_scr[cur] = _pack_pairs(srt[:, :HALF_D], srt[:, HALF_D:])

    def copies_from(buf):
        def make_copy(src_row, dst_row, rows):
            return pltpu.make_async_copy(sorted_scr.at[buf, pl.ds(src_row, rows)],
                                         xs_hbm.at[pl.ds(dst_row, rows)], sem.at[buf])
        return make_copy

    _run_copies(tile, n8_tbl, off_tbl, base_tbl, copies_from(cur), "start")

    @pl.when(tile > 0)
    def _():
        _run_copies(tile - 1, n8_tbl, off_tbl, base_tbl, copies_from(1 - cur), "wait")

    @pl.when(tile == N_TILES - 1)
    def _():
        _run_copies(tile, n8_tbl, off_tbl, base_tbl, copies_from(cur), "wait")


def _dispatch(tables, tails, h2, slots):
    grid_spec = pltpu.PrefetchScalarGridSpec(
        num_scalar_prefetch=6,
        grid=(N_TILES,),
        in_specs=[pl.BlockSpec((ROW_TILE, D_MODEL), lambda i, *_: (i, 0)),
                  pl.BlockSpec((ROW_TILE, LANES), lambda i, *_: (i, 0))],
        out_specs=pl.BlockSpec(memory_space=pl.ANY),
        scratch_shapes=[pltpu.VMEM((2, SORT_ROWS, HALF_D), jnp.uint32),
                        pltpu.VMEM((MOE_BM, HALF_D), jnp.uint32), pltpu.SemaphoreType.DMA((3,))],
    )
    return pl.pallas_call(
        _dispatch_body,
        grid_spec=grid_spec,
        out_shape=jax.ShapeDtypeStruct((MOE_ROWS, HALF_D), jnp.uint32),
        compiler_params=_params(("arbitrary",)),
        name="dispatch",
    )(*tables, *tails, h2, slots)


def _experts_body(bexp_ref, nused_ref, first_ref, next_ref, slot_ref, x_ref, wgu_hbm, bgu_ref, wdn_hbm, bdn_ref,
                  y_ref, wgu32, wdn32, wgu16, wdn16, sem, *, layer):
    i = pl.program_id(0)

    def weight_copies(expert, s):
        return (pltpu.make_async_copy(wgu_hbm.at[layer, expert], wgu32.at[s], sem.at[0, s]),
                pltpu.make_async_copy(wdn_hbm.at[layer, expert], wdn32.at[s], sem.at[1, s]))

    @pl.when(first_ref[i] == 1)
    def _():
        s = slot_ref[i]

        @pl.when(i == 0)
        def _():
            for cp in weight_copies(bexp_ref[i], s):
                cp.start()

        for cp in weight_copies(bexp_ref[i], s):
            cp.wait()

        @pl.when(next_ref[i] >= 0)
        def _():
            for cp in weight_copies(next_ref[i], 1 - s):
                cp.start()

        wgu16[...] = wgu32[s].astype(BF16)
        wdn16[...] = wdn32[s].astype(BF16)

    @pl.when(i < nused_ref[0])
    def _():
        x = jnp.concatenate(_unpack_pairs(x_ref[...]), axis=1).astype(BF16)
        gu = jnp.dot(x, wgu16[...], preferred_element_type=F32) + bgu_ref[0, 0]
        gate = jnp.minimum(gu[:, :D_FF], SWIGLU_LIMIT)
        up = jnp.clip(gu[:, D_FF:], -SWIGLU_LIMIT, SWIGLU_LIMIT)
        glu = gate * _sigmoid(gate * SWIGLU_ALPHA)
        act = ((up + 1.0) * glu).astype(BF16)
        y = jnp.dot(act, wdn16[...], preferred_element_type=F32) + bdn_ref[0, 0]
        y = y.astype(BF16).astype(F32)
        y_ref[...] = _pack_pairs(y[:, :HALF_D], y[:, HALF_D:])

    @pl.when(i >= nused_ref[0])
    def _():
        y_ref[...] = jnp.zeros_like(y_ref)


def _experts(sched, xs, wgu, bgu, wdn, bdn, layer):
    e = lambda i, be, *_: (layer, be[i], 0, 0)
    grid_spec = pltpu.PrefetchScalarGridSpec(
        num_scalar_prefetch=5,
        grid=(MOE_NB,),
        in_specs=[
            pl.BlockSpec((MOE_BM, HALF_D), lambda i, be, nu, *_: (jnp.minimum(i, nu[0] - 1), 0)),
            pl.BlockSpec(memory_space=pl.ANY),
            pl.BlockSpec((1, 1, 1, 2 * D_FF), e),
            pl.BlockSpec(memory_space=pl.ANY),
            pl.BlockSpec((1, 1, 1, D_MODEL), e),
        ],
        out_specs=pl.BlockSpec((MOE_BM, HALF_D), lambda i, *_: (i, 0)),
        scratch_shapes=[pltpu.VMEM((2, D_MODEL, 2 * D_FF), F32), pltpu.VMEM((2, D_FF, D_MODEL), F32),
                        pltpu.VMEM((D_MODEL, 2 * D_FF), BF16), pltpu.VMEM((D_FF, D_MODEL), BF16),
                        pltpu.SemaphoreType.DMA((2, 2))],
    )
    return pl.pallas_call(
        functools.partial(_experts_body, layer=layer),
        grid_spec=grid_spec,
        out_shape=jax.ShapeDtypeStruct((MOE_ROWS, HALF_D), jnp.uint32),
        compiler_params=_params(("arbitrary",)),
        name="experts",
    )(*sched, xs, wgu, bgu, wdn, bdn)


def _combine_body(n8_tbl, off_tbl, base_tbl, y_hbm, slot_ref, meta_ref, x1_ref, mod_ref, nf_ref, o_ref,
                  ybuf, sem, *, final, tile0, ntiles):
    step = pl.program_id(0)
    tile = tile0 + step
    cur = step & 1

    def fetch(t, buf):
        ybuf[buf] = jnp.zeros((SORT_ROWS, HALF_D), jnp.uint32)
        _run_copies(t, n8_tbl, off_tbl, base_tbl, copies_into(buf), "start")

    def copies_into(buf):
        def make_copy(buf_row, hbm_row, rows):
            return pltpu.make_async_copy(y_hbm.at[pl.ds(hbm_row, rows)],
                                         ybuf.at[buf, pl.ds(buf_row, rows)], sem.at[buf])
        return make_copy

    @pl.when(step == 0)
    def _():
        fetch(tile, cur)

    @pl.when(step + 1 < ntiles)
    def _():
        fetch(tile + 1, 1 - cur)

    _run_copies(tile, n8_tbl, off_tbl, base_tbl, copies_into(cur), "wait")
    pw = _slot_onehot(slot_ref[...], meta_ref[...][:, TOP_K:2 * TOP_K])
    hi, lo = _unpack_pairs(ybuf[cur])
    ff = jnp.concatenate([jnp.dot(pw, hi.astype(BF16), preferred_element_type=F32),
                          jnp.dot(pw, lo.astype(BF16), preferred_element_type=F32)], axis=1)
    x2 = x1_ref[...] + mod_ref[0][5:6] * ff
    if final:
        x2 = _rms(x2) * nf_ref[...]
    o_ref[...] = x2


def _combine(tables, y, slots, meta, x1, mod3, nfinal, *, final, tile0=0, ntiles=N_TILES):
    grid_spec = pltpu.PrefetchScalarGridSpec(
        num_scalar_prefetch=3,
        grid=(ntiles,),
        in_specs=[pl.BlockSpec(memory_space=pl.ANY),
                  pl.BlockSpec((ROW_TILE, LANES), lambda i, *_: (tile0 + i, 0)),
                  pl.BlockSpec((ROW_TILE, LANES), lambda i, *_: (tile0 + i, 0)),
                  pl.BlockSpec((ROW_TILE, D_MODEL), lambda i, *_: (tile0 + i, 0)),
                  pl.BlockSpec((1, 6, D_MODEL), lambda i, *_: (_mod_row(tile0 + i, ROW_TILE), 0, 0)),
                  pl.BlockSpec((1, D_MODEL), lambda i, *_: (0, 0))],
        out_specs=pl.BlockSpec((ROW_TILE, D_MODEL), lambda i, *_: (i, 0)),
        scratch_shapes=[pltpu.VMEM((2, SORT_ROWS, HALF_D), jnp.uint32), pltpu.SemaphoreType.DMA((2,))],
    )
    return pl.pallas_call(
        functools.partial(_combine_body, final=final, tile0=tile0, ntiles=ntiles),
        grid_spec=grid_spec,
        out_shape=jax.ShapeDtypeStruct((ntiles * ROW_TILE, D_MODEL), F32),
        compiler_params=_params(("arbitrary",)),
        name="combine",
    )(*tables, y, slots, meta, x1, mod3, nfinal)


def _state_to_blockdiag(s):
    eye = jnp.eye(A_HEADS, dtype=s.dtype)
    t = jnp.einsum("bdhkv,hg->bdhvgk", s, eye)
    return t.reshape(s.shape[0], 2, A_VAL_W, A_KEY_W)


def _blockdiag_to_state(t):
    b = t.shape[0]
    t = t.reshape(b, 2, A_HEADS, A_DV, A_HEADS, A_DK)
    s = jnp.stack([t[:, :, h, :, h, :] for h in range(A_HEADS)], axis=2)
    return jnp.swapaxes(s, -1, -2)


def kernel(x_prompt, x_sample, state_gla, cache_k, cache_v, c, c_ctx, w_ada, b_ada, norm_mix, norm_ffn,
           norm_final, w_in, w_out, gla_gate_w, gla_gate_b, gla_norm, q_norm, k_norm, router_w, router_b,
           expert_w_gu, expert_b_gu, expert_w_dn, expert_b_dn):
    xs = (x_prompt.reshape(N_CTX, D_MODEL), x_sample.reshape(N_DEC, D_MODEL))
    cond = jnp.zeros((MOD_ROWS, D_MODEL), F32).at[0].set(c_ctx).at[1:1 + DEC_BATCH].set(c)
    mod = _ada(cond, w_ada, b_ada)

    gm256 = jnp.asarray(np.kron(np.eye(A_HEADS), np.ones((A_DV, A_DV))) / A_DV, BF16)
    gm128 = jnp.asarray(np.kron(np.eye(2), np.ones((HEAD_DIM, HEAD_DIM))) / HEAD_DIM, BF16)
    c64, s64 = _dft_mats(F_GROUP_W, F_GROUP_W ** -0.5)
    cb = jnp.asarray(np.kron(np.eye(F_GROUPS), c64), BF16)
    sb = jnp.asarray(np.kron(np.eye(F_GROUPS), s64), BF16)
    dft_ctx = [jnp.asarray(m, BF16) for m in _dft_mats(SEQ, SEQ ** -0.5)]
    dft_dec = [jnp.asarray(m, BF16) for m in _dft_mats(DEC_SEQ, DEC_SEQ ** -0.5)]
    rope_tabs = _rope_tables()
    zero_state = jnp.zeros((BATCH, 2, A_VAL_W, A_KEY_W), F32)

    new_s, new_k, new_v = [], [], []
    for l in range(DEPTH):
        mod3 = mod[l].reshape(MOD_ROWS, 6, D_MODEL)
        w_l = w_in[l]
        w_r = jnp.concatenate([w_l[:, :768], w_l[:, 800:], w_l[:, 768:800],
                               jnp.zeros((D_MODEL, PROJ_W - 1824), F32)], axis=1).astype(BF16)
        proj = _inproj(xs, mod3, norm_mix[l][None, :], w_r)

        wg = jnp.zeros((LANES, 2 * A_KEY_W), F32)
        wg = wg.at[0:GATE_RANK, 0:A_KEY_W].set(gla_gate_w[l, 0])
        wg = wg.at[GATE_RANK:2 * GATE_RANK, A_KEY_W:].set(gla_gate_w[l, 1])
        gb = gla_gate_b[l].reshape(1, 2 * A_KEY_W)
        gn = jnp.tile(gla_norm[l], A_HEADS)[None, :]
        s0_dec = _state_to_blockdiag(state_gla[:, l])
        oa_ctx, sfin = _gla(proj, wg, gb, zero_state, gn, gm256, nseq=BATCH, seq=SEQ, row_block0=0)
        oa_dec, _ = _gla(proj, wg, gb, s0_dec, gn, gm256, nseq=DEC_BATCH, seq=DEC_SEQ,
                         row_block0=N_CTX // DEC_SEQ)
        of_ctx = _fnet(proj, cb, sb, *dft_ctx, nseq=BATCH, seq=SEQ, row_block0=0)
        of_dec = _fnet(proj, cb, sb, *dft_dec, nseq=DEC_BATCH, seq=DEC_SEQ, row_block0=N_CTX // DEC_SEQ)
        qn = jnp.tile(q_norm[l], 2)[None, :]
        kn = jnp.tile(k_norm[l], 2)[None, :]
        oc_ctx, ka = _attn(proj, qn, kn, gm128, nseq=BATCH, seq=SEQ, row_block0=0)
        past = (cache_k[:, l].reshape(DEC_BATCH, PAST_LEN, C_KV_W),
                cache_v[:, l].reshape(DEC_BATCH, PAST_LEN, C_KV_W))
        oc_dec, _ = _attn(proj, qn, kn, gm128, nseq=DEC_BATCH, seq=DEC_SEQ,
                          row_block0=N_CTX // DEC_SEQ, past=past, tables=rope_tabs)
        new_s.append(_blockdiag_to_state(sfin))
        new_k.append(ka.reshape(BATCH, SEQ, C_KV_HEADS, HEAD_DIM))
        new_v.append(proj[:N_CTX, COL_CV:COL_CV + C_KV_W].reshape(BATCH, SEQ, C_KV_HEADS, HEAD_DIM))

        rw = jnp.zeros((D_MODEL, LANES), F32).at[:, :N_EXPERTS].set(router_w[l])
        rb = jnp.zeros((1, LANES), F32).at[0, :N_EXPERTS].set(router_b[l])
        x1, h2, meta, slots, info = _outproj((oa_ctx, of_ctx, oc_ctx), (oa_dec, of_dec, oc_dec),
                                             w_out[l].astype(BF16), xs, mod3, norm_ffn[l][None, :], rw, rb)

        n8 = info[:, 0, :N_EXPERTS].astype(jnp.int32)
        off = info[:, 1, :N_EXPERTS].astype(jnp.int32)
        cnt = jnp.sum(n8, axis=0)
        pcnt = (cnt + MOE_BM - 1) // MOE_BM * MOE_BM
        pend = jnp.cumsum(pcnt)
        base = (pend - pcnt)[None, :] + jnp.cumsum(n8, axis=0) - n8
        tables = (n8.reshape(-1), off.reshape(-1), base.reshape(-1))
        blk = jnp.arange(MOE_NB, dtype=jnp.int32)
        blk_exp = jnp.minimum(jnp.sum((pend[None, :] <= (blk * MOE_BM)[:, None]).astype(jnp.int32), axis=1),
                              N_EXPERTS - 1)
        nused = (pend[-1:] // MOE_BM).astype(jnp.int32)
        tails = (pcnt - cnt, pend - (pcnt - cnt), nused)
        prev_exp = jnp.concatenate([jnp.full((1,), -1, jnp.int32), blk_exp[:-1]])
        first = ((blk < nused[0]) & (blk_exp != prev_exp)).astype(jnp.int32)
        end_blk = (pend // MOE_BM)[blk_exp]
        nxt = jnp.where(end_blk < nused[0], blk_exp[jnp.minimum(end_blk, MOE_NB - 1)], -1)
        wslot = (jnp.cumsum(first) - 1) & 1
        sched = (blk_exp, nused, first, nxt, wslot)

        xs = _dispatch(tables, tails, h2, slots)
        y = _experts(sched, xs, expert_w_gu,
                     expert_b_gu.reshape(DEPTH, N_EXPERTS, 1, 2 * D_FF), expert_w_dn,
                     expert_b_dn.reshape(DEPTH, N_EXPERTS, 1, D_MODEL), l)
        comb = functools.partial(_combine, tables, y, slots, meta, x1, mod3, norm_final[None, :])
        if l < DEPTH - 1:
            xs = (comb(final=False),)
        else:
            ctx_tiles = N_CTX // ROW_TILE
            xs = (comb(final=True, tile0=0, ntiles=ctx_tiles),
                  comb(final=True, tile0=ctx_tiles, ntiles=N_TILES - ctx_tiles))

    y_prompt = xs[0].reshape(BATCH, SEQ, D_MODEL)
    y_sample = xs[1].reshape(DEC_BATCH, DEC_SEQ, D_MODEL)
    return (y_prompt, y_sample, jnp.stack(new_s, axis=1), jnp.stack(new_k, axis=1), jnp.stack(new_v, axis=1))
```

```python
import functools

import jax
import jax.numpy as jnp
import numpy as np
from jax import lax
from jax.experimental import pallas as pl
from jax.experimental.pallas import tpu as pltpu

D_MODEL = 1024
BATCH = 32
SEQ = 256
DEPTH = 2
DEC_BATCH = 8
DEC_SEQ = 1024
PAST_LEN = 512
GRID_W = 64
A_HEADS = 4
A_DV = 64
A_DK = 32
A_KEY_W = 128
A_VAL_W = 256
GATE_RANK = 16
GATE_TEMP = 16.0
GLA_CHUNK = 32
F_GROUPS = 4
F_GROUP_W = 64
F_WIDTH = 256
HEAD_DIM = 64
C_HEADS = 8
C_KV_HEADS = 2
C_Q_W = 512
C_KV_W = 128
ROPE_THETA = 10000.0
N_EXPERTS = 32
TOP_K = 4
D_FF = 1024
SWIGLU_LIMIT = 7.0
SWIGLU_ALPHA = 1.702
EPS = 1e-6

N_CTX = BATCH * SEQ
N_DEC = DEC_BATCH * DEC_SEQ
N_TOK = N_CTX + N_DEC
MOD_ROWS = 16

COL_GLA = 0
COL_PF = 768
COL_CQ = 1024
COL_CK = 1536
COL_CV = 1664
COL_LR = 1792
PROJ_W = 1920

LANES = 128
TM = 512
GLA_TILE = 256
Q_TILE = 256
MOE_BM = 512
MOE_SUB = MOE_BM // 2
ROW_TILE = 256
N_TILES = N_TOK // ROW_TILE
RUN_ALIGN = 8
RUN_SHIFT = 3
RUN_BITS = 6
TAIL_BITS = 6
SORT_ROWS = ROW_TILE * TOP_K + N_EXPERTS * RUN_ALIGN
HALF_D = D_MODEL // 2
MOE_NB = (N_TOK * TOP_K + N_TILES * N_EXPERTS * (RUN_ALIGN - 1)) // MOE_BM + N_EXPERTS
MOE_ROWS = MOE_NB * MOE_BM
VMEM_LIMIT = 56 * 1024 * 1024

F32 = jnp.float32
BF16 = jnp.bfloat16


def _params(sem, vmem=VMEM_LIMIT):
    return pltpu.CompilerParams(dimension_semantics=sem, vmem_limit_bytes=vmem)


def _bdot(a, b):
    return jnp.dot(a.astype(BF16), b.astype(BF16), preferred_element_type=F32)


def _split(a):
    hi = a.astype(BF16)
    lo = (a - hi.astype(F32)).astype(BF16)
    return hi, lo


def _dot3(a, b):
    ah, al = _split(a)
    bh, bl = _split(b)
    d = functools.partial(jnp.dot, preferred_element_type=F32)
    return d(ah, bh) + (d(ah, bl) + d(al, bh))


def _dot2(a, b16):
    ah, al = _split(a)
    d = functools.partial(jnp.dot, preferred_element_type=F32)
    return d(ah, b16) + d(al, b16)


def _sigmoid(x):
    return 1.0 / (1.0 + jnp.exp(-x))


def _mod_row(i, tm):
    tok = i * tm
    return jnp.where(tok < N_CTX, 0, 1 + (tok - N_CTX) // DEC_SEQ)


def _ada_body(c_ref, w_ref, b_ref, o_ref):
    c = c_ref[...]
    o_ref[0] = _dot3(c * _sigmoid(c), w_ref[0]) + b_ref[0]


def _ada(cond, w_ada, b_ada):
    return pl.pallas_call(
        _ada_body,
        grid=(DEPTH, 6),
        in_specs=[
            pl.BlockSpec((MOD_ROWS, D_MODEL), lambda l, j: (0, 0)),
            pl.BlockSpec((1, D_MODEL, D_MODEL), lambda l, j: (l, 0, j)),
            pl.BlockSpec((1, 1, D_MODEL), lambda l, j: (l, 0, j)),
        ],
        out_specs=pl.BlockSpec((1, MOD_ROWS, D_MODEL), lambda l, j: (l, 0, j)),
        out_shape=jax.ShapeDtypeStruct((DEPTH, MOD_ROWS, 6 * D_MODEL), F32),
        compiler_params=_params(("parallel", "parallel")),
        name="ada",
    )(cond, w_ada, b_ada.reshape(DEPTH, 1, 6 * D_MODEL))


def _rms(x):
    return x * lax.rsqrt(jnp.mean(x * x, axis=-1, keepdims=True) + EPS)


def _inproj_body(*refs):
    *x_refs, mod_ref, g_ref, w_ref, o_ref = refs
    if len(x_refs) == 2:
        x = jnp.where(pl.program_id(0) < N_CTX // TM, x_refs[0][...], x_refs[1][...])
    else:
        x = x_refs[0][...]
    mod = mod_ref[0]
    h = _rms(x) * g_ref[...] * (1.0 + mod[1:2]) + mod[0:1]
    o_ref[...] = jnp.dot(h.astype(BF16), w_ref[...], preferred_element_type=F32)


def _inproj(xs, mod3, gain, w):
    n_ctx_tiles = N_CTX // TM
    if len(xs) == 2:
        x_specs = [pl.BlockSpec((TM, D_MODEL), lambda i: (jnp.minimum(i, n_ctx_tiles - 1), 0)),
                   pl.BlockSpec((TM, D_MODEL), lambda i: (jnp.maximum(i - n_ctx_tiles, 0), 0))]
    else:
        x_specs = [pl.BlockSpec((TM, D_MODEL), lambda i: (i, 0))]
    return pl.pallas_call(
        _inproj_body,
        grid=(N_TOK // TM,),
        in_specs=x_specs + [
            pl.BlockSpec((1, 6, D_MODEL), lambda i: (_mod_row(i, TM), 0, 0)),
            pl.BlockSpec((1, D_MODEL), lambda i: (0, 0)),
            pl.BlockSpec((D_MODEL, PROJ_W), lambda i: (0, 0)),
        ],
        out_specs=pl.BlockSpec((TM, PROJ_W), lambda i: (i, 0)),
        out_shape=jax.ShapeDtypeStruct((N_TOK, PROJ_W), F32),
        compiler_params=_params(("parallel",)),
        name="inproj",
    )(*xs, mod3, gain, w)


def _log_sigmoid(x):
    return jnp.minimum(x, 0.0) - jnp.log(1.0 + jnp.exp(-jnp.abs(x)))


def _chunk_cumsum(x, pos, forward):
    rows = x.shape[0]
    for s in (1, 2, 4, 8, 16):
        if forward:
            x = x + jnp.where(pos >= s, pltpu.roll(x, s, 0), 0.0)
        else:
            x = x + jnp.where(pos < GLA_CHUNK - s, pltpu.roll(x, rows - s, 0), 0.0)
    return x


def _gla_tile(q, k, v16, la, st, forward):
    R = GLA_TILE
    nch = R // GLA_CHUNK
    pos = lax.broadcasted_iota(jnp.int32, (R, 1), 0) & (GLA_CHUNK - 1)
    b = _chunk_cumsum(la, pos, forward)
    b3 = b.reshape(nch, GLA_CHUNK, A_KEY_W)
    tot3 = b3[:, GLA_CHUNK - 1:GLA_CHUNK, :] if forward else b3[:, 0:1, :]
    totb = jnp.broadcast_to(tot3, (nch, GLA_CHUNK, A_KEY_W)).reshape(R, A_KEY_W)
    qt = q * (A_DK ** -0.5) * jnp.exp(b)
    kt = (k * jnp.exp(-b)).astype(BF16)
    kend = (k * jnp.exp(totb - b)).astype(BF16)
    dec3 = jnp.exp(tot3)

    ri = lax.broadcasted_iota(jnp.int32, (R, R), 0)
    ci = lax.broadcasted_iota(jnp.int32, (R, R), 1)
    tri = (ci <= ri) if forward else (ci >= ri)
    mask = ((ri >> 5) == (ci >> 5)) & tri
    lane_k = lax.broadcasted_iota(jnp.int32, (R, A_KEY_W), 1) >> 5
    lane_v = lax.broadcasted_iota(jnp.int32, (R, A_VAL_W), 1) >> 6

    o = jnp.zeros((R, A_VAL_W), F32)
    for h in range(A_HEADS):
        qh = jnp.where(lane_k == h, qt, 0.0).astype(BF16)
        sc = lax.dot_general(qh, kt, (((1,), (1,)), ((), ())), preferred_element_type=F32)
        p = jnp.where(mask, sc, 0.0).astype(BF16)
        oh = jnp.dot(p, v16, preferred_element_type=F32)
        o = jnp.where(lane_v == h, oh, o)

    sr = lax.broadcasted_iota(jnp.int32, (A_VAL_W, A_KEY_W), 0) >> 6
    sl = lax.broadcasted_iota(jnp.int32, (A_VAL_W, A_KEY_W), 1) >> 5
    smask = sr == sl
    qt16 = qt.astype(BF16)
    inter = [None] * nch
    order = range(nch) if forward else range(nch - 1, -1, -1)
    for c in order:
        lo = c * GLA_CHUNK
        inter[c] = lax.dot_general(qt16[lo:lo + GLA_CHUNK], st.astype(BF16),
                                   (((1,), (1,)), ((), ())), preferred_element_type=F32)
        ds = lax.dot_general(v16[lo:lo + GLA_CHUNK], kend[lo:lo + GLA_CHUNK],
                             (((0,), (0,)), ((), ())), preferred_element_type=F32)
        st = st * dec3[c] + jnp.where(smask, ds, 0.0)
    return o + jnp.concatenate(inter, axis=0), st


def _gla_body(gla_ref, lr_ref, wg_ref, gb_ref, s0_ref, gn_ref, gm_ref, o_ref, sfin_ref, la_scr, *, seq):
    ntile = seq // GLA_TILE
    logit = _bdot(lr_ref[...], wg_ref[...]) + gb_ref[...]
    la_scr[...] = _log_sigmoid(logit) * (1.0 / GATE_TEMP)

    def run(forward):
        d = 0 if forward else 1

        def step(i, st):
            t = i if forward else ntile - 1 - i
            r0 = pl.multiple_of(t * GLA_TILE, GLA_TILE)
            rows = pl.ds(r0, GLA_TILE)
            q = gla_ref[rows, 0:128]
            k = gla_ref[rows, 128:256]
            v16 = gla_ref[rows, 256:512].astype(BF16)
            la = la_scr[rows, d * A_KEY_W:(d + 1) * A_KEY_W]
            o, st = _gla_tile(q, k, v16, la, st, forward)
            if forward:
                o_ref[rows, :] = o
            else:
                o_ref[rows, :] += o
            return st

        st = lax.fori_loop(0, ntile, step, s0_ref[0, d])
        s = st.T
        sfin_ref[0, d] = (s[:, 0:A_DV] + s[:, A_DV:2 * A_DV]) + (s[:, 2 * A_DV:3 * A_DV] + s[:, 3 * A_DV:])

    run(True)
    run(False)
    o = o_ref[...]
    ms = _dot2(o * o, gm_ref[...])
    g = gla_ref[:, 512:768]
    o_ref[...] = o * lax.rsqrt(ms + EPS) * gn_ref[...] * (g * _sigmoid(g))


def _gla(proj, wg, gb, s0t, gnorm, gmat, *, nseq, seq, row_block0):
    in_specs = [
        pl.BlockSpec((seq, 768), lambda b: (row_block0 + b, 0)),
        pl.BlockSpec((seq, LANES), lambda b: (row_block0 + b, COL_LR // LANES)),
        pl.BlockSpec((LANES, 2 * A_KEY_W), lambda b: (0, 0)),
        pl.BlockSpec((1, 2 * A_KEY_W), lambda b: (0, 0)),
        pl.BlockSpec((1, 2, A_VAL_W, A_KEY_W), lambda b: (b, 0, 0, 0)),
        pl.BlockSpec((1, A_VAL_W), lambda b: (0, 0)),
        pl.BlockSpec((A_VAL_W, A_VAL_W), lambda b: (0, 0)),
    ]
    return pl.pallas_call(
        functools.partial(_gla_body, seq=seq),
        grid=(nseq,),
        in_specs=in_specs,
        out_specs=[
            pl.BlockSpec((seq, A_VAL_W), lambda b: (b, 0)),
            pl.BlockSpec((1, 2, A_KEY_W, A_DV), lambda b: (b, 0, 0, 0)),
        ],
        out_shape=[
            jax.ShapeDtypeStruct((nseq * seq, A_VAL_W), F32),
            jax.ShapeDtypeStruct((nseq, 2, A_KEY_W, A_DV), F32),
        ],
        scratch_shapes=[pltpu.VMEM((seq, 2 * A_KEY_W), F32)],
        compiler_params=_params(("parallel",)),
        name=f"gla{seq}",
    )(proj, proj, wg, gb, s0t, gnorm, gmat)


def _fnet_body(x_ref, cb_ref, sb_ref, ct_ref, st_ref, o_ref):
    x = x_ref[...].astype(BF16)
    xc = jnp.dot(x, cb_ref[...], preferred_element_type=F32).astype(BF16)
    xs = jnp.dot(x, sb_ref[...], preferred_element_type=F32).astype(BF16)
    o_ref[...] = (jnp.dot(ct_ref[...], xc, preferred_element_type=F32)
                  - jnp.dot(st_ref[...], xs, preferred_element_type=F32))


def _fnet(proj, cb, sb, ct, st, *, nseq, seq, row_block0):
    in_specs = [
        pl.BlockSpec((seq, F_WIDTH), lambda b: (row_block0 + b, COL_PF // F_WIDTH)),
        pl.BlockSpec((F_WIDTH, F_WIDTH), lambda b: (0, 0)),
        pl.BlockSpec((F_WIDTH, F_WIDTH), lambda b: (0, 0)),
        pl.BlockSpec((seq, seq), lambda b: (0, 0)),
        pl.BlockSpec((seq, seq), lambda b: (0, 0)),
    ]
    return pl.pallas_call(
        _fnet_body,
        grid=(nseq,),
        in_specs=in_specs,
        out_specs=pl.BlockSpec((seq, F_WIDTH), lambda b: (b, 0)),
        out_shape=jax.ShapeDtypeStruct((nseq * seq, F_WIDTH), F32),
        compiler_params=_params(("parallel",)),
        name=f"fnet{seq}",
    )(proj, cb, sb, ct, st)


def _dft_mats(n, scale):
    jk = np.outer(np.arange(n), np.arange(n)) % n
    ang = jk.astype(np.float64) * (2.0 * np.pi / n)
    return np.cos(ang) * scale, np.sin(ang) * scale


def _head_rms(x, gm16, gain):
    ms = _dot2(x * x, gm16)
    return x * lax.rsqrt(ms + EPS) * gain


def _rope(x, cos, sin_signed):
    w = x.shape[1]
    lane = lax.broadcasted_iota(jnp.int32, x.shape, 1)
    partner = jnp.where((lane & 31) < 16, pltpu.roll(x, w - 16, 1), pltpu.roll(x, 16, 1))
    return x * cos + partner * sin_signed


def _attn_body(*refs, rope, n_new, n_past):
    if n_past:
        (q_ref, k_ref, v_ref, ck_ref, cv_ref, cq_ref, sq_ref, cosk_ref, sink_ref,
         qn_ref, kn_ref, gm_ref, o_ref, ka_ref, k_scr, ks_scr, v_scr, vs_scr) = refs
    else:
        (q_ref, k_ref, v_ref, qn_ref, kn_ref, gm_ref, o_ref, ka_ref,
         k_scr, ks_scr, v_scr, vs_scr) = refs
    gm = gm_ref[...]

    @pl.when(pl.program_id(1) == 0)
    def _():
        kn = _head_rms(k_ref[...], gm, kn_ref[...])
        ka_ref[...] = kn
        if rope:
            kn = _rope(kn, cosk_ref[...], sink_ref[...])
        v = v_ref[...]
        new = pl.ds(0, n_new)
        k_scr[new, :] = kn.astype(BF16)
        ks_scr[new, :] = pltpu.roll(kn, HEAD_DIM, 1).astype(BF16)
        v_scr[new, :] = v.astype(BF16)
        vs_scr[new, :] = pltpu.roll(v, HEAD_DIM, 1).astype(BF16)
        if n_past:
            past = pl.ds(n_new, n_past)
            ck = ck_ref[0]
            cv = cv_ref[0]
            k_scr[past, :] = ck.astype(BF16)
            ks_scr[past, :] = pltpu.roll(ck, HEAD_DIM, 1).astype(BF16)
            v_scr[past, :] = cv.astype(BF16)
            vs_scr[past, :] = pltpu.roll(cv, HEAD_DIM, 1).astype(BF16)

    lane = lax.broadcasted_iota(jnp.int32, (Q_TILE, LANES), 1)
    low = lane < HEAD_DIM
    for jp in range(C_Q_W // LANES):
        kv = jp // 2
        q2 = _head_rms(q_ref[:, jp * LANES:(jp + 1) * LANES], gm, qn_ref[...])
        if rope:
            q2 = _rope(q2, cq_ref[...], sq_ref[...])
        q2 = q2 * (HEAD_DIM ** -0.5)
        halves = []
        for half in range(2):
            swapped = (half == 1) != (kv == 1)
            kk = ks_scr[...] if swapped else k_scr[...]
            vv = vs_scr[...] if swapped else v_scr[...]
            qm = jnp.where(low if half == 0 else ~low, q2, 0.0).astype(BF16)
            s = lax.dot_general(qm, kk, (((1,), (1,)), ((), ())), preferred_element_type=F32)
            p = jnp.exp(s - jnp.max(s, axis=-1, keepdims=True))
            den = jnp.sum(p, axis=-1, keepdims=True)
            halves.append(jnp.dot(p.astype(BF16), vv, preferred_element_type=F32) / den)
        o_ref[:, jp * LANES:(jp + 1) * LANES] = jnp.where(low, halves[0], halves[1])


def _attn(proj, qn, kn, gm, *, nseq, seq, row_block0, past=None, tables=None, layer=0):
    nq = seq // Q_TILE
    n_past = 0 if past is None else PAST_LEN
    qrow = lambda b, j: ((row_block0 + b) * nq + j, COL_CQ // C_Q_W)
    in_specs = [
        pl.BlockSpec((Q_TILE, C_Q_W), qrow),
        pl.BlockSpec((seq, C_KV_W), lambda b, j: (row_block0 + b, COL_CK // C_KV_W)),
        pl.BlockSpec((seq, C_KV_W), lambda b, j: (row_block0 + b, COL_CV // C_KV_W)),
    ]
    args = [proj, proj, proj]
    if past is not None:
        in_specs += [pl.BlockSpec((1, n_past, C_KV_W), lambda b, j: (b, layer, 0))] * 2
        in_specs += [pl.BlockSpec((Q_TILE, LANES), lambda b, j: (j, 0))] * 2
        in_specs += [pl.BlockSpec((seq, LANES), lambda b, j: (0, 0))] * 2
        args += [past[0], past[1], tables[0], tables[1], tables[0], tables[1]]
    in_specs += [
        pl.BlockSpec((1, LANES), lambda b, j: (0, 0)),
        pl.BlockSpec((1, LANES), lambda b, j: (0, 0)),
        pl.BlockSpec((LANES, LANES), lambda b, j: (0, 0)),
    ]
    args += [qn, kn, gm]
    s_all = seq + n_past
    return pl.pallas_call(
        functools.partial(_attn_body, rope=past is not None, n_new=seq, n_past=n_past),
        grid=(nseq, nq),
        in_specs=in_specs,
        out_specs=[
            pl.BlockSpec((Q_TILE, C_Q_W), lambda b, j: (b * nq + j, 0)),
            pl.BlockSpec((seq, C_KV_W), lambda b, j: (b, 0)),
        ],
        out_shape=[
            jax.ShapeDtypeStruct((nseq * seq, C_Q_W), F32),
            jax.ShapeDtypeStruct((nseq * seq, C_KV_W), F32),
        ],
        scratch_shapes=[pltpu.VMEM((s_all, C_KV_W), BF16)] * 4,
        compiler_params=_params(("parallel", "arbitrary")),
        name=f"attn{seq}",
    )(*args)


def _rope_tables():
    t = np.arange(DEC_SEQ)
    rows = (t // GRID_W).astype(np.float32)
    cols = (t % GRID_W).astype(np.float32)
    nf = HEAD_DIM // 4
    inv_freq = (ROPE_THETA ** (-np.arange(nf, dtype=np.float32) / nf)).astype(np.float32)
    d = np.arange(LANES)
    pos = np.where(((d % HEAD_DIM) < HEAD_DIM // 2)[None, :], rows[:, None], cols[:, None])
    ang = (pos * inv_freq[d % nf][None, :]).astype(np.float32)
    sign = np.where((d % 32) < 16, -1.0, 1.0)[None, :]
    return jnp.asarray(np.cos(ang), F32), jnp.asarray(np.sin(ang) * sign, F32)


def _outproj_body(oa_c, of_c, oc_c, oa_d, of_d, oc_d, w_ref, *refs):
    *x_refs, mod_ref, g_ref, rw_ref, rb_ref, x1_ref, h2_ref, meta_ref, slot_ref, info_ref = refs
    mod = mod_ref[0]
    is_ctx = pl.program_id(0) < N_CTX // TM
    pick = lambda c_ref, d_ref: jnp.where(is_ctx, c_ref[...], d_ref[...]).astype(BF16)
    x_in = x_refs[0][...] if len(x_refs) == 1 else jnp.where(is_ctx, x_refs[0][...], x_refs[1][...])
    mixed = (jnp.dot(pick(oa_c, oa_d), w_ref[0:256, :], preferred_element_type=F32)
             + jnp.dot(pick(of_c, of_d), w_ref[256:512, :], preferred_element_type=F32)
             + jnp.dot(pick(oc_c, oc_d), w_ref[512:1024, :], preferred_element_type=F32))
    x1 = x_in + mod[2:3] * mixed
    x1_ref[...] = x1
    h2 = _rms(x1) * g_ref[...] * (1.0 + mod[4:5]) + mod[3:4]
    h2_ref[...] = h2.astype(BF16)
    lane = lax.broadcasted_iota(jnp.int32, (TM, LANES), 1)
    logits = jnp.where(lane < N_EXPERTS, _dot3(h2, rw_ref[...]) + rb_ref[...], -jnp.inf)
    meta = jnp.zeros((TM, LANES), F32)
    tops = []
    for k in range(TOP_K):
        m = jnp.max(logits, axis=-1, keepdims=True)
        idx = jnp.min(jnp.where(logits == m, lane, LANES), axis=-1, keepdims=True)
        hit = lane == idx
        logits = jnp.where(hit, -jnp.inf, logits)
        meta = jnp.where(lane == k, idx.astype(F32), meta)
        tops.append(m)
    ex = [jnp.exp(t - tops[0]) for t in tops]
    den = ex[0] + ex[1] + ex[2] + ex[3]
    for k in range(TOP_K):
        meta = jnp.where(lane == TOP_K + k, ex[k] / den, meta)
    meta_ref[...] = meta
    for t in range(TM // ROW_TILE):
        rows = slice(t * ROW_TILE, (t + 1) * ROW_TILE)
        slot_ref[rows, :], info_ref[t] = _tile_slots(meta[rows])


def _outproj(mix_ctx, mix_dec, w, xs, mod3, gain, rw, rb):
    n_ctx_tiles = N_CTX // TM
    tok = lambda w_: pl.BlockSpec((TM, w_), lambda i: (i, 0))
    ctx = lambda w_: pl.BlockSpec((TM, w_), lambda i: (jnp.minimum(i, n_ctx_tiles - 1), 0))
    dec = lambda w_: pl.BlockSpec((TM, w_), lambda i: (jnp.maximum(i - n_ctx_tiles, 0), 0))
    full = lambda a, b: pl.BlockSpec((a, b), lambda i: (0, 0))
    x_specs = [ctx(D_MODEL), dec(D_MODEL)] if len(xs) == 2 else [tok(D_MODEL)]
    tiles_per_step = TM // ROW_TILE
    return pl.pallas_call(
        _outproj_body,
        grid=(N_TOK // TM,),
        in_specs=[ctx(A_VAL_W), ctx(F_WIDTH), ctx(C_Q_W), dec(A_VAL_W), dec(F_WIDTH), dec(C_Q_W),
                  full(D_MODEL, D_MODEL)] + x_specs + [
                  pl.BlockSpec((1, 6, D_MODEL), lambda i: (_mod_row(i, TM), 0, 0)),
                  full(1, D_MODEL), full(D_MODEL, LANES), full(1, LANES)],
        out_specs=[tok(D_MODEL), tok(D_MODEL), tok(LANES), tok(LANES),
                   pl.BlockSpec((tiles_per_step, 8, LANES), lambda i: (i, 0, 0))],
        out_shape=[jax.ShapeDtypeStruct((N_TOK, D_MODEL), F32),
                   jax.ShapeDtypeStruct((N_TOK, D_MODEL), BF16),
                   jax.ShapeDtypeStruct((N_TOK, LANES), F32),
                   jax.ShapeDtypeStruct((N_TOK, LANES), F32),
                   jax.ShapeDtypeStruct((N_TILES, 8, LANES), F32)],
        compiler_params=_params(("parallel",)),
        name="outproj",
    )(*mix_ctx, *mix_dec, w, *xs, mod3, gain, rw, rb)


def _tile_slots(meta):
    lane = lax.broadcasted_iota(jnp.int32, (ROW_TILE, LANES), 1)
    idx = [meta[:, k:k + 1].astype(jnp.int32) for k in range(TOP_K)]
    sel = jnp.zeros((ROW_TILE, LANES), F32)
    for k in range(TOP_K):
        sel = jnp.where(lane == idx[k], 1.0, sel)
    ri = lax.broadcasted_iota(jnp.int32, (ROW_TILE, ROW_TILE), 0)
    ci = lax.broadcasted_iota(jnp.int32, (ROW_TILE, ROW_TILE), 1)
    lower = jnp.where(ci < ri, 1.0, 0.0).astype(BF16)
    rank = jnp.dot(lower, sel.astype(BF16), preferred_element_type=F32)
    cnt = jnp.sum(sel, axis=0, keepdims=True)
    n8 = jnp.floor((cnt + (RUN_ALIGN - 1)) * (1.0 / RUN_ALIGN)) * RUN_ALIGN
    er = lax.broadcasted_iota(jnp.int32, (LANES, LANES), 0)
    ec = lax.broadcasted_iota(jnp.int32, (LANES, LANES), 1)
    before = jnp.where(er < ec, 1.0, 0.0).astype(BF16)
    off = jnp.dot(jnp.broadcast_to(n8, (8, LANES)).astype(BF16), before, preferred_element_type=F32)
    tot = rank + off[0:1]
    out = jnp.zeros((ROW_TILE, LANES), F32)
    for k in range(TOP_K):
        pk = jnp.sum(jnp.where(lane == idx[k], tot, 0.0), axis=-1, keepdims=True)
        out = jnp.where(lane == k, pk, out)
    row = lax.broadcasted_iota(jnp.int32, (8, LANES), 0)
    return out, jnp.where(row == 0, n8, jnp.where(row == 1, off, 0.0))


def _piece_copies(n_rows, row_a, row_b, make_copy, action, bits):
    m = lax.shift_right_logical(n_rows, RUN_SHIFT)

    def piece(bit):
        rows = RUN_ALIGN << bit
        lo = (m & ((1 << bit) - 1)) * RUN_ALIGN

        @pl.when(((m >> bit) & 1) == 1)
        def _():
            cp = make_copy(pl.multiple_of(row_a + lo, RUN_ALIGN), pl.multiple_of(row_b + lo, RUN_ALIGN), rows)
            if action == "start":
                cp.start()
            else:
                cp.wait()

    small = min(bits, 3)
    for bit in range(small):
        piece(bit)
    if bits > small:
        @pl.when(lax.shift_right_logical(m, small) != 0)
        def _():
            for bit in range(small, bits):
                piece(bit)


def _run_copies(tile, n8_tbl, off_tbl, base_tbl, make_copy, action):
    def per_expert(e, c):
        j = tile * N_EXPERTS + e
        _piece_copies(n8_tbl[j], off_tbl[j], base_tbl[j], make_copy, action, RUN_BITS)
        return c

    lax.fori_loop(0, N_EXPERTS, per_expert, 0)


def _unpack_pairs(u):
    hi = lax.bitcast_convert_type(u & jnp.uint32(0xFFFF0000), F32)
    lo = lax.bitcast_convert_type(u << 16, F32)
    return hi, lo


def _pack_pairs(a, b):
    return lax.bitcast_convert_type(a, jnp.uint32) | (lax.bitcast_convert_type(b, jnp.uint32) >> 16)


def _slot_onehot(slot, weights=None):
    lane = lax.broadcasted_iota(jnp.int32, (ROW_TILE, SORT_ROWS), 1)
    p = jnp.zeros((ROW_TILE, SORT_ROWS), F32)
    for k in range(TOP_K):
        hit = lane == slot[:, k:k + 1].astype(jnp.int32)
        p = jnp.where(hit, 1.0 if weights is None else weights[:, k:k + 1], p)
    return p.astype(BF16)


def _dispatch_body(n8_tbl, off_tbl, base_tbl, tail_n, tail_base, nused, h_ref, slot_ref, xs_hbm,
                   sorted_scr, zero_scr, sem):
    tile = pl.program_id(0)

    @pl.when(tile == 0)
    def _():
        zero_scr[...] = jnp.zeros_like(zero_scr)

        def make_zero_copy(_, dst_row, rows):
            return pltpu.make_async_copy(zero_scr.at[pl.ds(0, rows)], xs_hbm.at[pl.ds(dst_row, rows)], sem.at[2])

        for action in ("start", "wait"):
            def per_expert(e, c):
                _piece_copies(tail_n[e], 0, tail_base[e], make_zero_copy, action, TAIL_BITS)
                return c

            def per_block(b, c):
                cp = make_zero_copy(0, pl.multiple_of(b * MOE_BM, MOE_BM), MOE_BM)
                if action == "start":
                    cp.start()
                else:
                    cp.wait()
                return c

            lax.fori_loop(0, N_EXPERTS, per_expert, 0)
            lax.fori_loop(nused[0], MOE_NB, per_block, 0)

    p = _slot_onehot(slot_ref[...])
    srt = lax.dot_general(p, h_ref[...], (((0,), (0,)), ((), ())), preferred_element_type=F32)
    cur = tile & 1
    sorted_scr[cur] = _pack_pairs(srt[:, :HALF_D], srt[:, HALF_D:])

    def copies_from(buf):
        def make_copy(src_row, dst_row, rows):
            return pltpu.make_async_copy(sorted_scr.at[buf, pl.ds(src_row, rows)],
                                         xs_hbm.at[pl.ds(dst_row, rows)], sem.at[buf])
        return make_copy

    _run_copies(tile, n8_tbl, off_tbl, base_tbl, copies_from(cur), "start")

    @pl.when(tile > 0)
    def _():
        _run_copies(tile - 1, n8_tbl, off_tbl, base_tbl, copies_from(1 - cur), "wait")

    @pl.when(tile == N_TILES - 1)
    def _():
        _run_copies(tile, n8_tbl, off_tbl, base_tbl, copies_from(cur), "wait")


def _dispatch(tables, tails, h2, slots):
    grid_spec = pltpu.PrefetchScalarGridSpec(
        num_scalar_prefetch=6,
        grid=(N_TILES,),
        in_specs=[pl.BlockSpec((ROW_TILE, D_MODEL), lambda i, *_: (i, 0)),
                  pl.BlockSpec((ROW_TILE, LANES), lambda i, *_: (i, 0))],
        out_specs=pl.BlockSpec(memory_space=pl.ANY),
        scratch_shapes=[pltpu.VMEM((2, SORT_ROWS, HALF_D), jnp.uint32),
                        pltpu.VMEM((MOE_BM, HALF_D), jnp.uint32), pltpu.SemaphoreType.DMA((3,))],
    )
    return pl.pallas_call(
        _dispatch_body,
        grid_spec=grid_spec,
        out_shape=jax.ShapeDtypeStruct((MOE_ROWS, HALF_D), jnp.uint32),
        compiler_params=_params(("arbitrary",)),
        name="dispatch",
    )(*tables, *tails, h2, slots)


def _experts_body(bexp_ref, nused_ref, first_ref, next_ref, slot_ref, valid_ref, x_ref, wgu_hbm, bgu_ref, wdn_hbm,
                  bdn_ref, y_ref, wgu32, wdn32, wgu16, wdn16, sem, *, layer):
    del nused_ref
    i = pl.program_id(0)

    def weight_copies(expert, s):
        return (pltpu.make_async_copy(wgu_hbm.at[layer, expert], wgu32.at[s], sem.at[0, s]),
                pltpu.make_async_copy(wdn_hbm.at[layer, expert], wdn32.at[s], sem.at[1, s]))

    @pl.when(first_ref[i] == 1)
    def _():
        s = slot_ref[i]

        @pl.when(i == 0)
        def _():
            for cp in weight_copies(bexp_ref[i], s):
                cp.start()

        for cp in weight_copies(bexp_ref[i], s):
            cp.wait()

        @pl.when(next_ref[i] >= 0)
        def _():
            for cp in weight_copies(next_ref[i], 1 - s):
                cp.start()

        wgu16[...] = wgu32[s].astype(BF16)
        wdn16[...] = wdn32[s].astype(BF16)

    def compute(nrows):
        rows = pl.ds(0, nrows)
        x = jnp.concatenate(_unpack_pairs(x_ref[rows, :]), axis=1).astype(BF16)
        gu = jnp.dot(x, wgu16[...], preferred_element_type=F32) + bgu_ref[0, 0]
        gate = jnp.minimum(gu[:, :D_FF], SWIGLU_LIMIT)
        up = jnp.clip(gu[:, D_FF:], -SWIGLU_LIMIT, SWIGLU_LIMIT)
        glu = gate * _sigmoid(gate * SWIGLU_ALPHA)
        act = ((up + 1.0) * glu).astype(BF16)
        y = jnp.dot(act, wdn16[...], preferred_element_type=F32) + bdn_ref[0, 0]
        y = y.astype(BF16).astype(F32)
        y_ref[rows, :] = _pack_pairs(y[:, :HALF_D], y[:, HALF_D:])

    valid = valid_ref[i]

    @pl.when(valid > MOE_SUB)
    def _():
        compute(MOE_BM)

    @pl.when(valid <= MOE_SUB)
    def _():
        y_ref[pl.ds(MOE_SUB, MOE_SUB), :] = jnp.zeros((MOE_SUB, HALF_D), jnp.uint32)

    @pl.when((valid > 0) & (valid <= MOE_SUB))
    def _():
        compute(MOE_SUB)

    @pl.when(valid == 0)
    def _():
        y_ref[pl.ds(0, MOE_SUB), :] = jnp.zeros((MOE_SUB, HALF_D), jnp.uint32)


def _experts(sched, xs, wgu, bgu, wdn, bdn, layer):
    e = lambda i, be, *_: (layer, be[i], 0, 0)
    grid_spec = pltpu.PrefetchScalarGridSpec(
        num_scalar_prefetch=6,
        grid=(MOE_NB,),
        in_specs=[
            pl.BlockSpec((MOE_BM, HALF_D), lambda i, be, nu, *_: (jnp.minimum(i, nu[0] - 1), 0)),
            pl.BlockSpec(memory_space=pl.ANY),
            pl.BlockSpec((1, 1, 1, 2 * D_FF), e),
            pl.BlockSpec(memory_space=pl.ANY),
            pl.BlockSpec((1, 1, 1, D_MODEL), e),
        ],
        out_specs=pl.BlockSpec((MOE_BM, HALF_D), lambda i, *_: (i, 0)),
        scratch_shapes=[pltpu.VMEM((2, D_MODEL, 2 * D_FF), F32), pltpu.VMEM((2, D_FF, D_MODEL), F32),
                        pltpu.VMEM((D_MODEL, 2 * D_FF), BF16), pltpu.VMEM((D_FF, D_MODEL), BF16),
                        pltpu.SemaphoreType.DMA((2, 2))],
    )
    return pl.pallas_call(
        functools.partial(_experts_body, layer=layer),
        grid_spec=grid_spec,
        out_shape=jax.ShapeDtypeStruct((MOE_ROWS, HALF_D), jnp.uint32),
        compiler_params=_params(("arbitrary",)),
        name="experts",
    )(*sched, xs, wgu, bgu, wdn, bdn)


def _combine_body(n8_tbl, off_tbl, base_tbl, y_hbm, slot_ref, meta_ref, x1_ref, mod_ref, nf_ref, o_ref,
                  ybuf, sem, *, final, tile0, ntiles):
    step = pl.program_id(0)
    tile = tile0 + step
    cur = step & 1

    def fetch(t, buf):
        ybuf[buf] = jnp.zeros((SORT_ROWS, HALF_D), jnp.uint32)
        _run_copies(t, n8_tbl, off_tbl, base_tbl, copies_into(buf), "start")

    def copies_into(buf):
        def make_copy(buf_row, hbm_row, rows):
            return pltpu.make_async_copy(y_hbm.at[pl.ds(hbm_row, rows)],
                                         ybuf.at[buf, pl.ds(buf_row, rows)], sem.at[buf])
        return make_copy

    @pl.when(step == 0)
    def _():
        fetch(tile, cur)

    @pl.when(step + 1 < ntiles)
    def _():
        fetch(tile + 1, 1 - cur)

    _run_copies(tile, n8_tbl, off_tbl, base_tbl, copies_into(cur), "wait")
    pw = _slot_onehot(slot_ref[...], meta_ref[...][:, TOP_K:2 * TOP_K])
    hi, lo = _unpack_pairs(ybuf[cur])
    ff = jnp.concatenate([jnp.dot(pw, hi.astype(BF16), preferred_element_type=F32),
                          jnp.dot(pw, lo.astype(BF16), preferred_element_type=F32)], axis=1)
    x2 = x1_ref[...] + mod_ref[0][5:6] * ff
    if final:
        x2 = _rms(x2) * nf_ref[...]
    o_ref[...] = x2


def _combine(tables, y, slots, meta, x1, mod3, nfinal, *, final, tile0=0, ntiles=N_TILES):
    grid_spec = pltpu.PrefetchScalarGridSpec(
        num_scalar_prefetch=3,
        grid=(ntiles,),
        in_specs=[pl.BlockSpec(memory_space=pl.ANY),
                  pl.BlockSpec((ROW_TILE, LANES), lambda i, *_: (tile0 + i, 0)),
                  pl.BlockSpec((ROW_TILE, LANES), lambda i, *_: (tile0 + i, 0)),
                  pl.BlockSpec((ROW_TILE, D_MODEL), lambda i, *_: (tile0 + i, 0)),
                  pl.BlockSpec((1, 6, D_MODEL), lambda i, *_: (_mod_row(tile0 + i, ROW_TILE), 0, 0)),
                  pl.BlockSpec((1, D_MODEL), lambda i, *_: (0, 0))],
        out_specs=pl.BlockSpec((ROW_TILE, D_MODEL), lambda i, *_: (i, 0)),
        scratch_shapes=[pltpu.VMEM((2, SORT_ROWS, HALF_D), jnp.uint32), pltpu.SemaphoreType.DMA((2,))],
    )
    return pl.pallas_call(
        functools.partial(_combine_body, final=final, tile0=tile0, ntiles=ntiles),
        grid_spec=grid_spec,
        out_shape=jax.ShapeDtypeStruct((ntiles * ROW_TILE, D_MODEL), F32),
        compiler_params=_params(("arbitrary",)),
        name="combine",
    )(*tables, y, slots, meta, x1, mod3, nfinal)


def _state_to_blockdiag(s):
    eye = jnp.eye(A_HEADS, dtype=s.dtype)
    t = jnp.einsum("bdhkv,hg->bdhvgk", s, eye)
    return t.reshape(s.shape[0], 2, A_VAL_W, A_KEY_W)


def kernel(x_prompt, x_sample, state_gla, cache_k, cache_v, c, c_ctx, w_ada, b_ada, norm_mix, norm_ffn,
           norm_final, w_in, w_out, gla_gate_w, gla_gate_b, gla_norm, q_norm, k_norm, router_w, router_b,
           expert_w_gu, expert_b_gu, expert_w_dn, expert_b_dn):
    xs = (x_prompt.reshape(N_CTX, D_MODEL), x_sample.reshape(N_DEC, D_MODEL))
    cond = jnp.zeros((MOD_ROWS, D_MODEL), F32).at[0].set(c_ctx).at[1:1 + DEC_BATCH].set(c)
    mod = _ada(cond, w_ada, b_ada)

    gm256 = jnp.asarray(np.kron(np.eye(A_HEADS), np.ones((A_DV, A_DV))) / A_DV, BF16)
    gm128 = jnp.asarray(np.kron(np.eye(2), np.ones((HEAD_DIM, HEAD_DIM))) / HEAD_DIM, BF16)
    c64, s64 = _dft_mats(F_GROUP_W, F_GROUP_W ** -0.5)
    cb = jnp.asarray(np.kron(np.eye(F_GROUPS), c64), BF16)
    sb = jnp.asarray(np.kron(np.eye(F_GROUPS), s64), BF16)
    dft_ctx = [jnp.asarray(m, BF16) for m in _dft_mats(SEQ, SEQ ** -0.5)]
    dft_dec = [jnp.asarray(m, BF16) for m in _dft_mats(DEC_SEQ, DEC_SEQ ** -0.5)]
    rope_tabs = _rope_tables()
    zero_state = jnp.zeros((BATCH, 2, A_VAL_W, A_KEY_W), F32)

    new_s, new_k, new_v = [], [], []
    for l in range(DEPTH):
        mod3 = mod[l].reshape(MOD_ROWS, 6, D_MODEL)
        w_l = w_in[l]
        w_r = jnp.concatenate([w_l[:, :768], w_l[:, 800:], w_l[:, 768:800],
                               jnp.zeros((D_MODEL, PROJ_W - 1824), F32)], axis=1).astype(BF16)
        proj = _inproj(xs, mod3, norm_mix[l][None, :], w_r)

        wg = jnp.zeros((LANES, 2 * A_KEY_W), F32)
        wg = wg.at[0:GATE_RANK, 0:A_KEY_W].set(gla_gate_w[l, 0])
        wg = wg.at[GATE_RANK:2 * GATE_RANK, A_KEY_W:].set(gla_gate_w[l, 1])
        gb = gla_gate_b[l].reshape(1, 2 * A_KEY_W)
        gn = jnp.tile(gla_norm[l], A_HEADS)[None, :]
        s0_dec = _state_to_blockdiag(state_gla[:, l])
        oa_ctx, sfin = _gla(proj, wg, gb, zero_state, gn, gm256, nseq=BATCH, seq=SEQ, row_block0=0)
        oa_dec, _ = _gla(proj, wg, gb, s0_dec, gn, gm256, nseq=DEC_BATCH, seq=DEC_SEQ,
                         row_block0=N_CTX // DEC_SEQ)
        of_ctx = _fnet(proj, cb, sb, *dft_ctx, nseq=BATCH, seq=SEQ, row_block0=0)
        of_dec = _fnet(proj, cb, sb, *dft_dec, nseq=DEC_BATCH, seq=DEC_SEQ, row_block0=N_CTX // DEC_SEQ)
        qn = jnp.tile(q_norm[l], 2)[None, :]
        kn = jnp.tile(k_norm[l], 2)[None, :]
        oc_ctx, ka = _attn(proj, qn, kn, gm128, nseq=BATCH, seq=SEQ, row_block0=0)
        past = (cache_k.reshape(DEC_BATCH, DEPTH * PAST_LEN, C_KV_W),
                cache_v.reshape(DEC_BATCH, DEPTH * PAST_LEN, C_KV_W))
        oc_dec, _ = _attn(proj, qn, kn, gm128, nseq=DEC_BATCH, seq=DEC_SEQ,
                          row_block0=N_CTX // DEC_SEQ, past=past, tables=rope_tabs, layer=l)
        new_s.append(sfin.reshape(BATCH, 2, A_HEADS, A_DK, A_DV))
        new_k.append(ka.reshape(BATCH, SEQ, C_KV_HEADS, HEAD_DIM))
        new_v.append(proj[:N_CTX, COL_CV:COL_CV + C_KV_W].reshape(BATCH, SEQ, C_KV_HEADS, HEAD_DIM))

        rw = jnp.zeros((D_MODEL, LANES), F32).at[:, :N_EXPERTS].set(router_w[l])
        rb = jnp.zeros((1, LANES), F32).at[0, :N_EXPERTS].set(router_b[l])
        x1, h2, meta, slots, info = _outproj((oa_ctx, of_ctx, oc_ctx), (oa_dec, of_dec, oc_dec),
                                             w_out[l].astype(BF16), xs, mod3, norm_ffn[l][None, :], rw, rb)

        n8 = info[:, 0, :N_EXPERTS].astype(jnp.int32)
        off = info[:, 1, :N_EXPERTS].astype(jnp.int32)
        cnt = jnp.sum(n8, axis=0)
        pcnt = (cnt + MOE_BM - 1) // MOE_BM * MOE_BM
        pend = jnp.cumsum(pcnt)
        base = (pend - pcnt)[None, :] + jnp.cumsum(n8, axis=0) - n8
        tables = (n8.reshape(-1), off.reshape(-1), base.reshape(-1))
        blk = jnp.arange(MOE_NB, dtype=jnp.int32)
        blk_exp = jnp.minimum(jnp.sum((pend[None, :] <= (blk * MOE_BM)[:, None]).astype(jnp.int32), axis=1),
                              N_EXPERTS - 1)
        nused = (pend[-1:] // MOE_BM).astype(jnp.int32)
        tails = (pcnt - cnt, pend - (pcnt - cnt), nused)
        prev_exp = jnp.concatenate([jnp.full((1,), -1, jnp.int32), blk_exp[:-1]])
        first = ((blk < nused[0]) & (blk_exp != prev_exp)).astype(jnp.int32)
        end_blk = (pend // MOE_BM)[blk_exp]
        nxt = jnp.where(end_blk < nused[0], blk_exp[jnp.minimum(end_blk, MOE_NB - 1)], -1)
        wslot = (jnp.cumsum(first) - 1) & 1
        start_blk = (pend - pcnt) // MOE_BM
        valid = jnp.where(blk < nused[0],
                          jnp.clip(cnt[blk_exp] - (blk - start_blk[blk_exp]) * MOE_BM, 0, MOE_BM), 0)
        sched = (blk_exp, nused, first, nxt, wslot, valid)

        xs = _dispatch(tables, tails, h2, slots)
        y = _experts(sched, xs, expert_w_gu,
                     expert_b_gu.reshape(DEPTH, N_EXPERTS, 1, 2 * D_FF), expert_w_dn,
                     expert_b_dn.reshape(DEPTH, N_EXPERTS, 1, D_MODEL), l)
        comb = functools.partial(_combine, tables, y, slots, meta, x1, mod3, norm_final[None, :])
        if l < DEPTH - 1:
            xs = (comb(final=False),)
        else:
            ctx_tiles = N_CTX // ROW_TILE
            xs = (comb(final=True, tile0=0, ntiles=ctx_tiles),
                  comb(final=True, tile0=ctx_tiles, ntiles=N_TILES - ctx_tiles))

    y_prompt = xs[0].reshape(BATCH, SEQ, D_MODEL)
    y_sample = xs[1].reshape(DEC_BATCH, DEC_SEQ, D_MODEL)
    return (y_prompt, y_sample, jnp.stack(new_s, axis=1), jnp.stack(new_k, axis=1), jnp.stack(new_v, axis=1))
```

```python
import functools

import jax
import jax.numpy as jnp
import numpy as np
from jax import lax
from jax.experimental import pallas as pl
from jax.experimental.pallas import tpu as pltpu

D_MODEL = 1024
BATCH = 32
SEQ = 256
DEPTH = 2
DEC_BATCH = 8
DEC_SEQ = 1024
PAST_LEN = 512
GRID_W = 64
A_HEADS = 4
A_DV = 64
A_DK = 32
A_KEY_W = 128
A_VAL_W = 256
GATE_RANK = 16
GATE_TEMP = 16.0
GLA_CHUNK = 32
F_GROUPS = 4
F_GROUP_W = 64
F_WIDTH = 256
HEAD_DIM = 64
C_HEADS = 8
C_KV_HEADS = 2
C_Q_W = 512
C_KV_W = 128
ROPE_THETA = 10000.0
N_EXPERTS = 32
TOP_K = 4
D_FF = 1024
SWIGLU_LIMIT = 7.0
SWIGLU_ALPHA = 1.702
EPS = 1e-6

N_CTX = BATCH * SEQ
N_DEC = DEC_BATCH * DEC_SEQ
N_TOK = N_CTX + N_DEC
MOD_ROWS = 16

COL_GLA = 0
COL_PF = 768
COL_CQ = 1024
COL_CK = 1536
COL_CV = 1664
COL_LR = 1792
PROJ_W = 1920

LANES = 128
TM = 512
GLA_TILE = 256
Q_TILE = 256
MOE_BM = 512
MOE_SUB = MOE_BM // 2
ROW_TILE = 256
N_TILES = N_TOK // ROW_TILE
RUN_ALIGN = 8
RUN_SHIFT = 3
RUN_BITS = 6
TAIL_BITS = 6
SORT_ROWS = ROW_TILE * TOP_K + N_EXPERTS * RUN_ALIGN
HALF_D = D_MODEL // 2
MOE_NB = (N_TOK * TOP_K + N_TILES * N_EXPERTS * (RUN_ALIGN - 1)) // MOE_BM + N_EXPERTS
MOE_ROWS = MOE_NB * MOE_BM
VMEM_LIMIT = 56 * 1024 * 1024

F32 = jnp.float32
BF16 = jnp.bfloat16


def _params(sem, vmem=VMEM_LIMIT):
    return pltpu.CompilerParams(dimension_semantics=sem, vmem_limit_bytes=vmem)


def _bdot(a, b):
    return jnp.dot(a.astype(BF16), b.astype(BF16), preferred_element_type=F32)


def _split(a):
    hi = a.astype(BF16)
    lo = (a - hi.astype(F32)).astype(BF16)
    return hi, lo


def _dot3(a, b):
    ah, al = _split(a)
    bh, bl = _split(b)
    d = functools.partial(jnp.dot, preferred_element_type=F32)
    return d(ah, bh) + (d(ah, bl) + d(al, bh))


def _dot2(a, b16):
    ah, al = _split(a)
    d = functools.partial(jnp.dot, preferred_element_type=F32)
    return d(ah, b16) + d(al, b16)


def _sigmoid(x):
    return 1.0 / (1.0 + jnp.exp(-x))


def _mod_row(i, tm):
    tok = i * tm
    return jnp.where(tok < N_CTX, 0, 1 + (tok - N_CTX) // DEC_SEQ)


def _ada_body(c_ref, w_ref, b_ref, o_ref):
    c = c_ref[...]
    o_ref[0] = _dot3(c * _sigmoid(c), w_ref[0]) + b_ref[0]


def _ada(cond, w_ada, b_ada):
    return pl.pallas_call(
        _ada_body,
        grid=(DEPTH, 6),
        in_specs=[
            pl.BlockSpec((MOD_ROWS, D_MODEL), lambda l, j: (0, 0)),
            pl.BlockSpec((1, D_MODEL, D_MODEL), lambda l, j: (l, 0, j)),
            pl.BlockSpec((1, 1, D_MODEL), lambda l, j: (l, 0, j)),
        ],
        out_specs=pl.BlockSpec((1, MOD_ROWS, D_MODEL), lambda l, j: (l, 0, j)),
        out_shape=jax.ShapeDtypeStruct((DEPTH, MOD_ROWS, 6 * D_MODEL), F32),
        compiler_params=_params(("parallel", "parallel")),
        name="ada",
    )(cond, w_ada, b_ada.reshape(DEPTH, 1, 6 * D_MODEL))


def _rms(x):
    return x * lax.rsqrt(jnp.mean(x * x, axis=-1, keepdims=True) + EPS)


def _inproj_body(*refs):
    *x_refs, mod_ref, g_ref, w_ref, o_ref = refs
    if len(x_refs) == 2:
        x = jnp.where(pl.program_id(0) < N_CTX // TM, x_refs[0][...], x_refs[1][...])
    else:
        x = x_refs[0][...]
    mod = mod_ref[0]
    h = _rms(x) * g_ref[...] * (1.0 + mod[1:2]) + mod[0:1]
    o_ref[...] = jnp.dot(h.astype(BF16), w_ref[...], preferred_element_type=F32)


def _inproj(xs, mod3, gain, w):
    n_ctx_tiles = N_CTX // TM
    if len(xs) == 2:
        x_specs = [pl.BlockSpec((TM, D_MODEL), lambda i: (jnp.minimum(i, n_ctx_tiles - 1), 0)),
                   pl.BlockSpec((TM, D_MODEL), lambda i: (jnp.maximum(i - n_ctx_tiles, 0), 0))]
    else:
        x_specs = [pl.BlockSpec((TM, D_MODEL), lambda i: (i, 0))]
    return pl.pallas_call(
        _inproj_body,
        grid=(N_TOK // TM,),
        in_specs=x_specs + [
            pl.BlockSpec((1, 6, D_MODEL), lambda i: (_mod_row(i, TM), 0, 0)),
            pl.BlockSpec((1, D_MODEL), lambda i: (0, 0)),
            pl.BlockSpec((D_MODEL, PROJ_W), lambda i: (0, 0)),
        ],
        out_specs=pl.BlockSpec((TM, PROJ_W), lambda i: (i, 0)),
        out_shape=jax.ShapeDtypeStruct((N_TOK, PROJ_W), F32),
        compiler_params=_params(("parallel",)),
        name="inproj",
    )(*xs, mod3, gain, w)


def _log_sigmoid(x):
    return jnp.minimum(x, 0.0) - jnp.log(1.0 + jnp.exp(-jnp.abs(x)))


def _chunk_cumsum(x, pos, forward):
    rows = x.shape[0]
    for s in (1, 2, 4, 8, 16):
        if forward:
            x = x + jnp.where(pos >= s, pltpu.roll(x, s, 0), 0.0)
        else:
            x = x + jnp.where(pos < GLA_CHUNK - s, pltpu.roll(x, rows - s, 0), 0.0)
    return x


def _gla_tile(q, k, v16, la, st, forward):
    R = GLA_TILE
    nch = R // GLA_CHUNK
    pos = lax.broadcasted_iota(jnp.int32, (R, 1), 0) & (GLA_CHUNK - 1)
    b = _chunk_cumsum(la, pos, forward)
    b3 = b.reshape(nch, GLA_CHUNK, A_KEY_W)
    tot3 = b3[:, GLA_CHUNK - 1:GLA_CHUNK, :] if forward else b3[:, 0:1, :]
    totb = jnp.broadcast_to(tot3, (nch, GLA_CHUNK, A_KEY_W)).reshape(R, A_KEY_W)
    qt = q * (A_DK ** -0.5) * jnp.exp(b)
    kt = (k * jnp.exp(-b)).astype(BF16)
    kend = (k * jnp.exp(totb - b)).astype(BF16)
    dec3 = jnp.exp(tot3)

    ri = lax.broadcasted_iota(jnp.int32, (R, R), 0)
    ci = lax.broadcasted_iota(jnp.int32, (R, R), 1)
    tri = (ci <= ri) if forward else (ci >= ri)
    mask = ((ri >> 5) == (ci >> 5)) & tri
    lane_k = lax.broadcasted_iota(jnp.int32, (R, A_KEY_W), 1) >> 5
    lane_v = lax.broadcasted_iota(jnp.int32, (R, A_VAL_W), 1) >> 6

    o = jnp.zeros((R, A_VAL_W), F32)
    for h in range(A_HEADS):
        qh = jnp.where(lane_k == h, qt, 0.0).astype(BF16)
        sc = lax.dot_general(qh, kt, (((1,), (1,)), ((), ())), preferred_element_type=F32)
        p = jnp.where(mask, sc, 0.0).astype(BF16)
        oh = jnp.dot(p, v16, preferred_element_type=F32)
        o = jnp.where(lane_v == h, oh, o)

    sr = lax.broadcasted_iota(jnp.int32, (A_VAL_W, A_KEY_W), 0) >> 6
    sl = lax.broadcasted_iota(jnp.int32, (A_VAL_W, A_KEY_W), 1) >> 5
    smask = sr == sl
    qt16 = qt.astype(BF16)
    inter = [None] * nch
    order = range(nch) if forward else range(nch - 1, -1, -1)
    for c in order:
        lo = c * GLA_CHUNK
        inter[c] = lax.dot_general(qt16[lo:lo + GLA_CHUNK], st.astype(BF16),
                                   (((1,), (1,)), ((), ())), preferred_element_type=F32)
        ds = lax.dot_general(v16[lo:lo + GLA_CHUNK], kend[lo:lo + GLA_CHUNK],
                             (((0,), (0,)), ((), ())), preferred_element_type=F32)
        st = st * dec3[c] + jnp.where(smask, ds, 0.0)
    return o + jnp.concatenate(inter, axis=0), st


def _gla_body(gla_ref, lr_ref, wg_ref, gb_ref, s0_ref, gn_ref, gm_ref, o_ref, sfin_ref, la_scr, *, seq):
    ntile = seq // GLA_TILE
    logit = _bdot(lr_ref[...], wg_ref[...]) + gb_ref[...]
    la_scr[...] = _log_sigmoid(logit) * (1.0 / GATE_TEMP)

    def run(forward):
        d = 0 if forward else 1

        def step(i, st):
            t = i if forward else ntile - 1 - i
            r0 = pl.multiple_of(t * GLA_TILE, GLA_TILE)
            rows = pl.ds(r0, GLA_TILE)
            q = gla_ref[rows, 0:128]
            k = gla_ref[rows, 128:256]
            v16 = gla_ref[rows, 256:512].astype(BF16)
            la = la_scr[rows, d * A_KEY_W:(d + 1) * A_KEY_W]
            o, st = _gla_tile(q, k, v16, la, st, forward)
            if forward:
                o_ref[rows, :] = o
            else:
                o_ref[rows, :] += o
            return st

        st = lax.fori_loop(0, ntile, step, s0_ref[0, d])
        s = st.T
        sfin_ref[0, d] = (s[:, 0:A_DV] + s[:, A_DV:2 * A_DV]) + (s[:, 2 * A_DV:3 * A_DV] + s[:, 3 * A_DV:])

    run(True)
    run(False)
    o = o_ref[...]
    ms = _dot2(o * o, gm_ref[...])
    g = gla_ref[:, 512:768]
    o_ref[...] = o * lax.rsqrt(ms + EPS) * gn_ref[...] * (g * _sigmoid(g))


def _gla(proj, wg, gb, s0t, gnorm, gmat, *, nseq, seq, row_block0):
    in_specs = [
        pl.BlockSpec((seq, 768), lambda b: (row_block0 + b, 0)),
        pl.BlockSpec((seq, LANES), lambda b: (row_block0 + b, COL_LR // LANES)),
        pl.BlockSpec((LANES, 2 * A_KEY_W), lambda b: (0, 0)),
        pl.BlockSpec((1, 2 * A_KEY_W), lambda b: (0, 0)),
        pl.BlockSpec((1, 2, A_VAL_W, A_KEY_W), lambda b: (b, 0, 0, 0)),
        pl.BlockSpec((1, A_VAL_W), lambda b: (0, 0)),
        pl.BlockSpec((A_VAL_W, A_VAL_W), lambda b: (0, 0)),
    ]
    return pl.pallas_call(
        functools.partial(_gla_body, seq=seq),
        grid=(nseq,),
        in_specs=in_specs,
        out_specs=[
            pl.BlockSpec((seq, A_VAL_W), lambda b: (b, 0)),
            pl.BlockSpec((1, 2, A_KEY_W, A_DV), lambda b: (b, 0, 0, 0)),
        ],
        out_shape=[
            jax.ShapeDtypeStruct((nseq * seq, A_VAL_W), F32),
            jax.ShapeDtypeStruct((nseq, 2, A_KEY_W, A_DV), F32),
        ],
        scratch_shapes=[pltpu.VMEM((seq, 2 * A_KEY_W), F32)],
        compiler_params=_params(("parallel",)),
        name=f"gla{seq}",
    )(proj, proj, wg, gb, s0t, gnorm, gmat)


def _fnet_body(x_ref, cb_ref, sb_ref, ct_ref, st_ref, o_ref):
    x = x_ref[...].astype(BF16)
    xc = jnp.dot(x, cb_ref[...], preferred_element_type=F32).astype(BF16)
    xs = jnp.dot(x, sb_ref[...], preferred_element_type=F32).astype(BF16)
    o_ref[...] = (jnp.dot(ct_ref[...], xc, preferred_element_type=F32)
                  - jnp.dot(st_ref[...], xs, preferred_element_type=F32))


def _fnet(proj, cb, sb, ct, st, *, nseq, seq, row_block0):
    in_specs = [
        pl.BlockSpec((seq, F_WIDTH), lambda b: (row_block0 + b, COL_PF // F_WIDTH)),
        pl.BlockSpec((F_WIDTH, F_WIDTH), lambda b: (0, 0)),
        pl.BlockSpec((F_WIDTH, F_WIDTH), lambda b: (0, 0)),
        pl.BlockSpec((seq, seq), lambda b: (0, 0)),
        pl.BlockSpec((seq, seq), lambda b: (0, 0)),
    ]
    return pl.pallas_call(
        _fnet_body,
        grid=(nseq,),
        in_specs=in_specs,
        out_specs=pl.BlockSpec((seq, F_WIDTH), lambda b: (b, 0)),
        out_shape=jax.ShapeDtypeStruct((nseq * seq, F_WIDTH), F32),
        compiler_params=_params(("parallel",)),
        name=f"fnet{seq}",
    )(proj, cb, sb, ct, st)


def _dft_mats(n, scale):
    jk = np.outer(np.arange(n), np.arange(n)) % n
    ang = jk.astype(np.float64) * (2.0 * np.pi / n)
    return np.cos(ang) * scale, np.sin(ang) * scale


def _head_rms(x, gm16, gain):
    ms = _dot2(x * x, gm16)
    return x * lax.rsqrt(ms + EPS) * gain


def _rope(x, cos, sin_signed):
    w = x.shape[1]
    lane = lax.broadcasted_iota(jnp.int32, x.shape, 1)
    partner = jnp.where((lane & 31) < 16, pltpu.roll(x, w - 16, 1), pltpu.roll(x, 16, 1))
    return x * cos + partner * sin_signed


def _attn_body(*refs, rope, n_new, n_past):
    if n_past:
        (q_ref, k_ref, v_ref, ck_ref, cv_ref, cq_ref, sq_ref, cosk_ref, sink_ref,
         qn_ref, kn_ref, gm_ref, o_ref, ka_ref, k_scr, ks_scr, v_scr, vs_scr) = refs
    else:
        (q_ref, k_ref, v_ref, qn_ref, kn_ref, gm_ref, o_ref, ka_ref,
         k_scr, ks_scr, v_scr, vs_scr) = refs
    gm = gm_ref[...]

    @pl.when(pl.program_id(1) == 0)
    def _():
        kn = _head_rms(k_ref[...], gm, kn_ref[...])
        ka_ref[...] = kn
        if rope:
            kn = _rope(kn, cosk_ref[...], sink_ref[...])
        v = v_ref[...]
        new = pl.ds(0, n_new)
        k_scr[new, :] = kn.astype(BF16)
        ks_scr[new, :] = pltpu.roll(kn, HEAD_DIM, 1).astype(BF16)
        v_scr[new, :] = v.astype(BF16)
        vs_scr[new, :] = pltpu.roll(v, HEAD_DIM, 1).astype(BF16)
        if n_past:
            past = pl.ds(n_new, n_past)
            ck = ck_ref[0]
            cv = cv_ref[0]
            k_scr[past, :] = ck.astype(BF16)
            ks_scr[past, :] = pltpu.roll(ck, HEAD_DIM, 1).astype(BF16)
            v_scr[past, :] = cv.astype(BF16)
            vs_scr[past, :] = pltpu.roll(cv, HEAD_DIM, 1).astype(BF16)

    lane = lax.broadcasted_iota(jnp.int32, (Q_TILE, LANES), 1)
    low = lane < HEAD_DIM
    for kv in range(C_KV_HEADS):
        q2 = []
        for jp in (2 * kv, 2 * kv + 1):
            q = _head_rms(q_ref[:, jp * LANES:(jp + 1) * LANES], gm, qn_ref[...])
            if rope:
                q = _rope(q, cq_ref[...], sq_ref[...])
            q2.append(q * (HEAD_DIM ** -0.5))
        halves = []
        for half in range(2):
            swapped = (half == 1) != (kv == 1)
            kk = ks_scr[...] if swapped else k_scr[...]
            vv = vs_scr[...] if swapped else v_scr[...]
            keep = low if half == 0 else ~low
            qm = jnp.concatenate([jnp.where(keep, q, 0.0) for q in q2], axis=0).astype(BF16)
            s = lax.dot_general(qm, kk, (((1,), (1,)), ((), ())), preferred_element_type=F32)
            p = jnp.exp(s - jnp.max(s, axis=-1, keepdims=True))
            den = jnp.sum(p, axis=-1, keepdims=True)
            halves.append(jnp.dot(p.astype(BF16), vv, preferred_element_type=F32) / den)
        for n, jp in enumerate((2 * kv, 2 * kv + 1)):
            rows = slice(n * Q_TILE, (n + 1) * Q_TILE)
            o_ref[:, jp * LANES:(jp + 1) * LANES] = jnp.where(low, halves[0][rows], halves[1][rows])


def _attn(proj, qn, kn, gm, *, nseq, seq, row_block0, past=None, tables=None):
    nq = seq // Q_TILE
    n_past = 0 if past is None else past[0].shape[1]
    qrow = lambda b, j: ((row_block0 + b) * nq + j, COL_CQ // C_Q_W)
    in_specs = [
        pl.BlockSpec((Q_TILE, C_Q_W), qrow),
        pl.BlockSpec((seq, C_KV_W), lambda b, j: (row_block0 + b, COL_CK // C_KV_W)),
        pl.BlockSpec((seq, C_KV_W), lambda b, j: (row_block0 + b, COL_CV // C_KV_W)),
    ]
    args = [proj, proj, proj]
    if past is not None:
        in_specs += [pl.BlockSpec((1, n_past, C_KV_W), lambda b, j: (b, 0, 0))] * 2
        in_specs += [pl.BlockSpec((Q_TILE, LANES), lambda b, j: (j, 0))] * 2
        in_specs += [pl.BlockSpec((seq, LANES), lambda b, j: (0, 0))] * 2
        args += [past[0], past[1], tables[0], tables[1], tables[0], tables[1]]
    in_specs += [
        pl.BlockSpec((1, LANES), lambda b, j: (0, 0)),
        pl.BlockSpec((1, LANES), lambda b, j: (0, 0)),
        pl.BlockSpec((LANES, LANES), lambda b, j: (0, 0)),
    ]
    args += [qn, kn, gm]
    s_all = seq + n_past
    return pl.pallas_call(
        functools.partial(_attn_body, rope=past is not None, n_new=seq, n_past=n_past),
        grid=(nseq, nq),
        in_specs=in_specs,
        out_specs=[
            pl.BlockSpec((Q_TILE, C_Q_W), lambda b, j: (b * nq + j, 0)),
            pl.BlockSpec((seq, C_KV_W), lambda b, j: (b, 0)),
        ],
        out_shape=[
            jax.ShapeDtypeStruct((nseq * seq, C_Q_W), F32),
            jax.ShapeDtypeStruct((nseq * seq, C_KV_W), F32),
        ],
        scratch_shapes=[pltpu.VMEM((s_all, C_KV_W), BF16)] * 4,
        compiler_params=_params(("parallel", "arbitrary")),
        name=f"attn{seq}",
    )(*args)


def _rope_tables():
    t = np.arange(DEC_SEQ)
    rows = (t // GRID_W).astype(np.float32)
    cols = (t % GRID_W).astype(np.float32)
    nf = HEAD_DIM // 4
    inv_freq = (ROPE_THETA ** (-np.arange(nf, dtype=np.float32) / nf)).astype(np.float32)
    d = np.arange(LANES)
    pos = np.where(((d % HEAD_DIM) < HEAD_DIM // 2)[None, :], rows[:, None], cols[:, None])
    ang = (pos * inv_freq[d % nf][None, :]).astype(np.float32)
    sign = np.where((d % 32) < 16, -1.0, 1.0)[None, :]
    return jnp.asarray(np.cos(ang), F32), jnp.asarray(np.sin(ang) * sign, F32)


def _outproj_body(oa_c, of_c, oc_c, oa_d, of_d, oc_d, w_ref, *refs):
    *x_refs, mod_ref, g_ref, rw_ref, rb_ref, x1_ref, h2_ref, meta_ref, slot_ref, info_ref = refs
    mod = mod_ref[0]
    is_ctx = pl.program_id(0) < N_CTX // TM
    pick = lambda c_ref, d_ref: jnp.where(is_ctx, c_ref[...], d_ref[...]).astype(BF16)
    x_in = x_refs[0][...] if len(x_refs) == 1 else jnp.where(is_ctx, x_refs[0][...], x_refs[1][...])
    mixed = (jnp.dot(pick(oa_c, oa_d), w_ref[0:256, :], preferred_element_type=F32)
             + jnp.dot(pick(of_c, of_d), w_ref[256:512, :], preferred_element_type=F32)
             + jnp.dot(pick(oc_c, oc_d), w_ref[512:1024, :], preferred_element_type=F32))
    x1 = x_in + mod[2:3] * mixed
    x1_ref[...] = x1
    h2 = _rms(x1) * g_ref[...] * (1.0 + mod[4:5]) + mod[3:4]
    h2_ref[...] = h2.astype(BF16)
    lane = lax.broadcasted_iota(jnp.int32, (TM, LANES), 1)
    logits = jnp.where(lane < N_EXPERTS, _dot3(h2, rw_ref[...]) + rb_ref[...], -jnp.inf)
    meta = jnp.zeros((TM, LANES), F32)
    tops = []
    for k in range(TOP_K):
        m = jnp.max(logits, axis=-1, keepdims=True)
        idx = jnp.min(jnp.where(logits == m, lane, LANES), axis=-1, keepdims=True)
        hit = lane == idx
        logits = jnp.where(hit, -jnp.inf, logits)
        meta = jnp.where(lane == k, idx.astype(F32), meta)
        tops.append(m)
    ex = [jnp.exp(t - tops[0]) for t in tops]
    den = ex[0] + ex[1] + ex[2] + ex[3]
    for k in range(TOP_K):
        meta = jnp.where(lane == TOP_K + k, ex[k] / den, meta)
    meta_ref[...] = meta
    for t in range(TM // ROW_TILE):
        rows = slice(t * ROW_TILE, (t + 1) * ROW_TILE)
        slot_ref[rows, :], info_ref[t] = _tile_slots(meta[rows])


def _outproj(mix_ctx, mix_dec, w, xs, mod3, gain, rw, rb):
    n_ctx_tiles = N_CTX // TM
    tok = lambda w_: pl.BlockSpec((TM, w_), lambda i: (i, 0))
    ctx = lambda w_: pl.BlockSpec((TM, w_), lambda i: (jnp.minimum(i, n_ctx_tiles - 1), 0))
    dec = lambda w_: pl.BlockSpec((TM, w_), lambda i: (jnp.maximum(i - n_ctx_tiles, 0), 0))
    full = lambda a, b: pl.BlockSpec((a, b), lambda i: (0, 0))
    x_specs = [ctx(D_MODEL), dec(D_MODEL)] if len(xs) == 2 else [tok(D_MODEL)]
    tiles_per_step = TM // ROW_TILE
    return pl.pallas_call(
        _outproj_body,
        grid=(N_TOK // TM,),
        in_specs=[ctx(A_VAL_W), ctx(F_WIDTH), ctx(C_Q_W), dec(A_VAL_W), dec(F_WIDTH), dec(C_Q_W),
                  full(D_MODEL, D_MODEL)] + x_specs + [
                  pl.BlockSpec((1, 6, D_MODEL), lambda i: (_mod_row(i, TM), 0, 0)),
                  full(1, D_MODEL), full(D_MODEL, LANES), full(1, LANES)],
        out_specs=[tok(D_MODEL), tok(D_MODEL), tok(LANES), tok(LANES),
                   pl.BlockSpec((tiles_per_step, 8, LANES), lambda i: (i, 0, 0))],
        out_shape=[jax.ShapeDtypeStruct((N_TOK, D_MODEL), F32),
                   jax.ShapeDtypeStruct((N_TOK, D_MODEL), BF16),
                   jax.ShapeDtypeStruct((N_TOK, LANES), F32),
                   jax.ShapeDtypeStruct((N_TOK, LANES), F32),
                   jax.ShapeDtypeStruct((N_TILES, 8, LANES), F32)],
        compiler_params=_params(("parallel",)),
        name="outproj",
    )(*mix_ctx, *mix_dec, w, *xs, mod3, gain, rw, rb)


def _tile_slots(meta):
    lane = lax.broadcasted_iota(jnp.int32, (ROW_TILE, LANES), 1)
    idx = [meta[:, k:k + 1].astype(jnp.int32) for k in range(TOP_K)]
    sel = jnp.zeros((ROW_TILE, LANES), F32)
    for k in range(TOP_K):
        sel = jnp.where(lane == idx[k], 1.0, sel)
    ri = lax.broadcasted_iota(jnp.int32, (ROW_TILE, ROW_TILE), 0)
    ci = lax.broadcasted_iota(jnp.int32, (ROW_TILE, ROW_TILE), 1)
    lower = jnp.where(ci < ri, 1.0, 0.0).astype(BF16)
    rank = jnp.dot(lower, sel.astype(BF16), preferred_element_type=F32)
    cnt = jnp.sum(sel, axis=0, keepdims=True)
    n8 = jnp.floor((cnt + (RUN_ALIGN - 1)) * (1.0 / RUN_ALIGN)) * RUN_ALIGN
    er = lax.broadcasted_iota(jnp.int32, (LANES, LANES), 0)
    ec = lax.broadcasted_iota(jnp.int32, (LANES, LANES), 1)
    before = jnp.where(er < ec, 1.0, 0.0).astype(BF16)
    off = jnp.dot(jnp.broadcast_to(n8, (8, LANES)).astype(BF16), before, preferred_element_type=F32)
    tot = rank + off[0:1]
    out = jnp.zeros((ROW_TILE, LANES), F32)
    for k in range(TOP_K):
        pk = jnp.sum(jnp.where(lane == idx[k], tot, 0.0), axis=-1, keepdims=True)
        out = jnp.where(lane == k, pk, out)
    row = lax.broadcasted_iota(jnp.int32, (8, LANES), 0)
    return out, jnp.where(row == 0, n8, jnp.where(row == 1, off, 0.0))


def _piece_copies(n_rows, row_a, row_b, make_copy, action, bits):
    m = lax.shift_right_logical(n_rows, RUN_SHIFT)

    def piece(bit):
        rows = RUN_ALIGN << bit
        lo = (m & ((1 << bit) - 1)) * RUN_ALIGN

        @pl.when(((m >> bit) & 1) == 1)
        def _():
            cp = make_copy(pl.multiple_of(row_a + lo, RUN_ALIGN), pl.multiple_of(row_b + lo, RUN_ALIGN), rows)
            if action == "start":
                cp.start()
            else:
                cp.wait()

    small = min(bits, 3)
    for bit in range(small):
        piece(bit)
    if bits > small:
        @pl.when(lax.shift_right_logical(m, small) != 0)
        def _():
            for bit in range(small, bits):
                piece(bit)


def _run_copies(tile, n8_tbl, off_tbl, base_tbl, make_copy, action):
    def per_expert(e, c):
        j = tile * N_EXPERTS + e
        _piece_copies(n8_tbl[j], off_tbl[j], base_tbl[j], make_copy, action, RUN_BITS)
        return c

    lax.fori_loop(0, N_EXPERTS, per_expert, 0)


def _unpack_pairs(u):
    hi = lax.bitcast_convert_type(u & jnp.uint32(0xFFFF0000), F32)
    lo = lax.bitcast_convert_type(u << 16, F32)
    return hi, lo


def _pack_pairs(a, b):
    return lax.bitcast_convert_type(a, jnp.uint32) | (lax.bitcast_convert_type(b, jnp.uint32) >> 16)


def _slot_onehot(slot, weights=None):
    lane = lax.broadcasted_iota(jnp.int32, (ROW_TILE, SORT_ROWS), 1)
    p = jnp.zeros((ROW_TILE, SORT_ROWS), F32)
    for k in range(TOP_K):
        hit = lane == slot[:, k:k + 1].astype(jnp.int32)
        p = jnp.where(hit, 1.0 if weights is None else weights[:, k:k + 1], p)
    return p.astype(BF16)


def _dispatch_body(n8_tbl, off_tbl, base_tbl, tail_n, tail_base, nused, h_ref, slot_ref, xs_hbm,
                   sorted_scr, zero_scr, sem):
    tile = pl.program_id(0)

    @pl.when(tile == 0)
    def _():
        zero_scr[...] = jnp.zeros_like(zero_scr)

        def make_zero_copy(_, dst_row, rows):
            return pltpu.make_async_copy(zero_scr.at[pl.ds(0, rows)], xs_hbm.at[pl.ds(dst_row, rows)], sem.at[2])

        for action in ("start", "wait"):
            def per_expert(e, c):
                _piece_copies(tail_n[e], 0, tail_base[e], make_zero_copy, action, TAIL_BITS)
                return c

            def per_block(b, c):
                cp = make_zero_copy(0, pl.multiple_of(b * MOE_BM, MOE_BM), MOE_BM)
                if action == "start":
                    cp.start()
                else:
                    cp.wait()
                return c

            lax.fori_loop(0, N_EXPERTS, per_expert, 0)
            lax.fori_loop(nused[0], MOE_NB, per_block, 0)

    p = _slot_onehot(slot_ref[...])
    srt = lax.dot_general(p, h_ref[...], (((0,), (0,)), ((), ())), preferred_element_type=F32)
    cur = tile & 1
    sorted_scr[cur] = _pack_pairs(srt[:, :HALF_D], srt[:, HALF_D:])

    def copies_from(buf):
        def make_copy(src_row, dst_row, rows):
            return pltpu.make_async_copy(sorted_scr.at[buf, pl.ds(src_row, rows)],
                                         xs_hbm.at[pl.ds(dst_row, rows)], sem.at[buf])
        return make_copy

    _run_copies(tile, n8_tbl, off_tbl, base_tbl, copies_from(cur), "start")

    @pl.when(tile > 0)
    def _():
        _run_copies(tile - 1, n8_tbl, off_tbl, base_tbl, copies_from(1 - cur), "wait")

    @pl.when(tile == N_TILES - 1)
    def _():
        _run_copies(tile, n8_tbl, off_tbl, base_tbl, copies_from(cur), "wait")


def _dispatch(tables, tails, h2, slots):
    grid_spec = pltpu.PrefetchScalarGridSpec(
        num_scalar_prefetch=6,
        grid=(N_TILES,),
        in_specs=[pl.BlockSpec((ROW_TILE, D_MODEL), lambda i, *_: (i, 0)),
                  pl.BlockSpec((ROW_TILE, LANES), lambda i, *_: (i, 0))],
        out_specs=pl.BlockSpec(memory_space=pl.ANY),
        scratch_shapes=[pltpu.VMEM((2, SORT_ROWS, HALF_D), jnp.uint32),
                        pltpu.VMEM((MOE_BM, HALF_D), jnp.uint32), pltpu.SemaphoreType.DMA((3,))],
    )
    return pl.pallas_call(
        _dispatch_body,
        grid_spec=grid_spec,
        out_shape=jax.ShapeDtypeStruct((MOE_ROWS, HALF_D), jnp.uint32),
        compiler_params=_params(("arbitrary",)),
        name="dispatch",
    )(*tables, *tails, h2, slots)


def _experts_body(bexp_ref, nused_ref, first_ref, next_ref, slot_ref, valid_ref, x_ref, wgu_hbm, bgu_ref, wdn_hbm,
                  bdn_ref, y_ref, wgu32, wdn32, wgu16, wdn16, sem, *, layer):
    del nused_ref
    i = pl.program_id(0)

    def weight_copies(expert, s):
        return (pltpu.make_async_copy(wgu_hbm.at[layer, expert], wgu32.at[s], sem.at[0, s]),
                pltpu.make_async_copy(wdn_hbm.at[layer, expert], wdn32.at[s], sem.at[1, s]))

    @pl.when(first_ref[i] == 1)
    def _():
        s = slot_ref[i]

        @pl.when(i == 0)
        def _():
            for cp in weight_copies(bexp_ref[i], s):
                cp.start()

        for cp in weight_copies(bexp_ref[i], s):
            cp.wait()

        @pl.when(next_ref[i] >= 0)
        def _():
            for cp in weight_copies(next_ref[i], 1 - s):
                cp.start()

        wgu16[...] = wgu32[s].astype(BF16)
        wdn16[...] = wdn32[s].astype(BF16)

    def compute(nrows):
        rows = pl.ds(0, nrows)
        x = jnp.concatenate(_unpack_pairs(x_ref[rows, :]), axis=1).astype(BF16)
        gu = jnp.dot(x, wgu16[...], preferred_element_type=F32) + bgu_ref[0, 0]
        gate = jnp.minimum(gu[:, :D_FF], SWIGLU_LIMIT)
        up = jnp.clip(gu[:, D_FF:], -SWIGLU_LIMIT, SWIGLU_LIMIT)
        glu = gate * _sigmoid(gate * SWIGLU_ALPHA)
        act = ((up + 1.0) * glu).astype(BF16)
        y = jnp.dot(act, wdn16[...], preferred_element_type=F32) + bdn_ref[0, 0]
        y = y.astype(BF16).astype(F32)
        y_ref[rows, :] = _pack_pairs(y[:, :HALF_D], y[:, HALF_D:])

    valid = valid_ref[i]

    @pl.when(valid > MOE_SUB)
    def _():
        compute(MOE_BM)

    @pl.when(valid <= MOE_SUB)
    def _():
        y_ref[pl.ds(MOE_SUB, MOE_SUB), :] = jnp.zeros((MOE_SUB, HALF_D), jnp.uint32)

    @pl.when((valid > 0) & (valid <= MOE_SUB))
    def _():
        compute(MOE_SUB)

    @pl.when(valid == 0)
    def _():
        y_ref[pl.ds(0, MOE_SUB), :] = jnp.zeros((MOE_SUB, HALF_D), jnp.uint32)


def _experts(sched, xs, wgu, bgu, wdn, bdn, layer):
    e = lambda i, be, *_: (layer, be[i], 0, 0)
    grid_spec = pltpu.PrefetchScalarGridSpec(
        num_scalar_prefetch=6,
        grid=(MOE_NB,),
        in_specs=[
            pl.BlockSpec((MOE_BM, HALF_D), lambda i, be, nu, *_: (jnp.minimum(i, nu[0] - 1), 0)),
            pl.BlockSpec(memory_space=pl.ANY),
            pl.BlockSpec((1, 1, 1, 2 * D_FF), e),
            pl.BlockSpec(memory_space=pl.ANY),
            pl.BlockSpec((1, 1, 1, D_MODEL), e),
        ],
        out_specs=pl.BlockSpec((MOE_BM, HALF_D), lambda i, *_: (i, 0)),
        scratch_shapes=[pltpu.VMEM((2, D_MODEL, 2 * D_FF), F32), pltpu.VMEM((2, D_FF, D_MODEL), F32),
                        pltpu.VMEM((D_MODEL, 2 * D_FF), BF16), pltpu.VMEM((D_FF, D_MODEL), BF16),
                        pltpu.SemaphoreType.DMA((2, 2))],
    )
    return pl.pallas_call(
        functools.partial(_experts_body, layer=layer),
        grid_spec=grid_spec,
        out_shape=jax.ShapeDtypeStruct((MOE_ROWS, HALF_D), jnp.uint32),
        compiler_params=_params(("arbitrary",)),
        name="experts",
    )(*sched, xs, wgu, bgu, wdn, bdn)


def _combine_body(n8_tbl, off_tbl, base_tbl, y_hbm, slot_ref, meta_ref, x1_ref, mod_ref, nf_ref, o_ref,
                  ybuf, sem, *, final, tile0, ntiles):
    step = pl.program_id(0)
    tile = tile0 + step
    cur = step & 1

    def fetch(t, buf):
        ybuf[buf] = jnp.zeros((SORT_ROWS, HALF_D), jnp.uint32)
        _run_copies(t, n8_tbl, off_tbl, base_tbl, copies_into(buf), "start")

    def copies_into(buf):
        def make_copy(buf_row, hbm_row, rows):
            return pltpu.make_async_copy(y_hbm.at[pl.ds(hbm_row, rows)],
                                         ybuf.at[buf, pl.ds(buf_row, rows)], sem.at[buf])
        return make_copy

    @pl.when(step == 0)
    def _():
        fetch(tile, cur)

    @pl.when(step + 1 < ntiles)
    def _():
        fetch(tile + 1, 1 - cur)

    _run_copies(tile, n8_tbl, off_tbl, base_tbl, copies_into(cur), "wait")
    pw = _slot_onehot(slot_ref[...], meta_ref[...][:, TOP_K:2 * TOP_K])
    hi, lo = _unpack_pairs(ybuf[cur])
    ff = jnp.concatenate([jnp.dot(pw, hi.astype(BF16), preferred_element_type=F32),
                          jnp.dot(pw, lo.astype(BF16), preferred_element_type=F32)], axis=1)
    x2 = x1_ref[...] + mod_ref[0][5:6] * ff
    if final:
        x2 = _rms(x2) * nf_ref[...]
    o_ref[...] = x2


def _combine(tables, y, slots, meta, x1, mod3, nfinal, *, final, tile0=0, ntiles=N_TILES):
    grid_spec = pltpu.PrefetchScalarGridSpec(
        num_scalar_prefetch=3,
        grid=(ntiles,),
        in_specs=[pl.BlockSpec(memory_space=pl.ANY),
                  pl.BlockSpec((ROW_TILE, LANES), lambda i, *_: (tile0 + i, 0)),
                  pl.BlockSpec((ROW_TILE, LANES), lambda i, *_: (tile0 + i, 0)),
                  pl.BlockSpec((ROW_TILE, D_MODEL), lambda i, *_: (tile0 + i, 0)),
                  pl.BlockSpec((1, 6, D_MODEL), lambda i, *_: (_mod_row(tile0 + i, ROW_TILE), 0, 0)),
                  pl.BlockSpec((1, D_MODEL), lambda i, *_: (0, 0))],
        out_specs=pl.BlockSpec((ROW_TILE, D_MODEL), lambda i, *_: (i, 0)),
        scratch_shapes=[pltpu.VMEM((2, SORT_ROWS, HALF_D), jnp.uint32), pltpu.SemaphoreType.DMA((2,))],
    )
    return pl.pallas_call(
        functools.partial(_combine_body, final=final, tile0=tile0, ntiles=ntiles),
        grid_spec=grid_spec,
        out_shape=jax.ShapeDtypeStruct((ntiles * ROW_TILE, D_MODEL), F32),
        compiler_params=_params(("arbitrary",)),
        name="combine",
    )(*tables, y, slots, meta, x1, mod3, nfinal)


def _state_to_blockdiag(s):
    eye = jnp.eye(A_HEADS, dtype=s.dtype)
    t = jnp.einsum("bdhkv,hg->bdhvgk", s, eye)
    return t.reshape(s.shape[0], 2, A_VAL_W, A_KEY_W)


def kernel(x_prompt, x_sample, state_gla, cache_k, cache_v, c, c_ctx, w_ada, b_ada, norm_mix, norm_ffn,
           norm_final, w_in, w_out, gla_gate_w, gla_gate_b, gla_norm, q_norm, k_norm, router_w, router_b,
           expert_w_gu, expert_b_gu, expert_w_dn, expert_b_dn):
    xs = (x_prompt.reshape(N_CTX, D_MODEL), x_sample.reshape(N_DEC, D_MODEL))
    cond = jnp.zeros((MOD_ROWS, D_MODEL), F32).at[0].set(c_ctx).at[1:1 + DEC_BATCH].set(c)
    mod = _ada(cond, w_ada, b_ada)

    gm256 = jnp.asarray(np.kron(np.eye(A_HEADS), np.ones((A_DV, A_DV))) / A_DV, BF16)
    gm128 = jnp.asarray(np.kron(np.eye(2), np.ones((HEAD_DIM, HEAD_DIM))) / HEAD_DIM, BF16)
    c64, s64 = _dft_mats(F_GROUP_W, F_GROUP_W ** -0.5)
    cb = jnp.asarray(np.kron(np.eye(F_GROUPS), c64), BF16)
    sb = jnp.asarray(np.kron(np.eye(F_GROUPS), s64), BF16)
    dft_ctx = [jnp.asarray(m, BF16) for m in _dft_mats(SEQ, SEQ ** -0.5)]
    dft_dec = [jnp.asarray(m, BF16) for m in _dft_mats(DEC_SEQ, DEC_SEQ ** -0.5)]
    rope_tabs = _rope_tables()
    zero_state = jnp.zeros((BATCH, 2, A_VAL_W, A_KEY_W), F32)

    new_s, new_k, new_v = [], [], []
    for l in range(DEPTH):
        mod3 = mod[l].reshape(MOD_ROWS, 6, D_MODEL)
        w_l = w_in[l]
        w_r = jnp.concatenate([w_l[:, :768], w_l[:, 800:], w_l[:, 768:800],
                               jnp.zeros((D_MODEL, PROJ_W - 1824), F32)], axis=1).astype(BF16)
        proj = _inproj(xs, mod3, norm_mix[l][None, :], w_r)

        wg = jnp.zeros((LANES, 2 * A_KEY_W), F32)
        wg = wg.at[0:GATE_RANK, 0:A_KEY_W].set(gla_gate_w[l, 0])
        wg = wg.at[GATE_RANK:2 * GATE_RANK, A_KEY_W:].set(gla_gate_w[l, 1])
        gb = gla_gate_b[l].reshape(1, 2 * A_KEY_W)
        gn = jnp.tile(gla_norm[l], A_HEADS)[None, :]
        s0_dec = _state_to_blockdiag(state_gla[:, l])
        oa_ctx, sfin = _gla(proj, wg, gb, zero_state, gn, gm256, nseq=BATCH, seq=SEQ, row_block0=0)
        oa_dec, _ = _gla(proj, wg, gb, s0_dec, gn, gm256, nseq=DEC_BATCH, seq=DEC_SEQ,
                         row_block0=N_CTX // DEC_SEQ)
        of_ctx = _fnet(proj, cb, sb, *dft_ctx, nseq=BATCH, seq=SEQ, row_block0=0)
        of_dec = _fnet(proj, cb, sb, *dft_dec, nseq=DEC_BATCH, seq=DEC_SEQ, row_block0=N_CTX // DEC_SEQ)
        qn = jnp.tile(q_norm[l], 2)[None, :]
        kn = jnp.tile(k_norm[l], 2)[None, :]
        oc_ctx, ka = _attn(proj, qn, kn, gm128, nseq=BATCH, seq=SEQ, row_block0=0)
        past = (cache_k[:, l].reshape(DEC_BATCH, PAST_LEN, C_KV_W),
                cache_v[:, l].reshape(DEC_BATCH, PAST_LEN, C_KV_W))
        oc_dec, _ = _attn(proj, qn, kn, gm128, nseq=DEC_BATCH, seq=DEC_SEQ,
                          row_block0=N_CTX // DEC_SEQ, past=past, tables=rope_tabs)
        new_s.append(sfin.reshape(BATCH, 2, A_HEADS, A_DK, A_DV))
        new_k.append(ka.reshape(BATCH, SEQ, C_KV_HEADS, HEAD_DIM))
        new_v.append(proj[:N_CTX, COL_CV:COL_CV + C_KV_W].reshape(BATCH, SEQ, C_KV_HEADS, HEAD_DIM))

        rw = jnp.zeros((D_MODEL, LANES), F32).at[:, :N_EXPERTS].set(router_w[l])
        rb = jnp.zeros((1, LANES), F32).at[0, :N_EXPERTS].set(router_b[l])
        x1, h2, meta, slots, info = _outproj((oa_ctx, of_ctx, oc_ctx), (oa_dec, of_dec, oc_dec),
                                             w_out[l].astype(BF16), xs, mod3, norm_ffn[l][None, :], rw, rb)

        n8 = info[:, 0, :N_EXPERTS].astype(jnp.int32)
        off = info[:, 1, :N_EXPERTS].astype(jnp.int32)
        cnt = jnp.sum(n8, axis=0)
        pcnt = (cnt + MOE_BM - 1) // MOE_BM * MOE_BM
        pend = jnp.cumsum(pcnt)
        base = (pend - pcnt)[None, :] + jnp.cumsum(n8, axis=0) - n8
        tables = (n8.reshape(-1), off.reshape(-1), base.reshape(-1))
        blk = jnp.arange(MOE_NB, dtype=jnp.int32)
        blk_exp = jnp.minimum(jnp.sum((pend[None, :] <= (blk * MOE_BM)[:, None]).astype(jnp.int32), axis=1),
                              N_EXPERTS - 1)
        nused = (pend[-1:] // MOE_BM).astype(jnp.int32)
        tails = (pcnt - cnt, pend - (pcnt - cnt), nused)
        prev_exp = jnp.concatenate([jnp.full((1,), -1, jnp.int32), blk_exp[:-1]])
        first = ((blk < nused[0]) & (blk_exp != prev_exp)).astype(jnp.int32)
        end_blk = (pend // MOE_BM)[blk_exp]
        nxt = jnp.where(end_blk < nused[0], blk_exp[jnp.minimum(end_blk, MOE_NB - 1)], -1)
        wslot = (jnp.cumsum(first) - 1) & 1
        start_blk = (pend - pcnt) // MOE_BM
        valid = jnp.where(blk < nused[0],
                          jnp.clip(cnt[blk_exp] - (blk - start_blk[blk_exp]) * MOE_BM, 0, MOE_BM), 0)
        sched = (blk_exp, nused, first, nxt, wslot, valid)

        xs = _dispatch(tables, tails, h2, slots)
        y = _experts(sched, xs, expert_w_gu,
                     expert_b_gu.reshape(DEPTH, N_EXPERTS, 1, 2 * D_FF), expert_w_dn,
                     expert_b_dn.reshape(DEPTH, N_EXPERTS, 1, D_MODEL), l)
        comb = functools.partial(_combine, tables, y, slots, meta, x1, mod3, norm_final[None, :])
        if l < DEPTH - 1:
            xs = (comb(final=False),)
        else:
            ctx_tiles = N_CTX // ROW_TILE
            xs = (comb(final=True, tile0=0, ntiles=ctx_tiles),
                  comb(final=True, tile0=ctx_tiles, ntiles=N_TILES - ctx_tiles))

    y_prompt = xs[0].reshape(BATCH, SEQ, D_MODEL)
    y_sample = xs[1].reshape(DEC_BATCH, DEC_SEQ, D_MODEL)
    return (y_prompt, y_sample, jnp.stack(new_s, axis=1), jnp.stack(new_k, axis=1), jnp.stack(new_v, axis=1))
```

```python
import functools

import jax
import jax.numpy as jnp
import numpy as np
from jax import lax
from jax.experimental import pallas as pl
from jax.experimental.pallas import tpu as pltpu

D_MODEL = 1024
BATCH = 32
SEQ = 256
DEPTH = 2
DEC_BATCH = 8
DEC_SEQ = 1024
PAST_LEN = 512
GRID_W = 64
A_HEADS = 4
A_DV = 64
A_DK = 32
A_KEY_W = 128
A_VAL_W = 256
GATE_RANK = 16
GATE_TEMP = 16.0
GLA_CHUNK = 32
F_GROUPS = 4
F_GROUP_W = 64
F_WIDTH = 256
HEAD_DIM = 64
C_HEADS = 8
C_KV_HEADS = 2
C_Q_W = 512
C_KV_W = 128
ROPE_THETA = 10000.0
N_EXPERTS = 32
TOP_K = 4
D_FF = 1024
SWIGLU_LIMIT = 7.0
SWIGLU_ALPHA = 1.702
EPS = 1e-6

N_CTX = BATCH * SEQ
N_DEC = DEC_BATCH * DEC_SEQ
N_TOK = N_CTX + N_DEC
MOD_ROWS = 16

COL_GLA = 0
COL_PF = 768
COL_CQ = 1024
COL_CK = 1536
COL_CV = 1664
COL_LR = 1792
PROJ_W = 1920

LANES = 128
TM = 512
GLA_TILE = 256
Q_TILE = 256
MOE_BM = 512
MOE_SUB = MOE_BM // 2
ROW_TILE = 256
N_TILES = N_TOK // ROW_TILE
RUN_ALIGN = 8
RUN_SHIFT = 3
RUN_BITS = 6
TAIL_BITS = 6
SORT_ROWS = ROW_TILE * TOP_K + N_EXPERTS * RUN_ALIGN
HALF_D = D_MODEL // 2
MOE_NB = (N_TOK * TOP_K + N_TILES * N_EXPERTS * (RUN_ALIGN - 1)) // MOE_BM + N_EXPERTS
MOE_ROWS = MOE_NB * MOE_BM
VMEM_LIMIT = 56 * 1024 * 1024

F32 = jnp.float32
BF16 = jnp.bfloat16


def _params(sem, vmem=VMEM_LIMIT):
    return pltpu.CompilerParams(dimension_semantics=sem, vmem_limit_bytes=vmem)


def _bdot(a, b):
    return jnp.dot(a.astype(BF16), b.astype(BF16), preferred_element_type=F32)


def _split(a):
    hi = a.astype(BF16)
    lo = (a - hi.astype(F32)).astype(BF16)
    return hi, lo


def _dot3(a, b):
    ah, al = _split(a)
    bh, bl = _split(b)
    d = functools.partial(jnp.dot, preferred_element_type=F32)
    return d(ah, bh) + (d(ah, bl) + d(al, bh))


def _dot2(a, b16):
    ah, al = _split(a)
    d = functools.partial(jnp.dot, preferred_element_type=F32)
    return d(ah, b16) + d(al, b16)


def _sigmoid(x):
    return 1.0 / (1.0 + jnp.exp(-x))


def _mod_row(i, tm):
    tok = i * tm
    return jnp.where(tok < N_CTX, 0, 1 + (tok - N_CTX) // DEC_SEQ)


def _ada_body(c_ref, w_ref, b_ref, o_ref):
    c = c_ref[...]
    o_ref[0] = _dot3(c * _sigmoid(c), w_ref[0]) + b_ref[0]


def _ada(cond, w_ada, b_ada):
    return pl.pallas_call(
        _ada_body,
        grid=(DEPTH, 6),
        in_specs=[
            pl.BlockSpec((MOD_ROWS, D_MODEL), lambda l, j: (0, 0)),
            pl.BlockSpec((1, D_MODEL, D_MODEL), lambda l, j: (l, 0, j)),
            pl.BlockSpec((1, 1, D_MODEL), lambda l, j: (l, 0, j)),
        ],
        out_specs=pl.BlockSpec((1, MOD_ROWS, D_MODEL), lambda l, j: (l, 0, j)),
        out_shape=jax.ShapeDtypeStruct((DEPTH, MOD_ROWS, 6 * D_MODEL), F32),
        compiler_params=_params(("parallel", "parallel")),
        name="ada",
    )(cond, w_ada, b_ada.reshape(DEPTH, 1, 6 * D_MODEL))


def _rms(x):
    return x * lax.rsqrt(jnp.mean(x * x, axis=-1, keepdims=True) + EPS)


def _inproj_body(*refs):
    *x_refs, mod_ref, g_ref, w_ref, o_ref = refs
    if len(x_refs) == 2:
        x = jnp.where(pl.program_id(0) < N_CTX // TM, x_refs[0][...], x_refs[1][...])
    else:
        x = x_refs[0][...]
    mod = mod_ref[0]
    h = _rms(x) * g_ref[...] * (1.0 + mod[1:2]) + mod[0:1]
    o_ref[...] = jnp.dot(h.astype(BF16), w_ref[...], preferred_element_type=F32)


def _inproj(xs, mod3, gain, w):
    n_ctx_tiles = N_CTX // TM
    if len(xs) == 2:
        x_specs = [pl.BlockSpec((TM, D_MODEL), lambda i: (jnp.minimum(i, n_ctx_tiles - 1), 0)),
                   pl.BlockSpec((TM, D_MODEL), lambda i: (jnp.maximum(i - n_ctx_tiles, 0), 0))]
    else:
        x_specs = [pl.BlockSpec((TM, D_MODEL), lambda i: (i, 0))]
    return pl.pallas_call(
        _inproj_body,
        grid=(N_TOK // TM,),
        in_specs=x_specs + [
            pl.BlockSpec((1, 6, D_MODEL), lambda i: (_mod_row(i, TM), 0, 0)),
            pl.BlockSpec((1, D_MODEL), lambda i: (0, 0)),
            pl.BlockSpec((D_MODEL, PROJ_W), lambda i: (0, 0)),
        ],
        out_specs=pl.BlockSpec((TM, PROJ_W), lambda i: (i, 0)),
        out_shape=jax.ShapeDtypeStruct((N_TOK, PROJ_W), F32),
        compiler_params=_params(("parallel",)),
        name="inproj",
    )(*xs, mod3, gain, w)


def _log_sigmoid(x):
    return jnp.minimum(x, 0.0) - jnp.log(1.0 + jnp.exp(-jnp.abs(x)))


def _chunk_cumsum(x, pos, forward):
    rows = x.shape[0]
    for s in (1, 2, 4, 8, 16):
        if forward:
            x = x + jnp.where(pos >= s, pltpu.roll(x, s, 0), 0.0)
        else:
            x = x + jnp.where(pos < GLA_CHUNK - s, pltpu.roll(x, rows - s, 0), 0.0)
    return x


def _gla_tile(q, k, v16, la, st, forward):
    R = GLA_TILE
    nch = R // GLA_CHUNK
    pos = lax.broadcasted_iota(jnp.int32, (R, 1), 0) & (GLA_CHUNK - 1)
    b = _chunk_cumsum(la, pos, forward)
    b3 = b.reshape(nch, GLA_CHUNK, A_KEY_W)
    tot3 = b3[:, GLA_CHUNK - 1:GLA_CHUNK, :] if forward else b3[:, 0:1, :]
    totb = jnp.broadcast_to(tot3, (nch, GLA_CHUNK, A_KEY_W)).reshape(R, A_KEY_W)
    qt = q * (A_DK ** -0.5) * jnp.exp(b)
    kt = (k * jnp.exp(-b)).astype(BF16)
    kend = (k * jnp.exp(totb - b)).astype(BF16)
    dec3 = jnp.exp(tot3)

    ri = lax.broadcasted_iota(jnp.int32, (R, R), 0)
    ci = lax.broadcasted_iota(jnp.int32, (R, R), 1)
    tri = (ci <= ri) if forward else (ci >= ri)
    mask = ((ri >> 5) == (ci >> 5)) & tri
    lane_k = lax.broadcasted_iota(jnp.int32, (R, A_KEY_W), 1) >> 5
    lane_v = lax.broadcasted_iota(jnp.int32, (R, A_VAL_W), 1) >> 6

    o = jnp.zeros((R, A_VAL_W), F32)
    for h in range(A_HEADS):
        qh = jnp.where(lane_k == h, qt, 0.0).astype(BF16)
        sc = lax.dot_general(qh, kt, (((1,), (1,)), ((), ())), preferred_element_type=F32)
        p = jnp.where(mask, sc, 0.0).astype(BF16)
        oh = jnp.dot(p, v16, preferred_element_type=F32)
        o = jnp.where(lane_v == h, oh, o)

    sr = lax.broadcasted_iota(jnp.int32, (A_VAL_W, A_KEY_W), 0) >> 6
    sl = lax.broadcasted_iota(jnp.int32, (A_VAL_W, A_KEY_W), 1) >> 5
    smask = sr == sl
    qt16 = qt.astype(BF16)
    inter = [None] * nch
    order = range(nch) if forward else range(nch - 1, -1, -1)
    for c in order:
        lo = c * GLA_CHUNK
        inter[c] = lax.dot_general(qt16[lo:lo + GLA_CHUNK], st.astype(BF16),
                                   (((1,), (1,)), ((), ())), preferred_element_type=F32)
        ds = lax.dot_general(v16[lo:lo + GLA_CHUNK], kend[lo:lo + GLA_CHUNK],
                             (((0,), (0,)), ((), ())), preferred_element_type=F32)
        st = st * dec3[c] + jnp.where(smask, ds, 0.0)
    return o + jnp.concatenate(inter, axis=0), st


def _gla_body(gla_ref, lr_ref, wg_ref, gb_ref, s0_ref, gn_ref, gm_ref, o_ref, sfin_ref, la_scr, *, seq):
    ntile = seq // GLA_TILE
    logit = _bdot(lr_ref[...], wg_ref[...]) + gb_ref[...]
    la_scr[...] = _log_sigmoid(logit) * (1.0 / GATE_TEMP)

    def run(forward):
        d = 0 if forward else 1

        def step(i, st):
            t = i if forward else ntile - 1 - i
            r0 = pl.multiple_of(t * GLA_TILE, GLA_TILE)
            rows = pl.ds(r0, GLA_TILE)
            q = gla_ref[rows, 0:128]
            k = gla_ref[rows, 128:256]
            v16 = gla_ref[rows, 256:512].astype(BF16)
            la = la_scr[rows, d * A_KEY_W:(d + 1) * A_KEY_W]
            o, st = _gla_tile(q, k, v16, la, st, forward)
            if forward:
                o_ref[rows, :] = o
            else:
                o_ref[rows, :] += o
            return st

        st = lax.fori_loop(0, ntile, step, s0_ref[0, d])
        s = st.T
        sfin_ref[0, d] = (s[:, 0:A_DV] + s[:, A_DV:2 * A_DV]) + (s[:, 2 * A_DV:3 * A_DV] + s[:, 3 * A_DV:])

    run(True)
    run(False)
    o = o_ref[...]
    ms = _dot2(o * o, gm_ref[...])
    g = gla_ref[:, 512:768]
    o_ref[...] = o * lax.rsqrt(ms + EPS) * gn_ref[...] * (g * _sigmoid(g))


def _gla(proj, wg, gb, s0t, gnorm, gmat, *, nseq, seq, row_block0):
    in_specs = [
        pl.BlockSpec((seq, 768), lambda b: (row_block0 + b, 0)),
        pl.BlockSpec((seq, LANES), lambda b: (row_block0 + b, COL_LR // LANES)),
        pl.BlockSpec((LANES, 2 * A_KEY_W), lambda b: (0, 0)),
        pl.BlockSpec((1, 2 * A_KEY_W), lambda b: (0, 0)),
        pl.BlockSpec((1, 2, A_VAL_W, A_KEY_W), lambda b: (b, 0, 0, 0)),
        pl.BlockSpec((1, A_VAL_W), lambda b: (0, 0)),
        pl.BlockSpec((A_VAL_W, A_VAL_W), lambda b: (0, 0)),
    ]
    return pl.pallas_call(
        functools.partial(_gla_body, seq=seq),
        grid=(nseq,),
        in_specs=in_specs,
        out_specs=[
            pl.BlockSpec((seq, A_VAL_W), lambda b: (b, 0)),
            pl.BlockSpec((1, 2, A_KEY_W, A_DV), lambda b: (b, 0, 0, 0)),
        ],
        out_shape=[
            jax.ShapeDtypeStruct((nseq * seq, A_VAL_W), F32),
            jax.ShapeDtypeStruct((nseq, 2, A_KEY_W, A_DV), F32),
        ],
        scratch_shapes=[pltpu.VMEM((seq, 2 * A_KEY_W), F32)],
        compiler_params=_params(("parallel",)),
        name=f"gla{seq}",
    )(proj, proj, wg, gb, s0t, gnorm, gmat)


def _fnet_body(x_ref, cb_ref, sb_ref, ct_ref, st_ref, o_ref):
    x = x_ref[...].astype(BF16)
    xc = jnp.dot(x, cb_ref[...], preferred_element_type=F32).astype(BF16)
    xs = jnp.dot(x, sb_ref[...], preferred_element_type=F32).astype(BF16)
    o_ref[...] = (jnp.dot(ct_ref[...], xc, preferred_element_type=F32)
                  - jnp.dot(st_ref[...], xs, preferred_element_type=F32))


def _fnet(proj, cb, sb, ct, st, *, nseq, seq, row_block0):
    in_specs = [
        pl.BlockSpec((seq, F_WIDTH), lambda b: (row_block0 + b, COL_PF // F_WIDTH)),
        pl.BlockSpec((F_WIDTH, F_WIDTH), lambda b: (0, 0)),
        pl.BlockSpec((F_WIDTH, F_WIDTH), lambda b: (0, 0)),
        pl.BlockSpec((seq, seq), lambda b: (0, 0)),
        pl.BlockSpec((seq, seq), lambda b: (0, 0)),
    ]
    return pl.pallas_call(
        _fnet_body,
        grid=(nseq,),
        in_specs=in_specs,
        out_specs=pl.BlockSpec((seq, F_WIDTH), lambda b: (b, 0)),
        out_shape=jax.ShapeDtypeStruct((nseq * seq, F_WIDTH), F32),
        compiler_params=_params(("parallel",)),
        name=f"fnet{seq}",
    )(proj, cb, sb, ct, st)


def _dft_mats(n, scale):
    jk = np.outer(np.arange(n), np.arange(n)) % n
    ang = jk.astype(np.float64) * (2.0 * np.pi / n)
    return np.cos(ang) * scale, np.sin(ang) * scale


def _head_rms(x, gm16, gain):
    ms = _dot2(x * x, gm16)
    return x * lax.rsqrt(ms + EPS) * gain


def _rope(x, cos, sin_signed):
    w = x.shape[1]
    lane = lax.broadcasted_iota(jnp.int32, x.shape, 1)
    partner = jnp.where((lane & 31) < 16, pltpu.roll(x, w - 16, 1), pltpu.roll(x, 16, 1))
    return x * cos + partner * sin_signed


def _attn_body(*refs, rope, n_new, n_past):
    if n_past:
        (q_ref, k_ref, v_ref, ck_ref, cv_ref, cq_ref, sq_ref, cosk_ref, sink_ref,
         qn_ref, kn_ref, gm_ref, o_ref, ka_ref, k_scr, ks_scr, v_scr, vs_scr) = refs
    else:
        (q_ref, k_ref, v_ref, qn_ref, kn_ref, gm_ref, o_ref, ka_ref,
         k_scr, ks_scr, v_scr, vs_scr) = refs
    gm = gm_ref[...]

    @pl.when(pl.program_id(1) == 0)
    def _():
        kn = _head_rms(k_ref[...], gm, kn_ref[...])
        ka_ref[...] = kn
        if rope:
            kn = _rope(kn, cosk_ref[...], sink_ref[...])
        v = v_ref[...]
        new = pl.ds(0, n_new)
        k_scr[new, :] = kn.astype(BF16)
        ks_scr[new, :] = pltpu.roll(kn, HEAD_DIM, 1).astype(BF16)
        v_scr[new, :] = v.astype(BF16)
        vs_scr[new, :] = pltpu.roll(v, HEAD_DIM, 1).astype(BF16)
        if n_past:
            past = pl.ds(n_new, n_past)
            ck = ck_ref[0]
            cv = cv_ref[0]
            k_scr[past, :] = ck.astype(BF16)
            ks_scr[past, :] = pltpu.roll(ck, HEAD_DIM, 1).astype(BF16)
            v_scr[past, :] = cv.astype(BF16)
            vs_scr[past, :] = pltpu.roll(cv, HEAD_DIM, 1).astype(BF16)

    lane = lax.broadcasted_iota(jnp.int32, (Q_TILE, LANES), 1)
    low = lane < HEAD_DIM
    for kv in range(C_KV_HEADS):
        q2 = []
        for jp in (2 * kv, 2 * kv + 1):
            q = _head_rms(q_ref[:, jp * LANES:(jp + 1) * LANES], gm, qn_ref[...])
            if rope:
                q = _rope(q, cq_ref[...], sq_ref[...])
            q2.append(q * (HEAD_DIM ** -0.5))
        halves = []
        for half in range(2):
            swapped = (half == 1) != (kv == 1)
            kk = ks_scr[...] if swapped else k_scr[...]
            vv = vs_scr[...] if swapped else v_scr[...]
            keep = low if half == 0 else ~low
            qm = jnp.concatenate([jnp.where(keep, q, 0.0) for q in q2], axis=0).astype(BF16)
            s = lax.dot_general(qm, kk, (((1,), (1,)), ((), ())), preferred_element_type=F32)
            p = jnp.exp(s - jnp.max(s, axis=-1, keepdims=True))
            den = jnp.sum(p, axis=-1, keepdims=True)
            halves.append(jnp.dot(p.astype(BF16), vv, preferred_element_type=F32) / den)
        for n, jp in enumerate((2 * kv, 2 * kv + 1)):
            rows = slice(n * Q_TILE, (n + 1) * Q_TILE)
            o_ref[:, jp * LANES:(jp + 1) * LANES] = jnp.where(low, halves[0][rows], halves[1][rows])


def _attn(proj, qn, kn, gm, *, nseq, seq, row_block0, past=None, tables=None):
    nq = seq // Q_TILE
    n_past = 0 if past is None else past[0].shape[1]
    qrow = lambda b, j: ((row_block0 + b) * nq + j, COL_CQ // C_Q_W)
    in_specs = [
        pl.BlockSpec((Q_TILE, C_Q_W), qrow),
        pl.BlockSpec((seq, C_KV_W), lambda b, j: (row_block0 + b, COL_CK // C_KV_W)),
        pl.BlockSpec((seq, C_KV_W), lambda b, j: (row_block0 + b, COL_CV // C_KV_W)),
    ]
    args = [proj, proj, proj]
    if past is not None:
        in_specs += [pl.BlockSpec((1, n_past, C_KV_W), lambda b, j: (b, 0, 0))] * 2
        in_specs += [pl.BlockSpec((Q_TILE, LANES), lambda b, j: (j, 0))] * 2
        in_specs += [pl.BlockSpec((seq, LANES), lambda b, j: (0, 0))] * 2
        args += [past[0], past[1], tables[0], tables[1], tables[0], tables[1]]
    in_specs += [
        pl.BlockSpec((1, LANES), lambda b, j: (0, 0)),
        pl.BlockSpec((1, LANES), lambda b, j: (0, 0)),
        pl.BlockSpec((LANES, LANES), lambda b, j: (0, 0)),
    ]
    args += [qn, kn, gm]
    s_all = seq + n_past
    return pl.pallas_call(
        functools.partial(_attn_body, rope=past is not None, n_new=seq, n_past=n_past),
        grid=(nseq, nq),
        in_specs=in_specs,
        out_specs=[
            pl.BlockSpec((Q_TILE, C_Q_W), lambda b, j: (b * nq + j, 0)),
            pl.BlockSpec((seq, C_KV_W), lambda b, j: (b, 0)),
        ],
        out_shape=[
            jax.ShapeDtypeStruct((nseq * seq, C_Q_W), F32),
            jax.ShapeDtypeStruct((nseq * seq, C_KV_W), F32),
        ],
        scratch_shapes=[pltpu.VMEM((s_all, C_KV_W), BF16)] * 4,
        compiler_params=_params(("parallel", "arbitrary")),
        name=f"attn{seq}",
    )(*args)


def _rope_tables():
    t = np.arange(DEC_SEQ)
    rows = (t // GRID_W).astype(np.float32)
    cols = (t % GRID_W).astype(np.float32)
    nf = HEAD_DIM // 4
    inv_freq = (ROPE_THETA ** (-np.arange(nf, dtype=np.float32) / nf)).astype(np.float32)
    d = np.arange(LANES)
    pos = np.where(((d % HEAD_DIM) < HEAD_DIM // 2)[None, :], rows[:, None], cols[:, None])
    ang = (pos * inv_freq[d % nf][None, :]).astype(np.float32)
    sign = np.where((d % 32) < 16, -1.0, 1.0)[None, :]
    return jnp.asarray(np.cos(ang), F32), jnp.asarray(np.sin(ang) * sign, F32)


def _outproj_body(oa_c, of_c, oc_c, oa_d, of_d, oc_d, w_ref, *refs):
    *x_refs, mod_ref, g_ref, rw_ref, rb_ref, x1_ref, h2_ref, meta_ref, slot_ref, info_ref = refs
    mod = mod_ref[0]
    is_ctx = pl.program_id(0) < N_CTX // TM
    pick = lambda c_ref, d_ref: jnp.where(is_ctx, c_ref[...], d_ref[...]).astype(BF16)
    x_in = x_refs[0][...] if len(x_refs) == 1 else jnp.where(is_ctx, x_refs[0][...], x_refs[1][...])
    mixed = (jnp.dot(pick(oa_c, oa_d), w_ref[0:256, :], preferred_element_type=F32)
             + jnp.dot(pick(of_c, of_d), w_ref[256:512, :], preferred_element_type=F32)
             + jnp.dot(pick(oc_c, oc_d), w_ref[512:1024, :], preferred_element_type=F32))
    x1 = x_in + mod[2:3] * mixed
    x1_ref[...] = x1
    h2 = _rms(x1) * g_ref[...] * (1.0 + mod[4:5]) + mod[3:4]
    h2_ref[...] = h2.astype(BF16)
    lane = lax.broadcasted_iota(jnp.int32, (TM, LANES), 1)
    logits = jnp.where(lane < N_EXPERTS, _dot3(h2, rw_ref[...]) + rb_ref[...], -jnp.inf)
    meta = jnp.zeros((TM, LANES), F32)
    tops = []
    for k in range(TOP_K):
        m = jnp.max(logits, axis=-1, keepdims=True)
        idx = jnp.min(jnp.where(logits == m, lane, LANES), axis=-1, keepdims=True)
        hit = lane == idx
        logits = jnp.where(hit, -jnp.inf, logits)
        meta = jnp.where(lane == k, idx.astype(F32), meta)
        tops.append(m)
    ex = [jnp.exp(t - tops[0]) for t in tops]
    den = ex[0] + ex[1] + ex[2] + ex[3]
    for k in range(TOP_K):
        meta = jnp.where(lane == TOP_K + k, ex[k] / den, meta)
    meta_ref[...] = meta
    for t in range(TM // ROW_TILE):
        rows = slice(t * ROW_TILE, (t + 1) * ROW_TILE)
        slot_ref[rows, :], info_ref[t] = _tile_slots(meta[rows])


def _outproj(mix_ctx, mix_dec, w, xs, mod3, gain, rw, rb):
    n_ctx_tiles = N_CTX // TM
    tok = lambda w_: pl.BlockSpec((TM, w_), lambda i: (i, 0))
    ctx = lambda w_: pl.BlockSpec((TM, w_), lambda i: (jnp.minimum(i, n_ctx_tiles - 1), 0))
    dec = lambda w_: pl.BlockSpec((TM, w_), lambda i: (jnp.maximum(i - n_ctx_tiles, 0), 0))
    full = lambda a, b: pl.BlockSpec((a, b), lambda i: (0, 0))
    x_specs = [ctx(D_MODEL), dec(D_MODEL)] if len(xs) == 2 else [tok(D_MODEL)]
    tiles_per_step = TM // ROW_TILE
    return pl.pallas_call(
        _outproj_body,
        grid=(N_TOK // TM,),
        in_specs=[ctx(A_VAL_W), ctx(F_WIDTH), ctx(C_Q_W), dec(A_VAL_W), dec(F_WIDTH), dec(C_Q_W),
                  full(D_MODEL, D_MODEL)] + x_specs + [
                  pl.BlockSpec((1, 6, D_MODEL), lambda i: (_mod_row(i, TM), 0, 0)),
                  full(1, D_MODEL), full(D_MODEL, LANES), full(1, LANES)],
        out_specs=[tok(D_MODEL), tok(D_MODEL), tok(LANES), tok(LANES),
                   pl.BlockSpec((tiles_per_step, 8, LANES), lambda i: (i, 0, 0))],
        out_shape=[jax.ShapeDtypeStruct((N_TOK, D_MODEL), F32),
                   jax.ShapeDtypeStruct((N_TOK, D_MODEL), BF16),
                   jax.ShapeDtypeStruct((N_TOK, LANES), F32),
                   jax.ShapeDtypeStruct((N_TOK, LANES), F32),
                   jax.ShapeDtypeStruct((N_TILES, 8, LANES), F32)],
        compiler_params=_params(("parallel",)),
        name="outproj",
    )(*mix_ctx, *mix_dec, w, *xs, mod3, gain, rw, rb)


def _tile_slots(meta):
    lane = lax.broadcasted_iota(jnp.int32, (ROW_TILE, LANES), 1)
    idx = [meta[:, k:k + 1].astype(jnp.int32) for k in range(TOP_K)]
    sel = jnp.zeros((ROW_TILE, LANES), F32)
    for k in range(TOP_K):
        sel = jnp.where(lane == idx[k], 1.0, sel)
    ri = lax.broadcasted_iota(jnp.int32, (ROW_TILE, ROW_TILE), 0)
    ci = lax.broadcasted_iota(jnp.int32, (ROW_TILE, ROW_TILE), 1)
    lower = jnp.where(ci < ri, 1.0, 0.0).astype(BF16)
    rank = jnp.dot(lower, sel.astype(BF16), preferred_element_type=F32)
    cnt = jnp.sum(sel, axis=0, keepdims=True)
    n8 = jnp.floor((cnt + (RUN_ALIGN - 1)) * (1.0 / RUN_ALIGN)) * RUN_ALIGN
    er = lax.broadcasted_iota(jnp.int32, (LANES, LANES), 0)
    ec = lax.broadcasted_iota(jnp.int32, (LANES, LANES), 1)
    before = jnp.where(er < ec, 1.0, 0.0).astype(BF16)
    off = jnp.dot(jnp.broadcast_to(n8, (8, LANES)).astype(BF16), before, preferred_element_type=F32)
    tot = rank + off[0:1]
    out = jnp.zeros((ROW_TILE, LANES), F32)
    for k in range(TOP_K):
        pk = jnp.sum(jnp.where(lane == idx[k], tot, 0.0), axis=-1, keepdims=True)
        out = jnp.where(lane == k, pk, out)
    row = lax.broadcasted_iota(jnp.int32, (8, LANES), 0)
    return out, jnp.where(row == 0, n8, jnp.where(row == 1, off, 0.0))


def _piece_copies(n_rows, row_a, row_b, make_copy, action, bits):
    m = lax.shift_right_logical(n_rows, RUN_SHIFT)

    def piece(bit):
        rows = RUN_ALIGN << bit
        lo = (m & ((1 << bit) - 1)) * RUN_ALIGN

        @pl.when(((m >> bit) & 1) == 1)
        def _():
            cp = make_copy(pl.multiple_of(row_a + lo, RUN_ALIGN), pl.multiple_of(row_b + lo, RUN_ALIGN), rows)
            if action == "start":
                cp.start()
            else:
                cp.wait()

    small = min(bits, 3)
    for bit in range(small):
        piece(bit)
    if bits > small:
        @pl.when(lax.shift_right_logical(m, small) != 0)
        def _():
            for bit in range(small, bits):
                piece(bit)


def _run_copies(tile, n8_tbl, off_tbl, base_tbl, make_copy, action):
    def per_expert(e, c):
        j = tile * N_EXPERTS + e
        _piece_copies(n8_tbl[j], off_tbl[j], base_tbl[j], make_copy, action, RUN_BITS)
        return c

    lax.fori_loop(0, N_EXPERTS, per_expert, 0)


def _unpack_pairs(u):
    hi = lax.bitcast_convert_type(u & jnp.uint32(0xFFFF0000), F32)
    lo = lax.bitcast_convert_type(u << 16, F32)
    return hi, lo


def _pack_pairs(a, b):
    return lax.bitcast_convert_type(a, jnp.uint32) | (lax.bitcast_convert_type(b, jnp.uint32) >> 16)


def _slot_onehot(slot, weights=None):
    lane = lax.broadcasted_iota(jnp.int32, (ROW_TILE, SORT_ROWS), 1)
    p = jnp.zeros((ROW_TILE, SORT_ROWS), F32)
    for k in range(TOP_K):
        hit = lane == slot[:, k:k + 1].astype(jnp.int32)
        p = jnp.where(hit, 1.0 if weights is None else weights[:, k:k + 1], p)
    return p.astype(BF16)


def _dispatch_body(n8_tbl, off_tbl, base_tbl, tail_n, tail_base, nused, h_ref, slot_ref, xs_hbm,
                   sorted_scr, zero_scr, sem):
    tile = pl.program_id(0)

    @pl.when(tile == 0)
    def _():
        zero_scr[...] = jnp.zeros_like(zero_scr)

        def make_zero_copy(_, dst_row, rows):
            return pltpu.make_async_copy(zero_scr.at[pl.ds(0, rows)], xs_hbm.at[pl.ds(dst_row, rows)], sem.at[2])

        for action in ("start", "wait"):
            def per_expert(e, c):
                _piece_copies(tail_n[e], 0, tail_base[e], make_zero_copy, action, TAIL_BITS)
                return c

            def per_block(b, c):
                cp = make_zero_copy(0, pl.multiple_of(b * MOE_BM, MOE_BM), MOE_BM)
                if action == "start":
                    cp.start()
                else:
                    cp.wait()
                return c

            lax.fori_loop(0, N_EXPERTS, per_expert, 0)
            lax.fori_loop(nused[0], MOE_NB, per_block, 0)

    p = _slot_onehot(slot_ref[...])
    srt = lax.dot_general(p, h_ref[...], (((0,), (0,)), ((), ())), preferred_element_type=F32)
    cur = tile & 1
    sorted_scr[cur] = _pack_pairs(srt[:, :HALF_D], srt[:, HALF_D:])

    def copies_from(buf):
        def make_copy(src_row, dst_row, rows):
            return pltpu.make_async_copy(sorted_scr.at[buf, pl.ds(src_row, rows)],
                                         xs_hbm.at[pl.ds(dst_row, rows)], sem.at[buf])
        return make_copy

    _run_copies(tile, n8_tbl, off_tbl, base_tbl, copies_from(cur), "start")

    @pl.when(tile > 0)
    def _():
        _run_copies(tile - 1, n8_tbl, off_tbl, base_tbl, copies_from(1 - cur), "wait")

    @pl.when(tile == N_TILES - 1)
    def _():
        _run_copies(tile, n8_tbl, off_tbl, base_tbl, copies_from(cur), "wait")


def _dispatch(tables, tails, h2, slots):
    grid_spec = pltpu.PrefetchScalarGridSpec(
        num_scalar_prefetch=6,
        grid=(N_TILES,),
        in_specs=[pl.BlockSpec((ROW_TILE, D_MODEL), lambda i, *_: (i, 0)),
                  pl.BlockSpec((ROW_TILE, LANES), lambda i, *_: (i, 0))],
        out_specs=pl.BlockSpec(memory_space=pl.ANY),
        scratch_shapes=[pltpu.VMEM((2, SORT_ROWS, HALF_D), jnp.uint32),
                        pltpu.VMEM((MOE_BM, HALF_D), jnp.uint32), pltpu.SemaphoreType.DMA((3,))],
    )
    return pl.pallas_call(
        _dispatch_body,
        grid_spec=grid_spec,
        out_shape=jax.ShapeDtypeStruct((MOE_ROWS, HALF_D), jnp.uint32),
        compiler_params=_params(("arbitrary",)),
        name="dispatch",
    )(*tables, *tails, h2, slots)


def _experts_body(bexp_ref, nused_ref, first_ref, next_ref, slot_ref, valid_ref, x_ref, wgu_hbm, bgu_ref, wdn_hbm,
                  bdn_ref, y_ref, wgu32, wdn32, wgu16, wdn16, sem, *, layer):
    del nused_ref
    i = pl.program_id(0)

    def weight_copies(expert, s):
        return (pltpu.make_async_copy(wgu_hbm.at[layer, expert], wgu32.at[s], sem.at[0, s]),
                pltpu.make_async_copy(wdn_hbm.at[layer, expert], wdn32.at[s], sem.at[1, s]))

    @pl.when(first_ref[i] == 1)
    def _():
        s = slot_ref[i]

        @pl.when(i == 0)
        def _():
            for cp in weight_copies(bexp_ref[i], s):
                cp.start()

        for cp in weight_copies(bexp_ref[i], s):
            cp.wait()

        @pl.when(next_ref[i] >= 0)
        def _():
            for cp in weight_copies(next_ref[i], 1 - s):
                cp.start()

        wgu16[...] = wgu32[s].astype(BF16)
        wdn16[...] = wdn32[s].astype(BF16)

    def compute(nrows):
        rows = pl.ds(0, nrows)
        x = jnp.concatenate(_unpack_pairs(x_ref[rows, :]), axis=1).astype(BF16)
        gu = jnp.dot(x, wgu16[...], preferred_element_type=F32) + bgu_ref[0, 0]
        gate = jnp.minimum(gu[:, :D_FF], SWIGLU_LIMIT)
        up = jnp.clip(gu[:, D_FF:], -SWIGLU_LIMIT, SWIGLU_LIMIT)
        glu = gate * _sigmoid(gate * SWIGLU_ALPHA)
        act = ((up + 1.0) * glu).astype(BF16)
        y = jnp.dot(act, wdn16[...], preferred_element_type=F32) + bdn_ref[0, 0]
        y = y.astype(BF16).astype(F32)
        y_ref[rows, :] = _pack_pairs(y[:, :HALF_D], y[:, HALF_D:])

    valid = valid_ref[i]

    @pl.when(valid > MOE_SUB)
    def _():
        compute(MOE_BM)

    @pl.when(valid <= MOE_SUB)
    def _():
        y_ref[pl.ds(MOE_SUB, MOE_SUB), :] = jnp.zeros((MOE_SUB, HALF_D), jnp.uint32)

    @pl.when((valid > 0) & (valid <= MOE_SUB))
    def _():
        compute(MOE_SUB)

    @pl.when(valid == 0)
    def _():
        y_ref[pl.ds(0, MOE_SUB), :] = jnp.zeros((MOE_SUB, HALF_D), jnp.uint32)


def _experts(sched, xs, wgu, bgu, wdn, bdn, layer):
    e = lambda i, be, *_: (layer, be[i], 0, 0)
    grid_spec = pltpu.PrefetchScalarGridSpec(
        num_scalar_prefetch=6,
        grid=(MOE_NB,),
        in_specs=[
            pl.BlockSpec((MOE_BM, HALF_D), lambda i, be, nu, *_: (jnp.minimum(i, nu[0] - 1), 0)),
            pl.BlockSpec(memory_space=pl.ANY),
            pl.BlockSpec((1, 1, 1, 2 * D_FF), e),
            pl.BlockSpec(memory_space=pl.ANY),
            pl.BlockSpec((1, 1, 1, D_MODEL), e),
        ],
        out_specs=pl.BlockSpec((MOE_BM, HALF_D), lambda i, *_: (i, 0)),
        scratch_shapes=[pltpu.VMEM((2, D_MODEL, 2 * D_FF), F32), pltpu.VMEM((2, D_FF, D_MODEL), F32),
                        pltpu.VMEM((D_MODEL, 2 * D_FF), BF16), pltpu.VMEM((D_FF, D_MODEL), BF16),
                        pltpu.SemaphoreType.DMA((2, 2))],
    )
    return pl.pallas_call(
        functools.partial(_experts_body, layer=layer),
        grid_spec=grid_spec,
        out_shape=jax.ShapeDtypeStruct((MOE_ROWS, HALF_D), jnp.uint32),
        compiler_params=_params(("arbitrary",)),
        name="experts",
    )(*sched, xs, wgu, bgu, wdn, bdn)


def _combine_body(n8_tbl, off_tbl, base_tbl, y_hbm, slot_ref, meta_ref, x1_ref, mod_ref, nf_ref, o_ref,
                  ybuf, sem, *, final, tile0, ntiles):
    step = pl.program_id(0)
    tile = tile0 + step
    cur = step & 1

    def fetch(t, buf):
        ybuf[buf] = jnp.zeros((SORT_ROWS, HALF_D), jnp.uint32)
        _run_copies(t, n8_tbl, off_tbl, base_tbl, copies_into(buf), "start")

    def copies_into(buf):
        def make_copy(buf_row, hbm_row, rows):
            return pltpu.make_async_copy(y_hbm.at[pl.ds(hbm_row, rows)],
                                         ybuf.at[buf, pl.ds(buf_row, rows)], sem.at[buf])
        return make_copy

    @pl.when(step == 0)
    def _():
        fetch(tile, cur)

    @pl.when(step + 1 < ntiles)
    def _():
        fetch(tile + 1, 1 - cur)

    _run_copies(tile, n8_tbl, off_tbl, base_tbl, copies_into(cur), "wait")
    pw = _slot_onehot(slot_ref[...], meta_ref[...][:, TOP_K:2 * TOP_K])
    hi, lo = _unpack_pairs(ybuf[cur])
    ff = jnp.concatenate([jnp.dot(pw, hi.astype(BF16), preferred_element_type=F32),
                          jnp.dot(pw, lo.astype(BF16), preferred_element_type=F32)], axis=1)
    x2 = x1_ref[...] + mod_ref[0][5:6] * ff
    if final:
        x2 = _rms(x2) * nf_ref[...]
    o_ref[...] = x2


def _combine(tables, y, slots, meta, x1, mod3, nfinal, *, final, tile0=0, ntiles=N_TILES):
    grid_spec = pltpu.PrefetchScalarGridSpec(
        num_scalar_prefetch=3,
        grid=(ntiles,),
        in_specs=[pl.BlockSpec(memory_space=pl.ANY),
                  pl.BlockSpec((ROW_TILE, LANES), lambda i, *_: (tile0 + i, 0)),
                  pl.BlockSpec((ROW_TILE, LANES), lambda i, *_: (tile0 + i, 0)),
                  pl.BlockSpec((ROW_TILE, D_MODEL), lambda i, *_: (tile0 + i, 0)),
                  pl.BlockSpec((1, 6, D_MODEL), lambda i, *_: (_mod_row(tile0 + i, ROW_TILE), 0, 0)),
                  pl.BlockSpec((1, D_MODEL), lambda i, *_: (0, 0))],
        out_specs=pl.BlockSpec((ROW_TILE, D_MODEL), lambda i, *_: (i, 0)),
        scratch_shapes=[pltpu.VMEM((2, SORT_ROWS, HALF_D), jnp.uint32), pltpu.SemaphoreType.DMA((2,))],
    )
    return pl.pallas_call(
        functools.partial(_combine_body, final=final, tile0=tile0, ntiles=ntiles),
        grid_spec=grid_spec,
        out_shape=jax.ShapeDtypeStruct((ntiles * ROW_TILE, D_MODEL), F32),
        compiler_params=_params(("arbitrary",)),
        name="combine",
    )(*tables, y, slots, meta, x1, mod3, nfinal)


def _cumsum0(x):
    n = x.shape[0]
    tri = jnp.asarray(np.tril(np.ones((n, n), np.bool_)))
    tri = tri.reshape((n, n) + (1,) * (x.ndim - 1))
    return jnp.sum(jnp.where(tri, x[None], 0), axis=1)


def _lookup(table, idx):
    hit = idx[:, None] == jnp.arange(table.shape[0], dtype=idx.dtype)[None, :]
    return jnp.sum(jnp.where(hit, table[None, :], 0), axis=1)


def _state_to_blockdiag(s):
    eye = jnp.eye(A_HEADS, dtype=s.dtype)
    t = jnp.einsum("bdhkv,hg->bdhvgk", s, eye)
    return t.reshape(s.shape[0], 2, A_VAL_W, A_KEY_W)


def kernel(x_prompt, x_sample, state_gla, cache_k, cache_v, c, c_ctx, w_ada, b_ada, norm_mix, norm_ffn,
           norm_final, w_in, w_out, gla_gate_w, gla_gate_b, gla_norm, q_norm, k_norm, router_w, router_b,
           expert_w_gu, expert_b_gu, expert_w_dn, expert_b_dn):
    xs = (x_prompt.reshape(N_CTX, D_MODEL), x_sample.reshape(N_DEC, D_MODEL))
    cond = jnp.zeros((MOD_ROWS, D_MODEL), F32).at[0].set(c_ctx).at[1:1 + DEC_BATCH].set(c)
    mod = _ada(cond, w_ada, b_ada)

    gm256 = jnp.asarray(np.kron(np.eye(A_HEADS), np.ones((A_DV, A_DV))) / A_DV, BF16)
    gm128 = jnp.asarray(np.kron(np.eye(2), np.ones((HEAD_DIM, HEAD_DIM))) / HEAD_DIM, BF16)
    c64, s64 = _dft_mats(F_GROUP_W, F_GROUP_W ** -0.5)
    cb = jnp.asarray(np.kron(np.eye(F_GROUPS), c64), BF16)
    sb = jnp.asarray(np.kron(np.eye(F_GROUPS), s64), BF16)
    dft_ctx = [jnp.asarray(m, BF16) for m in _dft_mats(SEQ, SEQ ** -0.5)]
    dft_dec = [jnp.asarray(m, BF16) for m in _dft_mats(DEC_SEQ, DEC_SEQ ** -0.5)]
    rope_tabs = _rope_tables()
    zero_state = jnp.zeros((BATCH, 2, A_VAL_W, A_KEY_W), F32)

    new_s, new_k, new_v = [], [], []
    for l in range(DEPTH):
        mod3 = mod[l].reshape(MOD_ROWS, 6, D_MODEL)
        w_l = w_in[l]
        w_r = jnp.concatenate([w_l[:, :768], w_l[:, 800:], w_l[:, 768:800],
                               jnp.zeros((D_MODEL, PROJ_W - 1824), F32)], axis=1).astype(BF16)
        proj = _inproj(xs, mod3, norm_mix[l][None, :], w_r)

        wg = jnp.zeros((LANES, 2 * A_KEY_W), F32)
        wg = wg.at[0:GATE_RANK, 0:A_KEY_W].set(gla_gate_w[l, 0])
        wg = wg.at[GATE_RANK:2 * GATE_RANK, A_KEY_W:].set(gla_gate_w[l, 1])
        gb = gla_gate_b[l].reshape(1, 2 * A_KEY_W)
        gn = jnp.tile(gla_norm[l], A_HEADS)[None, :]
        s0_dec = _state_to_blockdiag(state_gla[:, l])
        oa_ctx, sfin = _gla(proj, wg, gb, zero_state, gn, gm256, nseq=BATCH, seq=SEQ, row_block0=0)
        oa_dec, _ = _gla(proj, wg, gb, s0_dec, gn, gm256, nseq=DEC_BATCH, seq=DEC_SEQ,
                         row_block0=N_CTX // DEC_SEQ)
        of_ctx = _fnet(proj, cb, sb, *dft_ctx, nseq=BATCH, seq=SEQ, row_block0=0)
        of_dec = _fnet(proj, cb, sb, *dft_dec, nseq=DEC_BATCH, seq=DEC_SEQ, row_block0=N_CTX // DEC_SEQ)
        qn = jnp.tile(q_norm[l], 2)[None, :]
        kn = jnp.tile(k_norm[l], 2)[None, :]
        oc_ctx, ka = _attn(proj, qn, kn, gm128, nseq=BATCH, seq=SEQ, row_block0=0)
        past = (cache_k[:, l].reshape(DEC_BATCH, PAST_LEN, C_KV_W),
                cache_v[:, l].reshape(DEC_BATCH, PAST_LEN, C_KV_W))
        oc_dec, _ = _attn(proj, qn, kn, gm128, nseq=DEC_BATCH, seq=DEC_SEQ,
                          row_block0=N_CTX // DEC_SEQ, past=past, tables=rope_tabs)
        new_s.append(sfin.reshape(BATCH, 2, A_HEADS, A_DK, A_DV))
        new_k.append(ka.reshape(BATCH, SEQ, C_KV_HEADS, HEAD_DIM))
        new_v.append(proj[:N_CTX, COL_CV:COL_CV + C_KV_W].reshape(BATCH, SEQ, C_KV_HEADS, HEAD_DIM))

        rw = jnp.zeros((D_MODEL, LANES), F32).at[:, :N_EXPERTS].set(router_w[l])
        rb = jnp.zeros((1, LANES), F32).at[0, :N_EXPERTS].set(router_b[l])
        x1, h2, meta, slots, info = _outproj((oa_ctx, of_ctx, oc_ctx), (oa_dec, of_dec, oc_dec),
                                             w_out[l].astype(BF16), xs, mod3, norm_ffn[l][None, :], rw, rb)

        n8 = info[:, 0, :N_EXPERTS].astype(jnp.int32)
        off = info[:, 1, :N_EXPERTS].astype(jnp.int32)
        cnt = jnp.sum(n8, axis=0)
        pcnt = (cnt + MOE_BM - 1) // MOE_BM * MOE_BM
        pend = _cumsum0(pcnt)
        base = (pend - pcnt)[None, :] + _cumsum0(n8) - n8
        tables = (n8.reshape(-1), off.reshape(-1), base.reshape(-1))
        blk = jnp.arange(MOE_NB, dtype=jnp.int32)
        blk_exp = jnp.minimum(jnp.sum((pend[None, :] <= (blk * MOE_BM)[:, None]).astype(jnp.int32), axis=1),
                              N_EXPERTS - 1)
        nused = (pend[-1:] // MOE_BM).astype(jnp.int32)
        tails = (pcnt - cnt, pend - (pcnt - cnt), nused)
        prev_exp = jnp.concatenate([jnp.full((1,), -1, jnp.int32), blk_exp[:-1]])
        first = ((blk < nused[0]) & (blk_exp != prev_exp)).astype(jnp.int32)
        end_blk = _lookup(pend // MOE_BM, blk_exp)
        nxt = jnp.where(end_blk < nused[0], _lookup(blk_exp, jnp.minimum(end_blk, MOE_NB - 1)), -1)
        wslot = (_cumsum0(first) - 1) & 1
        start_blk = (pend - pcnt) // MOE_BM
        valid = jnp.where(blk < nused[0],
                          jnp.clip(_lookup(cnt, blk_exp) - (blk - _lookup(start_blk, blk_exp)) * MOE_BM,
                                   0, MOE_BM), 0)
        sched = (blk_exp, nused, first, nxt, wslot, valid)

        xs = _dispatch(tables, tails, h2, slots)
        y = _experts(sched, xs, expert_w_gu,
                     expert_b_gu.reshape(DEPTH, N_EXPERTS, 1, 2 * D_FF), expert_w_dn,
                     expert_b_dn.reshape(DEPTH, N_EXPERTS, 1, D_MODEL), l)
        comb = functools.partial(_combine, tables, y, slots, meta, x1, mod3, norm_final[None, :])
        if l < DEPTH - 1:
            xs = (comb(final=False),)
        else:
            ctx_tiles = N_CTX // ROW_TILE
            xs = (comb(final=True, tile0=0, ntiles=ctx_tiles),
                  comb(final=True, tile0=ctx_tiles, ntiles=N_TILES - ctx_tiles))

    y_prompt = xs[0].reshape(BATCH, SEQ, D_MODEL)
    y_sample = xs[1].reshape(DEC_BATCH, DEC_SEQ, D_MODEL)
    return (y_prompt, y_sample, jnp.stack(new_s, axis=1), jnp.stack(new_k, axis=1), jnp.stack(new_v, axis=1))
```

```python
import functools

import jax
import jax.numpy as jnp
import numpy as np
from jax import lax
from jax.experimental import pallas as pl
from jax.experimental.pallas import tpu as pltpu

D_MODEL = 1024
BATCH = 32
SEQ = 256
DEPTH = 2
DEC_BATCH = 8
DEC_SEQ = 1024
PAST_LEN = 512
GRID_W = 64
A_HEADS = 4
A_DV = 64
A_DK = 32
A_KEY_W = 128
A_VAL_W = 256
GATE_RANK = 16
GATE_TEMP = 16.0
GLA_CHUNK = 32
F_GROUPS = 4
F_GROUP_W = 64
F_WIDTH = 256
HEAD_DIM = 64
C_HEADS = 8
C_KV_HEADS = 2
C_Q_W = 512
C_KV_W = 128
ROPE_THETA = 10000.0
N_EXPERTS = 32
TOP_K = 4
D_FF = 1024
SWIGLU_LIMIT = 7.0
SWIGLU_ALPHA = 1.702
EPS = 1e-6

N_CTX = BATCH * SEQ
N_DEC = DEC_BATCH * DEC_SEQ
N_TOK = N_CTX + N_DEC
MOD_ROWS = 16

COL_GLA = 0
COL_PF = 768
COL_CQ = 1024
COL_CK = 1536
COL_CV = 1664
COL_LR = 1792
PROJ_W = 1920

LANES = 128
TM = 512
GLA_TILE = 256
Q_TILE = 512
MOE_BM = 512
MOE_SUB = MOE_BM // 2
ROW_TILE = 256
N_TILES = N_TOK // ROW_TILE
RUN_ALIGN = 8
RUN_SHIFT = 3
RUN_BITS = 6
TAIL_BITS = 6
SORT_ROWS = ROW_TILE * TOP_K + N_EXPERTS * RUN_ALIGN
HALF_D = D_MODEL // 2
MOE_NB = (N_TOK * TOP_K + N_TILES * N_EXPERTS * (RUN_ALIGN - 1)) // MOE_BM + N_EXPERTS
MOE_ROWS = MOE_NB * MOE_BM
VMEM_LIMIT = 56 * 1024 * 1024

F32 = jnp.float32
BF16 = jnp.bfloat16


def _params(sem, vmem=VMEM_LIMIT):
    return pltpu.CompilerParams(dimension_semantics=sem, vmem_limit_bytes=vmem)


def _bdot(a, b):
    return jnp.dot(a.astype(BF16), b.astype(BF16), preferred_element_type=F32)


def _split(a):
    hi = a.astype(BF16)
    lo = (a - hi.astype(F32)).astype(BF16)
    return hi, lo


def _dot3(a, b):
    ah, al = _split(a)
    bh, bl = _split(b)
    d = functools.partial(jnp.dot, preferred_element_type=F32)
    return d(ah, bh) + (d(ah, bl) + d(al, bh))


def _dot2(a, b16):
    ah, al = _split(a)
    d = functools.partial(jnp.dot, preferred_element_type=F32)
    return d(ah, b16) + d(al, b16)


def _sigmoid(x):
    return 1.0 / (1.0 + jnp.exp(-x))


def _mod_row(i, tm):
    tok = i * tm
    return jnp.where(tok < N_CTX, 0, 1 + (tok - N_CTX) // DEC_SEQ)


def _ada_body(c_ref, w_ref, b_ref, o_ref):
    c = c_ref[...]
    o_ref[0] = _dot3(c * _sigmoid(c), w_ref[0]) + b_ref[0]


def _ada(cond, w_ada, b_ada):
    return pl.pallas_call(
        _ada_body,
        grid=(DEPTH, 6),
        in_specs=[
            pl.BlockSpec((MOD_ROWS, D_MODEL), lambda l, j: (0, 0)),
            pl.BlockSpec((1, D_MODEL, D_MODEL), lambda l, j: (l, 0, j)),
            pl.BlockSpec((1, 1, D_MODEL), lambda l, j: (l, 0, j)),
        ],
        out_specs=pl.BlockSpec((1, MOD_ROWS, D_MODEL), lambda l, j: (l, 0, j)),
        out_shape=jax.ShapeDtypeStruct((DEPTH, MOD_ROWS, 6 * D_MODEL), F32),
        compiler_params=_params(("parallel", "parallel")),
        name="ada",
    )(cond, w_ada, b_ada.reshape(DEPTH, 1, 6 * D_MODEL))


def _rms(x):
    return x * lax.rsqrt(jnp.mean(x * x, axis=-1, keepdims=True) + EPS)


def _inproj_body(*refs):
    *x_refs, mod_ref, g_ref, w_ref, o_ref = refs
    if len(x_refs) == 2:
        x = jnp.where(pl.program_id(0) < N_CTX // TM, x_refs[0][...], x_refs[1][...])
    else:
        x = x_refs[0][...]
    mod = mod_ref[0]
    h = _rms(x) * g_ref[...] * (1.0 + mod[1:2]) + mod[0:1]
    o_ref[...] = jnp.dot(h.astype(BF16), w_ref[...], preferred_element_type=F32)


def _inproj(xs, mod3, gain, w):
    n_ctx_tiles = N_CTX // TM
    if len(xs) == 2:
        x_specs = [pl.BlockSpec((TM, D_MODEL), lambda i: (jnp.minimum(i, n_ctx_tiles - 1), 0)),
                   pl.BlockSpec((TM, D_MODEL), lambda i: (jnp.maximum(i - n_ctx_tiles, 0), 0))]
    else:
        x_specs = [pl.BlockSpec((TM, D_MODEL), lambda i: (i, 0))]
    return pl.pallas_call(
        _inproj_body,
        grid=(N_TOK // TM,),
        in_specs=x_specs + [
            pl.BlockSpec((1, 6, D_MODEL), lambda i: (_mod_row(i, TM), 0, 0)),
            pl.BlockSpec((1, D_MODEL), lambda i: (0, 0)),
            pl.BlockSpec((D_MODEL, PROJ_W), lambda i: (0, 0)),
        ],
        out_specs=pl.BlockSpec((TM, PROJ_W), lambda i: (i, 0)),
        out_shape=jax.ShapeDtypeStruct((N_TOK, PROJ_W), F32),
        compiler_params=_params(("parallel",)),
        name="inproj",
    )(*xs, mod3, gain, w)


def _log_sigmoid(x):
    return jnp.minimum(x, 0.0) - jnp.log(1.0 + jnp.exp(-jnp.abs(x)))


def _chunk_cumsum(x, pos, forward):
    rows = x.shape[0]
    for s in (1, 2, 4, 8, 16):
        if forward:
            x = x + jnp.where(pos >= s, pltpu.roll(x, s, 0), 0.0)
        else:
            x = x + jnp.where(pos < GLA_CHUNK - s, pltpu.roll(x, rows - s, 0), 0.0)
    return x


def _gla_tile(q, k, v16, la, st, forward):
    R = GLA_TILE
    nch = R // GLA_CHUNK
    pos = lax.broadcasted_iota(jnp.int32, (R, 1), 0) & (GLA_CHUNK - 1)
    b = _chunk_cumsum(la, pos, forward)
    b3 = b.reshape(nch, GLA_CHUNK, A_KEY_W)
    tot3 = b3[:, GLA_CHUNK - 1:GLA_CHUNK, :] if forward else b3[:, 0:1, :]
    totb = jnp.broadcast_to(tot3, (nch, GLA_CHUNK, A_KEY_W)).reshape(R, A_KEY_W)
    qt = q * (A_DK ** -0.5) * jnp.exp(b)
    kt = (k * jnp.exp(-b)).astype(BF16)
    kend = (k * jnp.exp(totb - b)).astype(BF16)
    dec3 = jnp.exp(tot3)

    ri = lax.broadcasted_iota(jnp.int32, (R, R), 0)
    ci = lax.broadcasted_iota(jnp.int32, (R, R), 1)
    tri = (ci <= ri) if forward else (ci >= ri)
    mask = ((ri >> 5) == (ci >> 5)) & tri
    lane_k = lax.broadcasted_iota(jnp.int32, (R, A_KEY_W), 1) >> 5
    lane_v = lax.broadcasted_iota(jnp.int32, (R, A_VAL_W), 1) >> 6

    o = jnp.zeros((R, A_VAL_W), F32)
    for h in range(A_HEADS):
        qh = jnp.where(lane_k == h, qt, 0.0).astype(BF16)
        sc = lax.dot_general(qh, kt, (((1,), (1,)), ((), ())), preferred_element_type=F32)
        p = jnp.where(mask, sc, 0.0).astype(BF16)
        oh = jnp.dot(p, v16, preferred_element_type=F32)
        o = jnp.where(lane_v == h, oh, o)

    sr = lax.broadcasted_iota(jnp.int32, (A_VAL_W, A_KEY_W), 0) >> 6
    sl = lax.broadcasted_iota(jnp.int32, (A_VAL_W, A_KEY_W), 1) >> 5
    smask = sr == sl
    qt16 = qt.astype(BF16)
    inter = [None] * nch
    order = range(nch) if forward else range(nch - 1, -1, -1)
    for c in order:
        lo = c * GLA_CHUNK
        inter[c] = lax.dot_general(qt16[lo:lo + GLA_CHUNK], st.astype(BF16),
                                   (((1,), (1,)), ((), ())), preferred_element_type=F32)
        ds = lax.dot_general(v16[lo:lo + GLA_CHUNK], kend[lo:lo + GLA_CHUNK],
                             (((0,), (0,)), ((), ())), preferred_element_type=F32)
        st = st * dec3[c] + jnp.where(smask, ds, 0.0)
    return o + jnp.concatenate(inter, axis=0), st


def _gla_body(gla_ref, lr_ref, wg_ref, gb_ref, s0_ref, gn_ref, gm_ref, o_ref, sfin_ref, la_scr, ob_scr, *, seq):
    ntile = seq // GLA_TILE
    logit = _bdot(lr_ref[...], wg_ref[...]) + gb_ref[...]
    la_scr[...] = _log_sigmoid(logit) * (1.0 / GATE_TEMP)

    def step(i, states):
        new_states = []
        for d, forward in ((0, True), (1, False)):
            t = i if forward else ntile - 1 - i
            rows = pl.ds(pl.multiple_of(t * GLA_TILE, GLA_TILE), GLA_TILE)
            q = gla_ref[rows, 0:128]
            k = gla_ref[rows, 128:256]
            v16 = gla_ref[rows, 256:512].astype(BF16)
            la = la_scr[rows, d * A_KEY_W:(d + 1) * A_KEY_W]
            o, st = _gla_tile(q, k, v16, la, states[d], forward)
            if forward:
                o_ref[rows, :] = o
            else:
                ob_scr[rows, :] = o
            new_states.append(st)
        return tuple(new_states)

    states = lax.fori_loop(0, ntile, step, (s0_ref[0, 0], s0_ref[0, 1]))
    for d in range(2):
        s = states[d].T
        sfin_ref[0, d] = (s[:, 0:A_DV] + s[:, A_DV:2 * A_DV]) + (s[:, 2 * A_DV:3 * A_DV] + s[:, 3 * A_DV:])

    o = o_ref[...] + ob_scr[...]
    ms = _dot2(o * o, gm_ref[...])
    g = gla_ref[:, 512:768]
    o_ref[...] = o * lax.rsqrt(ms + EPS) * gn_ref[...] * (g * _sigmoid(g))


def _gla(proj, wg, gb, s0t, gnorm, gmat, *, nseq, seq, row_block0):
    in_specs = [
        pl.BlockSpec((seq, 768), lambda b: (row_block0 + b, 0)),
        pl.BlockSpec((seq, LANES), lambda b: (row_block0 + b, COL_LR // LANES)),
        pl.BlockSpec((LANES, 2 * A_KEY_W), lambda b: (0, 0)),
        pl.BlockSpec((1, 2 * A_KEY_W), lambda b: (0, 0)),
        pl.BlockSpec((1, 2, A_VAL_W, A_KEY_W), lambda b: (b, 0, 0, 0)),
        pl.BlockSpec((1, A_VAL_W), lambda b: (0, 0)),
        pl.BlockSpec((A_VAL_W, A_VAL_W), lambda b: (0, 0)),
    ]
    return pl.pallas_call(
        functools.partial(_gla_body, seq=seq),
        grid=(nseq,),
        in_specs=in_specs,
        out_specs=[
            pl.BlockSpec((seq, A_VAL_W), lambda b: (b, 0)),
            pl.BlockSpec((1, 2, A_KEY_W, A_DV), lambda b: (b, 0, 0, 0)),
        ],
        out_shape=[
            jax.ShapeDtypeStruct((nseq * seq, A_VAL_W), F32),
            jax.ShapeDtypeStruct((nseq, 2, A_KEY_W, A_DV), F32),
        ],
        scratch_shapes=[pltpu.VMEM((seq, 2 * A_KEY_W), F32), pltpu.VMEM((seq, A_VAL_W), F32)],
        compiler_params=_params(("parallel",)),
        name=f"gla{seq}",
    )(proj, proj, wg, gb, s0t, gnorm, gmat)


def _fnet_body(x_ref, cb_ref, sb_ref, ct_ref, st_ref, o_ref):
    x = x_ref[...].astype(BF16)
    xc = jnp.dot(x, cb_ref[...], preferred_element_type=F32).astype(BF16)
    xs = jnp.dot(x, sb_ref[...], preferred_element_type=F32).astype(BF16)
    o_ref[...] = (jnp.dot(ct_ref[...], xc, preferred_element_type=F32)
                  - jnp.dot(st_ref[...], xs, preferred_element_type=F32))


def _fnet(proj, cb, sb, ct, st, *, nseq, seq, row_block0):
    in_specs = [
        pl.BlockSpec((seq, F_WIDTH), lambda b: (row_block0 + b, COL_PF // F_WIDTH)),
        pl.BlockSpec((F_WIDTH, F_WIDTH), lambda b: (0, 0)),
        pl.BlockSpec((F_WIDTH, F_WIDTH), lambda b: (0, 0)),
        pl.BlockSpec((seq, seq), lambda b: (0, 0)),
        pl.BlockSpec((seq, seq), lambda b: (0, 0)),
    ]
    return pl.pallas_call(
        _fnet_body,
        grid=(nseq,),
        in_specs=in_specs,
        out_specs=pl.BlockSpec((seq, F_WIDTH), lambda b: (b, 0)),
        out_shape=jax.ShapeDtypeStruct((nseq * seq, F_WIDTH), F32),
        compiler_params=_params(("parallel",)),
        name=f"fnet{seq}",
    )(proj, cb, sb, ct, st)


def _dft_mats(n, scale):
    jk = np.outer(np.arange(n), np.arange(n)) % n
    ang = jk.astype(np.float64) * (2.0 * np.pi / n)
    return np.cos(ang) * scale, np.sin(ang) * scale


def _head_rms(x, gm16, gain):
    ms = _dot2(x * x, gm16)
    return x * lax.rsqrt(ms + EPS) * gain


def _rope(x, cos, sin_signed):
    w = x.shape[1]
    lane = lax.broadcasted_iota(jnp.int32, x.shape, 1)
    partner = jnp.where((lane & 31) < 16, pltpu.roll(x, w - 16, 1), pltpu.roll(x, 16, 1))
    return x * cos + partner * sin_signed


def _attn_body(*refs, rope, n_new, n_past):
    if n_past:
        (q_ref, k_ref, v_ref, ck_ref, cv_ref, cq_ref, sq_ref, cosk_ref, sink_ref,
         qn_ref, kn_ref, gm_ref, o_ref, ka_ref, k_scr, ks_scr, v_scr, vs_scr) = refs
    else:
        (q_ref, k_ref, v_ref, qn_ref, kn_ref, gm_ref, o_ref, ka_ref,
         k_scr, ks_scr, v_scr, vs_scr) = refs
    gm = gm_ref[...]

    @pl.when(pl.program_id(1) == 0)
    def _():
        kn = _head_rms(k_ref[...], gm, kn_ref[...])
        ka_ref[...] = kn
        if rope:
            kn = _rope(kn, cosk_ref[...], sink_ref[...])
        v = v_ref[...]
        new = pl.ds(0, n_new)
        k_scr[new, :] = kn.astype(BF16)
        ks_scr[new, :] = pltpu.roll(kn, HEAD_DIM, 1).astype(BF16)
        v_scr[new, :] = v.astype(BF16)
        vs_scr[new, :] = pltpu.roll(v, HEAD_DIM, 1).astype(BF16)
        if n_past:
            past = pl.ds(n_new, n_past)
            ck = ck_ref[0]
            cv = cv_ref[0]
            k_scr[past, :] = ck.astype(BF16)
            ks_scr[past, :] = pltpu.roll(ck, HEAD_DIM, 1).astype(BF16)
            v_scr[past, :] = cv.astype(BF16)
            vs_scr[past, :] = pltpu.roll(cv, HEAD_DIM, 1).astype(BF16)

    q_tile = q_ref.shape[0]
    lane = lax.broadcasted_iota(jnp.int32, (q_tile, LANES), 1)
    low = lane < HEAD_DIM
    for kv in range(C_KV_HEADS):
        q2 = []
        for jp in (2 * kv, 2 * kv + 1):
            q = _head_rms(q_ref[:, jp * LANES:(jp + 1) * LANES], gm, qn_ref[...])
            if rope:
                q = _rope(q, cq_ref[...], sq_ref[...])
            q2.append(q * (HEAD_DIM ** -0.5))
        halves = []
        for half in range(2):
            swapped = (half == 1) != (kv == 1)
            kk = ks_scr[...] if swapped else k_scr[...]
            vv = vs_scr[...] if swapped else v_scr[...]
            keep = low if half == 0 else ~low
            qm = jnp.concatenate([jnp.where(keep, q, 0.0) for q in q2], axis=0).astype(BF16)
            s = lax.dot_general(qm, kk, (((1,), (1,)), ((), ())), preferred_element_type=F32)
            p = jnp.exp(s - jnp.max(s, axis=-1, keepdims=True))
            den = jnp.sum(p, axis=-1, keepdims=True)
            halves.append(jnp.dot(p.astype(BF16), vv, preferred_element_type=F32) / den)
        for n, jp in enumerate((2 * kv, 2 * kv + 1)):
            rows = slice(n * q_tile, (n + 1) * q_tile)
            o_ref[:, jp * LANES:(jp + 1) * LANES] = jnp.where(low, halves[0][rows], halves[1][rows])


def _attn(proj, qn, kn, gm, *, nseq, seq, row_block0, past=None, tables=None):
    q_tile = min(seq, Q_TILE)
    nq = seq // q_tile
    n_past = 0 if past is None else past[0].shape[1]
    qrow = lambda b, j: ((row_block0 + b) * nq + j, COL_CQ // C_Q_W)
    in_specs = [
        pl.BlockSpec((q_tile, C_Q_W), qrow),
        pl.BlockSpec((seq, C_KV_W), lambda b, j: (row_block0 + b, COL_CK // C_KV_W)),
        pl.BlockSpec((seq, C_KV_W), lambda b, j: (row_block0 + b, COL_CV // C_KV_W)),
    ]
    args = [proj, proj, proj]
    if past is not None:
        in_specs += [pl.BlockSpec((1, n_past, C_KV_W), lambda b, j: (b, 0, 0))] * 2
        in_specs += [pl.BlockSpec((q_tile, LANES), lambda b, j: (j, 0))] * 2
        in_specs += [pl.BlockSpec((seq, LANES), lambda b, j: (0, 0))] * 2
        args += [past[0], past[1], tables[0], tables[1], tables[0], tables[1]]
    in_specs += [
        pl.BlockSpec((1, LANES), lambda b, j: (0, 0)),
        pl.BlockSpec((1, LANES), lambda b, j: (0, 0)),
        pl.BlockSpec((LANES, LANES), lambda b, j: (0, 0)),
    ]
    args += [qn, kn, gm]
    s_all = seq + n_past
    return pl.pallas_call(
        functools.partial(_attn_body, rope=past is not None, n_new=seq, n_past=n_past),
        grid=(nseq, nq),
        in_specs=in_specs,
        out_specs=[
            pl.BlockSpec((q_tile, C_Q_W), lambda b, j: (b * nq + j, 0)),
            pl.BlockSpec((seq, C_KV_W), lambda b, j: (b, 0)),
        ],
        out_shape=[
            jax.ShapeDtypeStruct((nseq * seq, C_Q_W), F32),
            jax.ShapeDtypeStruct((nseq * seq, C_KV_W), F32),
        ],
        scratch_shapes=[pltpu.VMEM((s_all, C_KV_W), BF16)] * 4,
        compiler_params=_params(("parallel", "arbitrary")),
        name=f"attn{seq}",
    )(*args)


def _rope_tables():
    t = np.arange(DEC_SEQ)
    rows = (t // GRID_W).astype(np.float32)
    cols = (t % GRID_W).astype(np.float32)
    nf = HEAD_DIM // 4
    inv_freq = (ROPE_THETA ** (-np.arange(nf, dtype=np.float32) / nf)).astype(np.float32)
    d = np.arange(LANES)
    pos = np.where(((d % HEAD_DIM) < HEAD_DIM // 2)[None, :], rows[:, None], cols[:, None])
    ang = (pos * inv_freq[d % nf][None, :]).astype(np.float32)
    sign = np.where((d % 32) < 16, -1.0, 1.0)[None, :]
    return jnp.asarray(np.cos(ang), F32), jnp.asarray(np.sin(ang) * sign, F32)


def _outproj_body(oa_c, of_c, oc_c, oa_d, of_d, oc_d, w_ref, *refs):
    *x_refs, mod_ref, g_ref, rw_ref, rb_ref, x1_ref, h2_ref, meta_ref, slot_ref, info_ref = refs
    mod = mod_ref[0]
    is_ctx = pl.program_id(0) < N_CTX // TM
    pick = lambda c_ref, d_ref: jnp.where(is_ctx, c_ref[...], d_ref[...]).astype(BF16)
    x_in = x_refs[0][...] if len(x_refs) == 1 else jnp.where(is_ctx, x_refs[0][...], x_refs[1][...])
    mixed = (jnp.dot(pick(oa_c, oa_d), w_ref[0:256, :], preferred_element_type=F32)
             + jnp.dot(pick(of_c, of_d), w_ref[256:512, :], preferred_element_type=F32)
             + jnp.dot(pick(oc_c, oc_d), w_ref[512:1024, :], preferred_element_type=F32))
    x1 = x_in + mod[2:3] * mixed
    x1_ref[...] = x1
    h2 = _rms(x1) * g_ref[...] * (1.0 + mod[4:5]) + mod[3:4]
    h2_ref[...] = h2.astype(BF16)
    lane = lax.broadcasted_iota(jnp.int32, (TM, LANES), 1)
    logits = jnp.where(lane < N_EXPERTS, _dot3(h2, rw_ref[...]) + rb_ref[...], -jnp.inf)
    meta = jnp.zeros((TM, LANES), F32)
    tops = []
    for k in range(TOP_K):
        m = jnp.max(logits, axis=-1, keepdims=True)
        idx = jnp.min(jnp.where(logits == m, lane, LANES), axis=-1, keepdims=True)
        hit = lane == idx
        logits = jnp.where(hit, -jnp.inf, logits)
        meta = jnp.where(lane == k, idx.astype(F32), meta)
        tops.append(m)
    ex = [jnp.exp(t - tops[0]) for t in tops]
    den = ex[0] + ex[1] + ex[2] + ex[3]
    for k in range(TOP_K):
        meta = jnp.where(lane == TOP_K + k, ex[k] / den, meta)
    meta_ref[...] = meta
    for t in range(TM // ROW_TILE):
        rows = slice(t * ROW_TILE, (t + 1) * ROW_TILE)
        slot_ref[rows, :], info_ref[t] = _tile_slots(meta[rows])


def _outproj(mix_ctx, mix_dec, w, xs, mod3, gain, rw, rb):
    n_ctx_tiles = N_CTX // TM
    tok = lambda w_: pl.BlockSpec((TM, w_), lambda i: (i, 0))
    ctx = lambda w_: pl.BlockSpec((TM, w_), lambda i: (jnp.minimum(i, n_ctx_tiles - 1), 0))
    dec = lambda w_: pl.BlockSpec((TM, w_), lambda i: (jnp.maximum(i - n_ctx_tiles, 0), 0))
    full = lambda a, b: pl.BlockSpec((a, b), lambda i: (0, 0))
    x_specs = [ctx(D_MODEL), dec(D_MODEL)] if len(xs) == 2 else [tok(D_MODEL)]
    tiles_per_step = TM // ROW_TILE
    return pl.pallas_call(
        _outproj_body,
        grid=(N_TOK // TM,),
        in_specs=[ctx(A_VAL_W), ctx(F_WIDTH), ctx(C_Q_W), dec(A_VAL_W), dec(F_WIDTH), dec(C_Q_W),
                  full(D_MODEL, D_MODEL)] + x_specs + [
                  pl.BlockSpec((1, 6, D_MODEL), lambda i: (_mod_row(i, TM), 0, 0)),
                  full(1, D_MODEL), full(D_MODEL, LANES), full(1, LANES)],
        out_specs=[tok(D_MODEL), tok(D_MODEL), tok(LANES), tok(LANES),
                   pl.BlockSpec((tiles_per_step, 8, LANES), lambda i: (i, 0, 0))],
        out_shape=[jax.ShapeDtypeStruct((N_TOK, D_MODEL), F32),
                   jax.ShapeDtypeStruct((N_TOK, D_MODEL), BF16),
                   jax.ShapeDtypeStruct((N_TOK, LANES), F32),
                   jax.ShapeDtypeStruct((N_TOK, LANES), F32),
                   jax.ShapeDtypeStruct((N_TILES, 8, LANES), F32)],
        compiler_params=_params(("parallel",)),
        name="outproj",
    )(*mix_ctx, *mix_dec, w, *xs, mod3, gain, rw, rb)


def _tile_slots(meta):
    lane = lax.broadcasted_iota(jnp.int32, (ROW_TILE, LANES), 1)
    idx = [meta[:, k:k + 1].astype(jnp.int32) for k in range(TOP_K)]
    sel = jnp.zeros((ROW_TILE, LANES), F32)
    for k in range(TOP_K):
        sel = jnp.where(lane == idx[k], 1.0, sel)
    ri = lax.broadcasted_iota(jnp.int32, (ROW_TILE, ROW_TILE), 0)
    ci = lax.broadcasted_iota(jnp.int32, (ROW_TILE, ROW_TILE), 1)
    lower = jnp.where(ci < ri, 1.0, 0.0).astype(BF16)
    rank = jnp.dot(lower, sel.astype(BF16), preferred_element_type=F32)
    cnt = jnp.sum(sel, axis=0, keepdims=True)
    n8 = jnp.floor((cnt + (RUN_ALIGN - 1)) * (1.0 / RUN_ALIGN)) * RUN_ALIGN
    er = lax.broadcasted_iota(jnp.int32, (LANES, LANES), 0)
    ec = lax.broadcasted_iota(jnp.int32, (LANES, LANES), 1)
    before = jnp.where(er < ec, 1.0, 0.0).astype(BF16)
    off = jnp.dot(jnp.broadcast_to(n8, (8, LANES)).astype(BF16), before, preferred_element_type=F32)
    tot = rank + off[0:1]
    out = jnp.zeros((ROW_TILE, LANES), F32)
    for k in range(TOP_K):
        pk = jnp.sum(jnp.where(lane == idx[k], tot, 0.0), axis=-1, keepdims=True)
        out = jnp.where(lane == k, pk, out)
    row = lax.broadcasted_iota(jnp.int32, (8, LANES), 0)
    return out, jnp.where(row == 0, n8, jnp.where(row == 1, off, 0.0))


def _piece_copies(n_rows, row_a, row_b, make_copy, action, bits):
    m = lax.shift_right_logical(n_rows, RUN_SHIFT)

    def piece(bit):
        rows = RUN_ALIGN << bit
        lo = (m & ((1 << bit) - 1)) * RUN_ALIGN

        @pl.when(((m >> bit) & 1) == 1)
        def _():
            cp = make_copy(pl.multiple_of(row_a + lo, RUN_ALIGN), pl.multiple_of(row_b + lo, RUN_ALIGN), rows)
            if action == "start":
                cp.start()
            else:
                cp.wait()

    small = min(bits, 3)
    for bit in range(small):
        piece(bit)
    if bits > small:
        @pl.when(lax.shift_right_logical(m, small) != 0)
        def _():
            for bit in range(small, bits):
                piece(bit)


def _run_copies(tile, n8_tbl, off_tbl, base_tbl, make_copy, action):
    def per_expert(e, c):
        j = tile * N_EXPERTS + e
        _piece_copies(n8_tbl[j], off_tbl[j], base_tbl[j], make_copy, action, RUN_BITS)
        return c

    lax.fori_loop(0, N_EXPERTS, per_expert, 0)


def _unpack_pairs(u):
    hi = lax.bitcast_convert_type(u & jnp.uint32(0xFFFF0000), F32)
    lo = lax.bitcast_convert_type(u << 16, F32)
    return hi, lo


def _pack_pairs(a, b):
    return lax.bitcast_convert_type(a, jnp.uint32) | (lax.bitcast_convert_type(b, jnp.uint32) >> 16)


def _slot_onehot(slot, weights=None):
    lane = lax.broadcasted_iota(jnp.int32, (ROW_TILE, SORT_ROWS), 1)
    p = jnp.zeros((ROW_TILE, SORT_ROWS), F32)
    for k in range(TOP_K):
        hit = lane == slot[:, k:k + 1].astype(jnp.int32)
        p = jnp.where(hit, 1.0 if weights is None else weights[:, k:k + 1], p)
    return p.astype(BF16)


def _dispatch_body(n8_tbl, off_tbl, base_tbl, tail_n, tail_base, nused, h_ref, slot_ref, xs_hbm,
                   sorted_scr, zero_scr, sem):
    tile = pl.program_id(0)

    @pl.when(tile == 0)
    def _():
        zero_scr[...] = jnp.zeros_like(zero_scr)

        def make_zero_copy(_, dst_row, rows):
            return pltpu.make_async_copy(zero_scr.at[pl.ds(0, rows)], xs_hbm.at[pl.ds(dst_row, rows)], sem.at[2])

        for action in ("start", "wait"):
            def per_expert(e, c):
                _piece_copies(tail_n[e], 0, tail_base[e], make_zero_copy, action, TAIL_BITS)
                return c

            def per_block(b, c):
                cp = make_zero_copy(0, pl.multiple_of(b * MOE_BM, MOE_BM), MOE_BM)
                if action == "start":
                    cp.start()
                else:
                    cp.wait()
                return c

            lax.fori_loop(0, N_EXPERTS, per_expert, 0)
            lax.fori_loop(nused[0], MOE_NB, per_block, 0)

    p = _slot_onehot(slot_ref[...])
    srt = lax.dot_general(p, h_ref[...], (((0,), (0,)), ((), ())), preferred_element_type=F32)
    cur = tile & 1
    sorted_scr[cur] = _pack_pairs(srt[:, :HALF_D], srt[:, HALF_D:])

    def copies_from(buf):
        def make_copy(src_row, dst_row, rows):
            return pltpu.make_async_copy(sorted_scr.at[buf, pl.ds(src_row, rows)],
                                         xs_hbm.at[pl.ds(dst_row, rows)], sem.at[buf])
        return make_copy

    _run_copies(tile, n8_tbl, off_tbl, base_tbl, copies_from(cur), "start")

    @pl.when(tile > 0)
    def _():
        _run_copies(tile - 1, n8_tbl, off_tbl, base_tbl, copies_from(1 - cur), "wait")

    @pl.when(tile == N_TILES - 1)
    def _():
        _run_copies(tile, n8_tbl, off_tbl, base_tbl, copies_from(cur), "wait")


def _dispatch(tables, tails, h2, slots):
    grid_spec = pltpu.PrefetchScalarGridSpec(
        num_scalar_prefetch=6,
        grid=(N_TILES,),
        in_specs=[pl.BlockSpec((ROW_TILE, D_MODEL), lambda i, *_: (i, 0)),
                  pl.BlockSpec((ROW_TILE, LANES), lambda i, *_: (i, 0))],
        out_specs=pl.BlockSpec(memory_space=pl.ANY),
        scratch_shapes=[pltpu.VMEM((2, SORT_ROWS, HALF_D), jnp.uint32),
                        pltpu.VMEM((MOE_BM, HALF_D), jnp.uint32), pltpu.SemaphoreType.DMA((3,))],
    )
    return pl.pallas_call(
        _dispatch_body,
        grid_spec=grid_spec,
        out_shape=jax.ShapeDtypeStruct((MOE_ROWS, HALF_D), jnp.uint32),
        compiler_params=_params(("arbitrary",)),
        name="dispatch",
    )(*tables, *tails, h2, slots)


def _experts_body(bexp_ref, nused_ref, first_ref, next_ref, slot_ref, valid_ref, x_ref, wgu_hbm, bgu_ref, wdn_hbm,
                  bdn_ref, y_ref, wgu32, wdn32, wgu16, wdn16, sem, *, layer):
    del nused_ref
    i = pl.program_id(0)

    def weight_copies(expert, s):
        return (pltpu.make_async_copy(wgu_hbm.at[layer, expert], wgu32.at[s], sem.at[0, s]),
                pltpu.make_async_copy(wdn_hbm.at[layer, expert], wdn32.at[s], sem.at[1, s]))

    @pl.when(first_ref[i] == 1)
    def _():
        s = slot_ref[i]

        @pl.when(i == 0)
        def _():
            for cp in weight_copies(bexp_ref[i], s):
                cp.start()

        for cp in weight_copies(bexp_ref[i], s):
            cp.wait()

        @pl.when(next_ref[i] >= 0)
        def _():
            for cp in weight_copies(next_ref[i], 1 - s):
                cp.start()

        wgu16[...] = wgu32[s].astype(BF16)
        wdn16[...] = wdn32[s].astype(BF16)

    def compute(nrows):
        rows = pl.ds(0, nrows)
        x = jnp.concatenate(_unpack_pairs(x_ref[rows, :]), axis=1).astype(BF16)
        gu = jnp.dot(x, wgu16[...], preferred_element_type=F32) + bgu_ref[0, 0]
        gate = jnp.minimum(gu[:, :D_FF], SWIGLU_LIMIT)
        up = jnp.clip(gu[:, D_FF:], -SWIGLU_LIMIT, SWIGLU_LIMIT)
        glu = gate * _sigmoid(gate * SWIGLU_ALPHA)
        act = ((up + 1.0) * glu).astype(BF16)
        y = jnp.dot(act, wdn16[...], preferred_element_type=F32) + bdn_ref[0, 0]
        y = y.astype(BF16).astype(F32)
        y_ref[rows, :] = _pack_pairs(y[:, :HALF_D], y[:, HALF_D:])

    valid = valid_ref[i]

    @pl.when(valid > MOE_SUB)
    def _():
        compute(MOE_BM)

    @pl.when(valid <= MOE_SUB)
    def _():
        y_ref[pl.ds(MOE_SUB, MOE_SUB), :] = jnp.zeros((MOE_SUB, HALF_D), jnp.uint32)

    @pl.when((valid > 0) & (valid <= MOE_SUB))
    def _():
        compute(MOE_SUB)

    @pl.when(valid == 0)
    def _():
        y_ref[pl.ds(0, MOE_SUB), :] = jnp.zeros((MOE_SUB, HALF_D), jnp.uint32)


def _experts(sched, xs, wgu, bgu, wdn, bdn, layer):
    e = lambda i, be, *_: (layer, be[i], 0, 0)
    grid_spec = pltpu.PrefetchScalarGridSpec(
        num_scalar_prefetch=6,
        grid=(MOE_NB,),
        in_specs=[
            pl.BlockSpec((MOE_BM, HALF_D), lambda i, be, nu, *_: (jnp.minimum(i, nu[0] - 1), 0)),
            pl.BlockSpec(memory_space=pl.ANY),
            pl.BlockSpec((1, 1, 1, 2 * D_FF), e),
            pl.BlockSpec(memory_space=pl.ANY),
            pl.BlockSpec((1, 1, 1, D_MODEL), e),
        ],
        out_specs=pl.BlockSpec((MOE_BM, HALF_D), lambda i, *_: (i, 0)),
        scratch_shapes=[pltpu.VMEM((2, D_MODEL, 2 * D_FF), F32), pltpu.VMEM((2, D_FF, D_MODEL), F32),
                        pltpu.VMEM((D_MODEL, 2 * D_FF), BF16), pltpu.VMEM((D_FF, D_MODEL), BF16),
                        pltpu.SemaphoreType.DMA((2, 2))],
    )
    return pl.pallas_call(
        functools.partial(_experts_body, layer=layer),
        grid_spec=grid_spec,
        out_shape=jax.ShapeDtypeStruct((MOE_ROWS, HALF_D), jnp.uint32),
        compiler_params=_params(("arbitrary",)),
        name="experts",
    )(*sched, xs, wgu, bgu, wdn, bdn)


def _combine_body(n8_tbl, off_tbl, base_tbl, y_hbm, slot_ref, meta_ref, x1_ref, mod_ref, nf_ref, o_ref,
                  ybuf, sem, *, final, tile0, ntiles):
    step = pl.program_id(0)
    tile = tile0 + step
    cur = step & 1

    def fetch(t, buf):
        _run_copies(t, n8_tbl, off_tbl, base_tbl, copies_into(buf), "start")

    def copies_into(buf):
        def make_copy(buf_row, hbm_row, rows):
            return pltpu.make_async_copy(y_hbm.at[pl.ds(hbm_row, rows)],
                                         ybuf.at[buf, pl.ds(buf_row, rows)], sem.at[buf])
        return make_copy

    @pl.when(step == 0)
    def _():
        ybuf[...] = jnp.zeros_like(ybuf)
        fetch(tile, cur)

    @pl.when(step + 1 < ntiles)
    def _():
        fetch(tile + 1, 1 - cur)

    _run_copies(tile, n8_tbl, off_tbl, base_tbl, copies_into(cur), "wait")
    pw = _slot_onehot(slot_ref[...], meta_ref[...][:, TOP_K:2 * TOP_K])
    hi, lo = _unpack_pairs(ybuf[cur])
    ff = jnp.concatenate([jnp.dot(pw, hi.astype(BF16), preferred_element_type=F32),
                          jnp.dot(pw, lo.astype(BF16), preferred_element_type=F32)], axis=1)
    x2 = x1_ref[...] + mod_ref[0][5:6] * ff
    if final:
        x2 = _rms(x2) * nf_ref[...]
    o_ref[...] = x2


def _combine(tables, y, slots, meta, x1, mod3, nfinal, *, final, tile0=0, ntiles=N_TILES):
    grid_spec = pltpu.PrefetchScalarGridSpec(
        num_scalar_prefetch=3,
        grid=(ntiles,),
        in_specs=[pl.BlockSpec(memory_space=pl.ANY),
                  pl.BlockSpec((ROW_TILE, LANES), lambda i, *_: (tile0 + i, 0)),
                  pl.BlockSpec((ROW_TILE, LANES), lambda i, *_: (tile0 + i, 0)),
                  pl.BlockSpec((ROW_TILE, D_MODEL), lambda i, *_: (tile0 + i, 0)),
                  pl.BlockSpec((1, 6, D_MODEL), lambda i, *_: (_mod_row(tile0 + i, ROW_TILE), 0, 0)),
                  pl.BlockSpec((1, D_MODEL), lambda i, *_: (0, 0))],
        out_specs=pl.BlockSpec((ROW_TILE, D_MODEL), lambda i, *_: (i, 0)),
        scratch_shapes=[pltpu.VMEM((2, SORT_ROWS, HALF_D), jnp.uint32), pltpu.SemaphoreType.DMA((2,))],
    )
    return pl.pallas_call(
        functools.partial(_combine_body, final=final, tile0=tile0, ntiles=ntiles),
        grid_spec=grid_spec,
        out_shape=jax.ShapeDtypeStruct((ntiles * ROW_TILE, D_MODEL), F32),
        compiler_params=_params(("arbitrary",)),
        name="combine",
    )(*tables, y, slots, meta, x1, mod3, nfinal)


def _cumsum0(x):
    n = x.shape[0]
    tri = jnp.asarray(np.tril(np.ones((n, n), np.bool_)))
    tri = tri.reshape((n, n) + (1,) * (x.ndim - 1))
    return jnp.sum(jnp.where(tri, x[None], 0), axis=1)


def _lookup(table, idx):
    hit = idx[:, None] == jnp.arange(table.shape[0], dtype=idx.dtype)[None, :]
    return jnp.sum(jnp.where(hit, table[None, :], 0), axis=1)


def _state_to_blockdiag(s):
    eye = jnp.eye(A_HEADS, dtype=s.dtype)
    t = jnp.einsum("bdhkv,hg->bdhvgk", s, eye)
    return t.reshape(s.shape[0], 2, A_VAL_W, A_KEY_W)


def kernel(x_prompt, x_sample, state_gla, cache_k, cache_v, c, c_ctx, w_ada, b_ada, norm_mix, norm_ffn,
           norm_final, w_in, w_out, gla_gate_w, gla_gate_b, gla_norm, q_norm, k_norm, router_w, router_b,
           expert_w_gu, expert_b_gu, expert_w_dn, expert_b_dn):
    xs = (x_prompt.reshape(N_CTX, D_MODEL), x_sample.reshape(N_DEC, D_MODEL))
    cond = jnp.zeros((MOD_ROWS, D_MODEL), F32).at[0].set(c_ctx).at[1:1 + DEC_BATCH].set(c)
    mod = _ada(cond, w_ada, b_ada)

    gm256 = jnp.asarray(np.kron(np.eye(A_HEADS), np.ones((A_DV, A_DV))) / A_DV, BF16)
    gm128 = jnp.asarray(np.kron(np.eye(2), np.ones((HEAD_DIM, HEAD_DIM))) / HEAD_DIM, BF16)
    c64, s64 = _dft_mats(F_GROUP_W, F_GROUP_W ** -0.5)
    cb = jnp.asarray(np.kron(np.eye(F_GROUPS), c64), BF16)
    sb = jnp.asarray(np.kron(np.eye(F_GROUPS), s64), BF16)
    dft_ctx = [jnp.asarray(m, BF16) for m in _dft_mats(SEQ, SEQ ** -0.5)]
    dft_dec = [jnp.asarray(m, BF16) for m in _dft_mats(DEC_SEQ, DEC_SEQ ** -0.5)]
    rope_tabs = _rope_tables()
    zero_state = jnp.zeros((BATCH, 2, A_VAL_W, A_KEY_W), F32)

    new_s, new_k, new_v = [], [], []
    for l in range(DEPTH):
        mod3 = mod[l].reshape(MOD_ROWS, 6, D_MODEL)
        w_l = w_in[l]
        w_r = jnp.concatenate([w_l[:, :768], w_l[:, 800:], w_l[:, 768:800],
                               jnp.zeros((D_MODEL, PROJ_W - 1824), F32)], axis=1).astype(BF16)
        proj = _inproj(xs, mod3, norm_mix[l][None, :], w_r)

        wg = jnp.zeros((LANES, 2 * A_KEY_W), F32)
        wg = wg.at[0:GATE_RANK, 0:A_KEY_W].set(gla_gate_w[l, 0])
        wg = wg.at[GATE_RANK:2 * GATE_RANK, A_KEY_W:].set(gla_gate_w[l, 1])
        gb = gla_gate_b[l].reshape(1, 2 * A_KEY_W)
        gn = jnp.tile(gla_norm[l], A_HEADS)[None, :]
        s0_dec = _state_to_blockdiag(state_gla[:, l])
        oa_ctx, sfin = _gla(proj, wg, gb, zero_state, gn, gm256, nseq=BATCH, seq=SEQ, row_block0=0)
        oa_dec, _ = _gla(proj, wg, gb, s0_dec, gn, gm256, nseq=DEC_BATCH, seq=DEC_SEQ,
                         row_block0=N_CTX // DEC_SEQ)
        of_ctx = _fnet(proj, cb, sb, *dft_ctx, nseq=BATCH, seq=SEQ, row_block0=0)
        of_dec = _fnet(proj, cb, sb, *dft_dec, nseq=DEC_BATCH, seq=DEC_SEQ, row_block0=N_CTX // DEC_SEQ)
        qn = jnp.tile(q_norm[l], 2)[None, :]
        kn = jnp.tile(k_norm[l], 2)[None, :]
        oc_ctx, ka = _attn(proj, qn, kn, gm128, nseq=BATCH, seq=SEQ, row_block0=0)
        past = (cache_k[:, l].reshape(DEC_BATCH, PAST_LEN, C_KV_W),
                cache_v[:, l].reshape(DEC_BATCH, PAST_LEN, C_KV_W))
        oc_dec, _ = _attn(proj, qn, kn, gm128, nseq=DEC_BATCH, seq=DEC_SEQ,
                          row_block0=N_CTX // DEC_SEQ, past=past, tables=rope_tabs)
        new_s.append(sfin.reshape(BATCH, 2, A_HEADS, A_DK, A_DV))
        new_k.append(ka.reshape(BATCH, SEQ, C_KV_HEADS, HEAD_DIM))
        new_v.append(proj[:N_CTX, COL_CV:COL_CV + C_KV_W].reshape(BATCH, SEQ, C_KV_HEADS, HEAD_DIM))

        rw = jnp.zeros((D_MODEL, LANES), F32).at[:, :N_EXPERTS].set(router_w[l])
        rb = jnp.zeros((1, LANES), F32).at[0, :N_EXPERTS].set(router_b[l])
        x1, h2, meta, slots, info = _outproj((oa_ctx, of_ctx, oc_ctx), (oa_dec, of_dec, oc_dec),
                                             w_out[l].astype(BF16), xs, mod3, norm_ffn[l][None, :], rw, rb)

        n8 = info[:, 0, :N_EXPERTS].astype(jnp.int32)
        off = info[:, 1, :N_EXPERTS].astype(jnp.int32)
        cnt = jnp.sum(n8, axis=0)
        pcnt = (cnt + MOE_BM - 1) // MOE_BM * MOE_BM
        pend = _cumsum0(pcnt)
        base = (pend - pcnt)[None, :] + _cumsum0(n8) - n8
        tables = (n8.reshape(-1), off.reshape(-1), base.reshape(-1))
        blk = jnp.arange(MOE_NB, dtype=jnp.int32)
        blk_exp = jnp.minimum(jnp.sum((pend[None, :] <= (blk * MOE_BM)[:, None]).astype(jnp.int32), axis=1),
                              N_EXPERTS - 1)
        nused = (pend[-1:] // MOE_BM).astype(jnp.int32)
        tails = (pcnt - cnt, pend - (pcnt - cnt), nused)
        prev_exp = jnp.concatenate([jnp.full((1,), -1, jnp.int32), blk_exp[:-1]])
        first = ((blk < nused[0]) & (blk_exp != prev_exp)).astype(jnp.int32)
        end_blk = _lookup(pend // MOE_BM, blk_exp)
        nxt = jnp.where(end_blk < nused[0], _lookup(blk_exp, jnp.minimum(end_blk, MOE_NB - 1)), -1)
        wslot = (_cumsum0(first) - 1) & 1
        start_blk = (pend - pcnt) // MOE_BM
        valid = jnp.where(blk < nused[0],
                          jnp.clip(_lookup(cnt, blk_exp) - (blk - _lookup(start_blk, blk_exp)) * MOE_BM,
                                   0, MOE_BM), 0)
        sched = (blk_exp, nused, first, nxt, wslot, valid)

        xs = _dispatch(tables, tails, h2, slots)
        y = _experts(sched, xs, expert_w_gu,
                     expert_b_gu.reshape(DEPTH, N_EXPERTS, 1, 2 * D_FF), expert_w_dn,
                     expert_b_dn.reshape(DEPTH, N_EXPERTS, 1, D_MODEL), l)
        comb = functools.partial(_combine, tables, y, slots, meta, x1, mod3, norm_final[None, :])
        if l < DEPTH - 1:
            xs = (comb(final=False),)
        else:
            ctx_tiles = N_CTX // ROW_TILE
            xs = (comb(final=True, tile0=0, ntiles=ctx_tiles),
                  comb(final=True, tile0=ctx_tiles, ntiles=N_TILES - ctx_tiles))

    y_prompt = xs[0].reshape(BATCH, SEQ, D_MODEL)
    y_sample = xs[1].reshape(DEC_BATCH, DEC_SEQ, D_MODEL)
    return (y_prompt, y_sample, jnp.stack(new_s, axis=1), jnp.stack(new_k, axis=1), jnp.stack(new_v, axis=1))
```

```python
import functools

import jax
import jax.numpy as jnp
import numpy as np
from jax import lax
from jax.experimental import pallas as pl
from jax.experimental.pallas import tpu as pltpu

D_MODEL = 1024
BATCH = 32
SEQ = 256
DEPTH = 2
DEC_BATCH = 8
DEC_SEQ = 1024
PAST_LEN = 512
GRID_W = 64
A_HEADS = 4
A_DV = 64
A_DK = 32
A_KEY_W = 128
A_VAL_W = 256
GATE_RANK = 16
GATE_TEMP = 16.0
GLA_CHUNK = 32
F_GROUPS = 4
F_GROUP_W = 64
F_WIDTH = 256
HEAD_DIM = 64
C_HEADS = 8
C_KV_HEADS = 2
C_Q_W = 512
C_KV_W = 128
ROPE_THETA = 10000.0
N_EXPERTS = 32
TOP_K = 4
D_FF = 1024
SWIGLU_LIMIT = 7.0
SWIGLU_ALPHA = 1.702
EPS = 1e-6

N_CTX = BATCH * SEQ
N_DEC = DEC_BATCH * DEC_SEQ
N_TOK = N_CTX + N_DEC
MOD_ROWS = 16

COL_GLA = 0
COL_PF = 768
COL_CQ = 1024
COL_CK = 1536
COL_CV = 1664
COL_LR = 1792
PROJ_W = 1920

LANES = 128
TM = 512
GLA_TILE = 256
Q_TILE = 512
MOE_BM = 512
MOE_SUB = MOE_BM // 2
ROW_TILE = 256
N_TILES = N_TOK // ROW_TILE
RUN_ALIGN = 8
RUN_SHIFT = 3
RUN_BITS = 6
TAIL_BITS = 6
SORT_ROWS = ROW_TILE * TOP_K + N_EXPERTS * RUN_ALIGN
HALF_D = D_MODEL // 2
MOE_NB = (N_TOK * TOP_K + N_TILES * N_EXPERTS * (RUN_ALIGN - 1)) // MOE_BM + N_EXPERTS
MOE_ROWS = MOE_NB * MOE_BM
VMEM_LIMIT = 56 * 1024 * 1024

F32 = jnp.float32
BF16 = jnp.bfloat16


def _params(sem, vmem=VMEM_LIMIT):
    return pltpu.CompilerParams(dimension_semantics=sem, vmem_limit_bytes=vmem)


def _bdot(a, b):
    return jnp.dot(a.astype(BF16), b.astype(BF16), preferred_element_type=F32)


def _split(a):
    hi = a.astype(BF16)
    lo = (a - hi.astype(F32)).astype(BF16)
    return hi, lo


def _dot3(a, b):
    ah, al = _split(a)
    bh, bl = _split(b)
    d = functools.partial(jnp.dot, preferred_element_type=F32)
    return d(ah, bh) + (d(ah, bl) + d(al, bh))


def _dot2(a, b16):
    ah, al = _split(a)
    d = functools.partial(jnp.dot, preferred_element_type=F32)
    return d(ah, b16) + d(al, b16)


def _sigmoid(x):
    return 1.0 / (1.0 + jnp.exp(-x))


def _mod_row(i, tm):
    tok = i * tm
    return jnp.where(tok < N_CTX, 0, 1 + (tok - N_CTX) // DEC_SEQ)


def _ada_body(c_ref, w_ref, b_ref, o_ref):
    c = c_ref[...]
    o_ref[0] = _dot3(c * _sigmoid(c), w_ref[0]) + b_ref[0]


def _ada(cond, w_ada, b_ada):
    return pl.pallas_call(
        _ada_body,
        grid=(DEPTH, 6),
        in_specs=[
            pl.BlockSpec((MOD_ROWS, D_MODEL), lambda l, j: (0, 0)),
            pl.BlockSpec((1, D_MODEL, D_MODEL), lambda l, j: (l, 0, j)),
            pl.BlockSpec((1, 1, D_MODEL), lambda l, j: (l, 0, j)),
        ],
        out_specs=pl.BlockSpec((1, MOD_ROWS, D_MODEL), lambda l, j: (l, 0, j)),
        out_shape=jax.ShapeDtypeStruct((DEPTH, MOD_ROWS, 6 * D_MODEL), F32),
        compiler_params=_params(("parallel", "parallel")),
        name="ada",
    )(cond, w_ada, b_ada.reshape(DEPTH, 1, 6 * D_MODEL))


def _rms(x):
    return x * lax.rsqrt(jnp.mean(x * x, axis=-1, keepdims=True) + EPS)


def _inproj_body(*refs):
    *x_refs, mod_ref, g_ref, w_ref, o_ref = refs
    if len(x_refs) == 2:
        x = jnp.where(pl.program_id(0) < N_CTX // TM, x_refs[0][...], x_refs[1][...])
    else:
        x = x_refs[0][...]
    mod = mod_ref[0]
    h = _rms(x) * g_ref[...] * (1.0 + mod[1:2]) + mod[0:1]
    o_ref[...] = jnp.dot(h.astype(BF16), w_ref[...], preferred_element_type=F32)


def _inproj(xs, mod3, gain, w):
    n_ctx_tiles = N_CTX // TM
    if len(xs) == 2:
        x_specs = [pl.BlockSpec((TM, D_MODEL), lambda i: (jnp.minimum(i, n_ctx_tiles - 1), 0)),
                   pl.BlockSpec((TM, D_MODEL), lambda i: (jnp.maximum(i - n_ctx_tiles, 0), 0))]
    else:
        x_specs = [pl.BlockSpec((TM, D_MODEL), lambda i: (i, 0))]
    return pl.pallas_call(
        _inproj_body,
        grid=(N_TOK // TM,),
        in_specs=x_specs + [
            pl.BlockSpec((1, 6, D_MODEL), lambda i: (_mod_row(i, TM), 0, 0)),
            pl.BlockSpec((1, D_MODEL), lambda i: (0, 0)),
            pl.BlockSpec((D_MODEL, PROJ_W), lambda i: (0, 0)),
        ],
        out_specs=pl.BlockSpec((TM, PROJ_W), lambda i: (i, 0)),
        out_shape=jax.ShapeDtypeStruct((N_TOK, PROJ_W), F32),
        compiler_params=_params(("parallel",)),
        name="inproj",
    )(*xs, mod3, gain, w)


def _log_sigmoid(x):
    return jnp.minimum(x, 0.0) - jnp.log(1.0 + jnp.exp(-jnp.abs(x)))


def _chunk_cumsum(x, pos, forward):
    rows = x.shape[0]
    for s in (1, 2, 4, 8, 16):
        if forward:
            x = x + jnp.where(pos >= s, pltpu.roll(x, s, 0), 0.0)
        else:
            x = x + jnp.where(pos < GLA_CHUNK - s, pltpu.roll(x, rows - s, 0), 0.0)
    return x


def _gla_tile(q, k, v16, la, st, forward):
    R = GLA_TILE
    nch = R // GLA_CHUNK
    pos = lax.broadcasted_iota(jnp.int32, (R, 1), 0) & (GLA_CHUNK - 1)
    b = _chunk_cumsum(la, pos, forward)
    b3 = b.reshape(nch, GLA_CHUNK, A_KEY_W)
    tot3 = b3[:, GLA_CHUNK - 1:GLA_CHUNK, :] if forward else b3[:, 0:1, :]
    totb = jnp.broadcast_to(tot3, (nch, GLA_CHUNK, A_KEY_W)).reshape(R, A_KEY_W)
    qt = q * (A_DK ** -0.5) * jnp.exp(b)
    kt = (k * jnp.exp(-b)).astype(BF16)
    kend = (k * jnp.exp(totb - b)).astype(BF16)
    dec3 = jnp.exp(tot3)

    ri = lax.broadcasted_iota(jnp.int32, (R, R), 0)
    ci = lax.broadcasted_iota(jnp.int32, (R, R), 1)
    tri = (ci <= ri) if forward else (ci >= ri)
    mask = ((ri >> 5) == (ci >> 5)) & tri
    lane_k = lax.broadcasted_iota(jnp.int32, (R, A_KEY_W), 1) >> 5
    lane_v = lax.broadcasted_iota(jnp.int32, (R, A_VAL_W), 1) >> 6

    o = jnp.zeros((R, A_VAL_W), F32)
    for h in range(A_HEADS):
        qh = jnp.where(lane_k == h, qt, 0.0).astype(BF16)
        sc = lax.dot_general(qh, kt, (((1,), (1,)), ((), ())), preferred_element_type=F32)
        p = jnp.where(mask, sc, 0.0).astype(BF16)
        oh = jnp.dot(p, v16, preferred_element_type=F32)
        o = jnp.where(lane_v == h, oh, o)

    sr = lax.broadcasted_iota(jnp.int32, (A_VAL_W, A_KEY_W), 0) >> 6
    sl = lax.broadcasted_iota(jnp.int32, (A_VAL_W, A_KEY_W), 1) >> 5
    smask = sr == sl
    qt16 = qt.astype(BF16)
    inter = [None] * nch
    order = range(nch) if forward else range(nch - 1, -1, -1)
    for c in order:
        lo = c * GLA_CHUNK
        inter[c] = lax.dot_general(qt16[lo:lo + GLA_CHUNK], st.astype(BF16),
                                   (((1,), (1,)), ((), ())), preferred_element_type=F32)
        ds = lax.dot_general(v16[lo:lo + GLA_CHUNK], kend[lo:lo + GLA_CHUNK],
                             (((0,), (0,)), ((), ())), preferred_element_type=F32)
        st = st * dec3[c] + jnp.where(smask, ds, 0.0)
    return o + jnp.concatenate(inter, axis=0), st


def _gla_body(gla_ref, lr_ref, wg_ref, gb_ref, s0_ref, gn_ref, gm_ref, o_ref, sfin_ref, la_scr, ob_scr, *, seq):
    ntile = seq // GLA_TILE
    logit = _bdot(lr_ref[...], wg_ref[...]) + gb_ref[...]
    la_scr[...] = _log_sigmoid(logit) * (1.0 / GATE_TEMP)

    def step(i, states):
        new_states = []
        for d, forward in ((0, True), (1, False)):
            t = i if forward else ntile - 1 - i
            rows = pl.ds(pl.multiple_of(t * GLA_TILE, GLA_TILE), GLA_TILE)
            q = gla_ref[rows, 0:128]
            k = gla_ref[rows, 128:256]
            v16 = gla_ref[rows, 256:512].astype(BF16)
            la = la_scr[rows, d * A_KEY_W:(d + 1) * A_KEY_W]
            o, st = _gla_tile(q, k, v16, la, states[d], forward)
            if forward:
                o_ref[rows, :] = o
            else:
                ob_scr[rows, :] = o
            new_states.append(st)
        return tuple(new_states)

    states = lax.fori_loop(0, ntile, step, (s0_ref[0, 0], s0_ref[0, 1]))
    for d in range(2):
        s = states[d].T
        sfin_ref[0, d] = (s[:, 0:A_DV] + s[:, A_DV:2 * A_DV]) + (s[:, 2 * A_DV:3 * A_DV] + s[:, 3 * A_DV:])

    o = o_ref[...] + ob_scr[...]
    ms = _dot2(o * o, gm_ref[...])
    g = gla_ref[:, 512:768]
    o_ref[...] = o * lax.rsqrt(ms + EPS) * gn_ref[...] * (g * _sigmoid(g))


def _gla(proj, wg, gb, s0t, gnorm, gmat, *, nseq, seq, row_block0):
    in_specs = [
        pl.BlockSpec((seq, 768), lambda b: (row_block0 + b, 0)),
        pl.BlockSpec((seq, LANES), lambda b: (row_block0 + b, COL_LR // LANES)),
        pl.BlockSpec((LANES, 2 * A_KEY_W), lambda b: (0, 0)),
        pl.BlockSpec((1, 2 * A_KEY_W), lambda b: (0, 0)),
        pl.BlockSpec((1, 2, A_VAL_W, A_KEY_W), lambda b: (b, 0, 0, 0)),
        pl.BlockSpec((1, A_VAL_W), lambda b: (0, 0)),
        pl.BlockSpec((A_VAL_W, A_VAL_W), lambda b: (0, 0)),
    ]
    return pl.pallas_call(
        functools.partial(_gla_body, seq=seq),
        grid=(nseq,),
        in_specs=in_specs,
        out_specs=[
            pl.BlockSpec((seq, A_VAL_W), lambda b: (b, 0)),
            pl.BlockSpec((1, 2, A_KEY_W, A_DV), lambda b: (b, 0, 0, 0)),
        ],
        out_shape=[
            jax.ShapeDtypeStruct((nseq * seq, A_VAL_W), F32),
            jax.ShapeDtypeStruct((nseq, 2, A_KEY_W, A_DV), F32),
        ],
        scratch_shapes=[pltpu.VMEM((seq, 2 * A_KEY_W), F32), pltpu.VMEM((seq, A_VAL_W), F32)],
        compiler_params=_params(("parallel",)),
        name=f"gla{seq}",
    )(proj, proj, wg, gb, s0t, gnorm, gmat)


def _fnet_body(x_ref, cb_ref, sb_ref, ct_ref, st_ref, o_ref):
    x = x_ref[...].astype(BF16)
    xc = jnp.dot(x, cb_ref[...], preferred_element_type=F32).astype(BF16)
    xs = jnp.dot(x, sb_ref[...], preferred_element_type=F32).astype(BF16)
    o_ref[...] = (jnp.dot(ct_ref[...], xc, preferred_element_type=F32)
                  - jnp.dot(st_ref[...], xs, preferred_element_type=F32))


def _fnet(proj, cb, sb, ct, st, *, nseq, seq, row_block0):
    in_specs = [
        pl.BlockSpec((seq, F_WIDTH), lambda b: (row_block0 + b, COL_PF // F_WIDTH)),
        pl.BlockSpec((F_WIDTH, F_WIDTH), lambda b: (0, 0)),
        pl.BlockSpec((F_WIDTH, F_WIDTH), lambda b: (0, 0)),
        pl.BlockSpec((seq, seq), lambda b: (0, 0)),
        pl.BlockSpec((seq, seq), lambda b: (0, 0)),
    ]
    return pl.pallas_call(
        _fnet_body,
        grid=(nseq,),
        in_specs=in_specs,
        out_specs=pl.BlockSpec((seq, F_WIDTH), lambda b: (b, 0)),
        out_shape=jax.ShapeDtypeStruct((nseq * seq, F_WIDTH), F32),
        compiler_params=_params(("parallel",)),
        name=f"fnet{seq}",
    )(proj, cb, sb, ct, st)


def _dft_mats(n, scale):
    jk = np.outer(np.arange(n), np.arange(n)) % n
    ang = jk.astype(np.float64) * (2.0 * np.pi / n)
    return np.cos(ang) * scale, np.sin(ang) * scale


def _head_rms(x, gm16, gain):
    ms = _dot2(x * x, gm16)
    return x * lax.rsqrt(ms + EPS) * gain


def _rope(x, cos, sin_signed):
    w = x.shape[1]
    lane = lax.broadcasted_iota(jnp.int32, x.shape, 1)
    partner = jnp.where((lane & 31) < 16, pltpu.roll(x, w - 16, 1), pltpu.roll(x, 16, 1))
    return x * cos + partner * sin_signed


def _attn_body(*refs, rope, n_new, n_past):
    if n_past:
        (q_ref, k_ref, v_ref, ck_ref, cv_ref, cq_ref, sq_ref, cosk_ref, sink_ref,
         qn_ref, kn_ref, gm_ref, o_ref, ka_ref, k_scr, ks_scr, v_scr, vs_scr) = refs
    else:
        (q_ref, k_ref, v_ref, qn_ref, kn_ref, gm_ref, o_ref, ka_ref,
         k_scr, ks_scr, v_scr, vs_scr) = refs
    gm = gm_ref[...]

    @pl.when(pl.program_id(1) == 0)
    def _():
        kn = _head_rms(k_ref[...], gm, kn_ref[...])
        ka_ref[...] = kn
        if rope:
            kn = _rope(kn, cosk_ref[...], sink_ref[...])
        v = v_ref[...]
        new = pl.ds(0, n_new)
        k_scr[new, :] = kn.astype(BF16)
        ks_scr[new, :] = pltpu.roll(kn, HEAD_DIM, 1).astype(BF16)
        v_scr[new, :] = v.astype(BF16)
        vs_scr[new, :] = pltpu.roll(v, HEAD_DIM, 1).astype(BF16)
        if n_past:
            past = pl.ds(n_new, n_past)
            ck = ck_ref[0]
            cv = cv_ref[0]
            k_scr[past, :] = ck.astype(BF16)
            ks_scr[past, :] = pltpu.roll(ck, HEAD_DIM, 1).astype(BF16)
            v_scr[past, :] = cv.astype(BF16)
            vs_scr[past, :] = pltpu.roll(cv, HEAD_DIM, 1).astype(BF16)

    q_tile = q_ref.shape[0]
    lane = lax.broadcasted_iota(jnp.int32, (q_tile, LANES), 1)
    low = lane < HEAD_DIM
    for kv in range(C_KV_HEADS):
        q2 = []
        for jp in (2 * kv, 2 * kv + 1):
            q = _head_rms(q_ref[:, jp * LANES:(jp + 1) * LANES], gm, qn_ref[...])
            if rope:
                q = _rope(q, cq_ref[...], sq_ref[...])
            q2.append(q * (HEAD_DIM ** -0.5))
        halves = []
        for half in range(2):
            swapped = (half == 1) != (kv == 1)
            kk = ks_scr[...] if swapped else k_scr[...]
            vv = vs_scr[...] if swapped else v_scr[...]
            keep = low if half == 0 else ~low
            qm = jnp.concatenate([jnp.where(keep, q, 0.0) for q in q2], axis=0).astype(BF16)
            s = lax.dot_general(qm, kk, (((1,), (1,)), ((), ())), preferred_element_type=F32)
            p = jnp.exp(s - jnp.max(s, axis=-1, keepdims=True))
            den = jnp.sum(p, axis=-1, keepdims=True)
            halves.append(jnp.dot(p.astype(BF16), vv, preferred_element_type=F32) / den)
        for n, jp in enumerate((2 * kv, 2 * kv + 1)):
            rows = slice(n * q_tile, (n + 1) * q_tile)
            o_ref[:, jp * LANES:(jp + 1) * LANES] = jnp.where(low, halves[0][rows], halves[1][rows])


def _attn(proj, qn, kn, gm, *, nseq, seq, row_block0, past=None, tables=None):
    q_tile = min(seq, Q_TILE)
    nq = seq // q_tile
    n_past = 0 if past is None else past[0].shape[1]
    qrow = lambda b, j: ((row_block0 + b) * nq + j, COL_CQ // C_Q_W)
    in_specs = [
        pl.BlockSpec((q_tile, C_Q_W), qrow),
        pl.BlockSpec((seq, C_KV_W), lambda b, j: (row_block0 + b, COL_CK // C_KV_W)),
        pl.BlockSpec((seq, C_KV_W), lambda b, j: (row_block0 + b, COL_CV // C_KV_W)),
    ]
    args = [proj, proj, proj]
    if past is not None:
        in_specs += [pl.BlockSpec((1, n_past, C_KV_W), lambda b, j: (b, 0, 0))] * 2
        in_specs += [pl.BlockSpec((q_tile, LANES), lambda b, j: (j, 0))] * 2
        in_specs += [pl.BlockSpec((seq, LANES), lambda b, j: (0, 0))] * 2
        args += [past[0], past[1], tables[0], tables[1], tables[0], tables[1]]
    in_specs += [
        pl.BlockSpec((1, LANES), lambda b, j: (0, 0)),
        pl.BlockSpec((1, LANES), lambda b, j: (0, 0)),
        pl.BlockSpec((LANES, LANES), lambda b, j: (0, 0)),
    ]
    args += [qn, kn, gm]
    s_all = seq + n_past
    return pl.pallas_call(
        functools.partial(_attn_body, rope=past is not None, n_new=seq, n_past=n_past),
        grid=(nseq, nq),
        in_specs=in_specs,
        out_specs=[
            pl.BlockSpec((q_tile, C_Q_W), lambda b, j: (b * nq + j, 0)),
            pl.BlockSpec((seq, C_KV_W), lambda b, j: (b, 0)),
        ],
        out_shape=[
            jax.ShapeDtypeStruct((nseq * seq, C_Q_W), F32),
            jax.ShapeDtypeStruct((nseq * seq, C_KV_W), F32),
        ],
        scratch_shapes=[pltpu.VMEM((s_all, C_KV_W), BF16)] * 4,
        compiler_params=_params(("parallel", "arbitrary")),
        name=f"attn{seq}",
    )(*args)


def _rope_tables():
    t = np.arange(DEC_SEQ)
    rows = (t // GRID_W).astype(np.float32)
    cols = (t % GRID_W).astype(np.float32)
    nf = HEAD_DIM // 4
    inv_freq = (ROPE_THETA ** (-np.arange(nf, dtype=np.float32) / nf)).astype(np.float32)
    d = np.arange(LANES)
    pos = np.where(((d % HEAD_DIM) < HEAD_DIM // 2)[None, :], rows[:, None], cols[:, None])
    ang = (pos * inv_freq[d % nf][None, :]).astype(np.float32)
    sign = np.where((d % 32) < 16, -1.0, 1.0)[None, :]
    return jnp.asarray(np.cos(ang), F32), jnp.asarray(np.sin(ang) * sign, F32)


def _outproj_body(oa_c, of_c, oc_c, oa_d, of_d, oc_d, w_ref, *refs):
    *x_refs, mod_ref, g_ref, rw_ref, rb_ref, x1_ref, h2_ref, meta_ref, slot_ref, info_ref = refs
    mod = mod_ref[0]
    is_ctx = pl.program_id(0) < N_CTX // TM
    pick = lambda c_ref, d_ref: jnp.where(is_ctx, c_ref[...], d_ref[...]).astype(BF16)
    x_in = x_refs[0][...] if len(x_refs) == 1 else jnp.where(is_ctx, x_refs[0][...], x_refs[1][...])
    mixed = (jnp.dot(pick(oa_c, oa_d), w_ref[0:256, :], preferred_element_type=F32)
             + jnp.dot(pick(of_c, of_d), w_ref[256:512, :], preferred_element_type=F32)
             + jnp.dot(pick(oc_c, oc_d), w_ref[512:1024, :], preferred_element_type=F32))
    x1 = x_in + mod[2:3] * mixed
    x1_ref[...] = x1
    h2 = _rms(x1) * g_ref[...] * (1.0 + mod[4:5]) + mod[3:4]
    h2_ref[...] = h2.astype(BF16)
    lane = lax.broadcasted_iota(jnp.int32, (TM, LANES), 1)
    logits = jnp.where(lane < N_EXPERTS, _dot3(h2, rw_ref[...]) + rb_ref[...], -jnp.inf)
    meta = jnp.zeros((TM, LANES), F32)
    tops = []
    for k in range(TOP_K):
        m = jnp.max(logits, axis=-1, keepdims=True)
        idx = jnp.min(jnp.where(logits == m, lane, LANES), axis=-1, keepdims=True)
        hit = lane == idx
        logits = jnp.where(hit, -jnp.inf, logits)
        meta = jnp.where(lane == k, idx.astype(F32), meta)
        tops.append(m)
    ex = [jnp.exp(t - tops[0]) for t in tops]
    den = ex[0] + ex[1] + ex[2] + ex[3]
    for k in range(TOP_K):
        meta = jnp.where(lane == TOP_K + k, ex[k] / den, meta)
    meta_ref[...] = meta
    for t in range(TM // ROW_TILE):
        rows = slice(t * ROW_TILE, (t + 1) * ROW_TILE)
        slot_ref[rows, :], info_ref[t] = _tile_slots(meta[rows])


def _outproj(mix_ctx, mix_dec, w, xs, mod3, gain, rw, rb):
    n_ctx_tiles = N_CTX // TM
    tok = lambda w_: pl.BlockSpec((TM, w_), lambda i: (i, 0))
    ctx = lambda w_: pl.BlockSpec((TM, w_), lambda i: (jnp.minimum(i, n_ctx_tiles - 1), 0))
    dec = lambda w_: pl.BlockSpec((TM, w_), lambda i: (jnp.maximum(i - n_ctx_tiles, 0), 0))
    full = lambda a, b: pl.BlockSpec((a, b), lambda i: (0, 0))
    x_specs = [ctx(D_MODEL), dec(D_MODEL)] if len(xs) == 2 else [tok(D_MODEL)]
    tiles_per_step = TM // ROW_TILE
    return pl.pallas_call(
        _outproj_body,
        grid=(N_TOK // TM,),
        in_specs=[ctx(A_VAL_W), ctx(F_WIDTH), ctx(C_Q_W), dec(A_VAL_W), dec(F_WIDTH), dec(C_Q_W),
                  full(D_MODEL, D_MODEL)] + x_specs + [
                  pl.BlockSpec((1, 6, D_MODEL), lambda i: (_mod_row(i, TM), 0, 0)),
                  full(1, D_MODEL), full(D_MODEL, LANES), full(1, LANES)],
        out_specs=[tok(D_MODEL), tok(D_MODEL), tok(LANES), tok(LANES),
                   pl.BlockSpec((tiles_per_step, 8, LANES), lambda i: (i, 0, 0))],
        out_shape=[jax.ShapeDtypeStruct((N_TOK, D_MODEL), F32),
                   jax.ShapeDtypeStruct((N_TOK, D_MODEL), BF16),
                   jax.ShapeDtypeStruct((N_TOK, LANES), F32),
                   jax.ShapeDtypeStruct((N_TOK, LANES), F32),
                   jax.ShapeDtypeStruct((N_TILES, 8, LANES), F32)],
        compiler_params=_params(("parallel",)),
        name="outproj",
    )(*mix_ctx, *mix_dec, w, *xs, mod3, gain, rw, rb)


def _tile_slots(meta):
    lane = lax.broadcasted_iota(jnp.int32, (ROW_TILE, LANES), 1)
    idx = [meta[:, k:k + 1].astype(jnp.int32) for k in range(TOP_K)]
    sel = jnp.zeros((ROW_TILE, LANES), F32)
    for k in range(TOP_K):
        sel = jnp.where(lane == idx[k], 1.0, sel)
    ri = lax.broadcasted_iota(jnp.int32, (ROW_TILE, ROW_TILE), 0)
    ci = lax.broadcasted_iota(jnp.int32, (ROW_TILE, ROW_TILE), 1)
    lower = jnp.where(ci < ri, 1.0, 0.0).astype(BF16)
    rank = jnp.dot(lower, sel.astype(BF16), preferred_element_type=F32)
    cnt = jnp.sum(sel, axis=0, keepdims=True)
    n8 = jnp.floor((cnt + (RUN_ALIGN - 1)) * (1.0 / RUN_ALIGN)) * RUN_ALIGN
    er = lax.broadcasted_iota(jnp.int32, (LANES, LANES), 0)
    ec = lax.broadcasted_iota(jnp.int32, (LANES, LANES), 1)
    before = jnp.where(er < ec, 1.0, 0.0).astype(BF16)
    off = jnp.dot(jnp.broadcast_to(n8, (8, LANES)).astype(BF16), before, preferred_element_type=F32)
    tot = rank + off[0:1]
    out = jnp.zeros((ROW_TILE, LANES), F32)
    for k in range(TOP_K):
        pk = jnp.sum(jnp.where(lane == idx[k], tot, 0.0), axis=-1, keepdims=True)
        out = jnp.where(lane == k, pk, out)
    row = lax.broadcasted_iota(jnp.int32, (8, LANES), 0)
    return out, jnp.where(row == 0, n8, jnp.where(row == 1, off, 0.0))


def _piece_copies(n_rows, row_a, row_b, make_copy, action, bits):
    m = lax.shift_right_logical(n_rows, RUN_SHIFT)

    def piece(bit):
        rows = RUN_ALIGN << bit
        lo = (m & ((1 << bit) - 1)) * RUN_ALIGN

        @pl.when(((m >> bit) & 1) == 1)
        def _():
            cp = make_copy(pl.multiple_of(row_a + lo, RUN_ALIGN), pl.multiple_of(row_b + lo, RUN_ALIGN), rows)
            if action == "start":
                cp.start()
            else:
                cp.wait()

    small = min(bits, 3)
    for bit in range(small):
        piece(bit)
    if bits > small:
        @pl.when(lax.shift_right_logical(m, small) != 0)
        def _():
            for bit in range(small, bits):
                piece(bit)


def _run_copies(tile, n8_tbl, off_tbl, base_tbl, make_copy, action):
    def per_expert(e, c):
        j = tile * N_EXPERTS + e
        _piece_copies(n8_tbl[j], off_tbl[j], base_tbl[j], make_copy, action, RUN_BITS)
        return c

    lax.fori_loop(0, N_EXPERTS, per_expert, 0)


def _unpack_pairs(u):
    hi = lax.bitcast_convert_type(u & jnp.uint32(0xFFFF0000), F32)
    lo = lax.bitcast_convert_type(u << 16, F32)
    return hi, lo


def _pack_pairs(a, b):
    return lax.bitcast_convert_type(a, jnp.uint32) | (lax.bitcast_convert_type(b, jnp.uint32) >> 16)


def _slot_onehot(slot, weights=None):
    lane = lax.broadcasted_iota(jnp.int32, (ROW_TILE, SORT_ROWS), 1)
    p = jnp.zeros((ROW_TILE, SORT_ROWS), F32)
    for k in range(TOP_K):
        hit = lane == slot[:, k:k + 1].astype(jnp.int32)
        p = jnp.where(hit, 1.0 if weights is None else weights[:, k:k + 1], p)
    return p.astype(BF16)


def _dispatch_body(n8_tbl, off_tbl, base_tbl, tail_n, tail_base, nused, h_ref, slot_ref, xs_hbm,
                   sorted_scr, zero_scr, sem):
    tile = pl.program_id(0)

    @pl.when(tile == 0)
    def _():
        zero_scr[...] = jnp.zeros_like(zero_scr)

        def make_zero_copy(_, dst_row, rows):
            return pltpu.make_async_copy(zero_scr.at[pl.ds(0, rows)], xs_hbm.at[pl.ds(dst_row, rows)], sem.at[2])

        for action in ("start", "wait"):
            def per_expert(e, c):
                _piece_copies(tail_n[e], 0, tail_base[e], make_zero_copy, action, TAIL_BITS)
                return c

            def per_block(b, c):
                cp = make_zero_copy(0, pl.multiple_of(b * MOE_BM, MOE_BM), MOE_BM)
                if action == "start":
                    cp.start()
                else:
                    cp.wait()
                return c

            lax.fori_loop(0, N_EXPERTS, per_expert, 0)
            lax.fori_loop(nused[0], MOE_NB, per_block, 0)

    slot_t = slot_ref[...].T
    row = lax.broadcasted_iota(jnp.int32, (SORT_ROWS, ROW_TILE), 0)
    p = jnp.zeros((SORT_ROWS, ROW_TILE), F32)
    for k in range(TOP_K):
        p = jnp.where(row == slot_t[k:k + 1, :].astype(jnp.int32), 1.0, p)
    srt = jnp.dot(p.astype(BF16), h_ref[...], preferred_element_type=F32)
    cur = tile & 1
    sorted_scr[cur] = _pack_pairs(srt[:, :HALF_D], srt[:, HALF_D:])

    def copies_from(buf):
        def make_copy(src_row, dst_row, rows):
            return pltpu.make_async_copy(sorted_scr.at[buf, pl.ds(src_row, rows)],
                                         xs_hbm.at[pl.ds(dst_row, rows)], sem.at[buf])
        return make_copy

    _run_copies(tile, n8_tbl, off_tbl, base_tbl, copies_from(cur), "start")

    @pl.when(tile > 0)
    def _():
        _run_copies(tile - 1, n8_tbl, off_tbl, base_tbl, copies_from(1 - cur), "wait")

    @pl.when(tile == N_TILES - 1)
    def _():
        _run_copies(tile, n8_tbl, off_tbl, base_tbl, copies_from(cur), "wait")


def _dispatch(tables, tails, h2, slots):
    grid_spec = pltpu.PrefetchScalarGridSpec(
        num_scalar_prefetch=6,
        grid=(N_TILES,),
        in_specs=[pl.BlockSpec((ROW_TILE, D_MODEL), lambda i, *_: (i, 0)),
                  pl.BlockSpec((ROW_TILE, LANES), lambda i, *_: (i, 0))],
        out_specs=pl.BlockSpec(memory_space=pl.ANY),
        scratch_shapes=[pltpu.VMEM((2, SORT_ROWS, HALF_D), jnp.uint32),
                        pltpu.VMEM((MOE_BM, HALF_D), jnp.uint32), pltpu.SemaphoreType.DMA((3,))],
    )
    return pl.pallas_call(
        _dispatch_body,
        grid_spec=grid_spec,
        out_shape=jax.ShapeDtypeStruct((MOE_ROWS, HALF_D), jnp.uint32),
        compiler_params=_params(("arbitrary",)),
        name="dispatch",
    )(*tables, *tails, h2, slots)


def _experts_body(bexp_ref, nused_ref, first_ref, next_ref, slot_ref, valid_ref, x_ref, wgu_hbm, bgu_ref, wdn_hbm,
                  bdn_ref, y_ref, wgu32, wdn32, wgu16, wdn16, sem, *, layer):
    del nused_ref
    i = pl.program_id(0)

    def weight_copies(expert, s):
        return (pltpu.make_async_copy(wgu_hbm.at[layer, expert], wgu32.at[s], sem.at[0, s]),
                pltpu.make_async_copy(wdn_hbm.at[layer, expert], wdn32.at[s], sem.at[1, s]))

    @pl.when(first_ref[i] == 1)
    def _():
        s = slot_ref[i]

        @pl.when(i == 0)
        def _():
            for cp in weight_copies(bexp_ref[i], s):
                cp.start()

        for cp in weight_copies(bexp_ref[i], s):
            cp.wait()

        @pl.when(next_ref[i] >= 0)
        def _():
            for cp in weight_copies(next_ref[i], 1 - s):
                cp.start()

        wgu16[...] = wgu32[s].astype(BF16)
        wdn16[...] = wdn32[s].astype(BF16)

    def compute(nrows):
        rows = pl.ds(0, nrows)
        x = jnp.concatenate(_unpack_pairs(x_ref[rows, :]), axis=1).astype(BF16)
        gu = jnp.dot(x, wgu16[...], preferred_element_type=F32) + bgu_ref[0, 0]
        gate = jnp.minimum(gu[:, :D_FF], SWIGLU_LIMIT)
        up = jnp.clip(gu[:, D_FF:], -SWIGLU_LIMIT, SWIGLU_LIMIT)
        glu = gate * _sigmoid(gate * SWIGLU_ALPHA)
        act = ((up + 1.0) * glu).astype(BF16)
        y = jnp.dot(act, wdn16[...], preferred_element_type=F32) + bdn_ref[0, 0]
        y = y.astype(BF16).astype(F32)
        y_ref[rows, :] = _pack_pairs(y[:, :HALF_D], y[:, HALF_D:])

    valid = valid_ref[i]

    @pl.when(valid > MOE_SUB)
    def _():
        compute(MOE_BM)

    @pl.when(valid <= MOE_SUB)
    def _():
        y_ref[pl.ds(MOE_SUB, MOE_SUB), :] = jnp.zeros((MOE_SUB, HALF_D), jnp.uint32)

    @pl.when((valid > 0) & (valid <= MOE_SUB))
    def _():
        compute(MOE_SUB)

    @pl.when(valid == 0)
    def _():
        y_ref[pl.ds(0, MOE_SUB), :] = jnp.zeros((MOE_SUB, HALF_D), jnp.uint32)


def _experts(sched, xs, wgu, bgu, wdn, bdn, layer):
    e = lambda i, be, *_: (layer, be[i], 0, 0)
    grid_spec = pltpu.PrefetchScalarGridSpec(
        num_scalar_prefetch=6,
        grid=(MOE_NB,),
        in_specs=[
            pl.BlockSpec((MOE_BM, HALF_D), lambda i, be, nu, *_: (jnp.minimum(i, nu[0] - 1), 0)),
            pl.BlockSpec(memory_space=pl.ANY),
            pl.BlockSpec((1, 1, 1, 2 * D_FF), e),
            pl.BlockSpec(memory_space=pl.ANY),
            pl.BlockSpec((1, 1, 1, D_MODEL), e),
        ],
        out_specs=pl.BlockSpec((MOE_BM, HALF_D), lambda i, *_: (i, 0)),
        scratch_shapes=[pltpu.VMEM((2, D_MODEL, 2 * D_FF), F32), pltpu.VMEM((2, D_FF, D_MODEL), F32),
                        pltpu.VMEM((D_MODEL, 2 * D_FF), BF16), pltpu.VMEM((D_FF, D_MODEL), BF16),
                        pltpu.SemaphoreType.DMA((2, 2))],
    )
    return pl.pallas_call(
        functools.partial(_experts_body, layer=layer),
        grid_spec=grid_spec,
        out_shape=jax.ShapeDtypeStruct((MOE_ROWS, HALF_D), jnp.uint32),
        compiler_params=_params(("arbitrary",)),
        name="experts",
    )(*sched, xs, wgu, bgu, wdn, bdn)


def _combine_body(n8_tbl, off_tbl, base_tbl, y_hbm, slot_ref, meta_ref, x1_ref, mod_ref, nf_ref, o_ref,
                  ybuf, sem, *, final, tile0, ntiles):
    step = pl.program_id(0)
    tile = tile0 + step
    cur = step & 1

    def fetch(t, buf):
        _run_copies(t, n8_tbl, off_tbl, base_tbl, copies_into(buf), "start")

    def copies_into(buf):
        def make_copy(buf_row, hbm_row, rows):
            return pltpu.make_async_copy(y_hbm.at[pl.ds(hbm_row, rows)],
                                         ybuf.at[buf, pl.ds(buf_row, rows)], sem.at[buf])
        return make_copy

    @pl.when(step == 0)
    def _():
        ybuf[...] = jnp.zeros_like(ybuf)
        fetch(tile, cur)

    @pl.when(step + 1 < ntiles)
    def _():
        fetch(tile + 1, 1 - cur)

    _run_copies(tile, n8_tbl, off_tbl, base_tbl, copies_into(cur), "wait")
    pw = _slot_onehot(slot_ref[...], meta_ref[...][:, TOP_K:2 * TOP_K])
    hi, lo = _unpack_pairs(ybuf[cur])
    ff = jnp.concatenate([jnp.dot(pw, hi.astype(BF16), preferred_element_type=F32),
                          jnp.dot(pw, lo.astype(BF16), preferred_element_type=F32)], axis=1)
    x2 = x1_ref[...] + mod_ref[0][5:6] * ff
    if final:
        x2 = _rms(x2) * nf_ref[...]
    o_ref[...] = x2


def _combine(tables, y, slots, meta, x1, mod3, nfinal, *, final, tile0=0, ntiles=N_TILES):
    grid_spec = pltpu.PrefetchScalarGridSpec(
        num_scalar_prefetch=3,
        grid=(ntiles,),
        in_specs=[pl.BlockSpec(memory_space=pl.ANY),
                  pl.BlockSpec((ROW_TILE, LANES), lambda i, *_: (tile0 + i, 0)),
                  pl.BlockSpec((ROW_TILE, LANES), lambda i, *_: (tile0 + i, 0)),
                  pl.BlockSpec((ROW_TILE, D_MODEL), lambda i, *_: (tile0 + i, 0)),
                  pl.BlockSpec((1, 6, D_MODEL), lambda i, *_: (_mod_row(tile0 + i, ROW_TILE), 0, 0)),
                  pl.BlockSpec((1, D_MODEL), lambda i, *_: (0, 0))],
        out_specs=pl.BlockSpec((ROW_TILE, D_MODEL), lambda i, *_: (i, 0)),
        scratch_shapes=[pltpu.VMEM((2, SORT_ROWS, HALF_D), jnp.uint32), pltpu.SemaphoreType.DMA((2,))],
    )
    return pl.pallas_call(
        functools.partial(_combine_body, final=final, tile0=tile0, ntiles=ntiles),
        grid_spec=grid_spec,
        out_shape=jax.ShapeDtypeStruct((ntiles * ROW_TILE, D_MODEL), F32),
        compiler_params=_params(("arbitrary",)),
        name="combine",
    )(*tables, y, slots, meta, x1, mod3, nfinal)


def _cumsum0(x):
    n = x.shape[0]
    tri = jnp.asarray(np.tril(np.ones((n, n), np.bool_)))
    tri = tri.reshape((n, n) + (1,) * (x.ndim - 1))
    return jnp.sum(jnp.where(tri, x[None], 0), axis=1)


def _lookup(table, idx):
    hit = idx[:, None] == jnp.arange(table.shape[0], dtype=idx.dtype)[None, :]
    return jnp.sum(jnp.where(hit, table[None, :], 0), axis=1)


def _state_to_blockdiag(s):
    eye = jnp.eye(A_HEADS, dtype=s.dtype)
    t = jnp.einsum("bdhkv,hg->bdhvgk", s, eye)
    return t.reshape(s.shape[0], 2, A_VAL_W, A_KEY_W)


def kernel(x_prompt, x_sample, state_gla, cache_k, cache_v, c, c_ctx, w_ada, b_ada, norm_mix, norm_ffn,
           norm_final, w_in, w_out, gla_gate_w, gla_gate_b, gla_norm, q_norm, k_norm, router_w, router_b,
           expert_w_gu, expert_b_gu, expert_w_dn, expert_b_dn):
    xs = (x_prompt.reshape(N_CTX, D_MODEL), x_sample.reshape(N_DEC, D_MODEL))
    cond = jnp.zeros((MOD_ROWS, D_MODEL), F32).at[0].set(c_ctx).at[1:1 + DEC_BATCH].set(c)
    mod = _ada(cond, w_ada, b_ada)

    gm256 = jnp.asarray(np.kron(np.eye(A_HEADS), np.ones((A_DV, A_DV))) / A_DV, BF16)
    gm128 = jnp.asarray(np.kron(np.eye(2), np.ones((HEAD_DIM, HEAD_DIM))) / HEAD_DIM, BF16)
    c64, s64 = _dft_mats(F_GROUP_W, F_GROUP_W ** -0.5)
    cb = jnp.asarray(np.kron(np.eye(F_GROUPS), c64), BF16)
    sb = jnp.asarray(np.kron(np.eye(F_GROUPS), s64), BF16)
    dft_ctx = [jnp.asarray(m, BF16) for m in _dft_mats(SEQ, SEQ ** -0.5)]
    dft_dec = [jnp.asarray(m, BF16) for m in _dft_mats(DEC_SEQ, DEC_SEQ ** -0.5)]
    rope_tabs = _rope_tables()
    zero_state = jnp.zeros((BATCH, 2, A_VAL_W, A_KEY_W), F32)

    new_s, new_k, new_v = [], [], []
    for l in range(DEPTH):
        mod3 = mod[l].reshape(MOD_ROWS, 6, D_MODEL)
        w_l = w_in[l]
        w_r = jnp.concatenate([w_l[:, :768], w_l[:, 800:], w_l[:, 768:800],
                               jnp.zeros((D_MODEL, PROJ_W - 1824), F32)], axis=1).astype(BF16)
        proj = _inproj(xs, mod3, norm_mix[l][None, :], w_r)

        wg = jnp.zeros((LANES, 2 * A_KEY_W), F32)
        wg = wg.at[0:GATE_RANK, 0:A_KEY_W].set(gla_gate_w[l, 0])
        wg = wg.at[GATE_RANK:2 * GATE_RANK, A_KEY_W:].set(gla_gate_w[l, 1])
        gb = gla_gate_b[l].reshape(1, 2 * A_KEY_W)
        gn = jnp.tile(gla_norm[l], A_HEADS)[None, :]
        s0_dec = _state_to_blockdiag(state_gla[:, l])
        oa_ctx, sfin = _gla(proj, wg, gb, zero_state, gn, gm256, nseq=BATCH, seq=SEQ, row_block0=0)
        oa_dec, _ = _gla(proj, wg, gb, s0_dec, gn, gm256, nseq=DEC_BATCH, seq=DEC_SEQ,
                         row_block0=N_CTX // DEC_SEQ)
        of_ctx = _fnet(proj, cb, sb, *dft_ctx, nseq=BATCH, seq=SEQ, row_block0=0)
        of_dec = _fnet(proj, cb, sb, *dft_dec, nseq=DEC_BATCH, seq=DEC_SEQ, row_block0=N_CTX // DEC_SEQ)
        qn = jnp.tile(q_norm[l], 2)[None, :]
        kn = jnp.tile(k_norm[l], 2)[None, :]
        oc_ctx, ka = _attn(proj, qn, kn, gm128, nseq=BATCH, seq=SEQ, row_block0=0)
        past = (cache_k[:, l].reshape(DEC_BATCH, PAST_LEN, C_KV_W),
                cache_v[:, l].reshape(DEC_BATCH, PAST_LEN, C_KV_W))
        oc_dec, _ = _attn(proj, qn, kn, gm128, nseq=DEC_BATCH, seq=DEC_SEQ,
                          row_block0=N_CTX // DEC_SEQ, past=past, tables=rope_tabs)
        new_s.append(sfin.reshape(BATCH, 2, A_HEADS, A_DK, A_DV))
        new_k.append(ka.reshape(BATCH, SEQ, C_KV_HEADS, HEAD_DIM))
        new_v.append(proj[:N_CTX, COL_CV:COL_CV + C_KV_W].reshape(BATCH, SEQ, C_KV_HEADS, HEAD_DIM))

        rw = jnp.zeros((D_MODEL, LANES), F32).at[:, :N_EXPERTS].set(router_w[l])
        rb = jnp.zeros((1, LANES), F32).at[0, :N_EXPERTS].set(router_b[l])
        x1, h2, meta, slots, info = _outproj((oa_ctx, of_ctx, oc_ctx), (oa_dec, of_dec, oc_dec),
                                             w_out[l].astype(BF16), xs, mod3, norm_ffn[l][None, :], rw, rb)

        n8 = info[:, 0, :N_EXPERTS].astype(jnp.int32)
        off = info[:, 1, :N_EXPERTS].astype(jnp.int32)
        cnt = jnp.sum(n8, axis=0)
        pcnt = (cnt + MOE_BM - 1) // MOE_BM * MOE_BM
        pend = _cumsum0(pcnt)
        base = (pend - pcnt)[None, :] + _cumsum0(n8) - n8
        tables = (n8.reshape(-1), off.reshape(-1), base.reshape(-1))
        blk = jnp.arange(MOE_NB, dtype=jnp.int32)
        blk_exp = jnp.minimum(jnp.sum((pend[None, :] <= (blk * MOE_BM)[:, None]).astype(jnp.int32), axis=1),
                              N_EXPERTS - 1)
        nused = (pend[-1:] // MOE_BM).astype(jnp.int32)
        tails = (pcnt - cnt, pend - (pcnt - cnt), nused)
        prev_exp = jnp.concatenate([jnp.full((1,), -1, jnp.int32), blk_exp[:-1]])
        first = ((blk < nused[0]) & (blk_exp != prev_exp)).astype(jnp.int32)
        end_blk = _lookup(pend // MOE_BM, blk_exp)
        nxt = jnp.where(end_blk < nused[0], _lookup(blk_exp, jnp.minimum(end_blk, MOE_NB - 1)), -1)
        wslot = (_cumsum0(first) - 1) & 1
        start_blk = (pend - pcnt) // MOE_BM
        valid = jnp.where(blk < nused[0],
                          jnp.clip(_lookup(cnt, blk_exp) - (blk - _lookup(start_blk, blk_exp)) * MOE_BM,
                                   0, MOE_BM), 0)
        sched = (blk_exp, nused, first, nxt, wslot, valid)

        xs = _dispatch(tables, tails, h2, slots)
        y = _experts(sched, xs, expert_w_gu,
                     expert_b_gu.reshape(DEPTH, N_EXPERTS, 1, 2 * D_FF), expert_w_dn,
                     expert_b_dn.reshape(DEPTH, N_EXPERTS, 1, D_MODEL), l)
        comb = functools.partial(_combine, tables, y, slots, meta, x1, mod3, norm_final[None, :])
        if l < DEPTH - 1:
            xs = (comb(final=False),)
        else:
            ctx_tiles = N_CTX // ROW_TILE
            xs = (comb(final=True, tile0=0, ntiles=ctx_tiles),
                  comb(final=True, tile0=ctx_tiles, ntiles=N_TILES - ctx_tiles))

    y_prompt = xs[0].reshape(BATCH, SEQ, D_MODEL)
    y_sample = xs[1].reshape(DEC_BATCH, DEC_SEQ, D_MODEL)
    return (y_prompt, y_sample, jnp.stack(new_s, axis=1), jnp.stack(new_k, axis=1), jnp.stack(new_v, axis=1))
```

```python
import functools

import jax
import jax.numpy as jnp
import numpy as np
from jax import lax
from jax.experimental import pallas as pl
from jax.experimental.pallas import tpu as pltpu

D_MODEL = 1024
BATCH = 32
SEQ = 256
DEPTH = 2
DEC_BATCH = 8
DEC_SEQ = 1024
PAST_LEN = 512
GRID_W = 64
A_HEADS = 4
A_DV = 64
A_DK = 32
A_KEY_W = 128
A_VAL_W = 256
GATE_RANK = 16
GATE_TEMP = 16.0
GLA_CHUNK = 32
F_GROUPS = 4
F_GROUP_W = 64
F_WIDTH = 256
HEAD_DIM = 64
C_HEADS = 8
C_KV_HEADS = 2
C_Q_W = 512
C_KV_W = 128
ROPE_THETA = 10000.0
N_EXPERTS = 32
TOP_K = 4
D_FF = 1024
SWIGLU_LIMIT = 7.0
SWIGLU_ALPHA = 1.702
EPS = 1e-6

N_CTX = BATCH * SEQ
N_DEC = DEC_BATCH * DEC_SEQ
N_TOK = N_CTX + N_DEC
MOD_ROWS = 16

COL_GLA = 0
COL_PF = 768
COL_CQ = 1024
COL_CK = 1536
COL_CV = 1664
COL_LR = 1792
PROJ_W = 1920

LANES = 128
TM = 512
GLA_TILE = 256
Q_TILE = 512
MOE_BM = 512
MOE_SUB = 128
MOE_SUB_SHIFT = 7
ROW_TILE = 256
N_TILES = N_TOK // ROW_TILE
RUN_ALIGN = 8
RUN_SHIFT = 3
RUN_BITS = 6
TAIL_BITS = 6
SORT_ROWS = ROW_TILE * TOP_K + N_EXPERTS * RUN_ALIGN
HALF_D = D_MODEL // 2
MOE_NB = (N_TOK * TOP_K + N_TILES * N_EXPERTS * (RUN_ALIGN - 1)) // MOE_BM + N_EXPERTS
MOE_ROWS = MOE_NB * MOE_BM
VMEM_LIMIT = 56 * 1024 * 1024

F32 = jnp.float32
BF16 = jnp.bfloat16


def _params(sem, vmem=VMEM_LIMIT):
    return pltpu.CompilerParams(dimension_semantics=sem, vmem_limit_bytes=vmem)


def _bdot(a, b):
    return jnp.dot(a.astype(BF16), b.astype(BF16), preferred_element_type=F32)


def _split(a):
    hi = a.astype(BF16)
    lo = (a - hi.astype(F32)).astype(BF16)
    return hi, lo


def _dot3(a, b):
    ah, al = _split(a)
    bh, bl = _split(b)
    d = functools.partial(jnp.dot, preferred_element_type=F32)
    return d(ah, bh) + (d(ah, bl) + d(al, bh))


def _dot2(a, b16):
    ah, al = _split(a)
    d = functools.partial(jnp.dot, preferred_element_type=F32)
    return d(ah, b16) + d(al, b16)


def _sigmoid(x):
    return 1.0 / (1.0 + jnp.exp(-x))


def _mod_row(i, tm):
    tok = i * tm
    return jnp.where(tok < N_CTX, 0, 1 + (tok - N_CTX) // DEC_SEQ)


def _ada_body(c_ref, w_ref, b_ref, o_ref):
    c = c_ref[...]
    o_ref[0] = _dot3(c * _sigmoid(c), w_ref[0]) + b_ref[0]


def _ada(cond, w_ada, b_ada):
    return pl.pallas_call(
        _ada_body,
        grid=(DEPTH, 6),
        in_specs=[
            pl.BlockSpec((MOD_ROWS, D_MODEL), lambda l, j: (0, 0)),
            pl.BlockSpec((1, D_MODEL, D_MODEL), lambda l, j: (l, 0, j)),
            pl.BlockSpec((1, 1, D_MODEL), lambda l, j: (l, 0, j)),
        ],
        out_specs=pl.BlockSpec((1, MOD_ROWS, D_MODEL), lambda l, j: (l, 0, j)),
        out_shape=jax.ShapeDtypeStruct((DEPTH, MOD_ROWS, 6 * D_MODEL), F32),
        compiler_params=_params(("parallel", "parallel")),
        name="ada",
    )(cond, w_ada, b_ada.reshape(DEPTH, 1, 6 * D_MODEL))


def _rms(x):
    return x * lax.rsqrt(jnp.mean(x * x, axis=-1, keepdims=True) + EPS)


def _inproj_body(*refs):
    *x_refs, mod_ref, g_ref, w_ref, o_ref = refs
    if len(x_refs) == 2:
        x = jnp.where(pl.program_id(0) < N_CTX // TM, x_refs[0][...], x_refs[1][...])
    else:
        x = x_refs[0][...]
    mod = mod_ref[0]
    h = _rms(x) * g_ref[...] * (1.0 + mod[1:2]) + mod[0:1]
    o_ref[...] = jnp.dot(h.astype(BF16), w_ref[...], preferred_element_type=F32)


def _inproj(xs, mod3, gain, w):
    n_ctx_tiles = N_CTX // TM
    if len(xs) == 2:
        x_specs = [pl.BlockSpec((TM, D_MODEL), lambda i: (jnp.minimum(i, n_ctx_tiles - 1), 0)),
                   pl.BlockSpec((TM, D_MODEL), lambda i: (jnp.maximum(i - n_ctx_tiles, 0), 0))]
    else:
        x_specs = [pl.BlockSpec((TM, D_MODEL), lambda i: (i, 0))]
    return pl.pallas_call(
        _inproj_body,
        grid=(N_TOK // TM,),
        in_specs=x_specs + [
            pl.BlockSpec((1, 6, D_MODEL), lambda i: (_mod_row(i, TM), 0, 0)),
            pl.BlockSpec((1, D_MODEL), lambda i: (0, 0)),
            pl.BlockSpec((D_MODEL, PROJ_W), lambda i: (0, 0)),
        ],
        out_specs=pl.BlockSpec((TM, PROJ_W), lambda i: (i, 0)),
        out_shape=jax.ShapeDtypeStruct((N_TOK, PROJ_W), F32),
        compiler_params=_params(("parallel",)),
        name="inproj",
    )(*xs, mod3, gain, w)


def _log_sigmoid(x):
    return jnp.minimum(x, 0.0) - jnp.log(1.0 + jnp.exp(-jnp.abs(x)))


def _chunk_cumsum(x, pos, forward):
    rows = x.shape[0]
    for s in (1, 2, 4, 8, 16):
        if forward:
            x = x + jnp.where(pos >= s, pltpu.roll(x, s, 0), 0.0)
        else:
            x = x + jnp.where(pos < GLA_CHUNK - s, pltpu.roll(x, rows - s, 0), 0.0)
    return x


def _gla_tile(q, k, v16, la, st, forward):
    R = GLA_TILE
    nch = R // GLA_CHUNK
    pos = lax.broadcasted_iota(jnp.int32, (R, 1), 0) & (GLA_CHUNK - 1)
    b = _chunk_cumsum(la, pos, forward)
    b3 = b.reshape(nch, GLA_CHUNK, A_KEY_W)
    tot3 = b3[:, GLA_CHUNK - 1:GLA_CHUNK, :] if forward else b3[:, 0:1, :]
    totb = jnp.broadcast_to(tot3, (nch, GLA_CHUNK, A_KEY_W)).reshape(R, A_KEY_W)
    qt = q * (A_DK ** -0.5) * jnp.exp(b)
    kt = (k * jnp.exp(-b)).astype(BF16)
    kend = (k * jnp.exp(totb - b)).astype(BF16)
    dec3 = jnp.exp(tot3)

    ri = lax.broadcasted_iota(jnp.int32, (R, R), 0)
    ci = lax.broadcasted_iota(jnp.int32, (R, R), 1)
    tri = (ci <= ri) if forward else (ci >= ri)
    mask = ((ri >> 5) == (ci >> 5)) & tri
    lane_k = lax.broadcasted_iota(jnp.int32, (R, A_KEY_W), 1) >> 5
    lane_v = lax.broadcasted_iota(jnp.int32, (R, A_VAL_W), 1) >> 6

    o = jnp.zeros((R, A_VAL_W), F32)
    for h in range(A_HEADS):
        qh = jnp.where(lane_k == h, qt, 0.0).astype(BF16)
        sc = lax.dot_general(qh, kt, (((1,), (1,)), ((), ())), preferred_element_type=F32)
        p = jnp.where(mask, sc, 0.0).astype(BF16)
        oh = jnp.dot(p, v16, preferred_element_type=F32)
        o = jnp.where(lane_v == h, oh, o)

    sr = lax.broadcasted_iota(jnp.int32, (A_VAL_W, A_KEY_W), 0) >> 6
    sl = lax.broadcasted_iota(jnp.int32, (A_VAL_W, A_KEY_W), 1) >> 5
    smask = sr == sl
    qt16 = qt.astype(BF16)
    inter = [None] * nch
    order = range(nch) if forward else range(nch - 1, -1, -1)
    for c in order:
        lo = c * GLA_CHUNK
        inter[c] = lax.dot_general(qt16[lo:lo + GLA_CHUNK], st.astype(BF16),
                                   (((1,), (1,)), ((), ())), preferred_element_type=F32)
        ds = lax.dot_general(v16[lo:lo + GLA_CHUNK], kend[lo:lo + GLA_CHUNK],
                             (((0,), (0,)), ((), ())), preferred_element_type=F32)
        st = st * dec3[c] + jnp.where(smask, ds, 0.0)
    return o + jnp.concatenate(inter, axis=0), st


def _gla_body(gla_ref, lr_ref, wg_ref, gb_ref, s0_ref, gn_ref, gm_ref, o_ref, sfin_ref, la_scr, ob_scr, *, seq):
    ntile = seq // GLA_TILE
    logit = _bdot(lr_ref[...], wg_ref[...]) + gb_ref[...]
    la_scr[...] = _log_sigmoid(logit) * (1.0 / GATE_TEMP)

    def step(i, states):
        new_states = []
        for d, forward in ((0, True), (1, False)):
            t = i if forward else ntile - 1 - i
            rows = pl.ds(pl.multiple_of(t * GLA_TILE, GLA_TILE), GLA_TILE)
            q = gla_ref[rows, 0:128]
            k = gla_ref[rows, 128:256]
            v16 = gla_ref[rows, 256:512].astype(BF16)
            la = la_scr[rows, d * A_KEY_W:(d + 1) * A_KEY_W]
            o, st = _gla_tile(q, k, v16, la, states[d], forward)
            if forward:
                o_ref[rows, :] = o
            else:
                ob_scr[rows, :] = o
            new_states.append(st)
        return tuple(new_states)

    states = lax.fori_loop(0, ntile, step, (s0_ref[0, 0], s0_ref[0, 1]))
    for d in range(2):
        s = states[d].T
        sfin_ref[0, d] = (s[:, 0:A_DV] + s[:, A_DV:2 * A_DV]) + (s[:, 2 * A_DV:3 * A_DV] + s[:, 3 * A_DV:])

    o = o_ref[...] + ob_scr[...]
    ms = _dot2(o * o, gm_ref[...])
    g = gla_ref[:, 512:768]
    o_ref[...] = o * lax.rsqrt(ms + EPS) * gn_ref[...] * (g * _sigmoid(g))


def _gla(proj, wg, gb, s0t, gnorm, gmat, *, nseq, seq, row_block0):
    in_specs = [
        pl.BlockSpec((seq, 768), lambda b: (row_block0 + b, 0)),
        pl.BlockSpec((seq, LANES), lambda b: (row_block0 + b, COL_LR // LANES)),
        pl.BlockSpec((LANES, 2 * A_KEY_W), lambda b: (0, 0)),
        pl.BlockSpec((1, 2 * A_KEY_W), lambda b: (0, 0)),
        pl.BlockSpec((1, 2, A_VAL_W, A_KEY_W), lambda b: (b, 0, 0, 0)),
        pl.BlockSpec((1, A_VAL_W), lambda b: (0, 0)),
        pl.BlockSpec((A_VAL_W, A_VAL_W), lambda b: (0, 0)),
    ]
    return pl.pallas_call(
        functools.partial(_gla_body, seq=seq),
        grid=(nseq,),
        in_specs=in_specs,
        out_specs=[
            pl.BlockSpec((seq, A_VAL_W), lambda b: (b, 0)),
            pl.BlockSpec((1, 2, A_KEY_W, A_DV), lambda b: (b, 0, 0, 0)),
        ],
        out_shape=[
            jax.ShapeDtypeStruct((nseq * seq, A_VAL_W), F32),
            jax.ShapeDtypeStruct((nseq, 2, A_KEY_W, A_DV), F32),
        ],
        scratch_shapes=[pltpu.VMEM((seq, 2 * A_KEY_W), F32), pltpu.VMEM((seq, A_VAL_W), F32)],
        compiler_params=_params(("parallel",)),
        name=f"gla{seq}",
    )(proj, proj, wg, gb, s0t, gnorm, gmat)


def _fnet_body(x_ref, cb_ref, sb_ref, ct_ref, st_ref, o_ref):
    x = x_ref[...].astype(BF16)
    xc = jnp.dot(x, cb_ref[...], preferred_element_type=F32).astype(BF16)
    xs = jnp.dot(x, sb_ref[...], preferred_element_type=F32).astype(BF16)
    o_ref[...] = (jnp.dot(ct_ref[...], xc, preferred_element_type=F32)
                  - jnp.dot(st_ref[...], xs, preferred_element_type=F32))


def _fnet(proj, cb, sb, ct, st, *, nseq, seq, row_block0):
    in_specs = [
        pl.BlockSpec((seq, F_WIDTH), lambda b: (row_block0 + b, COL_PF // F_WIDTH)),
        pl.BlockSpec((F_WIDTH, F_WIDTH), lambda b: (0, 0)),
        pl.BlockSpec((F_WIDTH, F_WIDTH), lambda b: (0, 0)),
        pl.BlockSpec((seq, seq), lambda b: (0, 0)),
        pl.BlockSpec((seq, seq), lambda b: (0, 0)),
    ]
    return pl.pallas_call(
        _fnet_body,
        grid=(nseq,),
        in_specs=in_specs,
        out_specs=pl.BlockSpec((seq, F_WIDTH), lambda b: (b, 0)),
        out_shape=jax.ShapeDtypeStruct((nseq * seq, F_WIDTH), F32),
        compiler_params=_params(("parallel",)),
        name=f"fnet{seq}",
    )(proj, cb, sb, ct, st)


def _dft_mats(n, scale):
    jk = np.outer(np.arange(n), np.arange(n)) % n
    ang = jk.astype(np.float64) * (2.0 * np.pi / n)
    return np.cos(ang) * scale, np.sin(ang) * scale


def _head_rms(x, gm16, gain):
    ms = _dot2(x * x, gm16)
    return x * lax.rsqrt(ms + EPS) * gain


def _rope(x, cos, sin_signed):
    w = x.shape[1]
    lane = lax.broadcasted_iota(jnp.int32, x.shape, 1)
    partner = jnp.where((lane & 31) < 16, pltpu.roll(x, w - 16, 1), pltpu.roll(x, 16, 1))
    return x * cos + partner * sin_signed


def _attn_body(*refs, rope, n_new, n_past):
    if n_past:
        (q_ref, k_ref, v_ref, ck_ref, cv_ref, cq_ref, sq_ref, cosk_ref, sink_ref,
         qn_ref, kn_ref, gm_ref, o_ref, ka_ref, k_scr, ks_scr, v_scr, vs_scr) = refs
    else:
        (q_ref, k_ref, v_ref, qn_ref, kn_ref, gm_ref, o_ref, ka_ref,
         k_scr, ks_scr, v_scr, vs_scr) = refs
    gm = gm_ref[...]

    @pl.when(pl.program_id(1) == 0)
    def _():
        kn = _head_rms(k_ref[...], gm, kn_ref[...])
        ka_ref[...] = kn
        if rope:
            kn = _rope(kn, cosk_ref[...], sink_ref[...])
        v = v_ref[...]
        new = pl.ds(0, n_new)
        k_scr[new, :] = kn.astype(BF16)
        ks_scr[new, :] = pltpu.roll(kn, HEAD_DIM, 1).astype(BF16)
        v_scr[new, :] = v.astype(BF16)
        vs_scr[new, :] = pltpu.roll(v, HEAD_DIM, 1).astype(BF16)
        if n_past:
            past = pl.ds(n_new, n_past)
            ck = ck_ref[0]
            cv = cv_ref[0]
            k_scr[past, :] = ck.astype(BF16)
            ks_scr[past, :] = pltpu.roll(ck, HEAD_DIM, 1).astype(BF16)
            v_scr[past, :] = cv.astype(BF16)
            vs_scr[past, :] = pltpu.roll(cv, HEAD_DIM, 1).astype(BF16)

    q_tile = q_ref.shape[0]
    lane = lax.broadcasted_iota(jnp.int32, (q_tile, LANES), 1)
    low = lane < HEAD_DIM
    for kv in range(C_KV_HEADS):
        q2 = []
        for jp in (2 * kv, 2 * kv + 1):
            q = _head_rms(q_ref[:, jp * LANES:(jp + 1) * LANES], gm, qn_ref[...])
            if rope:
                q = _rope(q, cq_ref[...], sq_ref[...])
            q2.append(q * (HEAD_DIM ** -0.5))
        halves = []
        for half in range(2):
            swapped = (half == 1) != (kv == 1)
            kk = ks_scr[...] if swapped else k_scr[...]
            vv = vs_scr[...] if swapped else v_scr[...]
            keep = low if half == 0 else ~low
            qm = jnp.concatenate([jnp.where(keep, q, 0.0) for q in q2], axis=0).astype(BF16)
            s = lax.dot_general(qm, kk, (((1,), (1,)), ((), ())), preferred_element_type=F32)
            p = jnp.exp(s - jnp.max(s, axis=-1, keepdims=True))
            den = jnp.sum(p, axis=-1, keepdims=True)
            halves.append(jnp.dot(p.astype(BF16), vv, preferred_element_type=F32) / den)
        for n, jp in enumerate((2 * kv, 2 * kv + 1)):
            rows = slice(n * q_tile, (n + 1) * q_tile)
            o_ref[:, jp * LANES:(jp + 1) * LANES] = jnp.where(low, halves[0][rows], halves[1][rows])


def _attn(proj, qn, kn, gm, *, nseq, seq, row_block0, past=None, tables=None):
    q_tile = min(seq, Q_TILE)
    nq = seq // q_tile
    n_past = 0 if past is None else past[0].shape[1]
    qrow = lambda b, j: ((row_block0 + b) * nq + j, COL_CQ // C_Q_W)
    in_specs = [
        pl.BlockSpec((q_tile, C_Q_W), qrow),
        pl.BlockSpec((seq, C_KV_W), lambda b, j: (row_block0 + b, COL_CK // C_KV_W)),
        pl.BlockSpec((seq, C_KV_W), lambda b, j: (row_block0 + b, COL_CV // C_KV_W)),
    ]
    args = [proj, proj, proj]
    if past is not None:
        in_specs += [pl.BlockSpec((1, n_past, C_KV_W), lambda b, j: (b, 0, 0))] * 2
        in_specs += [pl.BlockSpec((q_tile, LANES), lambda b, j: (j, 0))] * 2
        in_specs += [pl.BlockSpec((seq, LANES), lambda b, j: (0, 0))] * 2
        args += [past[0], past[1], tables[0], tables[1], tables[0], tables[1]]
    in_specs += [
        pl.BlockSpec((1, LANES), lambda b, j: (0, 0)),
        pl.BlockSpec((1, LANES), lambda b, j: (0, 0)),
        pl.BlockSpec((LANES, LANES), lambda b, j: (0, 0)),
    ]
    args += [qn, kn, gm]
    s_all = seq + n_past
    return pl.pallas_call(
        functools.partial(_attn_body, rope=past is not None, n_new=seq, n_past=n_past),
        grid=(nseq, nq),
        in_specs=in_specs,
        out_specs=[
            pl.BlockSpec((q_tile, C_Q_W), lambda b, j: (b * nq + j, 0)),
            pl.BlockSpec((seq, C_KV_W), lambda b, j: (b, 0)),
        ],
        out_shape=[
            jax.ShapeDtypeStruct((nseq * seq, C_Q_W), F32),
            jax.ShapeDtypeStruct((nseq * seq, C_KV_W), F32),
        ],
        scratch_shapes=[pltpu.VMEM((s_all, C_KV_W), BF16)] * 4,
        compiler_params=_params(("parallel", "arbitrary")),
        name=f"attn{seq}",
    )(*args)


def _rope_tables():
    t = np.arange(DEC_SEQ)
    rows = (t // GRID_W).astype(np.float32)
    cols = (t % GRID_W).astype(np.float32)
    nf = HEAD_DIM // 4
    inv_freq = (ROPE_THETA ** (-np.arange(nf, dtype=np.float32) / nf)).astype(np.float32)
    d = np.arange(LANES)
    pos = np.where(((d % HEAD_DIM) < HEAD_DIM // 2)[None, :], rows[:, None], cols[:, None])
    ang = (pos * inv_freq[d % nf][None, :]).astype(np.float32)
    sign = np.where((d % 32) < 16, -1.0, 1.0)[None, :]
    return jnp.asarray(np.cos(ang), F32), jnp.asarray(np.sin(ang) * sign, F32)


def _outproj_body(oa_c, of_c, oc_c, oa_d, of_d, oc_d, w_ref, *refs):
    *x_refs, mod_ref, g_ref, rw_ref, rb_ref, x1_ref, h2_ref, meta_ref, slot_ref, info_ref = refs
    mod = mod_ref[0]
    is_ctx = pl.program_id(0) < N_CTX // TM
    pick = lambda c_ref, d_ref: jnp.where(is_ctx, c_ref[...], d_ref[...]).astype(BF16)
    x_in = x_refs[0][...] if len(x_refs) == 1 else jnp.where(is_ctx, x_refs[0][...], x_refs[1][...])
    mixed = (jnp.dot(pick(oa_c, oa_d), w_ref[0:256, :], preferred_element_type=F32)
             + jnp.dot(pick(of_c, of_d), w_ref[256:512, :], preferred_element_type=F32)
             + jnp.dot(pick(oc_c, oc_d), w_ref[512:1024, :], preferred_element_type=F32))
    x1 = x_in + mod[2:3] * mixed
    x1_ref[...] = x1
    h2 = _rms(x1) * g_ref[...] * (1.0 + mod[4:5]) + mod[3:4]
    h2_ref[...] = h2.astype(BF16)
    lane = lax.broadcasted_iota(jnp.int32, (TM, LANES), 1)
    logits = jnp.where(lane < N_EXPERTS, _dot3(h2, rw_ref[...]) + rb_ref[...], -jnp.inf)
    meta = jnp.zeros((TM, LANES), F32)
    tops = []
    for k in range(TOP_K):
        m = jnp.max(logits, axis=-1, keepdims=True)
        idx = jnp.min(jnp.where(logits == m, lane, LANES), axis=-1, keepdims=True)
        hit = lane == idx
        logits = jnp.where(hit, -jnp.inf, logits)
        meta = jnp.where(lane == k, idx.astype(F32), meta)
        tops.append(m)
    ex = [jnp.exp(t - tops[0]) for t in tops]
    den = ex[0] + ex[1] + ex[2] + ex[3]
    for k in range(TOP_K):
        meta = jnp.where(lane == TOP_K + k, ex[k] / den, meta)
    meta_ref[...] = meta
    for t in range(TM // ROW_TILE):
        rows = slice(t * ROW_TILE, (t + 1) * ROW_TILE)
        slot_ref[rows, :], info_ref[t] = _tile_slots(meta[rows])


def _outproj(mix_ctx, mix_dec, w, xs, mod3, gain, rw, rb):
    n_ctx_tiles = N_CTX // TM
    tok = lambda w_: pl.BlockSpec((TM, w_), lambda i: (i, 0))
    ctx = lambda w_: pl.BlockSpec((TM, w_), lambda i: (jnp.minimum(i, n_ctx_tiles - 1), 0))
    dec = lambda w_: pl.BlockSpec((TM, w_), lambda i: (jnp.maximum(i - n_ctx_tiles, 0), 0))
    full = lambda a, b: pl.BlockSpec((a, b), lambda i: (0, 0))
    x_specs = [ctx(D_MODEL), dec(D_MODEL)] if len(xs) == 2 else [tok(D_MODEL)]
    tiles_per_step = TM // ROW_TILE
    return pl.pallas_call(
        _outproj_body,
        grid=(N_TOK // TM,),
        in_specs=[ctx(A_VAL_W), ctx(F_WIDTH), ctx(C_Q_W), dec(A_VAL_W), dec(F_WIDTH), dec(C_Q_W),
                  full(D_MODEL, D_MODEL)] + x_specs + [
                  pl.BlockSpec((1, 6, D_MODEL), lambda i: (_mod_row(i, TM), 0, 0)),
                  full(1, D_MODEL), full(D_MODEL, LANES), full(1, LANES)],
        out_specs=[tok(D_MODEL), tok(D_MODEL), tok(LANES), tok(LANES),
                   pl.BlockSpec((tiles_per_step, 8, LANES), lambda i: (i, 0, 0))],
        out_shape=[jax.ShapeDtypeStruct((N_TOK, D_MODEL), F32),
                   jax.ShapeDtypeStruct((N_TOK, D_MODEL), BF16),
                   jax.ShapeDtypeStruct((N_TOK, LANES), F32),
                   jax.ShapeDtypeStruct((N_TOK, LANES), F32),
                   jax.ShapeDtypeStruct((N_TILES, 8, LANES), F32)],
        compiler_params=_params(("parallel",)),
        name="outproj",
    )(*mix_ctx, *mix_dec, w, *xs, mod3, gain, rw, rb)


def _tile_slots(meta):
    lane = lax.broadcasted_iota(jnp.int32, (ROW_TILE, LANES), 1)
    idx = [meta[:, k:k + 1].astype(jnp.int32) for k in range(TOP_K)]
    sel = jnp.zeros((ROW_TILE, LANES), F32)
    for k in range(TOP_K):
        sel = jnp.where(lane == idx[k], 1.0, sel)
    ri = lax.broadcasted_iota(jnp.int32, (ROW_TILE, ROW_TILE), 0)
    ci = lax.broadcasted_iota(jnp.int32, (ROW_TILE, ROW_TILE), 1)
    lower = jnp.where(ci < ri, 1.0, 0.0).astype(BF16)
    rank = jnp.dot(lower, sel.astype(BF16), preferred_element_type=F32)
    cnt = jnp.sum(sel, axis=0, keepdims=True)
    n8 = jnp.floor((cnt + (RUN_ALIGN - 1)) * (1.0 / RUN_ALIGN)) * RUN_ALIGN
    er = lax.broadcasted_iota(jnp.int32, (LANES, LANES), 0)
    ec = lax.broadcasted_iota(jnp.int32, (LANES, LANES), 1)
    before = jnp.where(er < ec, 1.0, 0.0).astype(BF16)
    off = jnp.dot(jnp.broadcast_to(n8, (8, LANES)).astype(BF16), before, preferred_element_type=F32)
    tot = rank + off[0:1]
    out = jnp.zeros((ROW_TILE, LANES), F32)
    for k in range(TOP_K):
        pk = jnp.sum(jnp.where(lane == idx[k], tot, 0.0), axis=-1, keepdims=True)
        out = jnp.where(lane == k, pk, out)
    row = lax.broadcasted_iota(jnp.int32, (8, LANES), 0)
    return out, jnp.where(row == 0, n8, jnp.where(row == 1, off, 0.0))


def _piece_copies(n_rows, row_a, row_b, make_copy, action, bits):
    m = lax.shift_right_logical(n_rows, RUN_SHIFT)

    def piece(bit):
        rows = RUN_ALIGN << bit
        lo = (m & ((1 << bit) - 1)) * RUN_ALIGN

        @pl.when(((m >> bit) & 1) == 1)
        def _():
            cp = make_copy(pl.multiple_of(row_a + lo, RUN_ALIGN), pl.multiple_of(row_b + lo, RUN_ALIGN), rows)
            if action == "start":
                cp.start()
            else:
                cp.wait()

    small = min(bits, 3)
    for bit in range(small):
        piece(bit)
    if bits > small:
        @pl.when(lax.shift_right_logical(m, small) != 0)
        def _():
            for bit in range(small, bits):
                piece(bit)


def _run_copies(tile, n8_tbl, off_tbl, base_tbl, make_copy, action):
    def per_expert(e, c):
        j = tile * N_EXPERTS + e
        _piece_copies(n8_tbl[j], off_tbl[j], base_tbl[j], make_copy, action, RUN_BITS)
        return c

    lax.fori_loop(0, N_EXPERTS, per_expert, 0)


def _unpack_pairs(u):
    hi = lax.bitcast_convert_type(u & jnp.uint32(0xFFFF0000), F32)
    lo = lax.bitcast_convert_type(u << 16, F32)
    return hi, lo


def _pack_pairs(a, b):
    return lax.bitcast_convert_type(a, jnp.uint32) | (lax.bitcast_convert_type(b, jnp.uint32) >> 16)


def _slot_onehot(slot, weights=None):
    lane = lax.broadcasted_iota(jnp.int32, (ROW_TILE, SORT_ROWS), 1)
    p = jnp.zeros((ROW_TILE, SORT_ROWS), F32)
    for k in range(TOP_K):
        hit = lane == slot[:, k:k + 1].astype(jnp.int32)
        p = jnp.where(hit, 1.0 if weights is None else weights[:, k:k + 1], p)
    return p.astype(BF16)


def _dispatch_body(n8_tbl, off_tbl, base_tbl, tail_n, tail_base, nused, h_ref, slot_ref, xs_hbm,
                   sorted_scr, zero_scr, sem):
    tile = pl.program_id(0)

    @pl.when(tile == 0)
    def _():
        zero_scr[...] = jnp.zeros_like(zero_scr)

        def make_zero_copy(_, dst_row, rows):
            return pltpu.make_async_copy(zero_scr.at[pl.ds(0, rows)], xs_hbm.at[pl.ds(dst_row, rows)], sem.at[2])

        for action in ("start", "wait"):
            def per_expert(e, c):
                _piece_copies(tail_n[e], 0, tail_base[e], make_zero_copy, action, TAIL_BITS)
                return c

            def per_block(b, c):
                cp = make_zero_copy(0, pl.multiple_of(b * MOE_BM, MOE_BM), MOE_BM)
                if action == "start":
                    cp.start()
                else:
                    cp.wait()
                return c

            lax.fori_loop(0, N_EXPERTS, per_expert, 0)
            lax.fori_loop(nused[0], MOE_NB, per_block, 0)

    slot_t = slot_ref[...].T
    row = lax.broadcasted_iota(jnp.int32, (SORT_ROWS, ROW_TILE), 0)
    p = jnp.zeros((SORT_ROWS, ROW_TILE), F32)
    for k in range(TOP_K):
        p = jnp.where(row == slot_t[k:k + 1, :].astype(jnp.int32), 1.0, p)
    srt = jnp.dot(p.astype(BF16), h_ref[...], preferred_element_type=F32)
    cur = tile & 1
    sorted_scr[cur] = _pack_pairs(srt[:, :HALF_D], srt[:, HALF_D:])

    def copies_from(buf):
        def make_copy(src_row, dst_row, rows):
            return pltpu.make_async_copy(sorted_scr.at[buf, pl.ds(src_row, rows)],
                                         xs_hbm.at[pl.ds(dst_row, rows)], sem.at[buf])
        return make_copy

    _run_copies(tile, n8_tbl, off_tbl, base_tbl, copies_from(cur), "start")

    @pl.when(tile > 0)
    def _():
        _run_copies(tile - 1, n8_tbl, off_tbl, base_tbl, copies_from(1 - cur), "wait")

    @pl.when(tile == N_TILES - 1)
    def _():
        _run_copies(tile, n8_tbl, off_tbl, base_tbl, copies_from(cur), "wait")


def _dispatch(tables, tails, h2, slots):
    grid_spec = pltpu.PrefetchScalarGridSpec(
        num_scalar_prefetch=6,
        grid=(N_TILES,),
        in_specs=[pl.BlockSpec((ROW_TILE, D_MODEL), lambda i, *_: (i, 0)),
                  pl.BlockSpec((ROW_TILE, LANES), lambda i, *_: (i, 0))],
        out_specs=pl.BlockSpec(memory_space=pl.ANY),
        scratch_shapes=[pltpu.VMEM((2, SORT_ROWS, HALF_D), jnp.uint32),
                        pltpu.VMEM((MOE_BM, HALF_D), jnp.uint32), pltpu.SemaphoreType.DMA((3,))],
    )
    return pl.pallas_call(
        _dispatch_body,
        grid_spec=grid_spec,
        out_shape=jax.ShapeDtypeStruct((MOE_ROWS, HALF_D), jnp.uint32),
        compiler_params=_params(("arbitrary",)),
        name="dispatch",
    )(*tables, *tails, h2, slots)


def _experts_body(bexp_ref, nused_ref, first_ref, next_ref, slot_ref, valid_ref, x_ref, wgu_hbm, bgu_ref, wdn_hbm,
                  bdn_ref, y_ref, wgu32, wdn32, wgu16, wdn16, sem, *, layer):
    del nused_ref
    i = pl.program_id(0)

    def weight_copies(expert, s):
        return (pltpu.make_async_copy(wgu_hbm.at[layer, expert], wgu32.at[s], sem.at[0, s]),
                pltpu.make_async_copy(wdn_hbm.at[layer, expert], wdn32.at[s], sem.at[1, s]))

    @pl.when(first_ref[i] == 1)
    def _():
        s = slot_ref[i]

        @pl.when(i == 0)
        def _():
            for cp in weight_copies(bexp_ref[i], s):
                cp.start()

        for cp in weight_copies(bexp_ref[i], s):
            cp.wait()

        @pl.when(next_ref[i] >= 0)
        def _():
            for cp in weight_copies(next_ref[i], 1 - s):
                cp.start()

        wgu16[...] = wgu32[s].astype(BF16)
        wdn16[...] = wdn32[s].astype(BF16)

    def compute(nrows):
        rows = pl.ds(0, nrows)
        x = jnp.concatenate(_unpack_pairs(x_ref[rows, :]), axis=1).astype(BF16)
        gu = jnp.dot(x, wgu16[...], preferred_element_type=F32) + bgu_ref[0, 0]
        gate = jnp.minimum(gu[:, :D_FF], SWIGLU_LIMIT)
        up = jnp.clip(gu[:, D_FF:], -SWIGLU_LIMIT, SWIGLU_LIMIT)
        glu = gate * _sigmoid(gate * SWIGLU_ALPHA)
        act = ((up + 1.0) * glu).astype(BF16)
        y = jnp.dot(act, wdn16[...], preferred_element_type=F32) + bdn_ref[0, 0]
        y = y.astype(BF16).astype(F32)
        y_ref[rows, :] = _pack_pairs(y[:, :HALF_D], y[:, HALF_D:])

    pieces = lax.shift_right_logical(valid_ref[i] + (MOE_SUB - 1), MOE_SUB_SHIFT)
    for n in range(MOE_BM // MOE_SUB + 1):
        @pl.when(pieces == n)
        def _():
            if n > 0:
                compute(n * MOE_SUB)
            if n * MOE_SUB < MOE_BM:
                rest = MOE_BM - n * MOE_SUB
                y_ref[pl.ds(n * MOE_SUB, rest), :] = jnp.zeros((rest, HALF_D), jnp.uint32)


def _experts(sched, xs, wgu, bgu, wdn, bdn, layer):
    e = lambda i, be, *_: (layer, be[i], 0, 0)
    grid_spec = pltpu.PrefetchScalarGridSpec(
        num_scalar_prefetch=6,
        grid=(MOE_NB,),
        in_specs=[
            pl.BlockSpec((MOE_BM, HALF_D), lambda i, be, nu, *_: (jnp.minimum(i, nu[0] - 1), 0)),
            pl.BlockSpec(memory_space=pl.ANY),
            pl.BlockSpec((1, 1, 1, 2 * D_FF), e),
            pl.BlockSpec(memory_space=pl.ANY),
            pl.BlockSpec((1, 1, 1, D_MODEL), e),
        ],
        out_specs=pl.BlockSpec((MOE_BM, HALF_D), lambda i, *_: (i, 0)),
        scratch_shapes=[pltpu.VMEM((2, D_MODEL, 2 * D_FF), F32), pltpu.VMEM((2, D_FF, D_MODEL), F32),
                        pltpu.VMEM((D_MODEL, 2 * D_FF), BF16), pltpu.VMEM((D_FF, D_MODEL), BF16),
                        pltpu.SemaphoreType.DMA((2, 2))],
    )
    return pl.pallas_call(
        functools.partial(_experts_body, layer=layer),
        grid_spec=grid_spec,
        out_shape=jax.ShapeDtypeStruct((MOE_ROWS, HALF_D), jnp.uint32),
        compiler_params=_params(("arbitrary",)),
        name="experts",
    )(*sched, xs, wgu, bgu, wdn, bdn)


def _combine_body(n8_tbl, off_tbl, base_tbl, y_hbm, slot_ref, meta_ref, x1_ref, mod_ref, nf_ref, o_ref,
                  ybuf, sem, *, final, tile0, ntiles):
    step = pl.program_id(0)
    tile = tile0 + step
    cur = step & 1

    def fetch(t, buf):
        _run_copies(t, n8_tbl, off_tbl, base_tbl, copies_into(buf), "start")

    def copies_into(buf):
        def make_copy(buf_row, hbm_row, rows):
            return pltpu.make_async_copy(y_hbm.at[pl.ds(hbm_row, rows)],
                                         ybuf.at[buf, pl.ds(buf_row, rows)], sem.at[buf])
        return make_copy

    @pl.when(step == 0)
    def _():
        ybuf[...] = jnp.zeros_like(ybuf)
        fetch(tile, cur)

    @pl.when(step + 1 < ntiles)
    def _():
        fetch(tile + 1, 1 - cur)

    _run_copies(tile, n8_tbl, off_tbl, base_tbl, copies_into(cur), "wait")
    pw = _slot_onehot(slot_ref[...], meta_ref[...][:, TOP_K:2 * TOP_K])
    hi, lo = _unpack_pairs(ybuf[cur])
    ff = jnp.concatenate([jnp.dot(pw, hi.astype(BF16), preferred_element_type=F32),
                          jnp.dot(pw, lo.astype(BF16), preferred_element_type=F32)], axis=1)
    x2 = x1_ref[...] + mod_ref[0][5:6] * ff
    if final:
        x2 = _rms(x2) * nf_ref[...]
    o_ref[...] = x2


def _combine(tables, y, slots, meta, x1, mod3, nfinal, *, final, tile0=0, ntiles=N_TILES):
    grid_spec = pltpu.PrefetchScalarGridSpec(
        num_scalar_prefetch=3,
        grid=(ntiles,),
        in_specs=[pl.BlockSpec(memory_space=pl.ANY),
                  pl.BlockSpec((ROW_TILE, LANES), lambda i, *_: (tile0 + i, 0)),
                  pl.BlockSpec((ROW_TILE, LANES), lambda i, *_: (tile0 + i, 0)),
                  pl.BlockSpec((ROW_TILE, D_MODEL), lambda i, *_: (tile0 + i, 0)),
                  pl.BlockSpec((1, 6, D_MODEL), lambda i, *_: (_mod_row(tile0 + i, ROW_TILE), 0, 0)),
                  pl.BlockSpec((1, D_MODEL), lambda i, *_: (0, 0))],
        out_specs=pl.BlockSpec((ROW_TILE, D_MODEL), lambda i, *_: (i, 0)),
        scratch_shapes=[pltpu.VMEM((2, SORT_ROWS, HALF_D), jnp.uint32), pltpu.SemaphoreType.DMA((2,))],
    )
    return pl.pallas_call(
        functools.partial(_combine_body, final=final, tile0=tile0, ntiles=ntiles),
        grid_spec=grid_spec,
        out_shape=jax.ShapeDtypeStruct((ntiles * ROW_TILE, D_MODEL), F32),
        compiler_params=_params(("arbitrary",)),
        name="combine",
    )(*tables, y, slots, meta, x1, mod3, nfinal)


def _cumsum0(x):
    n = x.shape[0]
    tri = jnp.asarray(np.tril(np.ones((n, n), np.bool_)))
    tri = tri.reshape((n, n) + (1,) * (x.ndim - 1))
    return jnp.sum(jnp.where(tri, x[None], 0), axis=1)


def _lookup(table, idx):
    hit = idx[:, None] == jnp.arange(table.shape[0], dtype=idx.dtype)[None, :]
    return jnp.sum(jnp.where(hit, table[None, :], 0), axis=1)


def _state_to_blockdiag(s):
    eye = jnp.eye(A_HEADS, dtype=s.dtype)
    t = jnp.einsum("bdhkv,hg->bdhvgk", s, eye)
    return t.reshape(s.shape[0], 2, A_VAL_W, A_KEY_W)


def kernel(x_prompt, x_sample, state_gla, cache_k, cache_v, c, c_ctx, w_ada, b_ada, norm_mix, norm_ffn,
           norm_final, w_in, w_out, gla_gate_w, gla_gate_b, gla_norm, q_norm, k_norm, router_w, router_b,
           expert_w_gu, expert_b_gu, expert_w_dn, expert_b_dn):
    xs = (x_prompt.reshape(N_CTX, D_MODEL), x_sample.reshape(N_DEC, D_MODEL))
    cond = jnp.zeros((MOD_ROWS, D_MODEL), F32).at[0].set(c_ctx).at[1:1 + DEC_BATCH].set(c)
    mod = _ada(cond, w_ada, b_ada)

    gm256 = jnp.asarray(np.kron(np.eye(A_HEADS), np.ones((A_DV, A_DV))) / A_DV, BF16)
    gm128 = jnp.asarray(np.kron(np.eye(2), np.ones((HEAD_DIM, HEAD_DIM))) / HEAD_DIM, BF16)
    c64, s64 = _dft_mats(F_GROUP_W, F_GROUP_W ** -0.5)
    cb = jnp.asarray(np.kron(np.eye(F_GROUPS), c64), BF16)
    sb = jnp.asarray(np.kron(np.eye(F_GROUPS), s64), BF16)
    dft_ctx = [jnp.asarray(m, BF16) for m in _dft_mats(SEQ, SEQ ** -0.5)]
    dft_dec = [jnp.asarray(m, BF16) for m in _dft_mats(DEC_SEQ, DEC_SEQ ** -0.5)]
    rope_tabs = _rope_tables()
    zero_state = jnp.zeros((BATCH, 2, A_VAL_W, A_KEY_W), F32)

    new_s, new_k, new_v = [], [], []
    for l in range(DEPTH):
        mod3 = mod[l].reshape(MOD_ROWS, 6, D_MODEL)
        w_l = w_in[l]
        w_r = jnp.concatenate([w_l[:, :768], w_l[:, 800:], w_l[:, 768:800],
                               jnp.zeros((D_MODEL, PROJ_W - 1824), F32)], axis=1).astype(BF16)
        proj = _inproj(xs, mod3, norm_mix[l][None, :], w_r)

        wg = jnp.zeros((LANES, 2 * A_KEY_W), F32)
        wg = wg.at[0:GATE_RANK, 0:A_KEY_W].set(gla_gate_w[l, 0])
        wg = wg.at[GATE_RANK:2 * GATE_RANK, A_KEY_W:].set(gla_gate_w[l, 1])
        gb = gla_gate_b[l].reshape(1, 2 * A_KEY_W)
        gn = jnp.tile(gla_norm[l], A_HEADS)[None, :]
        s0_dec = _state_to_blockdiag(state_gla[:, l])
        oa_ctx, sfin = _gla(proj, wg, gb, zero_state, gn, gm256, nseq=BATCH, seq=SEQ, row_block0=0)
        oa_dec, _ = _gla(proj, wg, gb, s0_dec, gn, gm256, nseq=DEC_BATCH, seq=DEC_SEQ,
                         row_block0=N_CTX // DEC_SEQ)
        of_ctx = _fnet(proj, cb, sb, *dft_ctx, nseq=BATCH, seq=SEQ, row_block0=0)
        of_dec = _fnet(proj, cb, sb, *dft_dec, nseq=DEC_BATCH, seq=DEC_SEQ, row_block0=N_CTX // DEC_SEQ)
        qn = jnp.tile(q_norm[l], 2)[None, :]
        kn = jnp.tile(k_norm[l], 2)[None, :]
        oc_ctx, ka = _attn(proj, qn, kn, gm128, nseq=BATCH, seq=SEQ, row_block0=0)
        past = (cache_k[:, l].reshape(DEC_BATCH, PAST_LEN, C_KV_W),
                cache_v[:, l].reshape(DEC_BATCH, PAST_LEN, C_KV_W))
        oc_dec, _ = _attn(proj, qn, kn, gm128, nseq=DEC_BATCH, seq=DEC_SEQ,
                          row_block0=N_CTX // DEC_SEQ, past=past, tables=rope_tabs)
        new_s.append(sfin.reshape(BATCH, 2, A_HEADS, A_DK, A_DV))
        new_k.append(ka.reshape(BATCH, SEQ, C_KV_HEADS, HEAD_DIM))
        new_v.append(proj[:N_CTX, COL_CV:COL_CV + C_KV_W].reshape(BATCH, SEQ, C_KV_HEADS, HEAD_DIM))

        rw = jnp.zeros((D_MODEL, LANES), F32).at[:, :N_EXPERTS].set(router_w[l])
        rb = jnp.zeros((1, LANES), F32).at[0, :N_EXPERTS].set(router_b[l])
        x1, h2, meta, slots, info = _outproj((oa_ctx, of_ctx, oc_ctx), (oa_dec, of_dec, oc_dec),
                                             w_out[l].astype(BF16), xs, mod3, norm_ffn[l][None, :], rw, rb)

        n8 = info[:, 0, :N_EXPERTS].astype(jnp.int32)
        off = info[:, 1, :N_EXPERTS].astype(jnp.int32)
        cnt = jnp.sum(n8, axis=0)
        pcnt = (cnt + MOE_BM - 1) // MOE_BM * MOE_BM
        pend = _cumsum0(pcnt)
        base = (pend - pcnt)[None, :] + _cumsum0(n8) - n8
        tables = (n8.reshape(-1), off.reshape(-1), base.reshape(-1))
        blk = jnp.arange(MOE_NB, dtype=jnp.int32)
        blk_exp = jnp.minimum(jnp.sum((pend[None, :] <= (blk * MOE_BM)[:, None]).astype(jnp.int32), axis=1),
                              N_EXPERTS - 1)
        nused = (pend[-1:] // MOE_BM).astype(jnp.int32)
        tails = (pcnt - cnt, pend - (pcnt - cnt), nused)
        prev_exp = jnp.concatenate([jnp.full((1,), -1, jnp.int32), blk_exp[:-1]])
        first = ((blk < nused[0]) & (blk_exp != prev_exp)).astype(jnp.int32)
        end_blk = _lookup(pend // MOE_BM, blk_exp)
        nxt = jnp.where(end_blk < nused[0], _lookup(blk_exp, jnp.minimum(end_blk, MOE_NB - 1)), -1)
        wslot = (_cumsum0(first) - 1) & 1
        start_blk = (pend - pcnt) // MOE_BM
        valid = jnp.where(blk < nused[0],
                          jnp.clip(_lookup(cnt, blk_exp) - (blk - _lookup(start_blk, blk_exp)) * MOE_BM,
                                   0, MOE_BM), 0)
        sched = (blk_exp, nused, first, nxt, wslot, valid)

        xs = _dispatch(tables, tails, h2, slots)
        y = _experts(sched, xs, expert_w_gu,
                     expert_b_gu.reshape(DEPTH, N_EXPERTS, 1, 2 * D_FF), expert_w_dn,
                     expert_b_dn.reshape(DEPTH, N_EXPERTS, 1, D_MODEL), l)
        comb = functools.partial(_combine, tables, y, slots, meta, x1, mod3, norm_final[None, :])
        if l < DEPTH - 1:
            xs = (comb(final=False),)
        else:
            ctx_tiles = N_CTX // ROW_TILE
            xs = (comb(final=True, tile0=0, ntiles=ctx_tiles),
                  comb(final=True, tile0=ctx_tiles, ntiles=N_TILES - ctx_tiles))

    y_prompt = xs[0].reshape(BATCH, SEQ, D_MODEL)
    y_sample = xs[1].reshape(DEC_BATCH, DEC_SEQ, D_MODEL)
    return (y_prompt, y_sample, jnp.stack(new_s, axis=1), jnp.stack(new_k, axis=1), jnp.stack(new_v, axis=1))
```

```python
import functools

import jax
import jax.numpy as jnp
import numpy as np
from jax import lax
from jax.experimental import pallas as pl
from jax.experimental.pallas import tpu as pltpu

D_MODEL = 1024
BATCH = 32
SEQ = 256
DEPTH = 2
DEC_BATCH = 8
DEC_SEQ = 1024
PAST_LEN = 512
GRID_W = 64
A_HEADS = 4
A_DV = 64
A_DK = 32
A_KEY_W = 128
A_VAL_W = 256
GATE_RANK = 16
GATE_TEMP = 16.0
GLA_CHUNK = 32
F_GROUPS = 4
F_GROUP_W = 64
F_WIDTH = 256
HEAD_DIM = 64
C_HEADS = 8
C_KV_HEADS = 2
C_Q_W = 512
C_KV_W = 128
ROPE_THETA = 10000.0
N_EXPERTS = 32
TOP_K = 4
D_FF = 1024
SWIGLU_LIMIT = 7.0
SWIGLU_ALPHA = 1.702
EPS = 1e-6

N_CTX = BATCH * SEQ
N_DEC = DEC_BATCH * DEC_SEQ
N_TOK = N_CTX + N_DEC
MOD_ROWS = 16

COL_GLA = 0
COL_PF = 768
COL_CQ = 1024
COL_CK = 1536
COL_CV = 1664
COL_LR = 1792
PROJ_W = 1920

LANES = 128
TM = 512
GLA_TILE = 256
Q_TILE = 512
KEY_CHUNK = 512
MOE_BM = 512
MOE_SUB = 128
MOE_SUB_SHIFT = 7
ROW_TILE = 256
N_TILES = N_TOK // ROW_TILE
RUN_ALIGN = 8
RUN_SHIFT = 3
RUN_BITS = 6
TAIL_BITS = 6
SORT_ROWS = ROW_TILE * TOP_K + N_EXPERTS * RUN_ALIGN
HALF_D = D_MODEL // 2
MOE_NB = (N_TOK * TOP_K + N_TILES * N_EXPERTS * (RUN_ALIGN - 1)) // MOE_BM + N_EXPERTS
MOE_ROWS = MOE_NB * MOE_BM
VMEM_LIMIT = 56 * 1024 * 1024

F32 = jnp.float32
BF16 = jnp.bfloat16


def _params(sem, vmem=VMEM_LIMIT):
    return pltpu.CompilerParams(dimension_semantics=sem, vmem_limit_bytes=vmem)


def _bdot(a, b):
    return jnp.dot(a.astype(BF16), b.astype(BF16), preferred_element_type=F32)


def _split(a):
    hi = a.astype(BF16)
    lo = (a - hi.astype(F32)).astype(BF16)
    return hi, lo


def _dot3(a, b):
    ah, al = _split(a)
    bh, bl = _split(b)
    d = functools.partial(jnp.dot, preferred_element_type=F32)
    return d(ah, bh) + (d(ah, bl) + d(al, bh))


def _dot2(a, b16):
    ah, al = _split(a)
    d = functools.partial(jnp.dot, preferred_element_type=F32)
    return d(ah, b16) + d(al, b16)


def _sigmoid(x):
    return 1.0 / (1.0 + jnp.exp(-x))


def _mod_row(i, tm):
    tok = i * tm
    return jnp.where(tok < N_CTX, 0, 1 + (tok - N_CTX) // DEC_SEQ)


def _ada_body(c_ref, w_ref, b_ref, o_ref):
    c = c_ref[...]
    o_ref[0] = _dot3(c * _sigmoid(c), w_ref[0]) + b_ref[0]


def _ada(cond, w_ada, b_ada):
    return pl.pallas_call(
        _ada_body,
        grid=(DEPTH, 6),
        in_specs=[
            pl.BlockSpec((MOD_ROWS, D_MODEL), lambda l, j: (0, 0)),
            pl.BlockSpec((1, D_MODEL, D_MODEL), lambda l, j: (l, 0, j)),
            pl.BlockSpec((1, 1, D_MODEL), lambda l, j: (l, 0, j)),
        ],
        out_specs=pl.BlockSpec((1, MOD_ROWS, D_MODEL), lambda l, j: (l, 0, j)),
        out_shape=jax.ShapeDtypeStruct((DEPTH, MOD_ROWS, 6 * D_MODEL), F32),
        compiler_params=_params(("parallel", "parallel")),
        name="ada",
    )(cond, w_ada, b_ada.reshape(DEPTH, 1, 6 * D_MODEL))


def _rms(x):
    return x * lax.rsqrt(jnp.mean(x * x, axis=-1, keepdims=True) + EPS)


def _inproj_body(*refs):
    *x_refs, mod_ref, g_ref, w_ref, o_ref = refs
    if len(x_refs) == 2:
        x = jnp.where(pl.program_id(0) < N_CTX // TM, x_refs[0][...], x_refs[1][...])
    else:
        x = x_refs[0][...]
    mod = mod_ref[0]
    h = _rms(x) * g_ref[...] * (1.0 + mod[1:2]) + mod[0:1]
    o_ref[...] = jnp.dot(h.astype(BF16), w_ref[...], preferred_element_type=F32)


def _inproj(xs, mod3, gain, w):
    n_ctx_tiles = N_CTX // TM
    if len(xs) == 2:
        x_specs = [pl.BlockSpec((TM, D_MODEL), lambda i: (jnp.minimum(i, n_ctx_tiles - 1), 0)),
                   pl.BlockSpec((TM, D_MODEL), lambda i: (jnp.maximum(i - n_ctx_tiles, 0), 0))]
    else:
        x_specs = [pl.BlockSpec((TM, D_MODEL), lambda i: (i, 0))]
    return pl.pallas_call(
        _inproj_body,
        grid=(N_TOK // TM,),
        in_specs=x_specs + [
            pl.BlockSpec((1, 6, D_MODEL), lambda i: (_mod_row(i, TM), 0, 0)),
            pl.BlockSpec((1, D_MODEL), lambda i: (0, 0)),
            pl.BlockSpec((D_MODEL, PROJ_W), lambda i: (0, 0)),
        ],
        out_specs=pl.BlockSpec((TM, PROJ_W), lambda i: (i, 0)),
        out_shape=jax.ShapeDtypeStruct((N_TOK, PROJ_W), F32),
        compiler_params=_params(("parallel",)),
        name="inproj",
    )(*xs, mod3, gain, w)


def _log_sigmoid(x):
    return jnp.minimum(x, 0.0) - jnp.log(1.0 + jnp.exp(-jnp.abs(x)))


def _chunk_cumsum(x, pos, forward):
    rows = x.shape[0]
    for s in (1, 2, 4, 8, 16):
        if forward:
            x = x + jnp.where(pos >= s, pltpu.roll(x, s, 0), 0.0)
        else:
            x = x + jnp.where(pos < GLA_CHUNK - s, pltpu.roll(x, rows - s, 0), 0.0)
    return x


def _gla_tile(q, k, v16, la, st, forward):
    R = GLA_TILE
    nch = R // GLA_CHUNK
    pos = lax.broadcasted_iota(jnp.int32, (R, 1), 0) & (GLA_CHUNK - 1)
    b = _chunk_cumsum(la, pos, forward)
    b3 = b.reshape(nch, GLA_CHUNK, A_KEY_W)
    tot3 = b3[:, GLA_CHUNK - 1:GLA_CHUNK, :] if forward else b3[:, 0:1, :]
    totb = jnp.broadcast_to(tot3, (nch, GLA_CHUNK, A_KEY_W)).reshape(R, A_KEY_W)
    qt = q * (A_DK ** -0.5) * jnp.exp(b)
    kt = (k * jnp.exp(-b)).astype(BF16)
    kend = (k * jnp.exp(totb - b)).astype(BF16)
    dec3 = jnp.exp(tot3)

    ri = lax.broadcasted_iota(jnp.int32, (R, R), 0)
    ci = lax.broadcasted_iota(jnp.int32, (R, R), 1)
    tri = (ci <= ri) if forward else (ci >= ri)
    mask = ((ri >> 5) == (ci >> 5)) & tri
    lane_k = lax.broadcasted_iota(jnp.int32, (R, A_KEY_W), 1) >> 5
    lane_v = lax.broadcasted_iota(jnp.int32, (R, A_VAL_W), 1) >> 6

    o = jnp.zeros((R, A_VAL_W), F32)
    for h in range(A_HEADS):
        qh = jnp.where(lane_k == h, qt, 0.0).astype(BF16)
        sc = lax.dot_general(qh, kt, (((1,), (1,)), ((), ())), preferred_element_type=F32)
        p = jnp.where(mask, sc, 0.0).astype(BF16)
        oh = jnp.dot(p, v16, preferred_element_type=F32)
        o = jnp.where(lane_v == h, oh, o)

    sr = lax.broadcasted_iota(jnp.int32, (A_VAL_W, A_KEY_W), 0) >> 6
    sl = lax.broadcasted_iota(jnp.int32, (A_VAL_W, A_KEY_W), 1) >> 5
    smask = sr == sl
    qt16 = qt.astype(BF16)
    inter = [None] * nch
    order = range(nch) if forward else range(nch - 1, -1, -1)
    for c in order:
        lo = c * GLA_CHUNK
        inter[c] = lax.dot_general(qt16[lo:lo + GLA_CHUNK], st.astype(BF16),
                                   (((1,), (1,)), ((), ())), preferred_element_type=F32)
        ds = lax.dot_general(v16[lo:lo + GLA_CHUNK], kend[lo:lo + GLA_CHUNK],
                             (((0,), (0,)), ((), ())), preferred_element_type=F32)
        st = st * dec3[c] + jnp.where(smask, ds, 0.0)
    return o + jnp.concatenate(inter, axis=0), st


def _gla_body(gla_ref, lr_ref, wg_ref, gb_ref, s0_ref, gn_ref, gm_ref, o_ref, sfin_ref, la_scr, ob_scr, *, seq):
    ntile = seq // GLA_TILE
    logit = _bdot(lr_ref[...], wg_ref[...]) + gb_ref[...]
    la_scr[...] = _log_sigmoid(logit) * (1.0 / GATE_TEMP)

    def step(i, states):
        new_states = []
        for d, forward in ((0, True), (1, False)):
            t = i if forward else ntile - 1 - i
            rows = pl.ds(pl.multiple_of(t * GLA_TILE, GLA_TILE), GLA_TILE)
            q = gla_ref[rows, 0:128]
            k = gla_ref[rows, 128:256]
            v16 = gla_ref[rows, 256:512].astype(BF16)
            la = la_scr[rows, d * A_KEY_W:(d + 1) * A_KEY_W]
            o, st = _gla_tile(q, k, v16, la, states[d], forward)
            if forward:
                o_ref[rows, :] = o
            else:
                ob_scr[rows, :] = o
            new_states.append(st)
        return tuple(new_states)

    states = lax.fori_loop(0, ntile, step, (s0_ref[0, 0], s0_ref[0, 1]))
    for d in range(2):
        s = states[d].T
        sfin_ref[0, d] = (s[:, 0:A_DV] + s[:, A_DV:2 * A_DV]) + (s[:, 2 * A_DV:3 * A_DV] + s[:, 3 * A_DV:])

    o = o_ref[...] + ob_scr[...]
    ms = _dot2(o * o, gm_ref[...])
    g = gla_ref[:, 512:768]
    o_ref[...] = o * lax.rsqrt(ms + EPS) * gn_ref[...] * (g * _sigmoid(g))


def _gla(proj, wg, gb, s0t, gnorm, gmat, *, nseq, seq, row_block0):
    in_specs = [
        pl.BlockSpec((seq, 768), lambda b: (row_block0 + b, 0)),
        pl.BlockSpec((seq, LANES), lambda b: (row_block0 + b, COL_LR // LANES)),
        pl.BlockSpec((LANES, 2 * A_KEY_W), lambda b: (0, 0)),
        pl.BlockSpec((1, 2 * A_KEY_W), lambda b: (0, 0)),
        pl.BlockSpec((1, 2, A_VAL_W, A_KEY_W), lambda b: (b, 0, 0, 0)),
        pl.BlockSpec((1, A_VAL_W), lambda b: (0, 0)),
        pl.BlockSpec((A_VAL_W, A_VAL_W), lambda b: (0, 0)),
    ]
    return pl.pallas_call(
        functools.partial(_gla_body, seq=seq),
        grid=(nseq,),
        in_specs=in_specs,
        out_specs=[
            pl.BlockSpec((seq, A_VAL_W), lambda b: (b, 0)),
            pl.BlockSpec((1, 2, A_KEY_W, A_DV), lambda b: (b, 0, 0, 0)),
        ],
        out_shape=[
            jax.ShapeDtypeStruct((nseq * seq, A_VAL_W), F32),
            jax.ShapeDtypeStruct((nseq, 2, A_KEY_W, A_DV), F32),
        ],
        scratch_shapes=[pltpu.VMEM((seq, 2 * A_KEY_W), F32), pltpu.VMEM((seq, A_VAL_W), F32)],
        compiler_params=_params(("parallel",)),
        name=f"gla{seq}",
    )(proj, proj, wg, gb, s0t, gnorm, gmat)


def _fnet_body(x_ref, cb_ref, sb_ref, ct_ref, st_ref, o_ref):
    x = x_ref[...].astype(BF16)
    xc = jnp.dot(x, cb_ref[...], preferred_element_type=F32).astype(BF16)
    xs = jnp.dot(x, sb_ref[...], preferred_element_type=F32).astype(BF16)
    o_ref[...] = (jnp.dot(ct_ref[...], xc, preferred_element_type=F32)
                  - jnp.dot(st_ref[...], xs, preferred_element_type=F32))


def _fnet(proj, cb, sb, ct, st, *, nseq, seq, row_block0):
    in_specs = [
        pl.BlockSpec((seq, F_WIDTH), lambda b: (row_block0 + b, COL_PF // F_WIDTH)),
        pl.BlockSpec((F_WIDTH, F_WIDTH), lambda b: (0, 0)),
        pl.BlockSpec((F_WIDTH, F_WIDTH), lambda b: (0, 0)),
        pl.BlockSpec((seq, seq), lambda b: (0, 0)),
        pl.BlockSpec((seq, seq), lambda b: (0, 0)),
    ]
    return pl.pallas_call(
        _fnet_body,
        grid=(nseq,),
        in_specs=in_specs,
        out_specs=pl.BlockSpec((seq, F_WIDTH), lambda b: (b, 0)),
        out_shape=jax.ShapeDtypeStruct((nseq * seq, F_WIDTH), F32),
        compiler_params=_params(("parallel",)),
        name=f"fnet{seq}",
    )(proj, cb, sb, ct, st)


def _dft_mats(n, scale):
    jk = np.outer(np.arange(n), np.arange(n)) % n
    ang = jk.astype(np.float64) * (2.0 * np.pi / n)
    return np.cos(ang) * scale, np.sin(ang) * scale


def _head_rms(x, gm16, gain):
    ms = _dot2(x * x, gm16)
    return x * lax.rsqrt(ms + EPS) * gain


def _rope(x, cos, sin_signed):
    w = x.shape[1]
    lane = lax.broadcasted_iota(jnp.int32, x.shape, 1)
    partner = jnp.where((lane & 31) < 16, pltpu.roll(x, w - 16, 1), pltpu.roll(x, 16, 1))
    return x * cos + partner * sin_signed


def _attn_body(*refs, rope, n_new, n_past):
    if n_past:
        (q_ref, k_ref, v_ref, ck_ref, cv_ref, cq_ref, sq_ref, cosk_ref, sink_ref,
         qn_ref, kn_ref, gm_ref, o_ref, ka_ref, k_scr, ks_scr, v_scr, vs_scr) = refs
    else:
        (q_ref, k_ref, v_ref, qn_ref, kn_ref, gm_ref, o_ref, ka_ref,
         k_scr, ks_scr, v_scr, vs_scr) = refs
    gm = gm_ref[...]

    @pl.when(pl.program_id(1) == 0)
    def _():
        kn = _head_rms(k_ref[...], gm, kn_ref[...])
        ka_ref[...] = kn
        if rope:
            kn = _rope(kn, cosk_ref[...], sink_ref[...])
        v = v_ref[...]
        new = pl.ds(0, n_new)
        k_scr[new, :] = kn.astype(BF16)
        ks_scr[new, :] = pltpu.roll(kn, HEAD_DIM, 1).astype(BF16)
        v_scr[new, :] = v.astype(BF16)
        vs_scr[new, :] = pltpu.roll(v, HEAD_DIM, 1).astype(BF16)
        if n_past:
            past = pl.ds(n_new, n_past)
            ck = ck_ref[0]
            cv = cv_ref[0]
            k_scr[past, :] = ck.astype(BF16)
            ks_scr[past, :] = pltpu.roll(ck, HEAD_DIM, 1).astype(BF16)
            v_scr[past, :] = cv.astype(BF16)
            vs_scr[past, :] = pltpu.roll(cv, HEAD_DIM, 1).astype(BF16)

    q_tile = q_ref.shape[0]
    lane = lax.broadcasted_iota(jnp.int32, (q_tile, LANES), 1)
    low = lane < HEAD_DIM
    for kv in range(C_KV_HEADS):
        q2 = []
        for jp in (2 * kv, 2 * kv + 1):
            q = _head_rms(q_ref[:, jp * LANES:(jp + 1) * LANES], gm, qn_ref[...])
            if rope:
                q = _rope(q, cq_ref[...], sq_ref[...])
            q2.append(q * (HEAD_DIM ** -0.5))
        halves = []
        for half in range(2):
            swapped = (half == 1) != (kv == 1)
            kk = ks_scr[...] if swapped else k_scr[...]
            vv = vs_scr[...] if swapped else v_scr[...]
            keep = low if half == 0 else ~low
            qm = jnp.concatenate([jnp.where(keep, q, 0.0) for q in q2], axis=0).astype(BF16)
            n_keys = kk.shape[0]
            step = min(n_keys, KEY_CHUNK)
            m = den = acc = None
            for c0 in range(0, n_keys, step):
                s = lax.dot_general(qm, kk[c0:c0 + step], (((1,), (1,)), ((), ())), preferred_element_type=F32)
                m_c = jnp.max(s, axis=-1, keepdims=True)
                if m is None:
                    m = m_c
                    p = jnp.exp(s - m)
                    den = jnp.sum(p, axis=-1, keepdims=True)
                    acc = jnp.dot(p.astype(BF16), vv[c0:c0 + step], preferred_element_type=F32)
                else:
                    m_new = jnp.maximum(m, m_c)
                    scale = jnp.exp(m - m_new)
                    p = jnp.exp(s - m_new)
                    den = den * scale + jnp.sum(p, axis=-1, keepdims=True)
                    acc = acc * scale + jnp.dot(p.astype(BF16), vv[c0:c0 + step], preferred_element_type=F32)
                    m = m_new
            halves.append(acc / den)
        for n, jp in enumerate((2 * kv, 2 * kv + 1)):
            rows = slice(n * q_tile, (n + 1) * q_tile)
            o_ref[:, jp * LANES:(jp + 1) * LANES] = jnp.where(low, halves[0][rows], halves[1][rows])


def _attn(proj, qn, kn, gm, *, nseq, seq, row_block0, past=None, tables=None):
    q_tile = min(seq, Q_TILE)
    nq = seq // q_tile
    n_past = 0 if past is None else past[0].shape[1]
    qrow = lambda b, j: ((row_block0 + b) * nq + j, COL_CQ // C_Q_W)
    in_specs = [
        pl.BlockSpec((q_tile, C_Q_W), qrow),
        pl.BlockSpec((seq, C_KV_W), lambda b, j: (row_block0 + b, COL_CK // C_KV_W)),
        pl.BlockSpec((seq, C_KV_W), lambda b, j: (row_block0 + b, COL_CV // C_KV_W)),
    ]
    args = [proj, proj, proj]
    if past is not None:
        in_specs += [pl.BlockSpec((1, n_past, C_KV_W), lambda b, j: (b, 0, 0))] * 2
        in_specs += [pl.BlockSpec((q_tile, LANES), lambda b, j: (j, 0))] * 2
        in_specs += [pl.BlockSpec((seq, LANES), lambda b, j: (0, 0))] * 2
        args += [past[0], past[1], tables[0], tables[1], tables[0], tables[1]]
    in_specs += [
        pl.BlockSpec((1, LANES), lambda b, j: (0, 0)),
        pl.BlockSpec((1, LANES), lambda b, j: (0, 0)),
        pl.BlockSpec((LANES, LANES), lambda b, j: (0, 0)),
    ]
    args += [qn, kn, gm]
    s_all = seq + n_past
    return pl.pallas_call(
        functools.partial(_attn_body, rope=past is not None, n_new=seq, n_past=n_past),
        grid=(nseq, nq),
        in_specs=in_specs,
        out_specs=[
            pl.BlockSpec((q_tile, C_Q_W), lambda b, j: (b * nq + j, 0)),
            pl.BlockSpec((seq, C_KV_W), lambda b, j: (b, 0)),
        ],
        out_shape=[
            jax.ShapeDtypeStruct((nseq * seq, C_Q_W), F32),
            jax.ShapeDtypeStruct((nseq * seq, C_KV_W), F32),
        ],
        scratch_shapes=[pltpu.VMEM((s_all, C_KV_W), BF16)] * 4,
        compiler_params=_params(("parallel", "arbitrary")),
        name=f"attn{seq}",
    )(*args)


def _rope_tables():
    t = np.arange(DEC_SEQ)
    rows = (t // GRID_W).astype(np.float32)
    cols = (t % GRID_W).astype(np.float32)
    nf = HEAD_DIM // 4
    inv_freq = (ROPE_THETA ** (-np.arange(nf, dtype=np.float32) / nf)).astype(np.float32)
    d = np.arange(LANES)
    pos = np.where(((d % HEAD_DIM) < HEAD_DIM // 2)[None, :], rows[:, None], cols[:, None])
    ang = (pos * inv_freq[d % nf][None, :]).astype(np.float32)
    sign = np.where((d % 32) < 16, -1.0, 1.0)[None, :]
    return jnp.asarray(np.cos(ang), F32), jnp.asarray(np.sin(ang) * sign, F32)


def _outproj_body(oa_c, of_c, oc_c, oa_d, of_d, oc_d, w_ref, *refs):
    *x_refs, mod_ref, g_ref, rw_ref, rb_ref, x1_ref, h2_ref, meta_ref, slot_ref, info_ref = refs
    mod = mod_ref[0]
    is_ctx = pl.program_id(0) < N_CTX // TM
    pick = lambda c_ref, d_ref: jnp.where(is_ctx, c_ref[...], d_ref[...]).astype(BF16)
    x_in = x_refs[0][...] if len(x_refs) == 1 else jnp.where(is_ctx, x_refs[0][...], x_refs[1][...])
    mixed = (jnp.dot(pick(oa_c, oa_d), w_ref[0:256, :], preferred_element_type=F32)
             + jnp.dot(pick(of_c, of_d), w_ref[256:512, :], preferred_element_type=F32)
             + jnp.dot(pick(oc_c, oc_d), w_ref[512:1024, :], preferred_element_type=F32))
    x1 = x_in + mod[2:3] * mixed
    x1_ref[...] = x1
    h2 = _rms(x1) * g_ref[...] * (1.0 + mod[4:5]) + mod[3:4]
    h2_ref[...] = h2.astype(BF16)
    lane = lax.broadcasted_iota(jnp.int32, (TM, LANES), 1)
    logits = jnp.where(lane < N_EXPERTS, _dot3(h2, rw_ref[...]) + rb_ref[...], -jnp.inf)
    meta = jnp.zeros((TM, LANES), F32)
    tops = []
    for k in range(TOP_K):
        m = jnp.max(logits, axis=-1, keepdims=True)
        idx = jnp.min(jnp.where(logits == m, lane, LANES), axis=-1, keepdims=True)
        hit = lane == idx
        logits = jnp.where(hit, -jnp.inf, logits)
        meta = jnp.where(lane == k, idx.astype(F32), meta)
        tops.append(m)
    ex = [jnp.exp(t - tops[0]) for t in tops]
    den = ex[0] + ex[1] + ex[2] + ex[3]
    for k in range(TOP_K):
        meta = jnp.where(lane == TOP_K + k, ex[k] / den, meta)
    meta_ref[...] = meta
    for t in range(TM // ROW_TILE):
        rows = slice(t * ROW_TILE, (t + 1) * ROW_TILE)
        slot_ref[rows, :], info_ref[t] = _tile_slots(meta[rows])


def _outproj(mix_ctx, mix_dec, w, xs, mod3, gain, rw, rb):
    n_ctx_tiles = N_CTX // TM
    tok = lambda w_: pl.BlockSpec((TM, w_), lambda i: (i, 0))
    ctx = lambda w_: pl.BlockSpec((TM, w_), lambda i: (jnp.minimum(i, n_ctx_tiles - 1), 0))
    dec = lambda w_: pl.BlockSpec((TM, w_), lambda i: (jnp.maximum(i - n_ctx_tiles, 0), 0))
    full = lambda a, b: pl.BlockSpec((a, b), lambda i: (0, 0))
    x_specs = [ctx(D_MODEL), dec(D_MODEL)] if len(xs) == 2 else [tok(D_MODEL)]
    tiles_per_step = TM // ROW_TILE
    return pl.pallas_call(
        _outproj_body,
        grid=(N_TOK // TM,),
        in_specs=[ctx(A_VAL_W), ctx(F_WIDTH), ctx(C_Q_W), dec(A_VAL_W), dec(F_WIDTH), dec(C_Q_W),
                  full(D_MODEL, D_MODEL)] + x_specs + [
                  pl.BlockSpec((1, 6, D_MODEL), lambda i: (_mod_row(i, TM), 0, 0)),
                  full(1, D_MODEL), full(D_MODEL, LANES), full(1, LANES)],
        out_specs=[tok(D_MODEL), tok(D_MODEL), tok(LANES), tok(LANES),
                   pl.BlockSpec((tiles_per_step, 8, LANES), lambda i: (i, 0, 0))],
        out_shape=[jax.ShapeDtypeStruct((N_TOK, D_MODEL), F32),
                   jax.ShapeDtypeStruct((N_TOK, D_MODEL), BF16),
                   jax.ShapeDtypeStruct((N_TOK, LANES), F32),
                   jax.ShapeDtypeStruct((N_TOK, LANES), F32),
                   jax.ShapeDtypeStruct((N_TILES, 8, LANES), F32)],
        compiler_params=_params(("parallel",)),
        name="outproj",
    )(*mix_ctx, *mix_dec, w, *xs, mod3, gain, rw, rb)


def _tile_slots(meta):
    lane = lax.broadcasted_iota(jnp.int32, (ROW_TILE, LANES), 1)
    idx = [meta[:, k:k + 1].astype(jnp.int32) for k in range(TOP_K)]
    sel = jnp.zeros((ROW_TILE, LANES), F32)
    for k in range(TOP_K):
        sel = jnp.where(lane == idx[k], 1.0, sel)
    ri = lax.broadcasted_iota(jnp.int32, (ROW_TILE, ROW_TILE), 0)
    ci = lax.broadcasted_iota(jnp.int32, (ROW_TILE, ROW_TILE), 1)
    lower = jnp.where(ci < ri, 1.0, 0.0).astype(BF16)
    rank = jnp.dot(lower, sel.astype(BF16), preferred_element_type=F32)
    cnt = jnp.sum(sel, axis=0, keepdims=True)
    n8 = jnp.floor((cnt + (RUN_ALIGN - 1)) * (1.0 / RUN_ALIGN)) * RUN_ALIGN
    er = lax.broadcasted_iota(jnp.int32, (LANES, LANES), 0)
    ec = lax.broadcasted_iota(jnp.int32, (LANES, LANES), 1)
    before = jnp.where(er < ec, 1.0, 0.0).astype(BF16)
    off = jnp.dot(jnp.broadcast_to(n8, (8, LANES)).astype(BF16), before, preferred_element_type=F32)
    tot = rank + off[0:1]
    out = jnp.zeros((ROW_TILE, LANES), F32)
    for k in range(TOP_K):
        pk = jnp.sum(jnp.where(lane == idx[k], tot, 0.0), axis=-1, keepdims=True)
        out = jnp.where(lane == k, pk, out)
    row = lax.broadcasted_iota(jnp.int32, (8, LANES), 0)
    return out, jnp.where(row == 0, n8, jnp.where(row == 1, off, 0.0))


def _piece_copies(n_rows, row_a, row_b, make_copy, action, bits):
    m = lax.shift_right_logical(n_rows, RUN_SHIFT)

    def piece(bit):
        rows = RUN_ALIGN << bit
        lo = (m & ((1 << bit) - 1)) * RUN_ALIGN

        @pl.when(((m >> bit) & 1) == 1)
        def _():
            cp = make_copy(pl.multiple_of(row_a + lo, RUN_ALIGN), pl.multiple_of(row_b + lo, RUN_ALIGN), rows)
            if action == "start":
                cp.start()
            else:
                cp.wait()

    small = min(bits, 3)
    for bit in range(small):
        piece(bit)
    if bits > small:
        @pl.when(lax.shift_right_logical(m, small) != 0)
        def _():
            for bit in range(small, bits):
                piece(bit)


def _run_copies(tile, n8_tbl, off_tbl, base_tbl, make_copy, action):
    def per_expert(e, c):
        j = tile * N_EXPERTS + e
        _piece_copies(n8_tbl[j], off_tbl[j], base_tbl[j], make_copy, action, RUN_BITS)
        return c

    lax.fori_loop(0, N_EXPERTS, per_expert, 0)


def _unpack_pairs(u):
    hi = lax.bitcast_convert_type(u & jnp.uint32(0xFFFF0000), F32)
    lo = lax.bitcast_convert_type(u << 16, F32)
    return hi, lo


def _pack_pairs(a, b):
    return lax.bitcast_convert_type(a, jnp.uint32) | (lax.bitcast_convert_type(b, jnp.uint32) >> 16)


def _slot_onehot(slot, weights=None):
    lane = lax.broadcasted_iota(jnp.int32, (ROW_TILE, SORT_ROWS), 1)
    p = jnp.zeros((ROW_TILE, SORT_ROWS), F32)
    for k in range(TOP_K):
        hit = lane == slot[:, k:k + 1].astype(jnp.int32)
        p = jnp.where(hit, 1.0 if weights is None else weights[:, k:k + 1], p)
    return p.astype(BF16)


def _dispatch_body(n8_tbl, off_tbl, base_tbl, tail_n, tail_base, nused, h_ref, slot_ref, xs_hbm,
                   sorted_scr, zero_scr, sem):
    tile = pl.program_id(0)

    @pl.when(tile == 0)
    def _():
        zero_scr[...] = jnp.zeros_like(zero_scr)

        def make_zero_copy(_, dst_row, rows):
            return pltpu.make_async_copy(zero_scr.at[pl.ds(0, rows)], xs_hbm.at[pl.ds(dst_row, rows)], sem.at[2])

        for action in ("start", "wait"):
            def per_expert(e, c):
                _piece_copies(tail_n[e], 0, tail_base[e], make_zero_copy, action, TAIL_BITS)
                return c

            def per_block(b, c):
                cp = make_zero_copy(0, pl.multiple_of(b * MOE_BM, MOE_BM), MOE_BM)
                if action == "start":
                    cp.start()
                else:
                    cp.wait()
                return c

            lax.fori_loop(0, N_EXPERTS, per_expert, 0)
            lax.fori_loop(nused[0], MOE_NB, per_block, 0)

    slot_t = slot_ref[...].T
    row = lax.broadcasted_iota(jnp.int32, (SORT_ROWS, ROW_TILE), 0)
    p = jnp.zeros((SORT_ROWS, ROW_TILE), F32)
    for k in range(TOP_K):
        p = jnp.where(row == slot_t[k:k + 1, :].astype(jnp.int32), 1.0, p)
    srt = jnp.dot(p.astype(BF16), h_ref[...], preferred_element_type=F32)
    cur = tile & 1
    sorted_scr[cur] = _pack_pairs(srt[:, :HALF_D], srt[:, HALF_D:])

    def copies_from(buf):
        def make_copy(src_row, dst_row, rows):
            return pltpu.make_async_copy(sorted_scr.at[buf, pl.ds(src_row, rows)],
                                         xs_hbm.at[pl.ds(dst_row, rows)], sem.at[buf])
        return make_copy

    _run_copies(tile, n8_tbl, off_tbl, base_tbl, copies_from(cur), "start")

    @pl.when(tile > 0)
    def _():
        _run_copies(tile - 1, n8_tbl, off_tbl, base_tbl, copies_from(1 - cur), "wait")

    @pl.when(tile == N_TILES - 1)
    def _():
        _run_copies(tile, n8_tbl, off_tbl, base_tbl, copies_from(cur), "wait")


def _dispatch(tables, tails, h2, slots):
    grid_spec = pltpu.PrefetchScalarGridSpec(
        num_scalar_prefetch=6,
        grid=(N_TILES,),
        in_specs=[pl.BlockSpec((ROW_TILE, D_MODEL), lambda i, *_: (i, 0)),
                  pl.BlockSpec((ROW_TILE, LANES), lambda i, *_: (i, 0))],
        out_specs=pl.BlockSpec(memory_space=pl.ANY),
        scratch_shapes=[pltpu.VMEM((2, SORT_ROWS, HALF_D), jnp.uint32),
                        pltpu.VMEM((MOE_BM, HALF_D), jnp.uint32), pltpu.SemaphoreType.DMA((3,))],
    )
    return pl.pallas_call(
        _dispatch_body,
        grid_spec=grid_spec,
        out_shape=jax.ShapeDtypeStruct((MOE_ROWS, HALF_D), jnp.uint32),
        compiler_params=_params(("arbitrary",)),
        name="dispatch",
    )(*tables, *tails, h2, slots)


def _experts_body(bexp_ref, nused_ref, first_ref, next_ref, slot_ref, valid_ref, x_ref, wgu_hbm, bgu_ref, wdn_hbm,
                  bdn_ref, y_ref, wgu32, wdn32, wgu16, wdn16, sem, *, layer):
    del nused_ref
    i = pl.program_id(0)

    def weight_copies(expert, s):
        return (pltpu.make_async_copy(wgu_hbm.at[layer, expert], wgu32.at[s], sem.at[0, s]),
                pltpu.make_async_copy(wdn_hbm.at[layer, expert], wdn32.at[s], sem.at[1, s]))

    @pl.when(first_ref[i] == 1)
    def _():
        s = slot_ref[i]

        @pl.when(i == 0)
        def _():
            for cp in weight_copies(bexp_ref[i], s):
                cp.start()

        for cp in weight_copies(bexp_ref[i], s):
            cp.wait()

        @pl.when(next_ref[i] >= 0)
        def _():
            for cp in weight_copies(next_ref[i], 1 - s):
                cp.start()

        wgu16[...] = wgu32[s].astype(BF16)
        wdn16[...] = wdn32[s].astype(BF16)

    def compute(nrows):
        rows = pl.ds(0, nrows)
        x = jnp.concatenate(_unpack_pairs(x_ref[rows, :]), axis=1).astype(BF16)
        gu = jnp.dot(x, wgu16[...], preferred_element_type=F32) + bgu_ref[0, 0]
        gate = jnp.minimum(gu[:, :D_FF], SWIGLU_LIMIT)
        up = jnp.clip(gu[:, D_FF:], -SWIGLU_LIMIT, SWIGLU_LIMIT)
        glu = gate * _sigmoid(gate * SWIGLU_ALPHA)
        act = ((up + 1.0) * glu).astype(BF16)
        y = jnp.dot(act, wdn16[...], preferred_element_type=F32) + bdn_ref[0, 0]
        y = y.astype(BF16).astype(F32)
        y_ref[rows, :] = _pack_pairs(y[:, :HALF_D], y[:, HALF_D:])

    pieces = lax.shift_right_logical(valid_ref[i] + (MOE_SUB - 1), MOE_SUB_SHIFT)
    for n in range(MOE_BM // MOE_SUB + 1):
        @pl.when(pieces == n)
        def _():
            if n > 0:
                compute(n * MOE_SUB)
            if n * MOE_SUB < MOE_BM:
                rest = MOE_BM - n * MOE_SUB
                y_ref[pl.ds(n * MOE_SUB, rest), :] = jnp.zeros((rest, HALF_D), jnp.uint32)


def _experts(sched, xs, wgu, bgu, wdn, bdn, layer):
    e = lambda i, be, *_: (layer, be[i], 0, 0)
    grid_spec = pltpu.PrefetchScalarGridSpec(
        num_scalar_prefetch=6,
        grid=(MOE_NB,),
        in_specs=[
            pl.BlockSpec((MOE_BM, HALF_D), lambda i, be, nu, *_: (jnp.minimum(i, nu[0] - 1), 0)),
            pl.BlockSpec(memory_space=pl.ANY),
            pl.BlockSpec((1, 1, 1, 2 * D_FF), e),
            pl.BlockSpec(memory_space=pl.ANY),
            pl.BlockSpec((1, 1, 1, D_MODEL), e),
        ],
        out_specs=pl.BlockSpec((MOE_BM, HALF_D), lambda i, *_: (i, 0)),
        scratch_shapes=[pltpu.VMEM((2, D_MODEL, 2 * D_FF), F32), pltpu.VMEM((2, D_FF, D_MODEL), F32),
                        pltpu.VMEM((D_MODEL, 2 * D_FF), BF16), pltpu.VMEM((D_FF, D_MODEL), BF16),
                        pltpu.SemaphoreType.DMA((2, 2))],
    )
    return pl.pallas_call(
        functools.partial(_experts_body, layer=layer),
        grid_spec=grid_spec,
        out_shape=jax.ShapeDtypeStruct((MOE_ROWS, HALF_D), jnp.uint32),
        compiler_params=_params(("arbitrary",)),
        name="experts",
    )(*sched, xs, wgu, bgu, wdn, bdn)


def _combine_body(n8_tbl, off_tbl, base_tbl, y_hbm, slot_ref, meta_ref, x1_ref, mod_ref, nf_ref, o_ref,
                  ybuf, sem, *, final, tile0, ntiles):
    step = pl.program_id(0)
    tile = tile0 + step
    cur = step & 1

    def fetch(t, buf):
        _run_copies(t, n8_tbl, off_tbl, base_tbl, copies_into(buf), "start")

    def copies_into(buf):
        def make_copy(buf_row, hbm_row, rows):
            return pltpu.make_async_copy(y_hbm.at[pl.ds(hbm_row, rows)],
                                         ybuf.at[buf, pl.ds(buf_row, rows)], sem.at[buf])
        return make_copy

    @pl.when(step == 0)
    def _():
        ybuf[...] = jnp.zeros_like(ybuf)
        fetch(tile, cur)

    @pl.when(step + 1 < ntiles)
    def _():
        fetch(tile + 1, 1 - cur)

    _run_copies(tile, n8_tbl, off_tbl, base_tbl, copies_into(cur), "wait")
    pw = _slot_onehot(slot_ref[...], meta_ref[...][:, TOP_K:2 * TOP_K])
    hi, lo = _unpack_pairs(ybuf[cur])
    ff = jnp.concatenate([jnp.dot(pw, hi.astype(BF16), preferred_element_type=F32),
                          jnp.dot(pw, lo.astype(BF16), preferred_element_type=F32)], axis=1)
    x2 = x1_ref[...] + mod_ref[0][5:6] * ff
    if final:
        x2 = _rms(x2) * nf_ref[...]
    o_ref[...] = x2


def _combine(tables, y, slots, meta, x1, mod3, nfinal, *, final, tile0=0, ntiles=N_TILES):
    grid_spec = pltpu.PrefetchScalarGridSpec(
        num_scalar_prefetch=3,
        grid=(ntiles,),
        in_specs=[pl.BlockSpec(memory_space=pl.ANY),
                  pl.BlockSpec((ROW_TILE, LANES), lambda i, *_: (tile0 + i, 0)),
                  pl.BlockSpec((ROW_TILE, LANES), lambda i, *_: (tile0 + i, 0)),
                  pl.BlockSpec((ROW_TILE, D_MODEL), lambda i, *_: (tile0 + i, 0)),
                  pl.BlockSpec((1, 6, D_MODEL), lambda i, *_: (_mod_row(tile0 + i, ROW_TILE), 0, 0)),
                  pl.BlockSpec((1, D_MODEL), lambda i, *_: (0, 0))],
        out_specs=pl.BlockSpec((ROW_TILE, D_MODEL), lambda i, *_: (i, 0)),
        scratch_shapes=[pltpu.VMEM((2, SORT_ROWS, HALF_D), jnp.uint32), pltpu.SemaphoreType.DMA((2,))],
    )
    return pl.pallas_call(
        functools.partial(_combine_body, final=final, tile0=tile0, ntiles=ntiles),
        grid_spec=grid_spec,
        out_shape=jax.ShapeDtypeStruct((ntiles * ROW_TILE, D_MODEL), F32),
        compiler_params=_params(("arbitrary",)),
        name="combine",
    )(*tables, y, slots, meta, x1, mod3, nfinal)


def _cumsum0(x):
    n = x.shape[0]
    tri = jnp.asarray(np.tril(np.ones((n, n), np.bool_)))
    tri = tri.reshape((n, n) + (1,) * (x.ndim - 1))
    return jnp.sum(jnp.where(tri, x[None], 0), axis=1)


def _lookup(table, idx):
    hit = idx[:, None] == jnp.arange(table.shape[0], dtype=idx.dtype)[None, :]
    return jnp.sum(jnp.where(hit, table[None, :], 0), axis=1)


def _state_to_blockdiag(s):
    eye = jnp.eye(A_HEADS, dtype=s.dtype)
    t = jnp.einsum("bdhkv,hg->bdhvgk", s, eye)
    return t.reshape(s.shape[0], 2, A_VAL_W, A_KEY_W)


def kernel(x_prompt, x_sample, state_gla, cache_k, cache_v, c, c_ctx, w_ada, b_ada, norm_mix, norm_ffn,
           norm_final, w_in, w_out, gla_gate_w, gla_gate_b, gla_norm, q_norm, k_norm, router_w, router_b,
           expert_w_gu, expert_b_gu, expert_w_dn, expert_b_dn):
    xs = (x_prompt.reshape(N_CTX, D_MODEL), x_sample.reshape(N_DEC, D_MODEL))
    cond = jnp.zeros((MOD_ROWS, D_MODEL), F32).at[0].set(c_ctx).at[1:1 + DEC_BATCH].set(c)
    mod = _ada(cond, w_ada, b_ada)

    gm256 = jnp.asarray(np.kron(np.eye(A_HEADS), np.ones((A_DV, A_DV))) / A_DV, BF16)
    gm128 = jnp.asarray(np.kron(np.eye(2), np.ones((HEAD_DIM, HEAD_DIM))) / HEAD_DIM, BF16)
    c64, s64 = _dft_mats(F_GROUP_W, F_GROUP_W ** -0.5)
    cb = jnp.asarray(np.kron(np.eye(F_GROUPS), c64), BF16)
    sb = jnp.asarray(np.kron(np.eye(F_GROUPS), s64), BF16)
    dft_ctx = [jnp.asarray(m, BF16) for m in _dft_mats(SEQ, SEQ ** -0.5)]
    dft_dec = [jnp.asarray(m, BF16) for m in _dft_mats(DEC_SEQ, DEC_SEQ ** -0.5)]
    rope_tabs = _rope_tables()
    zero_state = jnp.zeros((BATCH, 2, A_VAL_W, A_KEY_W), F32)

    new_s, new_k, new_v = [], [], []
    for l in range(DEPTH):
        mod3 = mod[l].reshape(MOD_ROWS, 6, D_MODEL)
        w_l = w_in[l]
        w_r = jnp.concatenate([w_l[:, :768], w_l[:, 800:], w_l[:, 768:800],
                               jnp.zeros((D_MODEL, PROJ_W - 1824), F32)], axis=1).astype(BF16)
        proj = _inproj(xs, mod3, norm_mix[l][None, :], w_r)

        wg = jnp.zeros((LANES, 2 * A_KEY_W), F32)
        wg = wg.at[0:GATE_RANK, 0:A_KEY_W].set(gla_gate_w[l, 0])
        wg = wg.at[GATE_RANK:2 * GATE_RANK, A_KEY_W:].set(gla_gate_w[l, 1])
        gb = gla_gate_b[l].reshape(1, 2 * A_KEY_W)
        gn = jnp.tile(gla_norm[l], A_HEADS)[None, :]
        s0_dec = _state_to_blockdiag(state_gla[:, l])
        oa_ctx, sfin = _gla(proj, wg, gb, zero_state, gn, gm256, nseq=BATCH, seq=SEQ, row_block0=0)
        oa_dec, _ = _gla(proj, wg, gb, s0_dec, gn, gm256, nseq=DEC_BATCH, seq=DEC_SEQ,
                         row_block0=N_CTX // DEC_SEQ)
        of_ctx = _fnet(proj, cb, sb, *dft_ctx, nseq=BATCH, seq=SEQ, row_block0=0)
        of_dec = _fnet(proj, cb, sb, *dft_dec, nseq=DEC_BATCH, seq=DEC_SEQ, row_block0=N_CTX // DEC_SEQ)
        qn = jnp.tile(q_norm[l], 2)[None, :]
        kn = jnp.tile(k_norm[l], 2)[None, :]
        oc_ctx, ka = _attn(proj, qn, kn, gm128, nseq=BATCH, seq=SEQ, row_block0=0)
        past = (cache_k[:, l].reshape(DEC_BATCH, PAST_LEN, C_KV_W),
                cache_v[:, l].reshape(DEC_BATCH, PAST_LEN, C_KV_W))
        oc_dec, _ = _attn(proj, qn, kn, gm128, nseq=DEC_BATCH, seq=DEC_SEQ,
                          row_block0=N_CTX // DEC_SEQ, past=past, tables=rope_tabs)
        new_s.append(sfin.reshape(BATCH, 2, A_HEADS, A_DK, A_DV))
        new_k.append(ka.reshape(BATCH, SEQ, C_KV_HEADS, HEAD_DIM))
        new_v.append(proj[:N_CTX, COL_CV:COL_CV + C_KV_W].reshape(BATCH, SEQ, C_KV_HEADS, HEAD_DIM))

        rw = jnp.zeros((D_MODEL, LANES), F32).at[:, :N_EXPERTS].set(router_w[l])
        rb = jnp.zeros((1, LANES), F32).at[0, :N_EXPERTS].set(router_b[l])
        x1, h2, meta, slots, info = _outproj((oa_ctx, of_ctx, oc_ctx), (oa_dec, of_dec, oc_dec),
                                             w_out[l].astype(BF16), xs, mod3, norm_ffn[l][None, :], rw, rb)

        n8 = info[:, 0, :N_EXPERTS].astype(jnp.int32)
        off = info[:, 1, :N_EXPERTS].astype(jnp.int32)
        cnt = jnp.sum(n8, axis=0)
        pcnt = (cnt + MOE_BM - 1) // MOE_BM * MOE_BM
        pend = _cumsum0(pcnt)
        base = (pend - pcnt)[None, :] + _cumsum0(n8) - n8
        tables = (n8.reshape(-1), off.reshape(-1), base.reshape(-1))
        blk = jnp.arange(MOE_NB, dtype=jnp.int32)
        blk_exp = jnp.minimum(jnp.sum((pend[None, :] <= (blk * MOE_BM)[:, None]).astype(jnp.int32), axis=1),
                              N_EXPERTS - 1)
        nused = (pend[-1:] // MOE_BM).astype(jnp.int32)
        tails = (pcnt - cnt, pend - (pcnt - cnt), nused)
        prev_exp = jnp.concatenate([jnp.full((1,), -1, jnp.int32), blk_exp[:-1]])
        first = ((blk < nused[0]) & (blk_exp != prev_exp)).astype(jnp.int32)
        end_blk = _lookup(pend // MOE_BM, blk_exp)
        nxt = jnp.where(end_blk < nused[0], _lookup(blk_exp, jnp.minimum(end_blk, MOE_NB - 1)), -1)
        wslot = (_cumsum0(first) - 1) & 1
        start_blk = (pend - pcnt) // MOE_BM
        valid = jnp.where(blk < nused[0],
                          jnp.clip(_lookup(cnt, blk_exp) - (blk - _lookup(start_blk, blk_exp)) * MOE_BM,
                                   0, MOE_BM), 0)
        sched = (blk_exp, nused, first, nxt, wslot, valid)

        xs = _dispatch(tables, tails, h2, slots)
        y = _experts(sched, xs, expert_w_gu,
                     expert_b_gu.reshape(DEPTH, N_EXPERTS, 1, 2 * D_FF), expert_w_dn,
                     expert_b_dn.reshape(DEPTH, N_EXPERTS, 1, D_MODEL), l)
        comb = functools.partial(_combine, tables, y, slots, meta, x1, mod3, norm_final[None, :])
        if l < DEPTH - 1:
            xs = (comb(final=False),)
        else:
            ctx_tiles = N_CTX // ROW_TILE
            xs = (comb(final=True, tile0=0, ntiles=ctx_tiles),
                  comb(final=True, tile0=ctx_tiles, ntiles=N_TILES - ctx_tiles))

    y_prompt = xs[0].reshape(BATCH, SEQ, D_MODEL)
    y_sample = xs[1].reshape(DEC_BATCH, DEC_SEQ, D_MODEL)
    return (y_prompt, y_sample, jnp.stack(new_s, axis=1), jnp.stack(new_k, axis=1), jnp.stack(new_v, axis=1))
```
